```python
import math
import jax, jax.numpy as jnp
from jax import lax
import numpy as np

D_MODEL = 1024
BATCH = 32
SEQ = 2048
DEPTH = 1

SSM_WIDTH = D_MODEL // 2
SSM_GROUP_CH = 16
SSM_GROUPS = SSM_WIDTH // SSM_GROUP_CH
SSM_STATE = 64
SGU_WIDTH = D_MODEL // 2
SGU_HEADS = 8
SGU_HEAD_DIM = SGU_WIDTH // SGU_HEADS
SGU_CHUNK = 128
IN_WIDTH = SSM_WIDTH + 2 * SGU_WIDTH
N_GROUPS = 4
EXPERTS_PER_GROUP = 8
N_EXPERTS = N_GROUPS * EXPERTS_PER_GROUP
TOP_K = 2
D_FF_EXPERT = D_MODEL // 2
EXPERT_BLOCK = 128
RMS_EPS = 1e-6
LN_EPS = 1e-5

kernel_name = "hybrid_s5_sgu_hmoe_block"


def rmsnorm(x, g):
    xf = x.astype(jnp.float32)
    y = xf * lax.rsqrt(jnp.mean(xf * xf, axis=-1, keepdims=True) + RMS_EPS)
    return (y * g.astype(jnp.float32)).astype(x.dtype)


def layernorm(x, g, b):
    xf = x.astype(jnp.float32)
    mu = jnp.mean(xf, axis=-1, keepdims=True)
    var = jnp.mean(jnp.square(xf - mu), axis=-1, keepdims=True)
    y = (xf - mu) * lax.rsqrt(var + LN_EPS) * g.astype(jnp.float32) + b.astype(jnp.float32)
    return y.astype(x.dtype)


def modulate(x, shift, scale):
    return x * (1.0 + scale[:, None, :]) + shift[:, None, :]


def s5_branch(u, a_re, a_im, b_re, b_im, c_re, c_im, d_skip, log_step, w_glu, b_glu):
    f32 = jnp.float32
    bq, sq, _ = u.shape
    uf = u.astype(f32).reshape(bq, sq, SSM_GROUPS, SSM_GROUP_CH)
    lam_re = jnp.minimum(a_re.astype(f32), -1e-4)
    lam_im = a_im.astype(f32)
    dt = jnp.exp(log_step.astype(f32))[:, None]
    mag = jnp.exp(lam_re * dt)
    ab_re = mag * jnp.cos(lam_im * dt)
    ab_im = mag * jnp.sin(lam_im * dt)
    den = lam_re * lam_re + lam_im * lam_im
    nr = ab_re - 1.0
    f_re = (nr * lam_re + ab_im * lam_im) / den
    f_im = (ab_im * lam_re - nr * lam_im) / den
    bu_re = jnp.einsum('bsgh,gnh->bsgn', uf, b_re.astype(f32))
    bu_im = jnp.einsum('bsgh,gnh->bsgn', uf, b_im.astype(f32))
    x_re = f_re * bu_re - f_im * bu_im
    x_im = f_re * bu_im + f_im * bu_re
    a_r = jnp.broadcast_to(ab_re[None, None], (1, sq, SSM_GROUPS, SSM_STATE))
    a_i = jnp.broadcast_to(ab_im[None, None], (1, sq, SSM_GROUPS, SSM_STATE))

    def combine(e1, e2):
        ar1, ai1, br1, bi1 = e1
        ar2, ai2, br2, bi2 = e2
        ar = ar2 * ar1 - ai2 * ai1
        ai = ar2 * ai1 + ai2 * ar1
        br = ar2 * br1 - ai2 * bi1 + br2
        bi = ar2 * bi1 + ai2 * br1 + bi2
        return (ar, ai, br, bi)

    _, _, h_re, h_im = lax.associative_scan(combine, (a_r, a_i, x_re, x_im), axis=1)
    y = (jnp.einsum('bsgn,ghn->bsgh', h_re, c_re.astype(f32))
         - jnp.einsum('bsgn,ghn->bsgh', h_im, c_im.astype(f32))
         + d_skip.astype(f32) * uf)
    y = jax.nn.gelu(y.reshape(bq, sq, SSM_WIDTH))
    y = y * jax.nn.sigmoid(y @ w_glu.astype(f32) + b_glu.astype(f32))
    return y.astype(u.dtype)


def sgu_branch(z_u, z_v, ln_g, ln_b, w_s, b_s):
    bq, sq, _ = z_u.shape
    u = jax.nn.gelu(z_u)
    v = layernorm(jax.nn.gelu(z_v), ln_g, ln_b)
    n_chunks = sq // SGU_CHUNK
    vh = v.reshape(bq, n_chunks, SGU_CHUNK, SGU_HEADS, SGU_HEAD_DIM)
    causal = jnp.tril(jnp.ones((SGU_CHUNK, SGU_CHUNK), w_s.dtype))
    ws = w_s * causal[None]
    mixed = jnp.einsum('hts,bcshd->bcthd', ws, vh) + b_s.T[None, None, :, :, None]
    return u * mixed.reshape(bq, sq, SGU_WIDTH)


def hierarchical_moe(xn, w_rg, b_rg, w_re, b_re, w1, w3, w2):
    f32 = jnp.float32
    bq, sq, dm = xn.shape
    n_tok = bq * sq
    xt = xn.reshape(n_tok, dm)
    lg = (xt @ w_rg + b_rg).astype(f32)
    pg = jax.nn.softmax(lg, axis=-1)
    grp = jnp.argmax(lg, axis=-1).astype(jnp.int32)
    pg_sel = jnp.take_along_axis(pg, grp[:, None], axis=1)[:, 0]
    le_all = (jnp.einsum('td,gde->tge', xt, w_re) + b_re[None]).astype(f32)
    le = jnp.take_along_axis(le_all, grp[:, None, None], axis=1)[:, 0]
    top_v, top_i = lax.top_k(le, TOP_K)
    pe = jax.nn.softmax(top_v, axis=-1)
    eid = grp[:, None] * EXPERTS_PER_GROUP + top_i.astype(jnp.int32)
    wgt = pg_sel[:, None] * pe

    n_asg = n_tok * TOP_K
    e_flat = eid.reshape(n_asg)
    w_flat = wgt.reshape(n_asg)
    tok_flat = jnp.repeat(jnp.arange(n_tok, dtype=jnp.int32), TOP_K)
    order = jnp.argsort(e_flat)
    e_s = e_flat[order]
    tok_s = tok_flat[order]
    w_s = w_flat[order]
    counts = jnp.bincount(e_flat, length=N_EXPERTS).astype(jnp.int32)
    padded = (counts + EXPERT_BLOCK - 1) // EXPERT_BLOCK * EXPERT_BLOCK
    starts = jnp.cumsum(counts) - counts
    ends_p = jnp.cumsum(padded)
    pstarts = ends_p - padded
    dest = pstarts[e_s] + (jnp.arange(n_asg, dtype=jnp.int32) - starts[e_s])
    n_buf = n_asg + N_EXPERTS * EXPERT_BLOCK
    n_blocks = n_buf // EXPERT_BLOCK
    buf_tok = jnp.full((n_buf,), n_tok, jnp.int32).at[dest].set(tok_s)
    buf_w = jnp.zeros((n_buf,), f32).at[dest].set(w_s)
    block_start = jnp.arange(n_blocks, dtype=jnp.int32) * EXPERT_BLOCK
    block_e = jnp.clip(jnp.searchsorted(ends_p, block_start, side='right'), 0, N_EXPERTS - 1)
    x_pad = jnp.concatenate([xt, jnp.zeros((1, dm), xt.dtype)], axis=0)
    xb = x_pad[buf_tok].reshape(n_blocks, EXPERT_BLOCK, dm)

    def expert_block(args):
        xblk, e = args
        hid = jax.nn.silu(xblk @ w1[e]) * (xblk @ w3[e])
        return hid @ w2[e]

    yb = lax.map(expert_block, (xb, block_e)).reshape(n_buf, dm)
    y = jnp.zeros((n_tok + 1, dm), yb.dtype).at[buf_tok].add(yb * buf_w[:, None].astype(yb.dtype))
    return y[:n_tok].reshape(bq, sq, dm)


def setup_inputs(seed: int = 0) -> dict:
    key = jax.random.key(seed)
    ks = iter(jax.random.split(key, 40))
    f32 = jnp.float32
    L = DEPTH

    def nrm(shape, scale):
        return jax.random.normal(next(ks), shape, f32) * scale

    d_inv = D_MODEL ** -0.5
    x = nrm((BATCH, SEQ, D_MODEL), 1.0)
    c = nrm((BATCH, D_MODEL), 1.0)
    w_ada = nrm((L, D_MODEL, 6 * D_MODEL), 0.2 * d_inv)
    b_ada = nrm((L, 6 * D_MODEL), 0.01)
    norm1_g = 1.0 + nrm((L, D_MODEL), 0.01)
    w_in = nrm((L, D_MODEL, IN_WIDTH), d_inv)
    w_gate = nrm((L, D_MODEL, 2 * D_MODEL), d_inv)
    b_gate = nrm((L, 2 * D_MODEL), 0.01)
    ssm_a_re = -0.5 + nrm((L, SSM_GROUPS, SSM_STATE), 0.01)
    ssm_a_im = jnp.pi * jnp.arange(SSM_STATE, dtype=f32) + nrm((L, SSM_GROUPS, SSM_STATE), 0.01)
    ssm_b_re = nrm((L, SSM_GROUPS, SSM_STATE, SSM_GROUP_CH), (2 * SSM_GROUP_CH) ** -0.5)
    ssm_b_im = nrm((L, SSM_GROUPS, SSM_STATE, SSM_GROUP_CH), (2 * SSM_GROUP_CH) ** -0.5)
    ssm_c_re = nrm((L, SSM_GROUPS, SSM_GROUP_CH, SSM_STATE), 0.5)
    ssm_c_im = nrm((L, SSM_GROUPS, SSM_GROUP_CH, SSM_STATE), 0.5)
    ssm_d = nrm((L, SSM_GROUPS, SSM_GROUP_CH), 1.0)
    ssm_log_step = jax.random.uniform(next(ks), (L, SSM_GROUPS), f32, math.log(1e-3), math.log(1e-1))
    w_glu = nrm((L, SSM_WIDTH, SSM_WIDTH), SSM_WIDTH ** -0.5)
    b_glu = nrm((L, SSM_WIDTH), 0.01)
    sgu_ln_g = 1.0 + nrm((L, SGU_WIDTH), 0.01)
    sgu_ln_b = nrm((L, SGU_WIDTH), 0.01)
    sgu_w = nrm((L, SGU_HEADS, SGU_CHUNK, SGU_CHUNK), SGU_CHUNK ** -0.5)
    sgu_b = 1.0 + nrm((L, SGU_HEADS, SGU_CHUNK), 0.01)
    w_branch_a = nrm((L, SSM_WIDTH, D_MODEL), SSM_WIDTH ** -0.5)
    w_branch_b = nrm((L, SGU_WIDTH, D_MODEL), SGU_WIDTH ** -0.5)
    w_out = nrm((L, D_MODEL, D_MODEL), d_inv)
    norm2_g = 1.0 + nrm((L, D_MODEL), 0.01)
    w_router_group = nrm((L, D_MODEL, N_GROUPS), d_inv)
    b_router_group = nrm((L, N_GROUPS), 0.01)
    w_router_expert = nrm((L, N_GROUPS, D_MODEL, EXPERTS_PER_GROUP), d_inv)
    b_router_expert = nrm((L, N_GROUPS, EXPERTS_PER_GROUP), 0.01)
    w1 = nrm((L, N_EXPERTS, D_MODEL, D_FF_EXPERT), d_inv)
    w3 = nrm((L, N_EXPERTS, D_MODEL, D_FF_EXPERT), d_inv)
    w2 = nrm((L, N_EXPERTS, D_FF_EXPERT, D_MODEL), D_FF_EXPERT ** -0.5)
    norm_f_g = 1.0 + nrm((D_MODEL,), 0.01)
    return {"x": x, "c": c, "w_ada": w_ada, "b_ada": b_ada, "norm1_g": norm1_g,
            "w_in": w_in, "w_gate": w_gate, "b_gate": b_gate,
            "ssm_a_re": ssm_a_re, "ssm_a_im": ssm_a_im, "ssm_b_re": ssm_b_re, "ssm_b_im": ssm_b_im,
            "ssm_c_re": ssm_c_re, "ssm_c_im": ssm_c_im, "ssm_d": ssm_d, "ssm_log_step": ssm_log_step,
            "w_glu": w_glu, "b_glu": b_glu, "sgu_ln_g": sgu_ln_g, "sgu_ln_b": sgu_ln_b,
            "sgu_w": sgu_w, "sgu_b": sgu_b, "w_branch_a": w_branch_a, "w_branch_b": w_branch_b,
            "w_out": w_out, "norm2_g": norm2_g, "w_router_group": w_router_group,
            "b_router_group": b_router_group, "w_router_expert": w_router_expert,
            "b_router_expert": b_router_expert, "w1": w1, "w3": w3, "w2": w2, "norm_f_g": norm_f_g}


def reference(x, c, w_ada, b_ada, norm1_g, w_in, w_gate, b_gate,
              ssm_a_re, ssm_a_im, ssm_b_re, ssm_b_im, ssm_c_re, ssm_c_im, ssm_d, ssm_log_step,
              w_glu, b_glu, sgu_ln_g, sgu_ln_b, sgu_w, sgu_b, w_branch_a, w_branch_b,
              w_out, norm2_g, w_router_group, b_router_group, w_router_expert,
              b_router_expert, w1, w3, w2, norm_f_g):
    h = x
    c_act = jax.nn.silu(c)
    for l in range(DEPTH):
        mod = c_act @ w_ada[l] + b_ada[l]
        sh1, sc1, gt1, sh2, sc2, gt2 = jnp.split(mod, 6, axis=-1)
        xn = modulate(rmsnorm(h, norm1_g[l]), sh1, sc1)
        proj = xn @ w_in[l]
        z_a = proj[..., :SSM_WIDTH]
        z_u = proj[..., SSM_WIDTH:SSM_WIDTH + SGU_WIDTH]
        z_v = proj[..., SSM_WIDTH + SGU_WIDTH:]
        y_a = s5_branch(z_a, ssm_a_re[l], ssm_a_im[l], ssm_b_re[l], ssm_b_im[l],
                        ssm_c_re[l], ssm_c_im[l], ssm_d[l], ssm_log_step[l], w_glu[l], b_glu[l])
        y_b = sgu_branch(z_u, z_v, sgu_ln_g[l], sgu_ln_b[l], sgu_w[l], sgu_b[l])
        gates = jax.nn.sigmoid(xn @ w_gate[l] + b_gate[l])
        g_a = gates[..., :D_MODEL]
        g_b = gates[..., D_MODEL:]
        merged = g_a * (y_a @ w_branch_a[l]) + g_b * (y_b @ w_branch_b[l])
        h = h + gt1[:, None, :] * (merged @ w_out[l])
        xn2 = modulate(rmsnorm(h, norm2_g[l]), sh2, sc2)
        y_m = hierarchical_moe(xn2, w_router_group[l], b_router_group[l], w_router_expert[l],
                               b_router_expert[l], w1[l], w3[l], w2[l])
        h = h + gt2[:, None, :] * y_m
    return rmsnorm(h, norm_f_g)
```

```python
import functools
import math

import jax
import jax.numpy as jnp
from jax import lax
from jax.experimental import pallas as pl
from jax.experimental.pallas import tpu as pltpu

F32 = jnp.float32
BF16 = jnp.bfloat16
I32 = jnp.int32

LANES = 128
SUBLANES = 8
VMEM_LIMIT = 56 * 1024 * 1024

RMS_EPS = 1e-6
LN_EPS = 1e-5

SSM_GROUP_CH = 16
SSM_STATE = 64
GROUPS_PER_BLOCK = LANES // SSM_GROUP_CH
SGU_HEADS = 8
SGU_CHUNK = 128
N_GROUPS = 4
EXPERTS_PER_GROUP = 8
PAIRS_PER_GROUP = EXPERTS_PER_GROUP * (EXPERTS_PER_GROUP - 1) // 2
N_CLASSES = N_GROUPS * PAIRS_PER_GROUP
EXPERT_ROWS = 128

SCAN_BATCHES = SUBLANES
SCAN_TS = 128
SLAB_PITCH = SCAN_TS + 8
SCAN_UNROLL = 8


def _gelu(x):
    return 0.5 * x * (1.0 + jnp.tanh(math.sqrt(2.0 / math.pi) * (x + 0.044715 * (x * x * x))))


def _dot(a, b):
    return jnp.dot(a, b, preferred_element_type=F32)


def _split_bf16(a):
    hi = a.astype(BF16)
    lo = (a - hi.astype(F32)).astype(BF16)
    return hi, lo


def _cparams(sem):
    return pltpu.CompilerParams(dimension_semantics=sem, vmem_limit_bytes=VMEM_LIMIT)


def _adaln_body(c_ref, w_ref, b_ref, o_ref):
    c = c_ref[...]
    act = c * jax.nn.sigmoid(c)
    a_hi, a_lo = _split_bf16(act)
    w_hi, w_lo = _split_bf16(w_ref[...])
    o_ref[...] = _dot(a_hi, w_hi) + _dot(a_hi, w_lo) + _dot(a_lo, w_hi) + b_ref[...]


def _adaln(c, w, b):
    bsz, d = c.shape
    n = w.shape[1]
    tn = 1024
    return pl.pallas_call(
        _adaln_body,
        grid=(n // tn,),
        in_specs=[pl.BlockSpec((bsz, d), lambda j: (0, 0)),
                  pl.BlockSpec((d, tn), lambda j: (0, j)),
                  pl.BlockSpec((1, tn), lambda j: (0, j))],
        out_specs=pl.BlockSpec((bsz, tn), lambda j: (0, j)),
        out_shape=jax.ShapeDtypeStruct((bsz, n), F32),
        compiler_params=_cparams(("arbitrary",)),
        name="adaln",
    )(c, w, b.reshape(1, n))


def _inproj_body(x_ref, mod_ref, g1_ref, win_ref, wgate_ref, bgate_ref, lng_ref, lnb_ref,
                 za_ref, u_ref, v_ref, gates_ref):
    x = x_ref[0]
    ms = jnp.mean(x * x, axis=-1, keepdims=True)
    xn = x * lax.rsqrt(ms + RMS_EPS) * g1_ref[...]
    xn = xn * (1.0 + mod_ref[0, 1:2, :]) + mod_ref[0, 0:1, :]
    xb = xn.astype(BF16)
    proj = _dot(xb, win_ref[...])
    w = za_ref.shape[-1]
    za_ref[0] = proj[:, :w]
    u_ref[0] = _gelu(proj[:, w:2 * w]).astype(BF16)
    gv = _gelu(proj[:, 2 * w:])
    mu = jnp.mean(gv, axis=-1, keepdims=True)
    cen = gv - mu
    var = jnp.mean(cen * cen, axis=-1, keepdims=True)
    v_ref[0] = (cen * lax.rsqrt(var + LN_EPS) * lng_ref[...] + lnb_ref[...]).astype(BF16)
    gates_ref[0] = jax.nn.sigmoid(_dot(xb, wgate_ref[...]) + bgate_ref[...]).astype(BF16)


def _inproj(x, mod, g1, w_in, w_gate, b_gate, ln_g, ln_b, tm):
    bsz, s, d = x.shape
    w = w_in.shape[1] // 3
    ng = w_gate.shape[1]
    tok = lambda n: pl.BlockSpec((1, tm, n), lambda b, i: (b, i, 0))
    full = lambda a: pl.BlockSpec(a.shape, lambda b, i: (0,) * a.ndim)
    args = (g1.reshape(1, d), w_in.astype(BF16), w_gate.astype(BF16), b_gate.reshape(1, ng),
            ln_g.reshape(1, w), ln_b.reshape(1, w))
    return pl.pallas_call(
        _inproj_body,
        grid=(bsz, s // tm),
        in_specs=[tok(d), pl.BlockSpec((1, 6, d), lambda b, i: (b, 0, 0))] + [full(a) for a in args],
        out_specs=[tok(w), tok(w), tok(w), tok(ng)],
        out_shape=[jax.ShapeDtypeStruct((bsz, s, w), F32),
                   jax.ShapeDtypeStruct((bsz, s, w), BF16),
                   jax.ShapeDtypeStruct((bsz, s, w), BF16),
                   jax.ShapeDtypeStruct((bsz, s, ng), BF16)],
        compiler_params=_cparams(("parallel", "parallel")),
        name="in_proj",
    )(x, mod, *args)


def _s5_params(a_re, a_im, b_re, b_im, c_re, c_im, log_step):
    g, n = a_re.shape
    nblk = g // GROUPS_PER_BLOCK
    lam_re = jnp.minimum(a_re.astype(F32), -1e-4)
    lam_im = a_im.astype(F32)
    dt = jnp.exp(log_step.astype(F32))[:, None]
    mag = jnp.exp(lam_re * dt)
    ab_re = mag * jnp.cos(lam_im * dt)
    ab_im = mag * jnp.sin(lam_im * dt)
    den = lam_re * lam_re + lam_im * lam_im
    nr = ab_re - 1.0
    f_re = (nr * lam_re + ab_im * lam_im) / den
    f_im = (ab_im * lam_re - nr * lam_im) / den
    bt_re = b_re.astype(F32).transpose(0, 2, 1)
    bt_im = b_im.astype(F32).transpose(0, 2, 1)
    w_re = f_re[:, None, :] * bt_re - f_im[:, None, :] * bt_im
    w_im = f_re[:, None, :] * bt_im + f_im[:, None, :] * bt_re
    eye = jnp.eye(GROUPS_PER_BLOCK, dtype=F32)

    def bdiag_in(wt):
        t = wt.reshape(nblk, GROUPS_PER_BLOCK, SSM_GROUP_CH, 1, n) * eye[None, :, None, :, None]
        return t.reshape(nblk, GROUPS_PER_BLOCK * SSM_GROUP_CH, GROUPS_PER_BLOCK * n)

    def bdiag_out(ct):
        t = ct.transpose(0, 2, 1).reshape(nblk, GROUPS_PER_BLOCK, n, 1, SSM_GROUP_CH)
        t = t * eye[None, :, None, :, None]
        return t.reshape(nblk, GROUPS_PER_BLOCK * n, GROUPS_PER_BLOCK * SSM_GROUP_CH)

    w_in = jnp.concatenate([bdiag_in(w_re), bdiag_in(w_im)], axis=2).astype(BF16)
    w_out = jnp.concatenate([bdiag_out(c_re.astype(F32)), -bdiag_out(c_im.astype(F32))],
                            axis=1).astype(BF16)
    a_row = jnp.concatenate([ab_re.reshape(nblk, -1), ab_im.reshape(nblk, -1)], axis=1).reshape(1, -1)
    return a_row, w_in, w_out


def _s5_body(z_ref, a_ref, win_ref, wout_ref, d_ref, wglu_ref, bglu_ref, o_ref,
             zslab, zsb, xs, hstate, yslab):
    nb, ts, w = z_ref.shape
    rows = nb * ts
    nblk = w // LANES
    half = GROUPS_PER_BLOCK * SSM_STATE
    sw = 2 * half

    @pl.when(pl.program_id(1) == 0)
    def _():
        hstate[...] = jnp.zeros_like(hstate)

    for b in range(nb):
        for c in range(nblk):
            zslab[c, b * SLAB_PITCH:b * SLAB_PITCH + ts, :] = z_ref[b, :, c * LANES:(c + 1) * LANES]

    def to_pos_major(s, carry):
        r0 = pl.multiple_of(s * nb, nb)
        for c in range(nblk):
            zsb[pl.ds(r0, nb), c * LANES:(c + 1) * LANES] = zslab.at[c][pl.ds(s, nb, stride=SLAB_PITCH), :]
        return carry

    lax.fori_loop(0, ts, to_pos_major, 0)

    for j in range(nblk):
        xs[:, j * sw:(j + 1) * sw] = _dot(zsb[:, j * LANES:(j + 1) * LANES].astype(BF16), win_ref[j])

    for j in range(nblk):
        re_sl = slice(j * sw, j * sw + half)
        im_sl = slice(j * sw + half, (j + 1) * sw)
        a_r = jnp.broadcast_to(a_ref[:, re_sl], (nb, half))
        a_i = jnp.broadcast_to(a_ref[:, im_sl], (nb, half))

        def scan_chunk(k, carry):
            h_r, h_i = carry
            for q in range(SCAN_UNROLL):
                r0 = pl.multiple_of((k * SCAN_UNROLL + q) * nb, nb)
                x_r = xs[pl.ds(r0, nb), re_sl]
                x_i = xs[pl.ds(r0, nb), im_sl]
                n_r = a_r * h_r - a_i * h_i + x_r
                n_i = a_r * h_i + a_i * h_r + x_i
                xs[pl.ds(r0, nb), re_sl] = n_r
                xs[pl.ds(r0, nb), im_sl] = n_i
                h_r, h_i = n_r, n_i
            return h_r, h_i

        h_r, h_i = lax.fori_loop(0, ts // SCAN_UNROLL, scan_chunk, (hstate[:, re_sl], hstate[:, im_sl]))
        hstate[:, re_sl] = h_r
        hstate[:, im_sl] = h_i

    ys = []
    for j in range(nblk):
        y = _dot(xs[:, j * sw:(j + 1) * sw].astype(BF16), wout_ref[j])
        y = y + d_ref[:, j * LANES:(j + 1) * LANES] * zsb[:, j * LANES:(j + 1) * LANES]
        ys.append(_gelu(y))
    g = jnp.concatenate(ys, axis=1)
    g = g * jax.nn.sigmoid(_dot(g.astype(BF16), wglu_ref[...]) + bglu_ref[...])
    for c in range(nblk):
        yslab[c] = g[:, c * LANES:(c + 1) * LANES]

    for b in range(nb):
        for c in range(nblk):
            o_ref[b, :, c * LANES:(c + 1) * LANES] = yslab.at[c][pl.ds(b, ts, stride=nb), :].astype(BF16)


def _s5(za, a_row, w_in, w_out, d_skip, w_glu, b_glu):
    bsz, s, w = za.shape
    nb, ts = SCAN_BATCHES, SCAN_TS
    rows = nb * ts
    nblk = w // LANES
    nstate = a_row.shape[1]
    full = lambda a: pl.BlockSpec(a.shape, lambda b, i: (0,) * a.ndim)
    args = (a_row, w_in, w_out, d_skip.reshape(1, w).astype(F32), w_glu.astype(BF16), b_glu.reshape(1, w))
    return pl.pallas_call(
        _s5_body,
        grid=(bsz // nb, s // ts),
        in_specs=[pl.BlockSpec((nb, ts, w), lambda b, i: (b, i, 0))] + [full(a) for a in args],
        out_specs=pl.BlockSpec((nb, ts, w), lambda b, i: (b, i, 0)),
        out_shape=jax.ShapeDtypeStruct((bsz, s, w), BF16),
        scratch_shapes=[pltpu.VMEM((nblk, nb * SLAB_PITCH, LANES), F32),
                        pltpu.VMEM((rows, w), F32),
                        pltpu.VMEM((rows, nstate), F32),
                        pltpu.VMEM((nb, nstate), F32),
                        pltpu.VMEM((nblk, rows, LANES), F32)],
        compiler_params=_cparams(("parallel", "arbitrary")),
        name="s5",
    )(za, *args)


def _mix_body(x_ref, u_ref, v_ref, ya_ref, gates_ref, mod_ref, wp_ref, sb_ref, wba_ref, wbb_ref,
              wout_ref, g2_ref, wrh_ref, wrl_ref, br_ref, h1_ref, xp_ref, route_ref):
    tm, d = x_ref.shape[1], x_ref.shape[2]
    w = u_ref.shape[2]
    npair = w // LANES
    hd = LANES // 2

    t_idx = lax.broadcasted_iota(I32, (SGU_CHUNK, 2 * SGU_CHUNK), 0)
    s_idx = lax.broadcasted_iota(I32, (SGU_CHUNK, 2 * SGU_CHUNK), 1) & (SGU_CHUNK - 1)
    causal = s_idx <= t_idx
    lane = lax.broadcasted_iota(I32, (SGU_CHUNK, LANES), 1)
    first_head = lane < hd
    mixed_rows = []
    for c in range(tm // SGU_CHUNK):
        vc = v_ref[0, c * SGU_CHUNK:(c + 1) * SGU_CHUNK, :]
        blocks = []
        for j in range(npair):
            vb = vc[:, j * LANES:(j + 1) * LANES]
            zero = jnp.zeros_like(vb)
            rhs = jnp.concatenate([jnp.where(first_head, vb, zero), jnp.where(first_head, zero, vb)], axis=0)
            wj = jnp.where(causal, wp_ref[j], jnp.zeros_like(wp_ref[j]))
            blocks.append(_dot(wj, rhs))
        mixed_rows.append(jnp.concatenate(blocks, axis=1) + sb_ref[...])
    mixed = jnp.concatenate(mixed_rows, axis=0)
    yb = (u_ref[0].astype(F32) * mixed).astype(BF16)

    pa = _dot(ya_ref[0], wba_ref[...])
    pb = _dot(yb, wbb_ref[...])
    merged = gates_ref[0, :, :d].astype(F32) * pa + gates_ref[0, :, d:].astype(F32) * pb
    o = _dot(merged.astype(BF16), wout_ref[...])
    h1 = x_ref[0] + mod_ref[0, 2:3, :] * o
    h1_ref[0] = h1

    ms = jnp.mean(h1 * h1, axis=-1, keepdims=True)
    xn = h1 * lax.rsqrt(ms + RMS_EPS) * g2_ref[...]
    xn = xn * (1.0 + mod_ref[0, 4:5, :]) + mod_ref[0, 3:4, :]
    x_hi, x_lo = _split_bf16(xn)

    bits = lax.bitcast_convert_type(x_hi.astype(F32), I32)
    xp_ref[0] = lax.shift_right_logical(bits[:, :d // 2], 16) | (bits[:, d // 2:] & jnp.int32(-65536))

    logits = _dot(x_hi, wrh_ref[...]) + _dot(x_hi, wrl_ref[...]) + _dot(x_lo, wrh_ref[...]) + br_ref[...]
    lt = logits.T
    best = lt[0:1, :]
    grp = jnp.zeros((1, tm), I32)
    for gi in range(1, N_GROUPS):
        better = lt[gi:gi + 1, :] > best
        grp = jnp.where(better, gi, grp)
        best = jnp.where(better, lt[gi:gi + 1, :], best)
    den = jnp.zeros((1, tm), F32)
    for gi in range(N_GROUPS):
        den = den + jnp.exp(lt[gi:gi + 1, :] - best)
    pg = 1.0 / den
    le = lt[SUBLANES:2 * SUBLANES, :]
    for gi in range(1, N_GROUPS):
        le = jnp.where(grp == gi, lt[SUBLANES * (gi + 1):SUBLANES * (gi + 2), :], le)
    eidx = lax.broadcasted_iota(I32, (EXPERTS_PER_GROUP, tm), 0).astype(F32)
    none = float(EXPERTS_PER_GROUP)
    v1 = jnp.max(le, axis=0, keepdims=True)
    i1 = jnp.min(jnp.where(le == v1, eidx, none), axis=0, keepdims=True)
    rest = jnp.where(eidx == i1, -jnp.inf, le)
    v2 = jnp.max(rest, axis=0, keepdims=True)
    i2 = jnp.min(jnp.where(rest == v2, eidx, none), axis=0, keepdims=True)
    ex = jnp.exp(v2 - v1)
    p1 = 1.0 / (1.0 + ex)
    wt1 = pg * p1
    wt2 = pg * (ex * p1)
    first_lo = i1 < i2
    lo = jnp.where(first_lo, i1, i2)
    hi = jnp.where(first_lo, i2, i1)
    pair = lo * (2 * EXPERTS_PER_GROUP - 1.0 - lo) * 0.5 + (hi - lo - 1.0)
    cls = grp.astype(F32) * PAIRS_PER_GROUP + pair
    route_ref[...] = jnp.concatenate(
        [cls, jnp.where(first_lo, wt1, wt2), jnp.where(first_lo, wt2, wt1),
         jnp.zeros((SUBLANES - 3, tm), F32)], axis=0)


def _mix(x, u, v, ya, gates, mod, sgu_w, sgu_b, w_ba, w_bb, w_out, g2, w_rg, b_rg, w_re, b_re, tm):
    bsz, s, d = x.shape
    w = u.shape[2]
    nt = bsz * s
    wp = sgu_w.reshape(SGU_HEADS // 2, 2, SGU_CHUNK, SGU_CHUNK).transpose(0, 2, 1, 3)
    wp = wp.reshape(SGU_HEADS // 2, SGU_CHUNK, 2 * SGU_CHUNK).astype(BF16)
    sb = jnp.repeat(sgu_b.T, w // SGU_HEADS, axis=1).astype(F32)
    wr = jnp.zeros((d, LANES), F32)
    wr = wr.at[:, :N_GROUPS].set(w_rg)
    wr = wr.at[:, SUBLANES:SUBLANES + N_GROUPS * EXPERTS_PER_GROUP].set(
        w_re.transpose(1, 0, 2).reshape(d, N_GROUPS * EXPERTS_PER_GROUP))
    br = jnp.zeros((1, LANES), F32)
    br = br.at[0, :N_GROUPS].set(b_rg)
    br = br.at[0, SUBLANES:SUBLANES + N_GROUPS * EXPERTS_PER_GROUP].set(b_re.reshape(-1))
    wr_hi, wr_lo = _split_bf16(wr)
    tok = lambda n: pl.BlockSpec((1, tm, n), lambda b, i: (b, i, 0))
    full = lambda a: pl.BlockSpec(a.shape, lambda b, i: (0,) * a.ndim)
    args = (wp, sb, w_ba.astype(BF16), w_bb.astype(BF16), w_out.astype(BF16), g2.reshape(1, d),
            wr_hi, wr_lo, br)
    nsteps = s // tm
    return pl.pallas_call(
        _mix_body,
        grid=(bsz, nsteps),
        in_specs=[tok(d), tok(w), tok(w), tok(w), tok(2 * d),
                  pl.BlockSpec((1, 6, d), lambda b, i: (b, 0, 0))] + [full(a) for a in args],
        out_specs=[tok(d), tok(d // 2),
                   pl.BlockSpec((SUBLANES, tm), lambda b, i: (0, b * nsteps + i))],
        out_shape=[jax.ShapeDtypeStruct((bsz, s, d), F32),
                   jax.ShapeDtypeStruct((bsz, s, d // 2), I32),
                   jax.ShapeDtypeStruct((SUBLANES, nt), F32)],
        compiler_params=_cparams(("parallel", "parallel")),
        name="mix",
    )(x, u, v, ya, gates, mod, *args)


def _dispatch_tables(route, n_tok):
    cls = route[0].astype(I32)
    n_blocks = n_tok // EXPERT_ROWS + N_CLASSES
    n_buf = n_blocks * EXPERT_ROWS
    counts = jnp.zeros((N_CLASSES,), I32).at[cls].add(1)
    padded = (counts + EXPERT_ROWS - 1) // EXPERT_ROWS * EXPERT_ROWS
    ends_p = jnp.cumsum(padded)
    pstarts = ends_p - padded
    starts = jnp.cumsum(counts) - counts
    order = jnp.argsort(cls).astype(I32)
    cls_s = cls[order]
    dest = pstarts[cls_s] + (jnp.arange(n_tok, dtype=I32) - starts[cls_s])
    pos = jnp.zeros((n_tok,), I32).at[order].set(dest)
    tok_sorted = jnp.zeros((n_buf,), I32).at[dest].set(order)
    w_lo = jnp.zeros((n_buf,), F32).at[dest].set(route[1][order])
    w_hi = jnp.zeros((n_buf,), F32).at[dest].set(route[2][order])
    n_used = (ends_p[-1] // EXPERT_ROWS).astype(I32)
    blk = jnp.arange(n_blocks, dtype=I32)
    blk_cls = jnp.searchsorted(ends_p, jnp.minimum(blk, n_used - 1) * EXPERT_ROWS, side='right')
    blk_cls = jnp.clip(blk_cls, 0, N_CLASSES - 1).astype(I32)
    lo_tab, hi_tab = [], []
    for g in range(N_GROUPS):
        for a in range(EXPERTS_PER_GROUP):
            for b in range(a + 1, EXPERTS_PER_GROUP):
                lo_tab.append(g * EXPERTS_PER_GROUP + a)
                hi_tab.append(g * EXPERTS_PER_GROUP + b)
    e_lo = jnp.asarray(lo_tab, I32)[blk_cls]
    e_hi = jnp.asarray(hi_tab, I32)[blk_cls]
    wts = jnp.zeros((n_blocks, SUBLANES, EXPERT_ROWS), F32)
    wts = wts.at[:, 0, :].set(w_lo.reshape(n_blocks, EXPERT_ROWS))
    wts = wts.at[:, 1, :].set(w_hi.reshape(n_blocks, EXPERT_ROWS))
    return pos, tok_sorted.reshape(n_blocks, 1, EXPERT_ROWS), wts, e_lo, e_hi, n_used.reshape(1)


def _gather_rows(idx_ref, n, src_hbm, dst, sem):
    for r in range(n):
        pltpu.make_async_copy(src_hbm.at[pl.ds(idx_ref[0, 0, r], 1)], dst.at[pl.ds(r, 1)], sem).start()


def _wait_rows(n, src_hbm, dst, sem):
    pltpu.make_async_copy(src_hbm.at[pl.ds(0, n)], dst, sem).wait()


def _unpack_rows(xp):
    left = lax.bitcast_convert_type(lax.shift_left(xp, 16), F32).astype(BF16)
    right = lax.bitcast_convert_type(xp & jnp.int32(-65536), F32).astype(BF16)
    return jnp.concatenate([left, right], axis=1)


def _experts_body(elo_ref, ehi_ref, nu_ref, tokc_ref, tokn_ref, wts_ref, xp_hbm,
                  w1a_ref, w3a_ref, w2a_ref, w1b_ref, w3b_ref, w2b_ref, o_ref, xbuf, sem):
    i = pl.program_id(0)
    n_used = nu_ref[0]
    slot = i % 2

    @pl.when(i == 0)
    def _():
        _gather_rows(tokc_ref, EXPERT_ROWS, xp_hbm, xbuf.at[0], sem.at[0])

    @pl.when(i + 1 < n_used)
    def _():
        _gather_rows(tokn_ref, EXPERT_ROWS, xp_hbm, xbuf.at[1 - slot], sem.at[1 - slot])

    @pl.when(i < n_used)
    def _():
        _wait_rows(EXPERT_ROWS, xp_hbm, xbuf.at[slot], sem.at[slot])
        x = _unpack_rows(xbuf[slot])

        def ffn(w1_ref, w3_ref, w2_ref):
            a = _dot(x, w1_ref[0])
            hid = (a * jax.nn.sigmoid(a)) * _dot(x, w3_ref[0])
            return _dot(hid.astype(BF16), w2_ref[0])

        wt = wts_ref[0].T
        o_ref[...] = wt[:, 0:1] * ffn(w1a_ref, w3a_ref, w2a_ref) + wt[:, 1:2] * ffn(w1b_ref, w3b_ref, w2b_ref)

    @pl.when(i >= n_used)
    def _():
        o_ref[...] = jnp.zeros_like(o_ref)


def _experts(xp, tok_sorted, wts, e_lo, e_hi, n_used, w1, w3, w2):
    n_blocks = tok_sorted.shape[0]
    nt, dh = xp.shape
    d = 2 * dh
    ff = w1.shape[2]
    idx_spec = lambda f: pl.BlockSpec((1, 1, EXPERT_ROWS), f, memory_space=pltpu.SMEM)
    wsel = lambda shape, use_hi: pl.BlockSpec(
        (1,) + shape, (lambda i, lo, hi, nu: (hi[i], 0, 0)) if use_hi else (lambda i, lo, hi, nu: (lo[i], 0, 0)))
    w1b, w3b, w2b = w1.astype(BF16), w3.astype(BF16), w2.astype(BF16)
    grid_spec = pltpu.PrefetchScalarGridSpec(
        num_scalar_prefetch=3,
        grid=(n_blocks,),
        in_specs=[idx_spec(lambda i, lo, hi, nu: (i, 0, 0)),
                  idx_spec(lambda i, lo, hi, nu: (jnp.minimum(i + 1, n_blocks - 1), 0, 0)),
                  pl.BlockSpec((1, SUBLANES, EXPERT_ROWS), lambda i, lo, hi, nu: (i, 0, 0)),
                  pl.BlockSpec(memory_space=pl.ANY),
                  wsel((d, ff), False), wsel((d, ff), False), wsel((ff, d), False),
                  wsel((d, ff), True), wsel((d, ff), True), wsel((ff, d), True)],
        out_specs=pl.BlockSpec((EXPERT_ROWS, d), lambda i, lo, hi, nu: (i, 0)),
        scratch_shapes=[pltpu.VMEM((2, EXPERT_ROWS, dh), I32), pltpu.SemaphoreType.DMA((2,))],
    )
    return pl.pallas_call(
        _experts_body,
        grid_spec=grid_spec,
        out_shape=jax.ShapeDtypeStruct((n_blocks * EXPERT_ROWS, d), F32),
        compiler_params=_cparams(("arbitrary",)),
        name="experts",
    )(e_lo, e_hi, n_used, tok_sorted, tok_sorted, wts, xp, w1b, w3b, w2b, w1b, w3b, w2b)


def _final_body(posc_ref, posn_ref, y_hbm, h1_ref, mod_ref, gf_ref, o_ref, ybuf, sem):
    i = pl.program_id(0)
    n = pl.num_programs(0)
    tm = h1_ref.shape[0]
    slot = i % 2

    @pl.when(i == 0)
    def _():
        _gather_rows(posc_ref, tm, y_hbm, ybuf.at[0], sem.at[0])

    @pl.when(i + 1 < n)
    def _():
        _gather_rows(posn_ref, tm, y_hbm, ybuf.at[1 - slot], sem.at[1 - slot])

    _wait_rows(tm, y_hbm, ybuf.at[slot], sem.at[slot])
    h = h1_ref[...] + mod_ref[0, 5:6, :] * ybuf[slot]
    ms = jnp.mean(h * h, axis=-1, keepdims=True)
    o_ref[...] = h * lax.rsqrt(ms + RMS_EPS) * gf_ref[...]


def _final(pos, y_sorted, h1, mod, gf, seq, tm):
    nt, d = h1.shape
    n = nt // tm
    pos3 = pos.reshape(n, 1, tm)
    idx_spec = lambda f: pl.BlockSpec((1, 1, tm), f, memory_space=pltpu.SMEM)
    return pl.pallas_call(
        _final_body,
        grid=(n,),
        in_specs=[idx_spec(lambda i: (i, 0, 0)),
                  idx_spec(lambda i: (jnp.minimum(i + 1, n - 1), 0, 0)),
                  pl.BlockSpec(memory_space=pl.ANY),
                  pl.BlockSpec((tm, d), lambda i: (i, 0)),
                  pl.BlockSpec((1, 6, d), lambda i: (i * tm // seq, 0, 0)),
                  pl.BlockSpec((1, d), lambda i: (0, 0))],
        out_specs=pl.BlockSpec((tm, d), lambda i: (i, 0)),
        out_shape=jax.ShapeDtypeStruct((nt, d), F32),
        scratch_shapes=[pltpu.VMEM((2, tm, d), F32), pltpu.SemaphoreType.DMA((2,))],
        compiler_params=_cparams(("arbitrary",)),
        name="final",
    )(pos3, pos3, y_sorted, h1, mod, gf.reshape(1, d))


def kernel(x, c, w_ada, b_ada, norm1_g, w_in, w_gate, b_gate, ssm_a_re, ssm_a_im, ssm_b_re, ssm_b_im, ssm_c_re, ssm_c_im, ssm_d, ssm_log_step, w_glu, b_glu, sgu_ln_g, sgu_ln_b, sgu_w, sgu_b, w_branch_a, w_branch_b, w_out, norm2_g, w_router_group, b_router_group, w_router_expert, b_router_expert, w1, w3, w2, norm_f_g):
    bsz, seq, d = x.shape
    depth = w_ada.shape[0]
    assert depth == 1 and bsz % SCAN_BATCHES == 0 and seq % 512 == 0
    l = 0
    mod = _adaln(c, w_ada[l], b_ada[l]).reshape(bsz, 6, d)
    za, u, v, gates = _inproj(x, mod, norm1_g[l], w_in[l], w_gate[l], b_gate[l],
                              sgu_ln_g[l], sgu_ln_b[l], tm=512)
    a_row, s5_in, s5_out = _s5_params(ssm_a_re[l], ssm_a_im[l], ssm_b_re[l], ssm_b_im[l],
                                      ssm_c_re[l], ssm_c_im[l], ssm_log_step[l])
    ya = _s5(za, a_row, s5_in, s5_out, ssm_d[l], w_glu[l], b_glu[l])
    h1, xp, route = _mix(x, u, v, ya, gates, mod, sgu_w[l], sgu_b[l], w_branch_a[l], w_branch_b[l],
                         w_out[l], norm2_g[l], w_router_group[l], b_router_group[l],
                         w_router_expert[l], b_router_expert[l], tm=256)
    nt = bsz * seq
    pos, tok_sorted, wts, e_lo, e_hi, n_used = _dispatch_tables(route, nt)
    y_sorted = _experts(xp.reshape(nt, d // 2), tok_sorted, wts, e_lo, e_hi, n_used, w1[l], w3[l], w2[l])
    out = _final(pos, y_sorted, h1.reshape(nt, d), mod, norm_f_g, seq, tm=256)
    return out.reshape(bsz, seq, d)
```

```python
import functools
import math

import jax
import jax.numpy as jnp
from jax import lax
from jax.experimental import pallas as pl
from jax.experimental.pallas import tpu as pltpu

F32 = jnp.float32
BF16 = jnp.bfloat16
I32 = jnp.int32

LANES = 128
SUBLANES = 8
VMEM_LIMIT = 56 * 1024 * 1024

RMS_EPS = 1e-6
LN_EPS = 1e-5

SSM_GROUP_CH = 16
SSM_STATE = 64
GROUPS_PER_BLOCK = LANES // SSM_GROUP_CH
SGU_HEADS = 8
SGU_CHUNK = 128
N_GROUPS = 4
EXPERTS_PER_GROUP = 8
PAIRS_PER_GROUP = EXPERTS_PER_GROUP * (EXPERTS_PER_GROUP - 1) // 2
N_CLASSES = N_GROUPS * PAIRS_PER_GROUP
EXPERT_ROWS = 128

SCAN_BATCHES = SUBLANES
SCAN_TS = 128
SLAB_PITCH = SCAN_TS + 8
SCAN_UNROLL = 8


def _gelu(x):
    return 0.5 * x * (1.0 + jnp.tanh(math.sqrt(2.0 / math.pi) * (x + 0.044715 * (x * x * x))))


def _dot(a, b):
    return jnp.dot(a, b, preferred_element_type=F32)


def _split_bf16(a):
    hi = a.astype(BF16)
    lo = (a - hi.astype(F32)).astype(BF16)
    return hi, lo


def _cparams(sem):
    return pltpu.CompilerParams(dimension_semantics=sem, vmem_limit_bytes=VMEM_LIMIT)


def _adaln_body(c_ref, w_ref, b_ref, o_ref):
    c = c_ref[...]
    act = c * jax.nn.sigmoid(c)
    a_hi, a_lo = _split_bf16(act)
    w_hi, w_lo = _split_bf16(w_ref[...])
    o_ref[...] = _dot(a_hi, w_hi) + _dot(a_hi, w_lo) + _dot(a_lo, w_hi) + b_ref[...]


def _adaln(c, w, b):
    bsz, d = c.shape
    n = w.shape[1]
    tn = 1024
    return pl.pallas_call(
        _adaln_body,
        grid=(n // tn,),
        in_specs=[pl.BlockSpec((bsz, d), lambda j: (0, 0)),
                  pl.BlockSpec((d, tn), lambda j: (0, j)),
                  pl.BlockSpec((1, tn), lambda j: (0, j))],
        out_specs=pl.BlockSpec((bsz, tn), lambda j: (0, j)),
        out_shape=jax.ShapeDtypeStruct((bsz, n), F32),
        compiler_params=_cparams(("arbitrary",)),
        name="adaln",
    )(c, w, b.reshape(1, n))


def _inproj_body(x_ref, mod_ref, g1_ref, win_ref, wgate_ref, bgate_ref, lng_ref, lnb_ref,
                 za_ref, u_ref, v_ref, gates_ref):
    x = x_ref[0]
    ms = jnp.mean(x * x, axis=-1, keepdims=True)
    xn = x * lax.rsqrt(ms + RMS_EPS) * g1_ref[...]
    xn = xn * (1.0 + mod_ref[0, 1:2, :]) + mod_ref[0, 0:1, :]
    xb = xn.astype(BF16)
    proj = _dot(xb, win_ref[...])
    w = za_ref.shape[-1]
    za_ref[0] = proj[:, :w]
    u_ref[0] = _gelu(proj[:, w:2 * w]).astype(BF16)
    gv = _gelu(proj[:, 2 * w:])
    mu = jnp.mean(gv, axis=-1, keepdims=True)
    cen = gv - mu
    var = jnp.mean(cen * cen, axis=-1, keepdims=True)
    v_ref[0] = (cen * lax.rsqrt(var + LN_EPS) * lng_ref[...] + lnb_ref[...]).astype(BF16)
    gates_ref[0] = jax.nn.sigmoid(_dot(xb, wgate_ref[...]) + bgate_ref[...]).astype(BF16)


def _inproj(x, mod, g1, w_in, w_gate, b_gate, ln_g, ln_b, tm):
    bsz, s, d = x.shape
    w = w_in.shape[1] // 3
    ng = w_gate.shape[1]
    tok = lambda n: pl.BlockSpec((1, tm, n), lambda b, i: (b, i, 0))
    full = lambda a: pl.BlockSpec(a.shape, lambda b, i: (0,) * a.ndim)
    args = (g1.reshape(1, d), w_in.astype(BF16), w_gate.astype(BF16), b_gate.reshape(1, ng),
            ln_g.reshape(1, w), ln_b.reshape(1, w))
    return pl.pallas_call(
        _inproj_body,
        grid=(bsz, s // tm),
        in_specs=[tok(d), pl.BlockSpec((1, 6, d), lambda b, i: (b, 0, 0))] + [full(a) for a in args],
        out_specs=[tok(w), tok(w), tok(w), tok(ng)],
        out_shape=[jax.ShapeDtypeStruct((bsz, s, w), F32),
                   jax.ShapeDtypeStruct((bsz, s, w), BF16),
                   jax.ShapeDtypeStruct((bsz, s, w), BF16),
                   jax.ShapeDtypeStruct((bsz, s, ng), BF16)],
        compiler_params=_cparams(("parallel", "parallel")),
        name="in_proj",
    )(x, mod, *args)


def _s5_params(a_re, a_im, b_re, b_im, c_re, c_im, log_step):
    g, n = a_re.shape
    nblk = g // GROUPS_PER_BLOCK
    lam_re = jnp.minimum(a_re.astype(F32), -1e-4)
    lam_im = a_im.astype(F32)
    dt = jnp.exp(log_step.astype(F32))[:, None]
    mag = jnp.exp(lam_re * dt)
    ab_re = mag * jnp.cos(lam_im * dt)
    ab_im = mag * jnp.sin(lam_im * dt)
    den = lam_re * lam_re + lam_im * lam_im
    nr = ab_re - 1.0
    f_re = (nr * lam_re + ab_im * lam_im) / den
    f_im = (ab_im * lam_re - nr * lam_im) / den
    bt_re = b_re.astype(F32).transpose(0, 2, 1)
    bt_im = b_im.astype(F32).transpose(0, 2, 1)
    w_re = f_re[:, None, :] * bt_re - f_im[:, None, :] * bt_im
    w_im = f_re[:, None, :] * bt_im + f_im[:, None, :] * bt_re
    eye = jnp.eye(GROUPS_PER_BLOCK, dtype=F32)

    def bdiag_in(wt):
        t = wt.reshape(nblk, GROUPS_PER_BLOCK, SSM_GROUP_CH, 1, n) * eye[None, :, None, :, None]
        return t.reshape(nblk, GROUPS_PER_BLOCK * SSM_GROUP_CH, GROUPS_PER_BLOCK * n)

    def bdiag_out(ct):
        t = ct.transpose(0, 2, 1).reshape(nblk, GROUPS_PER_BLOCK, n, 1, SSM_GROUP_CH)
        t = t * eye[None, :, None, :, None]
        return t.reshape(nblk, GROUPS_PER_BLOCK * n, GROUPS_PER_BLOCK * SSM_GROUP_CH)

    w_in = jnp.concatenate([bdiag_in(w_re), bdiag_in(w_im)], axis=2).astype(BF16)
    w_out = jnp.concatenate([bdiag_out(c_re.astype(F32)), -bdiag_out(c_im.astype(F32))],
                            axis=1).astype(BF16)
    a_row = jnp.concatenate([ab_re.reshape(nblk, -1), ab_im.reshape(nblk, -1)], axis=1).reshape(1, -1)
    return a_row, w_in, w_out


def _s5_body(z_ref, a_ref, win_ref, wout_ref, d_ref, wglu_ref, bglu_ref, o_ref,
             zslab, zsb, xs, hstate, yslab):
    nb, ts, w = z_ref.shape
    rows = nb * ts
    nblk = w // LANES
    half = GROUPS_PER_BLOCK * SSM_STATE
    sw = 2 * half

    @pl.when(pl.program_id(1) == 0)
    def _():
        hstate[...] = jnp.zeros_like(hstate)

    for b in range(nb):
        for c in range(nblk):
            zslab[c, b * SLAB_PITCH:b * SLAB_PITCH + ts, :] = z_ref[b, :, c * LANES:(c + 1) * LANES]

    def to_pos_major(s, carry):
        r0 = pl.multiple_of(s * nb, nb)
        for c in range(nblk):
            zsb[pl.ds(r0, nb), c * LANES:(c + 1) * LANES] = zslab.at[c][pl.ds(s, nb, stride=SLAB_PITCH), :]
        return carry

    lax.fori_loop(0, ts, to_pos_major, 0)

    for j in range(nblk):
        xs[:, j * sw:(j + 1) * sw] = _dot(zsb[:, j * LANES:(j + 1) * LANES].astype(BF16), win_ref[j])

    for j in range(nblk):
        re_sl = slice(j * sw, j * sw + half)
        im_sl = slice(j * sw + half, (j + 1) * sw)
        a_r = jnp.broadcast_to(a_ref[:, re_sl], (nb, half))
        a_i = jnp.broadcast_to(a_ref[:, im_sl], (nb, half))

        def scan_chunk(k, carry):
            h_r, h_i = carry
            for q in range(SCAN_UNROLL):
                r0 = pl.multiple_of((k * SCAN_UNROLL + q) * nb, nb)
                x_r = xs[pl.ds(r0, nb), re_sl]
                x_i = xs[pl.ds(r0, nb), im_sl]
                n_r = a_r * h_r - a_i * h_i + x_r
                n_i = a_r * h_i + a_i * h_r + x_i
                xs[pl.ds(r0, nb), re_sl] = n_r
                xs[pl.ds(r0, nb), im_sl] = n_i
                h_r, h_i = n_r, n_i
            return h_r, h_i

        h_r, h_i = lax.fori_loop(0, ts // SCAN_UNROLL, scan_chunk, (hstate[:, re_sl], hstate[:, im_sl]))
        hstate[:, re_sl] = h_r
        hstate[:, im_sl] = h_i

    ys = []
    for j in range(nblk):
        y = _dot(xs[:, j * sw:(j + 1) * sw].astype(BF16), wout_ref[j])
        y = y + d_ref[:, j * LANES:(j + 1) * LANES] * zsb[:, j * LANES:(j + 1) * LANES]
        ys.append(_gelu(y))
    g = jnp.concatenate(ys, axis=1)
    g = g * jax.nn.sigmoid(_dot(g.astype(BF16), wglu_ref[...]) + bglu_ref[...])
    for c in range(nblk):
        yslab[c] = g[:, c * LANES:(c + 1) * LANES]

    for b in range(nb):
        for c in range(nblk):
            o_ref[b, :, c * LANES:(c + 1) * LANES] = yslab.at[c][pl.ds(b, ts, stride=nb), :].astype(BF16)


def _s5(za, a_row, w_in, w_out, d_skip, w_glu, b_glu):
    bsz, s, w = za.shape
    nb, ts = SCAN_BATCHES, SCAN_TS
    rows = nb * ts
    nblk = w // LANES
    nstate = a_row.shape[1]
    full = lambda a: pl.BlockSpec(a.shape, lambda b, i: (0,) * a.ndim)
    args = (a_row, w_in, w_out, d_skip.reshape(1, w).astype(F32), w_glu.astype(BF16), b_glu.reshape(1, w))
    return pl.pallas_call(
        _s5_body,
        grid=(bsz // nb, s // ts),
        in_specs=[pl.BlockSpec((nb, ts, w), lambda b, i: (b, i, 0))] + [full(a) for a in args],
        out_specs=pl.BlockSpec((nb, ts, w), lambda b, i: (b, i, 0)),
        out_shape=jax.ShapeDtypeStruct((bsz, s, w), BF16),
        scratch_shapes=[pltpu.VMEM((nblk, nb * SLAB_PITCH, LANES), F32),
                        pltpu.VMEM((rows, w), F32),
                        pltpu.VMEM((rows, nstate), F32),
                        pltpu.VMEM((nb, nstate), F32),
                        pltpu.VMEM((nblk, rows, LANES), F32)],
        compiler_params=_cparams(("parallel", "arbitrary")),
        name="s5",
    )(za, *args)


def _mix_body(x_ref, u_ref, v_ref, ya_ref, gates_ref, mod_ref, wp_ref, sb_ref, wba_ref, wbb_ref,
              wout_ref, g2_ref, wrh_ref, wrl_ref, br_ref, h1_ref, xp_ref, route_ref):
    tm, d = x_ref.shape[1], x_ref.shape[2]
    w = u_ref.shape[2]
    npair = w // LANES
    hd = LANES // 2

    t_idx = lax.broadcasted_iota(I32, (SGU_CHUNK, 2 * SGU_CHUNK), 0)
    s_idx = lax.broadcasted_iota(I32, (SGU_CHUNK, 2 * SGU_CHUNK), 1) & (SGU_CHUNK - 1)
    causal = s_idx <= t_idx
    lane = lax.broadcasted_iota(I32, (SGU_CHUNK, LANES), 1)
    first_head = lane < hd
    mixed_rows = []
    for c in range(tm // SGU_CHUNK):
        vc = v_ref[0, c * SGU_CHUNK:(c + 1) * SGU_CHUNK, :]
        blocks = []
        for j in range(npair):
            vb = vc[:, j * LANES:(j + 1) * LANES]
            zero = jnp.zeros_like(vb)
            rhs = jnp.concatenate([jnp.where(first_head, vb, zero), jnp.where(first_head, zero, vb)], axis=0)
            wj = jnp.where(causal, wp_ref[j], jnp.zeros_like(wp_ref[j]))
            blocks.append(_dot(wj, rhs))
        mixed_rows.append(jnp.concatenate(blocks, axis=1) + sb_ref[...])
    mixed = jnp.concatenate(mixed_rows, axis=0)
    yb = (u_ref[0].astype(F32) * mixed).astype(BF16)

    pa = _dot(ya_ref[0], wba_ref[...])
    pb = _dot(yb, wbb_ref[...])
    merged = gates_ref[0, :, :d].astype(F32) * pa + gates_ref[0, :, d:].astype(F32) * pb
    o = _dot(merged.astype(BF16), wout_ref[...])
    h1 = x_ref[0] + mod_ref[0, 2:3, :] * o
    h1_ref[0] = h1

    ms = jnp.mean(h1 * h1, axis=-1, keepdims=True)
    xn = h1 * lax.rsqrt(ms + RMS_EPS) * g2_ref[...]
    xn = xn * (1.0 + mod_ref[0, 4:5, :]) + mod_ref[0, 3:4, :]
    x_hi, x_lo = _split_bf16(xn)

    logits = _dot(x_hi, wrh_ref[...]) + _dot(x_hi, wrl_ref[...]) + _dot(x_lo, wrh_ref[...]) + br_ref[...]
    lt = logits.T
    best = lt[0:1, :]
    grp = jnp.zeros((1, tm), I32)
    for gi in range(1, N_GROUPS):
        better = lt[gi:gi + 1, :] > best
        grp = jnp.where(better, gi, grp)
        best = jnp.where(better, lt[gi:gi + 1, :], best)
    den = jnp.zeros((1, tm), F32)
    for gi in range(N_GROUPS):
        den = den + jnp.exp(lt[gi:gi + 1, :] - best)
    pg = 1.0 / den
    le = lt[SUBLANES:2 * SUBLANES, :]
    for gi in range(1, N_GROUPS):
        le = jnp.where(grp == gi, lt[SUBLANES * (gi + 1):SUBLANES * (gi + 2), :], le)
    eidx = lax.broadcasted_iota(I32, (EXPERTS_PER_GROUP, tm), 0).astype(F32)
    none = float(EXPERTS_PER_GROUP)
    v1 = jnp.max(le, axis=0, keepdims=True)
    i1 = jnp.min(jnp.where(le == v1, eidx, none), axis=0, keepdims=True)
    rest = jnp.where(eidx == i1, -jnp.inf, le)
    v2 = jnp.max(rest, axis=0, keepdims=True)
    i2 = jnp.min(jnp.where(rest == v2, eidx, none), axis=0, keepdims=True)
    ex = jnp.exp(v2 - v1)
    p1 = 1.0 / (1.0 + ex)
    wt1 = pg * p1
    wt2 = pg * (ex * p1)
    first_lo = i1 < i2
    lo = jnp.where(first_lo, i1, i2)
    hi = jnp.where(first_lo, i2, i1)
    pair = lo * (2 * EXPERTS_PER_GROUP - 1.0 - lo) * 0.5 + (hi - lo - 1.0)
    cls = grp.astype(F32) * PAIRS_PER_GROUP + pair
    info = jnp.concatenate([cls, jnp.where(first_lo, wt1, wt2), jnp.where(first_lo, wt2, wt1)], axis=0)
    route_ref[...] = jnp.concatenate([info, jnp.zeros((SUBLANES - 3, tm), F32)], axis=0)

    bits = lax.bitcast_convert_type(x_hi.astype(F32), I32)
    packed = lax.shift_right_logical(bits[:, :d // 2], 16) | (bits[:, d // 2:] & jnp.int32(-65536))
    info_cols = jnp.concatenate([info, jnp.zeros((LANES - 3, tm), F32)], axis=0).T
    xp_ref[0] = jnp.concatenate([packed, lax.bitcast_convert_type(info_cols, I32)], axis=1)


def _mix(x, u, v, ya, gates, mod, sgu_w, sgu_b, w_ba, w_bb, w_out, g2, w_rg, b_rg, w_re, b_re, tm):
    bsz, s, d = x.shape
    w = u.shape[2]
    nt = bsz * s
    wp = sgu_w.reshape(SGU_HEADS // 2, 2, SGU_CHUNK, SGU_CHUNK).transpose(0, 2, 1, 3)
    wp = wp.reshape(SGU_HEADS // 2, SGU_CHUNK, 2 * SGU_CHUNK).astype(BF16)
    sb = jnp.repeat(sgu_b.T, w // SGU_HEADS, axis=1).astype(F32)
    wr = jnp.zeros((d, LANES), F32)
    wr = wr.at[:, :N_GROUPS].set(w_rg)
    wr = wr.at[:, SUBLANES:SUBLANES + N_GROUPS * EXPERTS_PER_GROUP].set(
        w_re.transpose(1, 0, 2).reshape(d, N_GROUPS * EXPERTS_PER_GROUP))
    br = jnp.zeros((1, LANES), F32)
    br = br.at[0, :N_GROUPS].set(b_rg)
    br = br.at[0, SUBLANES:SUBLANES + N_GROUPS * EXPERTS_PER_GROUP].set(b_re.reshape(-1))
    wr_hi, wr_lo = _split_bf16(wr)
    tok = lambda n: pl.BlockSpec((1, tm, n), lambda b, i: (b, i, 0))
    full = lambda a: pl.BlockSpec(a.shape, lambda b, i: (0,) * a.ndim)
    args = (wp, sb, w_ba.astype(BF16), w_bb.astype(BF16), w_out.astype(BF16), g2.reshape(1, d),
            wr_hi, wr_lo, br)
    nsteps = s // tm
    return pl.pallas_call(
        _mix_body,
        grid=(bsz, nsteps),
        in_specs=[tok(d), tok(w), tok(w), tok(w), tok(2 * d),
                  pl.BlockSpec((1, 6, d), lambda b, i: (b, 0, 0))] + [full(a) for a in args],
        out_specs=[tok(d), tok(d // 2 + LANES),
                   pl.BlockSpec((SUBLANES, tm), lambda b, i: (0, b * nsteps + i))],
        out_shape=[jax.ShapeDtypeStruct((bsz, s, d), F32),
                   jax.ShapeDtypeStruct((bsz, s, d // 2 + LANES), I32),
                   jax.ShapeDtypeStruct((SUBLANES, nt), F32)],
        compiler_params=_cparams(("parallel", "parallel")),
        name="mix",
    )(x, u, v, ya, gates, mod, *args)


def _tile_lanes(a, reps):
    return jnp.concatenate([a] * reps, axis=1)


def _dispatch_body(route_ref, lo_ref, hi_ref, pos_ref, blk_ref, cls_ref, run, pstart, ranks):
    phase = pl.program_id(0)
    i = pl.program_id(1)
    n = pl.num_programs(1)
    tt = route_ref.shape[1]
    reps = tt // LANES
    cls = route_ref[0:1, :]
    cid = lax.broadcasted_iota(I32, (LANES, tt), 0).astype(F32)
    member = cls == cid

    @pl.when(jnp.logical_and(phase == 0, i == 0))
    def _():
        run[...] = jnp.zeros_like(run)

    @pl.when(phase == 0)
    def _():
        onehot = jnp.where(member, 1.0, 0.0)
        earlier = lax.broadcasted_iota(I32, (tt, tt), 0) < lax.broadcasted_iota(I32, (tt, tt), 1)
        before = _dot(onehot.astype(BF16), jnp.where(earlier, 1.0, 0.0).astype(BF16))
        ranks[i] = jnp.sum(onehot * (before + _tile_lanes(run[...], reps)), axis=0, keepdims=True)
        run[...] = run[...] + jnp.sum(onehot, axis=1, keepdims=True)

    @pl.when(jnp.logical_and(phase == 0, i == n - 1))
    def _():
        counts = run[...]
        nblk = jnp.floor((counts + (EXPERT_ROWS - 1.0)) * (1.0 / EXPERT_ROWS))
        hi_part = jnp.floor(nblk * (1.0 / 16.0))
        lo_part = nblk - 16.0 * hi_part
        upto = lax.broadcasted_iota(I32, (LANES, LANES), 1) <= lax.broadcasted_iota(I32, (LANES, LANES), 0)
        upto = jnp.where(upto, 1.0, 0.0).astype(BF16)
        ends = 16.0 * _dot(upto, hi_part.astype(BF16)) + _dot(upto, lo_part.astype(BF16))
        pstart[...] = (ends - nblk) * EXPERT_ROWS
        diag = lax.broadcasted_iota(I32, (LANES, LANES), 0) == lax.broadcasted_iota(I32, (LANES, LANES), 1)
        on_lanes = lambda a: jnp.sum(jnp.where(diag, a, 0.0), axis=0, keepdims=True)
        cls_ref[...] = jnp.concatenate([on_lanes(counts), on_lanes(pstart[...]),
                                        jnp.zeros((SUBLANES - 2, LANES), F32)], axis=0)
        nb_lanes = blk_ref.shape[1]
        breps = nb_lanes // LANES
        n_used = _tile_lanes(ends[LANES - 1:LANES, :], breps)
        b_idx = lax.broadcasted_iota(I32, (LANES, nb_lanes), 1).astype(F32)
        b_idx = jnp.minimum(b_idx, n_used - 1.0)
        b_cls = jnp.sum(jnp.where(_tile_lanes(ends, breps) <= b_idx, 1.0, 0.0), axis=0, keepdims=True)
        b_member = b_cls == lax.broadcasted_iota(I32, (LANES, nb_lanes), 0).astype(F32)
        pick = lambda tab: jnp.sum(jnp.where(b_member, _tile_lanes(tab, breps), 0.0), axis=0, keepdims=True)
        blk_ref[...] = jnp.concatenate([pick(lo_ref[...]), pick(hi_ref[...]), n_used,
                                        jnp.zeros((SUBLANES - 3, nb_lanes), F32)], axis=0)

    @pl.when(phase == 1)
    def _():
        base = jnp.sum(jnp.where(member, _tile_lanes(pstart[...], reps), 0.0), axis=0, keepdims=True)
        pos_ref[0] = (base + ranks[i]).astype(I32)


def _dispatch(route, n_blocks, tt=512):
    nt = route.shape[1]
    n = nt // tt
    nb_lanes = pl.cdiv(n_blocks, LANES) * LANES
    lo_tab, hi_tab = [], []
    for g in range(N_GROUPS):
        for a in range(EXPERTS_PER_GROUP):
            for b in range(a + 1, EXPERTS_PER_GROUP):
                lo_tab.append(g * EXPERTS_PER_GROUP + a)
                hi_tab.append(g * EXPERTS_PER_GROUP + b)
    pad = [0] * (LANES - N_CLASSES)
    lo_tile = jnp.broadcast_to(jnp.asarray(lo_tab + pad, F32)[:, None], (LANES, LANES))
    hi_tile = jnp.broadcast_to(jnp.asarray(hi_tab + pad, F32)[:, None], (LANES, LANES))
    const = lambda shape: pl.BlockSpec(shape, lambda p, i: (0, 0))
    pos, blk_tab, cls_tab = pl.pallas_call(
        _dispatch_body,
        grid=(2, n),
        in_specs=[pl.BlockSpec((SUBLANES, tt), lambda p, i: (0, i)),
                  const((LANES, LANES)), const((LANES, LANES))],
        out_specs=[pl.BlockSpec((1, 1, tt), lambda p, i: (p * i, 0, 0)),
                   const((SUBLANES, nb_lanes)), const((SUBLANES, LANES))],
        out_shape=[jax.ShapeDtypeStruct((n, 1, tt), I32),
                   jax.ShapeDtypeStruct((SUBLANES, nb_lanes), F32),
                   jax.ShapeDtypeStruct((SUBLANES, LANES), F32)],
        scratch_shapes=[pltpu.VMEM((LANES, LANES), F32), pltpu.VMEM((LANES, LANES), F32),
                        pltpu.VMEM((n, 1, tt), F32)],
        compiler_params=_cparams(("arbitrary", "arbitrary")),
        name="dispatch",
    )(route, lo_tile, hi_tile)
    blk = blk_tab.astype(I32)
    cls = cls_tab.astype(I32)
    return pos.reshape(nt), blk[0, :n_blocks], blk[1, :n_blocks], blk[2, :1], cls[0], cls[1]


ZERO_FILL_ROWS = (8, 16, 32, 64)


def _scatter_body(cnt_ref, pst_ref, pos_ref, xp_ref, xs_hbm, ring, zeros, sem, zsem):
    i = pl.program_id(0)
    n = pl.num_programs(0)
    tm = xp_ref.shape[0]
    slot = i % 2

    def rows_done(s):
        pltpu.make_async_copy(ring.at[s], xs_hbm.at[pl.ds(0, tm)], sem.at[s]).wait()

    @pl.when(i >= 2)
    def _():
        rows_done(slot)

    ring[slot] = xp_ref[...]
    for r in range(tm):
        pltpu.make_async_copy(ring.at[slot, pl.ds(r, 1)], xs_hbm.at[pl.ds(pos_ref[0, 0, r], 1)],
                              sem.at[slot]).start()

    @pl.when(i == n - 1)
    def _():
        if n > 1:
            rows_done(1 - slot)
        rows_done(slot)
        zeros[...] = jnp.zeros_like(zeros)

        def fill(wait):
            def per_class(k, carry):
                cnt = cnt_ref[k]
                n_pad = (-cnt) & (EXPERT_ROWS - 1)
                off = pst_ref[k] + cnt
                for r in range(SUBLANES - 1):
                    row = pltpu.make_async_copy(zeros.at[pl.ds(0, 1)], xs_hbm.at[pl.ds(off + r, 1)], zsem)

                    @pl.when(r < (n_pad & (SUBLANES - 1)))
                    def _():
                        row.wait() if wait else row.start()

                off = off + (n_pad & (SUBLANES - 1))
                for size in ZERO_FILL_ROWS:
                    piece = pltpu.make_async_copy(zeros.at[pl.ds(0, size)],
                                                  xs_hbm.at[pl.ds(pl.multiple_of(off, size), size)], zsem)

                    @pl.when((n_pad & size) != 0)
                    def _():
                        piece.wait() if wait else piece.start()

                    off = off + (n_pad & size)
                return carry

            lax.fori_loop(0, N_CLASSES, per_class, 0)

            last = N_CLASSES - 1
            used_rows = pst_ref[last] + cnt_ref[last] + ((-cnt_ref[last]) & (EXPERT_ROWS - 1))
            tail = ZERO_FILL_ROWS[-1]

            def per_piece(j, carry):
                piece = pltpu.make_async_copy(
                    zeros, xs_hbm.at[pl.ds(pl.multiple_of(used_rows + j * tail, tail), tail)], zsem)
                piece.wait() if wait else piece.start()
                return carry

            lax.fori_loop(0, (xs_hbm.shape[0] - used_rows) // tail, per_piece, 0)

        fill(False)
        fill(True)


def _scatter_rows(xp, pos, counts, pstarts, n_blocks, tm=256):
    nt, wrow = xp.shape
    n = nt // tm
    grid_spec = pltpu.PrefetchScalarGridSpec(
        num_scalar_prefetch=2,
        grid=(n,),
        in_specs=[pl.BlockSpec((1, 1, tm), lambda i, c, p: (i, 0, 0), memory_space=pltpu.SMEM),
                  pl.BlockSpec((tm, wrow), lambda i, c, p: (i, 0))],
        out_specs=pl.BlockSpec(memory_space=pl.ANY),
        scratch_shapes=[pltpu.VMEM((2, tm, wrow), I32), pltpu.VMEM((ZERO_FILL_ROWS[-1], wrow), I32),
                        pltpu.SemaphoreType.DMA((2,)), pltpu.SemaphoreType.DMA(())],
    )
    return pl.pallas_call(
        _scatter_body,
        grid_spec=grid_spec,
        out_shape=jax.ShapeDtypeStruct((n_blocks * EXPERT_ROWS, wrow), I32),
        compiler_params=_cparams(("arbitrary",)),
        name="scatter_rows",
    )(counts, pstarts, pos.reshape(n, 1, tm), xp)


def _gather_rows(idx_ref, n, src_hbm, dst, sem):
    for r in range(n):
        pltpu.make_async_copy(src_hbm.at[pl.ds(idx_ref[0, 0, r], 1)], dst.at[pl.ds(r, 1)], sem).start()


def _wait_rows(n, src_hbm, dst, sem):
    pltpu.make_async_copy(src_hbm.at[pl.ds(0, n)], dst, sem).wait()


def _unpack_rows(xp):
    left = lax.bitcast_convert_type(lax.shift_left(xp, 16), F32).astype(BF16)
    right = lax.bitcast_convert_type(xp & jnp.int32(-65536), F32).astype(BF16)
    return jnp.concatenate([left, right], axis=1)


def _experts_body(elo_ref, ehi_ref, nu_ref, xs_ref,
                  w1a_ref, w3a_ref, w2a_ref, w1b_ref, w3b_ref, w2b_ref, o_ref):
    i = pl.program_id(0)
    dh = xs_ref.shape[1] - LANES

    @pl.when(i < nu_ref[0])
    def _():
        x = _unpack_rows(xs_ref[:, :dh])
        info = lax.bitcast_convert_type(xs_ref[:, dh:], F32)

        def ffn(w1_ref, w3_ref, w2_ref):
            a = _dot(x, w1_ref[0])
            hid = (a * jax.nn.sigmoid(a)) * _dot(x, w3_ref[0])
            return _dot(hid.astype(BF16), w2_ref[0])

        o_ref[...] = (info[:, 1:2] * ffn(w1a_ref, w3a_ref, w2a_ref)
                      + info[:, 2:3] * ffn(w1b_ref, w3b_ref, w2b_ref))

    @pl.when(i >= nu_ref[0])
    def _():
        o_ref[...] = jnp.zeros_like(o_ref)


def _experts(xs, e_lo, e_hi, n_used, w1, w3, w2):
    n_rows, wrow = xs.shape
    n_blocks = n_rows // EXPERT_ROWS
    d, ff = w1.shape[1], w1.shape[2]
    wsel = lambda shape, use_hi: pl.BlockSpec(
        (1,) + shape, (lambda i, lo, hi, nu: (hi[i], 0, 0)) if use_hi else (lambda i, lo, hi, nu: (lo[i], 0, 0)))
    w1b, w3b, w2b = w1.astype(BF16), w3.astype(BF16), w2.astype(BF16)
    grid_spec = pltpu.PrefetchScalarGridSpec(
        num_scalar_prefetch=3,
        grid=(n_blocks,),
        in_specs=[pl.BlockSpec((EXPERT_ROWS, wrow), lambda i, lo, hi, nu: (jnp.minimum(i, nu[0] - 1), 0)),
                  wsel((d, ff), False), wsel((d, ff), False), wsel((ff, d), False),
                  wsel((d, ff), True), wsel((d, ff), True), wsel((ff, d), True)],
        out_specs=pl.BlockSpec((EXPERT_ROWS, d), lambda i, lo, hi, nu: (i, 0)),
    )
    return pl.pallas_call(
        _experts_body,
        grid_spec=grid_spec,
        out_shape=jax.ShapeDtypeStruct((n_rows, d), F32),
        compiler_params=_cparams(("arbitrary",)),
        name="experts",
    )(e_lo, e_hi, n_used, xs, w1b, w3b, w2b, w1b, w3b, w2b)


def _final_body(posc_ref, posn_ref, y_hbm, h1_ref, mod_ref, gf_ref, o_ref, ybuf, sem):
    i = pl.program_id(0)
    n = pl.num_programs(0)
    tm = h1_ref.shape[0]
    slot = i % 2

    @pl.when(i == 0)
    def _():
        _gather_rows(posc_ref, tm, y_hbm, ybuf.at[0], sem.at[0])

    @pl.when(i + 1 < n)
    def _():
        _gather_rows(posn_ref, tm, y_hbm, ybuf.at[1 - slot], sem.at[1 - slot])

    _wait_rows(tm, y_hbm, ybuf.at[slot], sem.at[slot])
    h = h1_ref[...] + mod_ref[0, 5:6, :] * ybuf[slot]
    ms = jnp.mean(h * h, axis=-1, keepdims=True)
    o_ref[...] = h * lax.rsqrt(ms + RMS_EPS) * gf_ref[...]


def _final(pos, y_sorted, h1, mod, gf, seq, tm):
    nt, d = h1.shape
    n = nt // tm
    pos3 = pos.reshape(n, 1, tm)
    idx_spec = lambda f: pl.BlockSpec((1, 1, tm), f, memory_space=pltpu.SMEM)
    return pl.pallas_call(
        _final_body,
        grid=(n,),
        in_specs=[idx_spec(lambda i: (i, 0, 0)),
                  idx_spec(lambda i: (jnp.minimum(i + 1, n - 1), 0, 0)),
                  pl.BlockSpec(memory_space=pl.ANY),
                  pl.BlockSpec((tm, d), lambda i: (i, 0)),
                  pl.BlockSpec((1, 6, d), lambda i: (i * tm // seq, 0, 0)),
                  pl.BlockSpec((1, d), lambda i: (0, 0))],
        out_specs=pl.BlockSpec((tm, d), lambda i: (i, 0)),
        out_shape=jax.ShapeDtypeStruct((nt, d), F32),
        scratch_shapes=[pltpu.VMEM((2, tm, d), F32), pltpu.SemaphoreType.DMA((2,))],
        compiler_params=_cparams(("arbitrary",)),
        name="final",
    )(pos3, pos3, y_sorted, h1, mod, gf.reshape(1, d))


def kernel(x, c, w_ada, b_ada, norm1_g, w_in, w_gate, b_gate, ssm_a_re, ssm_a_im, ssm_b_re, ssm_b_im, ssm_c_re, ssm_c_im, ssm_d, ssm_log_step, w_glu, b_glu, sgu_ln_g, sgu_ln_b, sgu_w, sgu_b, w_branch_a, w_branch_b, w_out, norm2_g, w_router_group, b_router_group, w_router_expert, b_router_expert, w1, w3, w2, norm_f_g):
    bsz, seq, d = x.shape
    depth = w_ada.shape[0]
    assert depth == 1 and bsz % SCAN_BATCHES == 0 and seq % 512 == 0
    l = 0
    mod = _adaln(c, w_ada[l], b_ada[l]).reshape(bsz, 6, d)
    za, u, v, gates = _inproj(x, mod, norm1_g[l], w_in[l], w_gate[l], b_gate[l],
                              sgu_ln_g[l], sgu_ln_b[l], tm=512)
    a_row, s5_in, s5_out = _s5_params(ssm_a_re[l], ssm_a_im[l], ssm_b_re[l], ssm_b_im[l],
                                      ssm_c_re[l], ssm_c_im[l], ssm_log_step[l])
    ya = _s5(za, a_row, s5_in, s5_out, ssm_d[l], w_glu[l], b_glu[l])
    h1, xp, route = _mix(x, u, v, ya, gates, mod, sgu_w[l], sgu_b[l], w_branch_a[l], w_branch_b[l],
                         w_out[l], norm2_g[l], w_router_group[l], b_router_group[l],
                         w_router_expert[l], b_router_expert[l], tm=256)
    nt = bsz * seq
    n_blocks = nt // EXPERT_ROWS + N_CLASSES
    pos, e_lo, e_hi, n_used, counts, pstarts = _dispatch(route, n_blocks)
    xs = _scatter_rows(xp.reshape(nt, xp.shape[-1]), pos, counts, pstarts, n_blocks)
    y_sorted = _experts(xs, e_lo, e_hi, n_used, w1[l], w3[l], w2[l])
    out = _final(pos, y_sorted, h1.reshape(nt, d), mod, norm_f_g, seq, tm=256)
    return out.reshape(bsz, seq, d)
```

```python
import functools
import math

import jax
import jax.numpy as jnp
from jax import lax
from jax.experimental import pallas as pl
from jax.experimental.pallas import tpu as pltpu

F32 = jnp.float32
BF16 = jnp.bfloat16
I32 = jnp.int32

LANES = 128
SUBLANES = 8
VMEM_LIMIT = 56 * 1024 * 1024

RMS_EPS = 1e-6
LN_EPS = 1e-5

SSM_GROUP_CH = 16
SSM_STATE = 64
GROUPS_PER_BLOCK = LANES // SSM_GROUP_CH
SGU_HEADS = 8
SGU_CHUNK = 128
N_GROUPS = 4
EXPERTS_PER_GROUP = 8
PAIRS_PER_GROUP = EXPERTS_PER_GROUP * (EXPERTS_PER_GROUP - 1) // 2
N_CLASSES = N_GROUPS * PAIRS_PER_GROUP
EXPERT_ROWS = 128

SCAN_BATCHES = SUBLANES
SCAN_TS = 128
SLAB_PITCH = SCAN_TS + 8
SCAN_UNROLL = 8


def _gelu(x):
    return 0.5 * x * (1.0 + jnp.tanh(math.sqrt(2.0 / math.pi) * (x + 0.044715 * (x * x * x))))


def _dot(a, b):
    return jnp.dot(a, b, preferred_element_type=F32)


def _split_bf16(a):
    hi = a.astype(BF16)
    lo = (a - hi.astype(F32)).astype(BF16)
    return hi, lo


def _cparams(sem):
    return pltpu.CompilerParams(dimension_semantics=sem, vmem_limit_bytes=VMEM_LIMIT)


def _adaln_body(c_ref, w_ref, b_ref, o_ref):
    c = c_ref[...]
    act = c * jax.nn.sigmoid(c)
    a_hi, a_lo = _split_bf16(act)
    w_hi, w_lo = _split_bf16(w_ref[...])
    o_ref[...] = _dot(a_hi, w_hi) + _dot(a_hi, w_lo) + _dot(a_lo, w_hi) + b_ref[...]


def _adaln(c, w, b):
    bsz, d = c.shape
    n = w.shape[1]
    tn = 1024
    return pl.pallas_call(
        _adaln_body,
        grid=(n // tn,),
        in_specs=[pl.BlockSpec((bsz, d), lambda j: (0, 0)),
                  pl.BlockSpec((d, tn), lambda j: (0, j)),
                  pl.BlockSpec((1, tn), lambda j: (0, j))],
        out_specs=pl.BlockSpec((bsz, tn), lambda j: (0, j)),
        out_shape=jax.ShapeDtypeStruct((bsz, n), F32),
        compiler_params=_cparams(("arbitrary",)),
        name="adaln",
    )(c, w, b.reshape(1, n))


def _inproj_body(x_ref, mod_ref, g1_ref, win_ref, wgate_ref, bgate_ref, lng_ref, lnb_ref,
                 za_ref, u_ref, v_ref, gates_ref):
    x = x_ref[0]
    ms = jnp.mean(x * x, axis=-1, keepdims=True)
    xn = x * lax.rsqrt(ms + RMS_EPS) * g1_ref[...]
    xn = xn * (1.0 + mod_ref[0, 1:2, :]) + mod_ref[0, 0:1, :]
    xb = xn.astype(BF16)
    proj = _dot(xb, win_ref[...])
    w = za_ref.shape[-1]
    za_ref[0] = proj[:, :w]
    u_ref[0] = _gelu(proj[:, w:2 * w]).astype(BF16)
    gv = _gelu(proj[:, 2 * w:])
    mu = jnp.mean(gv, axis=-1, keepdims=True)
    cen = gv - mu
    var = jnp.mean(cen * cen, axis=-1, keepdims=True)
    v_ref[0] = (cen * lax.rsqrt(var + LN_EPS) * lng_ref[...] + lnb_ref[...]).astype(BF16)
    gates_ref[0] = jax.nn.sigmoid(_dot(xb, wgate_ref[...]) + bgate_ref[...]).astype(BF16)


def _inproj(x, mod, g1, w_in, w_gate, b_gate, ln_g, ln_b, tm):
    bsz, s, d = x.shape
    w = w_in.shape[1] // 3
    ng = w_gate.shape[1]
    tok = lambda n: pl.BlockSpec((1, tm, n), lambda b, i: (b, i, 0))
    full = lambda a: pl.BlockSpec(a.shape, lambda b, i: (0,) * a.ndim)
    args = (g1.reshape(1, d), w_in.astype(BF16), w_gate.astype(BF16), b_gate.reshape(1, ng),
            ln_g.reshape(1, w), ln_b.reshape(1, w))
    return pl.pallas_call(
        _inproj_body,
        grid=(bsz, s // tm),
        in_specs=[tok(d), pl.BlockSpec((1, 6, d), lambda b, i: (b, 0, 0))] + [full(a) for a in args],
        out_specs=[tok(w), tok(w), tok(w), tok(ng)],
        out_shape=[jax.ShapeDtypeStruct((bsz, s, w), F32),
                   jax.ShapeDtypeStruct((bsz, s, w), BF16),
                   jax.ShapeDtypeStruct((bsz, s, w), BF16),
                   jax.ShapeDtypeStruct((bsz, s, ng), BF16)],
        compiler_params=_cparams(("parallel", "parallel")),
        name="in_proj",
    )(x, mod, *args)


def _s5_params(a_re, a_im, b_re, b_im, c_re, c_im, log_step):
    g, n = a_re.shape
    nblk = g // GROUPS_PER_BLOCK
    lam_re = jnp.minimum(a_re.astype(F32), -1e-4)
    lam_im = a_im.astype(F32)
    dt = jnp.exp(log_step.astype(F32))[:, None]
    mag = jnp.exp(lam_re * dt)
    ab_re = mag * jnp.cos(lam_im * dt)
    ab_im = mag * jnp.sin(lam_im * dt)
    den = lam_re * lam_re + lam_im * lam_im
    nr = ab_re - 1.0
    f_re = (nr * lam_re + ab_im * lam_im) / den
    f_im = (ab_im * lam_re - nr * lam_im) / den
    bt_re = b_re.astype(F32).transpose(0, 2, 1)
    bt_im = b_im.astype(F32).transpose(0, 2, 1)
    w_re = f_re[:, None, :] * bt_re - f_im[:, None, :] * bt_im
    w_im = f_re[:, None, :] * bt_im + f_im[:, None, :] * bt_re
    eye = jnp.eye(GROUPS_PER_BLOCK, dtype=F32)

    def bdiag_in(wt):
        t = wt.reshape(nblk, GROUPS_PER_BLOCK, SSM_GROUP_CH, 1, n) * eye[None, :, None, :, None]
        return t.reshape(nblk, GROUPS_PER_BLOCK * SSM_GROUP_CH, GROUPS_PER_BLOCK * n)

    def bdiag_out(ct):
        t = ct.transpose(0, 2, 1).reshape(nblk, GROUPS_PER_BLOCK, n, 1, SSM_GROUP_CH)
        t = t * eye[None, :, None, :, None]
        return t.reshape(nblk, GROUPS_PER_BLOCK * n, GROUPS_PER_BLOCK * SSM_GROUP_CH)

    w_in = jnp.concatenate([bdiag_in(w_re), bdiag_in(w_im)], axis=2).astype(BF16)
    w_out = jnp.concatenate([bdiag_out(c_re.astype(F32)), -bdiag_out(c_im.astype(F32))],
                            axis=1).astype(BF16)
    a_row = jnp.concatenate([ab_re.reshape(nblk, -1), ab_im.reshape(nblk, -1)], axis=1).reshape(1, -1)
    return a_row, w_in, w_out


def _s5_body(z_ref, a_ref, win_ref, wout_ref, d_ref, wglu_ref, bglu_ref, o_ref,
             zslab, zsb, xs, hstate, yslab):
    nb, ts, w = z_ref.shape
    rows = nb * ts
    nblk = w // LANES
    half = GROUPS_PER_BLOCK * SSM_STATE
    sw = 2 * half

    @pl.when(pl.program_id(1) == 0)
    def _():
        hstate[...] = jnp.zeros_like(hstate)

    for b in range(nb):
        for c in range(nblk):
            zslab[c, b * SLAB_PITCH:b * SLAB_PITCH + ts, :] = z_ref[b, :, c * LANES:(c + 1) * LANES]

    for s in range(ts):
        for c in range(nblk):
            zsb[s * nb:(s + 1) * nb, c * LANES:(c + 1) * LANES] = zslab.at[c][pl.ds(s, nb, stride=SLAB_PITCH), :]

    def in_map(j):
        xs[:, j * sw:(j + 1) * sw] = _dot(zsb[:, j * LANES:(j + 1) * LANES].astype(BF16), win_ref[j])

    def scan(j):
        re_sl = slice(j * sw, j * sw + half)
        im_sl = slice(j * sw + half, (j + 1) * sw)
        a_r = jnp.broadcast_to(a_ref[:, re_sl], (nb, half))
        a_i = jnp.broadcast_to(a_ref[:, im_sl], (nb, half))
        h_r, h_i = hstate[:, re_sl], hstate[:, im_sl]
        for s in range(ts):
            x_r = xs[s * nb:(s + 1) * nb, re_sl]
            x_i = xs[s * nb:(s + 1) * nb, im_sl]
            h_r, h_i = a_r * h_r - a_i * h_i + x_r, a_r * h_i + a_i * h_r + x_i
            xs[s * nb:(s + 1) * nb, re_sl] = h_r
            xs[s * nb:(s + 1) * nb, im_sl] = h_i
        hstate[:, re_sl] = h_r
        hstate[:, im_sl] = h_i

    def out_map(j):
        y = _dot(xs[:, j * sw:(j + 1) * sw].astype(BF16), wout_ref[j])
        y = y + d_ref[:, j * LANES:(j + 1) * LANES] * zsb[:, j * LANES:(j + 1) * LANES]
        return _gelu(y)

    ys = []
    for step in range(nblk + 2):
        if step < nblk:
            in_map(step)
        if 0 <= step - 1 < nblk:
            scan(step - 1)
        if 0 <= step - 2 < nblk:
            ys.append(out_map(step - 2))
    g = jnp.concatenate(ys, axis=1)
    g = g * jax.nn.sigmoid(_dot(g.astype(BF16), wglu_ref[...]) + bglu_ref[...])
    for c in range(nblk):
        yslab[c] = g[:, c * LANES:(c + 1) * LANES]

    for b in range(nb):
        for c in range(nblk):
            o_ref[b, :, c * LANES:(c + 1) * LANES] = yslab.at[c][pl.ds(b, ts, stride=nb), :].astype(BF16)


def _s5(za, a_row, w_in, w_out, d_skip, w_glu, b_glu):
    bsz, s, w = za.shape
    nb, ts = SCAN_BATCHES, SCAN_TS
    rows = nb * ts
    nblk = w // LANES
    nstate = a_row.shape[1]
    full = lambda a: pl.BlockSpec(a.shape, lambda b, i: (0,) * a.ndim)
    args = (a_row, w_in, w_out, d_skip.reshape(1, w).astype(F32), w_glu.astype(BF16), b_glu.reshape(1, w))
    return pl.pallas_call(
        _s5_body,
        grid=(bsz // nb, s // ts),
        in_specs=[pl.BlockSpec((nb, ts, w), lambda b, i: (b, i, 0))] + [full(a) for a in args],
        out_specs=pl.BlockSpec((nb, ts, w), lambda b, i: (b, i, 0)),
        out_shape=jax.ShapeDtypeStruct((bsz, s, w), BF16),
        scratch_shapes=[pltpu.VMEM((nblk, nb * SLAB_PITCH, LANES), F32),
                        pltpu.VMEM((rows, w), F32),
                        pltpu.VMEM((rows, nstate), F32),
                        pltpu.VMEM((nb, nstate), F32),
                        pltpu.VMEM((nblk, rows, LANES), F32)],
        compiler_params=_cparams(("parallel", "arbitrary")),
        name="s5",
    )(za, *args)


def _mix_body(x_ref, u_ref, v_ref, ya_ref, gates_ref, mod_ref, wp_ref, sb_ref, wba_ref, wbb_ref,
              wout_ref, g2_ref, wr_ref, br_ref, h1_ref, xp_ref, route_ref):
    tm, d = x_ref.shape[1], x_ref.shape[2]
    w = u_ref.shape[2]
    npair = w // LANES
    hd = LANES // 2

    t_idx = lax.broadcasted_iota(I32, (SGU_CHUNK, 2 * SGU_CHUNK), 0)
    s_idx = lax.broadcasted_iota(I32, (SGU_CHUNK, 2 * SGU_CHUNK), 1) & (SGU_CHUNK - 1)
    causal = s_idx <= t_idx
    lane = lax.broadcasted_iota(I32, (SGU_CHUNK, LANES), 1)
    first_head = lane < hd
    mixed_rows = []
    for c in range(tm // SGU_CHUNK):
        vc = v_ref[0, c * SGU_CHUNK:(c + 1) * SGU_CHUNK, :]
        blocks = []
        for j in range(npair):
            vb = vc[:, j * LANES:(j + 1) * LANES]
            zero = jnp.zeros_like(vb)
            rhs = jnp.concatenate([jnp.where(first_head, vb, zero), jnp.where(first_head, zero, vb)], axis=0)
            wj = jnp.where(causal, wp_ref[j], jnp.zeros_like(wp_ref[j]))
            blocks.append(_dot(wj, rhs))
        mixed_rows.append(jnp.concatenate(blocks, axis=1) + sb_ref[...])
    mixed = jnp.concatenate(mixed_rows, axis=0)
    yb = (u_ref[0].astype(F32) * mixed).astype(BF16)

    pa = _dot(ya_ref[0], wba_ref[...])
    pb = _dot(yb, wbb_ref[...])
    merged = gates_ref[0, :, :d].astype(F32) * pa + gates_ref[0, :, d:].astype(F32) * pb
    o = _dot(merged.astype(BF16), wout_ref[...])
    h1 = x_ref[0] + mod_ref[0, 2:3, :] * o
    h1_ref[0] = h1

    ms = jnp.mean(h1 * h1, axis=-1, keepdims=True)
    xn = h1 * lax.rsqrt(ms + RMS_EPS) * g2_ref[...]
    xn = xn * (1.0 + mod_ref[0, 4:5, :]) + mod_ref[0, 3:4, :]
    x_hi = xn.astype(BF16)

    logits = _dot(x_hi, wr_ref[...]) + br_ref[...]
    lt = logits.T
    best = lt[0:1, :]
    grp = jnp.zeros((1, tm), I32)
    for gi in range(1, N_GROUPS):
        better = lt[gi:gi + 1, :] > best
        grp = jnp.where(better, gi, grp)
        best = jnp.where(better, lt[gi:gi + 1, :], best)
    den = jnp.zeros((1, tm), F32)
    for gi in range(N_GROUPS):
        den = den + jnp.exp(lt[gi:gi + 1, :] - best)
    pg = 1.0 / den
    le = lt[SUBLANES:2 * SUBLANES, :]
    for gi in range(1, N_GROUPS):
        le = jnp.where(grp == gi, lt[SUBLANES * (gi + 1):SUBLANES * (gi + 2), :], le)
    eidx = lax.broadcasted_iota(I32, (EXPERTS_PER_GROUP, tm), 0).astype(F32)
    none = float(EXPERTS_PER_GROUP)
    v1 = jnp.max(le, axis=0, keepdims=True)
    i1 = jnp.min(jnp.where(le == v1, eidx, none), axis=0, keepdims=True)
    rest = jnp.where(eidx == i1, -jnp.inf, le)
    v2 = jnp.max(rest, axis=0, keepdims=True)
    i2 = jnp.min(jnp.where(rest == v2, eidx, none), axis=0, keepdims=True)
    ex = jnp.exp(v2 - v1)
    p1 = 1.0 / (1.0 + ex)
    wt1 = pg * p1
    wt2 = pg * (ex * p1)
    first_lo = i1 < i2
    lo = jnp.where(first_lo, i1, i2)
    hi = jnp.where(first_lo, i2, i1)
    pair = lo * (2 * EXPERTS_PER_GROUP - 1.0 - lo) * 0.5 + (hi - lo - 1.0)
    cls = grp.astype(F32) * PAIRS_PER_GROUP + pair
    info = jnp.concatenate([cls, jnp.where(first_lo, wt1, wt2), jnp.where(first_lo, wt2, wt1)], axis=0)
    route_ref[...] = jnp.concatenate([info, jnp.zeros((SUBLANES - 3, tm), F32)], axis=0)

    bits = lax.bitcast_convert_type(x_hi.astype(F32), I32)
    packed = lax.shift_right_logical(bits[:, :d // 2], 16) | (bits[:, d // 2:] & jnp.int32(-65536))
    info_cols = jnp.concatenate([info, jnp.zeros((LANES - 3, tm), F32)], axis=0).T
    xp_ref[0] = jnp.concatenate([packed, lax.bitcast_convert_type(info_cols, I32)], axis=1)


def _mix(x, u, v, ya, gates, mod, sgu_w, sgu_b, w_ba, w_bb, w_out, g2, w_rg, b_rg, w_re, b_re, tm):
    bsz, s, d = x.shape
    w = u.shape[2]
    nt = bsz * s
    wp = sgu_w.reshape(SGU_HEADS // 2, 2, SGU_CHUNK, SGU_CHUNK).transpose(0, 2, 1, 3)
    wp = wp.reshape(SGU_HEADS // 2, SGU_CHUNK, 2 * SGU_CHUNK).astype(BF16)
    sb = jnp.repeat(sgu_b.T, w // SGU_HEADS, axis=1).astype(F32)
    wr = jnp.zeros((d, LANES), F32)
    wr = wr.at[:, :N_GROUPS].set(w_rg)
    wr = wr.at[:, SUBLANES:SUBLANES + N_GROUPS * EXPERTS_PER_GROUP].set(
        w_re.transpose(1, 0, 2).reshape(d, N_GROUPS * EXPERTS_PER_GROUP))
    br = jnp.zeros((1, LANES), F32)
    br = br.at[0, :N_GROUPS].set(b_rg)
    br = br.at[0, SUBLANES:SUBLANES + N_GROUPS * EXPERTS_PER_GROUP].set(b_re.reshape(-1))
    tok = lambda n: pl.BlockSpec((1, tm, n), lambda b, i: (b, i, 0))
    full = lambda a: pl.BlockSpec(a.shape, lambda b, i: (0,) * a.ndim)
    args = (wp, sb, w_ba.astype(BF16), w_bb.astype(BF16), w_out.astype(BF16), g2.reshape(1, d),
            wr.astype(BF16), br)
    nsteps = s // tm
    return pl.pallas_call(
        _mix_body,
        grid=(bsz, nsteps),
        in_specs=[tok(d), tok(w), tok(w), tok(w), tok(2 * d),
                  pl.BlockSpec((1, 6, d), lambda b, i: (b, 0, 0))] + [full(a) for a in args],
        out_specs=[tok(d), tok(d // 2 + LANES),
                   pl.BlockSpec((SUBLANES, tm), lambda b, i: (0, b * nsteps + i))],
        out_shape=[jax.ShapeDtypeStruct((bsz, s, d), F32),
                   jax.ShapeDtypeStruct((bsz, s, d // 2 + LANES), I32),
                   jax.ShapeDtypeStruct((SUBLANES, nt), F32)],
        compiler_params=_cparams(("parallel", "parallel")),
        name="mix",
    )(x, u, v, ya, gates, mod, *args)


def _tile_lanes(a, reps):
    return jnp.concatenate([a] * reps, axis=1)


def _dispatch_body(route_ref, lo_ref, hi_ref, pos_ref, blk_ref, cls_ref, run, pstart, ranks):
    phase = pl.program_id(0)
    i = pl.program_id(1)
    n = pl.num_programs(1)
    tt = route_ref.shape[1]
    reps = tt // LANES
    cls = route_ref[0:1, :]
    cid = lax.broadcasted_iota(I32, (LANES, tt), 0).astype(F32)
    member = cls == cid

    @pl.when(jnp.logical_and(phase == 0, i == 0))
    def _():
        run[...] = jnp.zeros_like(run)

    @pl.when(phase == 0)
    def _():
        onehot = jnp.where(member, 1.0, 0.0)
        earlier = lax.broadcasted_iota(I32, (tt, tt), 0) < lax.broadcasted_iota(I32, (tt, tt), 1)
        before = _dot(onehot.astype(BF16), jnp.where(earlier, 1.0, 0.0).astype(BF16))
        ranks[i] = jnp.sum(onehot * (before + _tile_lanes(run[...], reps)), axis=0, keepdims=True)
        run[...] = run[...] + jnp.sum(onehot, axis=1, keepdims=True)

    @pl.when(jnp.logical_and(phase == 0, i == n - 1))
    def _():
        counts = run[...]
        nblk = jnp.floor((counts + (EXPERT_ROWS - 1.0)) * (1.0 / EXPERT_ROWS))
        hi_part = jnp.floor(nblk * (1.0 / 16.0))
        lo_part = nblk - 16.0 * hi_part
        upto = lax.broadcasted_iota(I32, (LANES, LANES), 1) <= lax.broadcasted_iota(I32, (LANES, LANES), 0)
        upto = jnp.where(upto, 1.0, 0.0).astype(BF16)
        ends = 16.0 * _dot(upto, hi_part.astype(BF16)) + _dot(upto, lo_part.astype(BF16))
        pstart[...] = (ends - nblk) * EXPERT_ROWS
        diag = lax.broadcasted_iota(I32, (LANES, LANES), 0) == lax.broadcasted_iota(I32, (LANES, LANES), 1)
        on_lanes = lambda a: jnp.sum(jnp.where(diag, a, 0.0), axis=0, keepdims=True)
        cls_ref[...] = jnp.concatenate([on_lanes(counts), on_lanes(pstart[...]),
                                        jnp.zeros((SUBLANES - 2, LANES), F32)], axis=0)
        nb_lanes = blk_ref.shape[1]
        breps = nb_lanes // LANES
        n_used = _tile_lanes(ends[LANES - 1:LANES, :], breps)
        b_idx = lax.broadcasted_iota(I32, (LANES, nb_lanes), 1).astype(F32)
        b_idx = jnp.minimum(b_idx, n_used - 1.0)
        b_cls = jnp.sum(jnp.where(_tile_lanes(ends, breps) <= b_idx, 1.0, 0.0), axis=0, keepdims=True)
        b_member = b_cls == lax.broadcasted_iota(I32, (LANES, nb_lanes), 0).astype(F32)
        pick = lambda tab: jnp.sum(jnp.where(b_member, _tile_lanes(tab, breps), 0.0), axis=0, keepdims=True)
        blk_ref[...] = jnp.concatenate([pick(lo_ref[...]), pick(hi_ref[...]), n_used,
                                        jnp.zeros((SUBLANES - 3, nb_lanes), F32)], axis=0)

    @pl.when(phase == 1)
    def _():
        base = jnp.sum(jnp.where(member, _tile_lanes(pstart[...], reps), 0.0), axis=0, keepdims=True)
        pos_ref[0] = (base + ranks[i]).astype(I32)


def _dispatch(route, n_blocks, tt=512):
    nt = route.shape[1]
    n = nt // tt
    nb_lanes = pl.cdiv(n_blocks, LANES) * LANES
    lo_tab, hi_tab = [], []
    for g in range(N_GROUPS):
        for a in range(EXPERTS_PER_GROUP):
            for b in range(a + 1, EXPERTS_PER_GROUP):
                lo_tab.append(g * EXPERTS_PER_GROUP + a)
                hi_tab.append(g * EXPERTS_PER_GROUP + b)
    pad = [0] * (LANES - N_CLASSES)
    lo_tile = jnp.broadcast_to(jnp.asarray(lo_tab + pad, F32)[:, None], (LANES, LANES))
    hi_tile = jnp.broadcast_to(jnp.asarray(hi_tab + pad, F32)[:, None], (LANES, LANES))
    const = lambda shape: pl.BlockSpec(shape, lambda p, i: (0, 0))
    pos, blk_tab, cls_tab = pl.pallas_call(
        _dispatch_body,
        grid=(2, n),
        in_specs=[pl.BlockSpec((SUBLANES, tt), lambda p, i: (0, i)),
                  const((LANES, LANES)), const((LANES, LANES))],
        out_specs=[pl.BlockSpec((1, 1, tt), lambda p, i: (p * i, 0, 0)),
                   const((SUBLANES, nb_lanes)), const((SUBLANES, LANES))],
        out_shape=[jax.ShapeDtypeStruct((n, 1, tt), I32),
                   jax.ShapeDtypeStruct((SUBLANES, nb_lanes), F32),
                   jax.ShapeDtypeStruct((SUBLANES, LANES), F32)],
        scratch_shapes=[pltpu.VMEM((LANES, LANES), F32), pltpu.VMEM((LANES, LANES), F32),
                        pltpu.VMEM((n, 1, tt), F32)],
        compiler_params=_cparams(("arbitrary", "arbitrary")),
        name="dispatch",
    )(route, lo_tile, hi_tile)
    blk = blk_tab.astype(I32)
    cls = cls_tab.astype(I32)
    return pos.reshape(nt), blk[0, :n_blocks], blk[1, :n_blocks], blk[2, :1], cls[0], cls[1]


ZERO_FILL_ROWS = (8, 16, 32, 64)


def _scatter_body(cnt_ref, pst_ref, pos_ref, xp_ref, xs_hbm, ring, zeros, sem, zsem):
    i = pl.program_id(0)
    n = pl.num_programs(0)
    tm = xp_ref.shape[0]
    slot = i % 2

    def rows_done(s):
        pltpu.make_async_copy(ring.at[s], xs_hbm.at[pl.ds(0, tm)], sem.at[s]).wait()

    @pl.when(i >= 2)
    def _():
        rows_done(slot)

    ring[slot] = xp_ref[...]
    for r in range(tm):
        pltpu.make_async_copy(ring.at[slot, pl.ds(r, 1)], xs_hbm.at[pl.ds(pos_ref[0, 0, r], 1)],
                              sem.at[slot]).start()

    @pl.when(i == n - 1)
    def _():
        if n > 1:
            rows_done(1 - slot)
        rows_done(slot)
        zeros[...] = jnp.zeros_like(zeros)

        def fill(wait):
            def per_class(k, carry):
                cnt = cnt_ref[k]
                n_pad = (-cnt) & (EXPERT_ROWS - 1)
                off = pst_ref[k] + cnt
                for r in range(SUBLANES - 1):
                    row = pltpu.make_async_copy(zeros.at[pl.ds(0, 1)], xs_hbm.at[pl.ds(off + r, 1)], zsem)

                    @pl.when(r < (n_pad & (SUBLANES - 1)))
                    def _():
                        row.wait() if wait else row.start()

                off = off + (n_pad & (SUBLANES - 1))
                for size in ZERO_FILL_ROWS:
                    piece = pltpu.make_async_copy(zeros.at[pl.ds(0, size)],
                                                  xs_hbm.at[pl.ds(pl.multiple_of(off, size), size)], zsem)

                    @pl.when((n_pad & size) != 0)
                    def _():
                        piece.wait() if wait else piece.start()

                    off = off + (n_pad & size)
                return carry

            lax.fori_loop(0, N_CLASSES, per_class, 0)

            last = N_CLASSES - 1
            used_rows = pst_ref[last] + cnt_ref[last] + ((-cnt_ref[last]) & (EXPERT_ROWS - 1))
            tail = ZERO_FILL_ROWS[-1]

            def per_piece(j, carry):
                piece = pltpu.make_async_copy(
                    zeros, xs_hbm.at[pl.ds(pl.multiple_of(used_rows + j * tail, tail), tail)], zsem)
                piece.wait() if wait else piece.start()
                return carry

            lax.fori_loop(0, (xs_hbm.shape[0] - used_rows) // tail, per_piece, 0)

        fill(False)
        fill(True)


def _scatter_rows(xp, pos, counts, pstarts, n_blocks, tm=256):
    nt, wrow = xp.shape
    n = nt // tm
    grid_spec = pltpu.PrefetchScalarGridSpec(
        num_scalar_prefetch=2,
        grid=(n,),
        in_specs=[pl.BlockSpec((1, 1, tm), lambda i, c, p: (i, 0, 0), memory_space=pltpu.SMEM),
                  pl.BlockSpec((tm, wrow), lambda i, c, p: (i, 0))],
        out_specs=pl.BlockSpec(memory_space=pl.ANY),
        scratch_shapes=[pltpu.VMEM((2, tm, wrow), I32), pltpu.VMEM((ZERO_FILL_ROWS[-1], wrow), I32),
                        pltpu.SemaphoreType.DMA((2,)), pltpu.SemaphoreType.DMA(())],
    )
    return pl.pallas_call(
        _scatter_body,
        grid_spec=grid_spec,
        out_shape=jax.ShapeDtypeStruct((n_blocks * EXPERT_ROWS, wrow), I32),
        compiler_params=_cparams(("arbitrary",)),
        name="scatter_rows",
    )(counts, pstarts, pos.reshape(n, 1, tm), xp)


def _gather_rows(idx_ref, n, src_hbm, dst, sem):
    for r in range(n):
        pltpu.make_async_copy(src_hbm.at[pl.ds(idx_ref[0, 0, r], 1)], dst.at[pl.ds(r, 1)], sem).start()


def _wait_rows(n, src_hbm, dst, sem):
    pltpu.make_async_copy(src_hbm.at[pl.ds(0, n)], dst, sem).wait()


def _unpack_rows(xp):
    left = lax.bitcast_convert_type(lax.shift_left(xp, 16), F32).astype(BF16)
    right = lax.bitcast_convert_type(xp & jnp.int32(-65536), F32).astype(BF16)
    return jnp.concatenate([left, right], axis=1)


def _experts_body(elo_ref, ehi_ref, nu_ref, xs_ref, wa_ref, wb_ref, o_ref):
    i = pl.program_id(0)
    dh = xs_ref.shape[1] - LANES
    d = o_ref.shape[1]
    ff = wa_ref.shape[1] - d

    @pl.when(i < nu_ref[0])
    def _():
        x = _unpack_rows(xs_ref[:, :dh])
        info = lax.bitcast_convert_type(xs_ref[:, dh:], F32)

        def ffn(w_ref, weight):
            ab = _dot(x, w_ref[0, :d, :])
            a = ab[:, :ff]
            hid = (a * jax.nn.sigmoid(a)) * ab[:, ff:] * weight
            return _dot(hid.astype(BF16), w_ref[0, d:, :])

        o_ref[...] = ffn(wa_ref, info[:, 1:2]) + ffn(wb_ref, info[:, 2:3])

    @pl.when(i >= nu_ref[0])
    def _():
        o_ref[...] = jnp.zeros_like(o_ref)


def _experts(xs, e_lo, e_hi, n_used, w1, w3, w2):
    n_rows, wrow = xs.shape
    n_blocks = n_rows // EXPERT_ROWS
    n_exp, d, ff = w1.shape
    assert 2 * ff == d
    wcat = jnp.concatenate([jnp.concatenate([w1, w3], axis=2), w2], axis=1).astype(BF16)
    grid_spec = pltpu.PrefetchScalarGridSpec(
        num_scalar_prefetch=3,
        grid=(n_blocks,),
        in_specs=[pl.BlockSpec((EXPERT_ROWS, wrow), lambda i, lo, hi, nu: (jnp.minimum(i, nu[0] - 1), 0)),
                  pl.BlockSpec((1, d + ff, d), lambda i, lo, hi, nu: (lo[i], 0, 0)),
                  pl.BlockSpec((1, d + ff, d), lambda i, lo, hi, nu: (hi[i], 0, 0))],
        out_specs=pl.BlockSpec((EXPERT_ROWS, d), lambda i, lo, hi, nu: (i, 0)),
    )
    return pl.pallas_call(
        _experts_body,
        grid_spec=grid_spec,
        out_shape=jax.ShapeDtypeStruct((n_rows, d), F32),
        compiler_params=_cparams(("arbitrary",)),
        name="experts",
    )(e_lo, e_hi, n_used, xs, wcat, wcat)


def _final_body(posc_ref, posn_ref, y_hbm, h1_ref, mod_ref, gf_ref, o_ref, ybuf, sem):
    i = pl.program_id(0)
    n = pl.num_programs(0)
    tm = h1_ref.shape[0]
    slot = i % 2

    @pl.when(i == 0)
    def _():
        _gather_rows(posc_ref, tm, y_hbm, ybuf.at[0], sem.at[0])

    _gather_rows(posn_ref, tm, y_hbm, ybuf.at[1 - slot], sem.at[1 - slot])

    _wait_rows(tm, y_hbm, ybuf.at[slot], sem.at[slot])
    h = h1_ref[...] + mod_ref[0, 5:6, :] * ybuf[slot]
    ms = jnp.mean(h * h, axis=-1, keepdims=True)
    o_ref[...] = h * lax.rsqrt(ms + RMS_EPS) * gf_ref[...]

    @pl.when(i == n - 1)
    def _():
        _wait_rows(tm, y_hbm, ybuf.at[1 - slot], sem.at[1 - slot])


def _final(pos, y_sorted, h1, mod, gf, seq, tm):
    nt, d = h1.shape
    n = nt // tm
    pos3 = pos.reshape(n, 1, tm)
    idx_spec = lambda f: pl.BlockSpec((1, 1, tm), f, memory_space=pltpu.SMEM)
    return pl.pallas_call(
        _final_body,
        grid=(n,),
        in_specs=[idx_spec(lambda i: (i, 0, 0)),
                  idx_spec(lambda i: (jnp.minimum(i + 1, n - 1), 0, 0)),
                  pl.BlockSpec(memory_space=pl.ANY),
                  pl.BlockSpec((tm, d), lambda i: (i, 0)),
                  pl.BlockSpec((1, 6, d), lambda i: (i * tm // seq, 0, 0)),
                  pl.BlockSpec((1, d), lambda i: (0, 0))],
        out_specs=pl.BlockSpec((tm, d), lambda i: (i, 0)),
        out_shape=jax.ShapeDtypeStruct((nt, d), F32),
        scratch_shapes=[pltpu.VMEM((2, tm, d), F32), pltpu.SemaphoreType.DMA((2,))],
        compiler_params=_cparams(("arbitrary",)),
        name="final",
    )(pos3, pos3, y_sorted, h1, mod, gf.reshape(1, d))


def kernel(x, c, w_ada, b_ada, norm1_g, w_in, w_gate, b_gate, ssm_a_re, ssm_a_im, ssm_b_re, ssm_b_im, ssm_c_re, ssm_c_im, ssm_d, ssm_log_step, w_glu, b_glu, sgu_ln_g, sgu_ln_b, sgu_w, sgu_b, w_branch_a, w_branch_b, w_out, norm2_g, w_router_group, b_router_group, w_router_expert, b_router_expert, w1, w3, w2, norm_f_g):
    bsz, seq, d = x.shape
    depth = w_ada.shape[0]
    assert depth == 1 and bsz % SCAN_BATCHES == 0 and seq % 512 == 0
    l = 0
    mod = _adaln(c, w_ada[l], b_ada[l]).reshape(bsz, 6, d)
    za, u, v, gates = _inproj(x, mod, norm1_g[l], w_in[l], w_gate[l], b_gate[l],
                              sgu_ln_g[l], sgu_ln_b[l], tm=512)
    a_row, s5_in, s5_out = _s5_params(ssm_a_re[l], ssm_a_im[l], ssm_b_re[l], ssm_b_im[l],
                                      ssm_c_re[l], ssm_c_im[l], ssm_log_step[l])
    ya = _s5(za, a_row, s5_in, s5_out, ssm_d[l], w_glu[l], b_glu[l])
    h1, xp, route = _mix(x, u, v, ya, gates, mod, sgu_w[l], sgu_b[l], w_branch_a[l], w_branch_b[l],
                         w_out[l], norm2_g[l], w_router_group[l], b_router_group[l],
                         w_router_expert[l], b_router_expert[l], tm=256)
    nt = bsz * seq
    n_blocks = nt // EXPERT_ROWS + N_CLASSES
    pos, e_lo, e_hi, n_used, counts, pstarts = _dispatch(route, n_blocks)
    xs = _scatter_rows(xp.reshape(nt, xp.shape[-1]), pos, counts, pstarts, n_blocks)
    y_sorted = _experts(xs, e_lo, e_hi, n_used, w1[l], w3[l], w2[l])
    out = _final(pos, y_sorted, h1.reshape(nt, d), mod, norm_f_g, seq, tm=256)
    return out.reshape(bsz, seq, d)
```

```python
import functools
import math

import jax
import jax.numpy as jnp
from jax import lax
from jax.experimental import pallas as pl
from jax.experimental.pallas import tpu as pltpu

F32 = jnp.float32
BF16 = jnp.bfloat16
I32 = jnp.int32

LANES = 128
SUBLANES = 8
VMEM_LIMIT = 56 * 1024 * 1024

RMS_EPS = 1e-6
LN_EPS = 1e-5

SSM_GROUP_CH = 16
SSM_STATE = 64
GROUPS_PER_BLOCK = LANES // SSM_GROUP_CH
SGU_HEADS = 8
SGU_CHUNK = 128
N_GROUPS = 4
EXPERTS_PER_GROUP = 8
PAIRS_PER_GROUP = EXPERTS_PER_GROUP * (EXPERTS_PER_GROUP - 1) // 2
N_CLASSES = N_GROUPS * PAIRS_PER_GROUP
EXPERT_ROWS = 256

SCAN_BATCHES = SUBLANES
SCAN_TS = 128
SLAB_PITCH = SCAN_TS + 8
INPROJ_SUBTILE = 256
MIX_SUBTILE = 256


def _gelu(x):
    return 0.5 * x * (1.0 + jnp.tanh(math.sqrt(2.0 / math.pi) * (x + 0.044715 * (x * x * x))))


def _dot(a, b):
    return jnp.dot(a, b, preferred_element_type=F32)


def _split_bf16(a):
    hi = a.astype(BF16)
    lo = (a - hi.astype(F32)).astype(BF16)
    return hi, lo


def _cparams(sem):
    return pltpu.CompilerParams(dimension_semantics=sem, vmem_limit_bytes=VMEM_LIMIT)


def _adaln_body(c_ref, w_ref, b_ref, o_ref):
    c = c_ref[...]
    act = c * jax.nn.sigmoid(c)
    a_hi, a_lo = _split_bf16(act)
    w_hi, w_lo = _split_bf16(w_ref[...])
    o_ref[...] = _dot(a_hi, w_hi) + _dot(a_hi, w_lo) + _dot(a_lo, w_hi) + b_ref[...]


def _adaln(c, w, b):
    bsz, d = c.shape
    n = w.shape[1]
    tn = 1024
    return pl.pallas_call(
        _adaln_body,
        grid=(n // tn,),
        in_specs=[pl.BlockSpec((bsz, d), lambda j: (0, 0)),
                  pl.BlockSpec((d, tn), lambda j: (0, j)),
                  pl.BlockSpec((1, tn), lambda j: (0, j))],
        out_specs=pl.BlockSpec((bsz, tn), lambda j: (0, j)),
        out_shape=jax.ShapeDtypeStruct((bsz, n), F32),
        compiler_params=_cparams(("arbitrary",)),
        name="adaln",
    )(c, w, b.reshape(1, n))


def _inproj_body(x_ref, mod_ref, g1_ref, win_ref, wgate_ref, bgate_ref, lng_ref, lnb_ref,
                 za_ref, u_ref, v_ref, gates_ref):
    w = za_ref.shape[-1]
    tm = x_ref.shape[1]
    for r0 in range(0, tm, INPROJ_SUBTILE):
        rs = slice(r0, r0 + INPROJ_SUBTILE)
        x = x_ref[0, rs, :]
        ms = jnp.mean(x * x, axis=-1, keepdims=True)
        xn = x * lax.rsqrt(ms + RMS_EPS) * g1_ref[...]
        xn = xn * (1.0 + mod_ref[0, 1:2, :]) + mod_ref[0, 0:1, :]
        xb = xn.astype(BF16)
        proj = _dot(xb, win_ref[...])
        za_ref[0, rs, :] = proj[:, :w]
        u_ref[0, rs, :] = _gelu(proj[:, w:2 * w]).astype(BF16)
        gv = _gelu(proj[:, 2 * w:])
        mu = jnp.mean(gv, axis=-1, keepdims=True)
        cen = gv - mu
        var = jnp.mean(cen * cen, axis=-1, keepdims=True)
        v_ref[0, rs, :] = (cen * lax.rsqrt(var + LN_EPS) * lng_ref[...] + lnb_ref[...]).astype(BF16)
        gates_ref[0, rs, :] = jax.nn.sigmoid(_dot(xb, wgate_ref[...]) + bgate_ref[...]).astype(BF16)


def _inproj(x, mod, g1, w_in, w_gate, b_gate, ln_g, ln_b, tm):
    bsz, s, d = x.shape
    w = w_in.shape[1] // 3
    ng = w_gate.shape[1]
    tok = lambda n: pl.BlockSpec((1, tm, n), lambda b, i: (b, i, 0))
    full = lambda a: pl.BlockSpec(a.shape, lambda b, i: (0,) * a.ndim)
    args = (g1.reshape(1, d), w_in.astype(BF16), w_gate.astype(BF16), b_gate.reshape(1, ng),
            ln_g.reshape(1, w), ln_b.reshape(1, w))
    return pl.pallas_call(
        _inproj_body,
        grid=(bsz, s // tm),
        in_specs=[tok(d), pl.BlockSpec((1, 6, d), lambda b, i: (b, 0, 0))] + [full(a) for a in args],
        out_specs=[tok(w), tok(w), tok(w), tok(ng)],
        out_shape=[jax.ShapeDtypeStruct((bsz, s, w), F32),
                   jax.ShapeDtypeStruct((bsz, s, w), BF16),
                   jax.ShapeDtypeStruct((bsz, s, w), BF16),
                   jax.ShapeDtypeStruct((bsz, s, ng), BF16)],
        compiler_params=_cparams(("parallel", "parallel")),
        name="in_proj",
    )(x, mod, *args)


def _s5_params(a_re, a_im, b_re, b_im, c_re, c_im, log_step):
    g, n = a_re.shape
    nblk = g // GROUPS_PER_BLOCK
    lam_re = jnp.minimum(a_re.astype(F32), -1e-4)
    lam_im = a_im.astype(F32)
    dt = jnp.exp(log_step.astype(F32))[:, None]
    mag = jnp.exp(lam_re * dt)
    ab_re = mag * jnp.cos(lam_im * dt)
    ab_im = mag * jnp.sin(lam_im * dt)
    den = lam_re * lam_re + lam_im * lam_im
    nr = ab_re - 1.0
    f_re = (nr * lam_re + ab_im * lam_im) / den
    f_im = (ab_im * lam_re - nr * lam_im) / den
    bt_re = b_re.astype(F32).transpose(0, 2, 1)
    bt_im = b_im.astype(F32).transpose(0, 2, 1)
    w_re = f_re[:, None, :] * bt_re - f_im[:, None, :] * bt_im
    w_im = f_re[:, None, :] * bt_im + f_im[:, None, :] * bt_re
    eye = jnp.eye(GROUPS_PER_BLOCK, dtype=F32)

    def bdiag_in(wt):
        t = wt.reshape(nblk, GROUPS_PER_BLOCK, SSM_GROUP_CH, 1, n) * eye[None, :, None, :, None]
        return t.reshape(nblk, GROUPS_PER_BLOCK * SSM_GROUP_CH, GROUPS_PER_BLOCK * n)

    def bdiag_out(ct):
        t = ct.transpose(0, 2, 1).reshape(nblk, GROUPS_PER_BLOCK, n, 1, SSM_GROUP_CH)
        t = t * eye[None, :, None, :, None]
        return t.reshape(nblk, GROUPS_PER_BLOCK * n, GROUPS_PER_BLOCK * SSM_GROUP_CH)

    w_in = jnp.concatenate([bdiag_in(w_re), bdiag_in(w_im)], axis=2).astype(BF16)
    w_out = jnp.concatenate([bdiag_out(c_re.astype(F32)), -bdiag_out(c_im.astype(F32))],
                            axis=1).astype(BF16)
    a_row = jnp.concatenate([ab_re.reshape(nblk, -1), ab_im.reshape(nblk, -1)], axis=1).reshape(1, -1)
    return a_row, w_in, w_out


def _s5_body(z_ref, a_ref, win_ref, wout_ref, d_ref, wglu_ref, bglu_ref, o_ref,
             zslab, zsb, xs, hstate, yslab):
    nb, ts, w = z_ref.shape
    rows = nb * ts
    nblk = w // LANES
    half = GROUPS_PER_BLOCK * SSM_STATE
    sw = 2 * half

    @pl.when(pl.program_id(1) == 0)
    def _():
        hstate[...] = jnp.zeros_like(hstate)

    for b in range(nb):
        for c in range(nblk):
            zslab[c, b * SLAB_PITCH:b * SLAB_PITCH + ts, :] = z_ref[b, :, c * LANES:(c + 1) * LANES]

    for s in range(ts):
        for c in range(nblk):
            zsb[s * nb:(s + 1) * nb, c * LANES:(c + 1) * LANES] = zslab.at[c][pl.ds(s, nb, stride=SLAB_PITCH), :]

    def in_map(j):
        xs[:, j * sw:(j + 1) * sw] = _dot(zsb[:, j * LANES:(j + 1) * LANES].astype(BF16), win_ref[j])

    def scan(j):
        re_sl = slice(j * sw, j * sw + half)
        im_sl = slice(j * sw + half, (j + 1) * sw)
        a_r = jnp.broadcast_to(a_ref[:, re_sl], (nb, half))
        a_i = jnp.broadcast_to(a_ref[:, im_sl], (nb, half))
        h_r, h_i = hstate[:, re_sl], hstate[:, im_sl]
        for s in range(ts):
            x_r = xs[s * nb:(s + 1) * nb, re_sl]
            x_i = xs[s * nb:(s + 1) * nb, im_sl]
            h_r, h_i = a_r * h_r - a_i * h_i + x_r, a_r * h_i + a_i * h_r + x_i
            xs[s * nb:(s + 1) * nb, re_sl] = h_r
            xs[s * nb:(s + 1) * nb, im_sl] = h_i
        hstate[:, re_sl] = h_r
        hstate[:, im_sl] = h_i

    def out_map(j):
        y = _dot(xs[:, j * sw:(j + 1) * sw].astype(BF16), wout_ref[j])
        y = y + d_ref[:, j * LANES:(j + 1) * LANES] * zsb[:, j * LANES:(j + 1) * LANES]
        return _gelu(y)

    ys = []
    for step in range(nblk + 2):
        if step < nblk:
            in_map(step)
        if 0 <= step - 1 < nblk:
            scan(step - 1)
        if 0 <= step - 2 < nblk:
            ys.append(out_map(step - 2))
    g = jnp.concatenate(ys, axis=1)
    g = g * jax.nn.sigmoid(_dot(g.astype(BF16), wglu_ref[...]) + bglu_ref[...])
    for c in range(nblk):
        yslab[c] = g[:, c * LANES:(c + 1) * LANES]

    for b in range(nb):
        for c in range(nblk):
            o_ref[b, :, c * LANES:(c + 1) * LANES] = yslab.at[c][pl.ds(b, ts, stride=nb), :].astype(BF16)


def _s5(za, a_row, w_in, w_out, d_skip, w_glu, b_glu):
    bsz, s, w = za.shape
    nb, ts = SCAN_BATCHES, SCAN_TS
    rows = nb * ts
    nblk = w // LANES
    nstate = a_row.shape[1]
    full = lambda a: pl.BlockSpec(a.shape, lambda b, i: (0,) * a.ndim)
    args = (a_row, w_in, w_out, d_skip.reshape(1, w).astype(F32), w_glu.astype(BF16), b_glu.reshape(1, w))
    return pl.pallas_call(
        _s5_body,
        grid=(bsz // nb, s // ts),
        in_specs=[pl.BlockSpec((nb, ts, w), lambda b, i: (b, i, 0))] + [full(a) for a in args],
        out_specs=pl.BlockSpec((nb, ts, w), lambda b, i: (b, i, 0)),
        out_shape=jax.ShapeDtypeStruct((bsz, s, w), BF16),
        scratch_shapes=[pltpu.VMEM((nblk, nb * SLAB_PITCH, LANES), F32),
                        pltpu.VMEM((rows, w), F32),
                        pltpu.VMEM((rows, nstate), F32),
                        pltpu.VMEM((nb, nstate), F32),
                        pltpu.VMEM((nblk, rows, LANES), F32)],
        compiler_params=_cparams(("parallel", "arbitrary")),
        name="s5",
    )(za, *args)


def _mix_body(x_ref, u_ref, v_ref, ya_ref, gates_ref, mod_ref, wp_ref, sb_ref, wba_ref, wbb_ref,
              wout_ref, g2_ref, wr_ref, br_ref, h1_ref, xp_ref, route_ref):
    tm_full = x_ref.shape[1]
    npair = u_ref.shape[2] // LANES
    hd = LANES // 2

    t_idx = lax.broadcasted_iota(I32, (SGU_CHUNK, 2 * SGU_CHUNK), 0)
    s_idx = lax.broadcasted_iota(I32, (SGU_CHUNK, 2 * SGU_CHUNK), 1) & (SGU_CHUNK - 1)
    causal = s_idx <= t_idx
    lane = lax.broadcasted_iota(I32, (SGU_CHUNK, LANES), 1)
    first_head = lane < hd
    w_causal = [jnp.where(causal, wp_ref[j], jnp.zeros_like(wp_ref[j])) for j in range(npair)]
    for r0 in range(0, tm_full, MIX_SUBTILE):
        _mix_rows(slice(r0, r0 + MIX_SUBTILE), w_causal, first_head, x_ref, u_ref, v_ref, ya_ref, gates_ref,
                  mod_ref, sb_ref, wba_ref, wbb_ref, wout_ref, g2_ref, wr_ref, br_ref, h1_ref, xp_ref,
                  route_ref)


def _mix_rows(rs, w_causal, first_head, x_ref, u_ref, v_ref, ya_ref, gates_ref, mod_ref, sb_ref, wba_ref,
              wbb_ref, wout_ref, g2_ref, wr_ref, br_ref, h1_ref, xp_ref, route_ref):
    tm = rs.stop - rs.start
    d = x_ref.shape[2]
    npair = len(w_causal)
    mixed_rows = []
    for c in range(rs.start, rs.stop, SGU_CHUNK):
        vc = v_ref[0, c:c + SGU_CHUNK, :]
        blocks = []
        for j in range(npair):
            vb = vc[:, j * LANES:(j + 1) * LANES]
            zero = jnp.zeros_like(vb)
            rhs = jnp.concatenate([jnp.where(first_head, vb, zero), jnp.where(first_head, zero, vb)], axis=0)
            blocks.append(_dot(w_causal[j], rhs))
        mixed_rows.append(jnp.concatenate(blocks, axis=1) + sb_ref[...])
    mixed = jnp.concatenate(mixed_rows, axis=0)
    yb = (u_ref[0, rs, :].astype(F32) * mixed).astype(BF16)

    pa = _dot(ya_ref[0, rs, :], wba_ref[...])
    pb = _dot(yb, wbb_ref[...])
    merged = gates_ref[0, rs, :d].astype(F32) * pa + gates_ref[0, rs, d:].astype(F32) * pb
    o = _dot(merged.astype(BF16), wout_ref[...])
    h1 = x_ref[0, rs, :] + mod_ref[0, 2:3, :] * o
    h1_ref[0, rs, :] = h1

    ms = jnp.mean(h1 * h1, axis=-1, keepdims=True)
    xn = h1 * lax.rsqrt(ms + RMS_EPS) * g2_ref[...]
    xn = xn * (1.0 + mod_ref[0, 4:5, :]) + mod_ref[0, 3:4, :]
    x_hi = xn.astype(BF16)

    logits = _dot(x_hi, wr_ref[...]) + br_ref[...]
    lt = logits.T
    best = lt[0:1, :]
    grp = jnp.zeros((1, tm), I32)
    for gi in range(1, N_GROUPS):
        better = lt[gi:gi + 1, :] > best
        grp = jnp.where(better, gi, grp)
        best = jnp.where(better, lt[gi:gi + 1, :], best)
    den = jnp.zeros((1, tm), F32)
    for gi in range(N_GROUPS):
        den = den + jnp.exp(lt[gi:gi + 1, :] - best)
    pg = 1.0 / den
    le = lt[SUBLANES:2 * SUBLANES, :]
    for gi in range(1, N_GROUPS):
        le = jnp.where(grp == gi, lt[SUBLANES * (gi + 1):SUBLANES * (gi + 2), :], le)
    eidx = lax.broadcasted_iota(I32, (EXPERTS_PER_GROUP, tm), 0).astype(F32)
    none = float(EXPERTS_PER_GROUP)
    v1 = jnp.max(le, axis=0, keepdims=True)
    i1 = jnp.min(jnp.where(le == v1, eidx, none), axis=0, keepdims=True)
    rest = jnp.where(eidx == i1, -jnp.inf, le)
    v2 = jnp.max(rest, axis=0, keepdims=True)
    i2 = jnp.min(jnp.where(rest == v2, eidx, none), axis=0, keepdims=True)
    ex = jnp.exp(v2 - v1)
    p1 = 1.0 / (1.0 + ex)
    wt1 = pg * p1
    wt2 = pg * (ex * p1)
    first_lo = i1 < i2
    lo = jnp.where(first_lo, i1, i2)
    hi = jnp.where(first_lo, i2, i1)
    pair = lo * (2 * EXPERTS_PER_GROUP - 1.0 - lo) * 0.5 + (hi - lo - 1.0)
    cls = grp.astype(F32) * PAIRS_PER_GROUP + pair
    info = jnp.concatenate([cls, jnp.where(first_lo, wt1, wt2), jnp.where(first_lo, wt2, wt1)], axis=0)
    route_ref[:, rs] = jnp.concatenate([info, jnp.zeros((SUBLANES - 3, tm), F32)], axis=0)

    bits = lax.bitcast_convert_type(x_hi.astype(F32), I32)
    packed = lax.shift_right_logical(bits[:, :d // 2], 16) | (bits[:, d // 2:] & jnp.int32(-65536))
    info_cols = jnp.concatenate([info, jnp.zeros((LANES - 3, tm), F32)], axis=0).T
    xp_ref[0, rs, :] = jnp.concatenate([packed, lax.bitcast_convert_type(info_cols, I32)], axis=1)


def _mix(x, u, v, ya, gates, mod, sgu_w, sgu_b, w_ba, w_bb, w_out, g2, w_rg, b_rg, w_re, b_re, tm):
    bsz, s, d = x.shape
    w = u.shape[2]
    nt = bsz * s
    wp = sgu_w.reshape(SGU_HEADS // 2, 2, SGU_CHUNK, SGU_CHUNK).transpose(0, 2, 1, 3)
    wp = wp.reshape(SGU_HEADS // 2, SGU_CHUNK, 2 * SGU_CHUNK).astype(BF16)
    sb = jnp.repeat(sgu_b.T, w // SGU_HEADS, axis=1).astype(F32)
    wr = jnp.zeros((d, LANES), F32)
    wr = wr.at[:, :N_GROUPS].set(w_rg)
    wr = wr.at[:, SUBLANES:SUBLANES + N_GROUPS * EXPERTS_PER_GROUP].set(
        w_re.transpose(1, 0, 2).reshape(d, N_GROUPS * EXPERTS_PER_GROUP))
    br = jnp.zeros((1, LANES), F32)
    br = br.at[0, :N_GROUPS].set(b_rg)
    br = br.at[0, SUBLANES:SUBLANES + N_GROUPS * EXPERTS_PER_GROUP].set(b_re.reshape(-1))
    tok = lambda n: pl.BlockSpec((1, tm, n), lambda b, i: (b, i, 0))
    full = lambda a: pl.BlockSpec(a.shape, lambda b, i: (0,) * a.ndim)
    args = (wp, sb, w_ba.astype(BF16), w_bb.astype(BF16), w_out.astype(BF16), g2.reshape(1, d),
            wr.astype(BF16), br)
    nsteps = s // tm
    return pl.pallas_call(
        _mix_body,
        grid=(bsz, nsteps),
        in_specs=[tok(d), tok(w), tok(w), tok(w), tok(2 * d),
                  pl.BlockSpec((1, 6, d), lambda b, i: (b, 0, 0))] + [full(a) for a in args],
        out_specs=[tok(d), tok(d // 2 + LANES),
                   pl.BlockSpec((SUBLANES, tm), lambda b, i: (0, b * nsteps + i))],
        out_shape=[jax.ShapeDtypeStruct((bsz, s, d), F32),
                   jax.ShapeDtypeStruct((bsz, s, d // 2 + LANES), I32),
                   jax.ShapeDtypeStruct((SUBLANES, nt), F32)],
        compiler_params=_cparams(("parallel", "parallel")),
        name="mix",
    )(x, u, v, ya, gates, mod, *args)


def _tile_lanes(a, reps):
    return jnp.concatenate([a] * reps, axis=1)


def _dispatch_body(route_ref, lo_ref, hi_ref, pos_ref, blk_ref, cls_ref, run, pstart, ranks, earlier):
    phase = pl.program_id(0)
    i = pl.program_id(1)
    n = pl.num_programs(1)
    tt = route_ref.shape[1]
    reps = tt // LANES
    cls = route_ref[0:1, :]
    cid = lax.broadcasted_iota(I32, (LANES, tt), 0).astype(F32)
    member = cls == cid

    @pl.when(jnp.logical_and(phase == 0, i == 0))
    def _():
        run[...] = jnp.zeros_like(run)
        earlier_tok = lax.broadcasted_iota(I32, (tt, tt), 0) < lax.broadcasted_iota(I32, (tt, tt), 1)
        earlier[...] = jnp.where(earlier_tok, 1.0, 0.0).astype(BF16)

    @pl.when(phase == 0)
    def _():
        onehot = jnp.where(member, 1.0, 0.0)
        before = _dot(onehot.astype(BF16), earlier[...])
        ranks[i] = jnp.sum(onehot * (before + _tile_lanes(run[...], reps)), axis=0, keepdims=True)
        run[...] = run[...] + jnp.sum(onehot, axis=1, keepdims=True)

    @pl.when(jnp.logical_and(phase == 0, i == n - 1))
    def _():
        counts = run[...]
        nblk = jnp.floor((counts + (EXPERT_ROWS - 1.0)) * (1.0 / EXPERT_ROWS))
        hi_part = jnp.floor(nblk * (1.0 / 16.0))
        lo_part = nblk - 16.0 * hi_part
        upto = lax.broadcasted_iota(I32, (LANES, LANES), 1) <= lax.broadcasted_iota(I32, (LANES, LANES), 0)
        upto = jnp.where(upto, 1.0, 0.0).astype(BF16)
        ends = 16.0 * _dot(upto, hi_part.astype(BF16)) + _dot(upto, lo_part.astype(BF16))
        pstart[...] = (ends - nblk) * EXPERT_ROWS
        diag = lax.broadcasted_iota(I32, (LANES, LANES), 0) == lax.broadcasted_iota(I32, (LANES, LANES), 1)
        on_lanes = lambda a: jnp.sum(jnp.where(diag, a, 0.0), axis=0, keepdims=True)
        cls_ref[...] = jnp.concatenate([on_lanes(counts), on_lanes(pstart[...]),
                                        jnp.zeros((SUBLANES - 2, LANES), F32)], axis=0)
        nb_lanes = blk_ref.shape[1]
        breps = nb_lanes // LANES
        n_used = _tile_lanes(ends[LANES - 1:LANES, :], breps)
        b_idx = lax.broadcasted_iota(I32, (LANES, nb_lanes), 1).astype(F32)
        b_idx = jnp.minimum(b_idx, n_used - 1.0)
        b_cls = jnp.sum(jnp.where(_tile_lanes(ends, breps) <= b_idx, 1.0, 0.0), axis=0, keepdims=True)
        b_member = b_cls == lax.broadcasted_iota(I32, (LANES, nb_lanes), 0).astype(F32)
        pick = lambda tab: jnp.sum(jnp.where(b_member, _tile_lanes(tab, breps), 0.0), axis=0, keepdims=True)
        blk_ref[...] = jnp.concatenate([pick(lo_ref[...]), pick(hi_ref[...]), n_used,
                                        jnp.zeros((SUBLANES - 3, nb_lanes), F32)], axis=0)

    @pl.when(phase == 1)
    def _():
        base = jnp.sum(jnp.where(member, _tile_lanes(pstart[...], reps), 0.0), axis=0, keepdims=True)
        pos_ref[0] = (base + ranks[i]).astype(I32)


def _dispatch(route, n_blocks, tt=1024):
    nt = route.shape[1]
    n = nt // tt
    nb_lanes = pl.cdiv(n_blocks, LANES) * LANES
    lo_tab, hi_tab = [], []
    for g in range(N_GROUPS):
        for a in range(EXPERTS_PER_GROUP):
            for b in range(a + 1, EXPERTS_PER_GROUP):
                lo_tab.append(g * EXPERTS_PER_GROUP + a)
                hi_tab.append(g * EXPERTS_PER_GROUP + b)
    pad = [0] * (LANES - N_CLASSES)
    lo_tile = jnp.broadcast_to(jnp.asarray(lo_tab + pad, F32)[:, None], (LANES, LANES))
    hi_tile = jnp.broadcast_to(jnp.asarray(hi_tab + pad, F32)[:, None], (LANES, LANES))
    const = lambda shape: pl.BlockSpec(shape, lambda p, i: (0, 0))
    pos, blk_tab, cls_tab = pl.pallas_call(
        _dispatch_body,
        grid=(2, n),
        in_specs=[pl.BlockSpec((SUBLANES, tt), lambda p, i: (0, i)),
                  const((LANES, LANES)), const((LANES, LANES))],
        out_specs=[pl.BlockSpec((1, 1, tt), lambda p, i: (p * i, 0, 0)),
                   const((SUBLANES, nb_lanes)), const((SUBLANES, LANES))],
        out_shape=[jax.ShapeDtypeStruct((n, 1, tt), I32),
                   jax.ShapeDtypeStruct((SUBLANES, nb_lanes), F32),
                   jax.ShapeDtypeStruct((SUBLANES, LANES), F32)],
        scratch_shapes=[pltpu.VMEM((LANES, LANES), F32), pltpu.VMEM((LANES, LANES), F32),
                        pltpu.VMEM((n, 1, tt), F32), pltpu.VMEM((tt, tt), BF16)],
        compiler_params=_cparams(("arbitrary", "arbitrary")),
        name="dispatch",
    )(route, lo_tile, hi_tile)
    blk = blk_tab.astype(I32)
    cls = cls_tab.astype(I32)
    return pos.reshape(nt), blk[0, :n_blocks], blk[1, :n_blocks], blk[2, :1], cls[0], cls[1]


ZERO_FILL_ROWS = tuple(SUBLANES << k for k in range((EXPERT_ROWS // SUBLANES).bit_length() - 1))


def _scatter_body(cnt_ref, pst_ref, pos_ref, xp_ref, xs_hbm, ring, zeros, sem, zsem):
    i = pl.program_id(0)
    n = pl.num_programs(0)
    tm = xp_ref.shape[0]
    half = tm // 2

    def rows_done(h):
        pltpu.make_async_copy(ring.at[pl.ds(h * half, half)], xs_hbm.at[pl.ds(0, half)], sem.at[h]).wait()

    for h in range(2):
        @pl.when(i > 0)
        def _():
            rows_done(h)

        ring[h * half:(h + 1) * half, :] = xp_ref[h * half:(h + 1) * half, :]
        for r in range(h * half, (h + 1) * half):
            pltpu.make_async_copy(ring.at[pl.ds(r, 1)], xs_hbm.at[pl.ds(pos_ref[0, 0, r], 1)],
                                  sem.at[h]).start()

    @pl.when(i == n - 1)
    def _():
        rows_done(0)
        rows_done(1)
        zeros[...] = jnp.zeros_like(zeros)

        def fill(wait):
            def per_class(k, carry):
                cnt = cnt_ref[k]
                n_pad = (-cnt) & (EXPERT_ROWS - 1)
                off = pst_ref[k] + cnt
                for r in range(SUBLANES - 1):
                    row = pltpu.make_async_copy(zeros.at[pl.ds(0, 1)], xs_hbm.at[pl.ds(off + r, 1)], zsem)

                    @pl.when(r < (n_pad & (SUBLANES - 1)))
                    def _():
                        row.wait() if wait else row.start()

                off = off + (n_pad & (SUBLANES - 1))
                for size in ZERO_FILL_ROWS:
                    piece = pltpu.make_async_copy(zeros.at[pl.ds(0, size)],
                                                  xs_hbm.at[pl.ds(pl.multiple_of(off, size), size)], zsem)

                    @pl.when((n_pad & size) != 0)
                    def _():
                        piece.wait() if wait else piece.start()

                    off = off + (n_pad & size)
                return carry

            lax.fori_loop(0, N_CLASSES, per_class, 0)

            last = N_CLASSES - 1
            used_rows = pst_ref[last] + cnt_ref[last] + ((-cnt_ref[last]) & (EXPERT_ROWS - 1))
            tail = ZERO_FILL_ROWS[-1]

            def per_piece(j, carry):
                piece = pltpu.make_async_copy(
                    zeros, xs_hbm.at[pl.ds(pl.multiple_of(used_rows + j * tail, tail), tail)], zsem)
                piece.wait() if wait else piece.start()
                return carry

            lax.fori_loop(0, (xs_hbm.shape[0] - used_rows) // tail, per_piece, 0)

        fill(False)
        fill(True)


def _scatter_rows(xp, pos, counts, pstarts, n_blocks, tm=512):
    nt, wrow = xp.shape
    n = nt // tm
    grid_spec = pltpu.PrefetchScalarGridSpec(
        num_scalar_prefetch=2,
        grid=(n,),
        in_specs=[pl.BlockSpec((1, 1, tm), lambda i, c, p: (i, 0, 0), memory_space=pltpu.SMEM),
                  pl.BlockSpec((tm, wrow), lambda i, c, p: (i, 0))],
        out_specs=pl.BlockSpec(memory_space=pl.ANY),
        scratch_shapes=[pltpu.VMEM((tm, wrow), I32), pltpu.VMEM((ZERO_FILL_ROWS[-1], wrow), I32),
                        pltpu.SemaphoreType.DMA((2,)), pltpu.SemaphoreType.DMA(())],
    )
    return pl.pallas_call(
        _scatter_body,
        grid_spec=grid_spec,
        out_shape=jax.ShapeDtypeStruct((n_blocks * EXPERT_ROWS, wrow), I32),
        compiler_params=_cparams(("arbitrary",)),
        name="scatter_rows",
    )(counts, pstarts, pos.reshape(n, 1, tm), xp)


def _wait_rows(n, src_hbm, dst, sem):
    pltpu.make_async_copy(src_hbm.at[pl.ds(0, n)], dst, sem).wait()


def _unpack_rows(xp):
    left = lax.bitcast_convert_type(lax.shift_left(xp, 16), F32).astype(BF16)
    right = lax.bitcast_convert_type(xp & jnp.int32(-65536), F32).astype(BF16)
    return jnp.concatenate([left, right], axis=1)


def _experts_body(elo_ref, ehi_ref, nu_ref, xs_ref, wa_ref, wb_ref, o_ref):
    i = pl.program_id(0)
    dh = xs_ref.shape[1] - LANES
    d = o_ref.shape[1]
    ff = wa_ref.shape[1] - d

    @pl.when(i < nu_ref[0])
    def _():
        x = _unpack_rows(xs_ref[:, :dh])
        info = lax.bitcast_convert_type(xs_ref[:, dh:], F32)

        def ffn(w_ref, weight):
            ab = _dot(x, w_ref[0, :d, :])
            a = ab[:, :ff]
            hid = (a * jax.nn.sigmoid(a)) * ab[:, ff:] * weight
            return _dot(hid.astype(BF16), w_ref[0, d:, :])

        o_ref[...] = ffn(wa_ref, info[:, 1:2]) + ffn(wb_ref, info[:, 2:3])

    @pl.when(i >= nu_ref[0])
    def _():
        o_ref[...] = jnp.zeros_like(o_ref)


def _experts(xs, e_lo, e_hi, n_used, w1, w3, w2):
    n_rows, wrow = xs.shape
    n_blocks = n_rows // EXPERT_ROWS
    n_exp, d, ff = w1.shape
    assert 2 * ff == d
    wcat = jnp.concatenate([jnp.concatenate([w1, w3], axis=2), w2], axis=1).astype(BF16)
    grid_spec = pltpu.PrefetchScalarGridSpec(
        num_scalar_prefetch=3,
        grid=(n_blocks,),
        in_specs=[pl.BlockSpec((EXPERT_ROWS, wrow), lambda i, lo, hi, nu: (jnp.minimum(i, nu[0] - 1), 0)),
                  pl.BlockSpec((1, d + ff, d), lambda i, lo, hi, nu: (lo[i], 0, 0)),
                  pl.BlockSpec((1, d + ff, d), lambda i, lo, hi, nu: (hi[i], 0, 0))],
        out_specs=pl.BlockSpec((EXPERT_ROWS, d), lambda i, lo, hi, nu: (i, 0)),
    )
    return pl.pallas_call(
        _experts_body,
        grid_spec=grid_spec,
        out_shape=jax.ShapeDtypeStruct((n_rows, d), F32),
        compiler_params=_cparams(("arbitrary",)),
        name="experts",
    )(e_lo, e_hi, n_used, xs, wcat, wcat)


def _final_body(posc_ref, posn_ref, y_hbm, h1_ref, mod_ref, gf_ref, o_ref, ybuf, sem):
    i = pl.program_id(0)
    n = pl.num_programs(0)
    tm = h1_ref.shape[0]
    half = tm // 2

    def fetch(idx_ref, h):
        for r in range(half):
            pltpu.make_async_copy(y_hbm.at[pl.ds(idx_ref[0, 0, h * half + r], 1)],
                                  ybuf.at[h, pl.ds(r, 1)], sem.at[h]).start()

    @pl.when(i == 0)
    def _():
        fetch(posc_ref, 0)
        fetch(posc_ref, 1)

    for h in range(2):
        rows = slice(h * half, (h + 1) * half)
        _wait_rows(half, y_hbm, ybuf.at[h], sem.at[h])
        hres = h1_ref[rows, :] + mod_ref[0, 5:6, :] * ybuf[h]
        ms = jnp.mean(hres * hres, axis=-1, keepdims=True)
        o_ref[rows, :] = hres * lax.rsqrt(ms + RMS_EPS) * gf_ref[...]
        fetch(posn_ref, h)

    @pl.when(i == n - 1)
    def _():
        _wait_rows(half, y_hbm, ybuf.at[0], sem.at[0])
        _wait_rows(half, y_hbm, ybuf.at[1], sem.at[1])


def _final(pos, y_sorted, h1, mod, gf, seq, tm):
    nt, d = h1.shape
    n = nt // tm
    pos3 = pos.reshape(n, 1, tm)
    idx_spec = lambda f: pl.BlockSpec((1, 1, tm), f, memory_space=pltpu.SMEM)
    return pl.pallas_call(
        _final_body,
        grid=(n,),
        in_specs=[idx_spec(lambda i: (i, 0, 0)),
                  idx_spec(lambda i: (jnp.minimum(i + 1, n - 1), 0, 0)),
                  pl.BlockSpec(memory_space=pl.ANY),
                  pl.BlockSpec((tm, d), lambda i: (i, 0)),
                  pl.BlockSpec((1, 6, d), lambda i: (i * tm // seq, 0, 0)),
                  pl.BlockSpec((1, d), lambda i: (0, 0))],
        out_specs=pl.BlockSpec((tm, d), lambda i: (i, 0)),
        out_shape=jax.ShapeDtypeStruct((nt, d), F32),
        scratch_shapes=[pltpu.VMEM((2, tm // 2, d), F32), pltpu.SemaphoreType.DMA((2,))],
        compiler_params=_cparams(("arbitrary",)),
        name="final",
    )(pos3, pos3, y_sorted, h1, mod, gf.reshape(1, d))


def kernel(x, c, w_ada, b_ada, norm1_g, w_in, w_gate, b_gate, ssm_a_re, ssm_a_im, ssm_b_re, ssm_b_im, ssm_c_re, ssm_c_im, ssm_d, ssm_log_step, w_glu, b_glu, sgu_ln_g, sgu_ln_b, sgu_w, sgu_b, w_branch_a, w_branch_b, w_out, norm2_g, w_router_group, b_router_group, w_router_expert, b_router_expert, w1, w3, w2, norm_f_g):
    bsz, seq, d = x.shape
    depth = w_ada.shape[0]
    assert depth == 1 and bsz % SCAN_BATCHES == 0 and seq % 512 == 0
    l = 0
    mod = _adaln(c, w_ada[l], b_ada[l]).reshape(bsz, 6, d)
    za, u, v, gates = _inproj(x, mod, norm1_g[l], w_in[l], w_gate[l], b_gate[l],
                              sgu_ln_g[l], sgu_ln_b[l], tm=512)
    a_row, s5_in, s5_out = _s5_params(ssm_a_re[l], ssm_a_im[l], ssm_b_re[l], ssm_b_im[l],
                                      ssm_c_re[l], ssm_c_im[l], ssm_log_step[l])
    ya = _s5(za, a_row, s5_in, s5_out, ssm_d[l], w_glu[l], b_glu[l])
    h1, xp, route = _mix(x, u, v, ya, gates, mod, sgu_w[l], sgu_b[l], w_branch_a[l], w_branch_b[l],
                         w_out[l], norm2_g[l], w_router_group[l], b_router_group[l],
                         w_router_expert[l], b_router_expert[l], tm=512)
    nt = bsz * seq
    n_blocks = nt // EXPERT_ROWS + N_CLASSES
    pos, e_lo, e_hi, n_used, counts, pstarts = _dispatch(route, n_blocks)
    xs = _scatter_rows(xp.reshape(nt, xp.shape[-1]), pos, counts, pstarts, n_blocks)
    y_sorted = _experts(xs, e_lo, e_hi, n_used, w1[l], w3[l], w2[l])
    out = _final(pos, y_sorted, h1.reshape(nt, d), mod, norm_f_g, seq, tm=512)
    return out.reshape(bsz, seq, d)
```

```python
import functools
import math

import jax
import jax.numpy as jnp
from jax import lax
from jax.experimental import pallas as pl
from jax.experimental.pallas import tpu as pltpu

F32 = jnp.float32
BF16 = jnp.bfloat16
I32 = jnp.int32

LANES = 128
SUBLANES = 8
VMEM_LIMIT = 56 * 1024 * 1024

RMS_EPS = 1e-6
LN_EPS = 1e-5

SSM_GROUP_CH = 16
SSM_STATE = 64
GROUPS_PER_BLOCK = LANES // SSM_GROUP_CH
SGU_HEADS = 8
SGU_CHUNK = 128
N_GROUPS = 4
EXPERTS_PER_GROUP = 8
PAIRS_PER_GROUP = EXPERTS_PER_GROUP * (EXPERTS_PER_GROUP - 1) // 2
N_CLASSES = N_GROUPS * PAIRS_PER_GROUP
EXPERT_ROWS = 256

SCAN_BATCHES = SUBLANES
SCAN_TS = 128
SLAB_PITCH = SCAN_TS + 8
INPROJ_SUBTILE = 256
MIX_SUBTILE = 256


def _gelu(x):
    return 0.5 * x * (1.0 + jnp.tanh(math.sqrt(2.0 / math.pi) * (x + 0.044715 * (x * x * x))))


def _dot(a, b):
    return jnp.dot(a, b, preferred_element_type=F32)


def _split_bf16(a):
    hi = a.astype(BF16)
    lo = (a - hi.astype(F32)).astype(BF16)
    return hi, lo


def _cparams(sem):
    return pltpu.CompilerParams(dimension_semantics=sem, vmem_limit_bytes=VMEM_LIMIT)


def _adaln_body(c_ref, w_ref, b_ref, o_ref):
    c = c_ref[...]
    act = c * jax.nn.sigmoid(c)
    a_hi, a_lo = _split_bf16(act)
    w_hi, w_lo = _split_bf16(w_ref[...])
    o_ref[...] = _dot(a_hi, w_hi) + _dot(a_hi, w_lo) + _dot(a_lo, w_hi) + b_ref[...]


def _adaln(c, w, b):
    bsz, d = c.shape
    n = w.shape[1]
    tn = 1024
    return pl.pallas_call(
        _adaln_body,
        grid=(n // tn,),
        in_specs=[pl.BlockSpec((bsz, d), lambda j: (0, 0)),
                  pl.BlockSpec((d, tn), lambda j: (0, j)),
                  pl.BlockSpec((1, tn), lambda j: (0, j))],
        out_specs=pl.BlockSpec((bsz, tn), lambda j: (0, j)),
        out_shape=jax.ShapeDtypeStruct((bsz, n), F32),
        compiler_params=_cparams(("arbitrary",)),
        name="adaln",
    )(c, w, b.reshape(1, n))


def _inproj_body(x_ref, mod_ref, g1_ref, win_ref, wgate_ref, bgate_ref, lng_ref, lnb_ref,
                 za_ref, u_ref, v_ref, gates_ref):
    w = za_ref.shape[-1]
    tm = x_ref.shape[1]
    for r0 in range(0, tm, INPROJ_SUBTILE):
        rs = slice(r0, r0 + INPROJ_SUBTILE)
        x = x_ref[0, rs, :]
        ms = jnp.mean(x * x, axis=-1, keepdims=True)
        xn = x * lax.rsqrt(ms + RMS_EPS) * g1_ref[...]
        xn = xn * (1.0 + mod_ref[0, 1:2, :]) + mod_ref[0, 0:1, :]
        xb = xn.astype(BF16)
        proj = _dot(xb, win_ref[...])
        za_ref[0, rs, :] = proj[:, :w]
        u_ref[0, rs, :] = _gelu(proj[:, w:2 * w]).astype(BF16)
        gv = _gelu(proj[:, 2 * w:])
        mu = jnp.mean(gv, axis=-1, keepdims=True)
        cen = gv - mu
        var = jnp.mean(cen * cen, axis=-1, keepdims=True)
        v_ref[0, rs, :] = (cen * lax.rsqrt(var + LN_EPS) * lng_ref[...] + lnb_ref[...]).astype(BF16)
        gates_ref[0, rs, :] = jax.nn.sigmoid(_dot(xb, wgate_ref[...]) + bgate_ref[...]).astype(BF16)


def _inproj(x, mod, g1, w_in, w_gate, b_gate, ln_g, ln_b, tm):
    bsz, s, d = x.shape
    w = w_in.shape[1] // 3
    ng = w_gate.shape[1]
    tok = lambda n: pl.BlockSpec((1, tm, n), lambda b, i: (b, i, 0))
    full = lambda a: pl.BlockSpec(a.shape, lambda b, i: (0,) * a.ndim)
    args = (g1.reshape(1, d), w_in.astype(BF16), w_gate.astype(BF16), b_gate.reshape(1, ng),
            ln_g.reshape(1, w), ln_b.reshape(1, w))
    return pl.pallas_call(
        _inproj_body,
        grid=(bsz, s // tm),
        in_specs=[tok(d), pl.BlockSpec((1, 6, d), lambda b, i: (b, 0, 0))] + [full(a) for a in args],
        out_specs=[tok(w), tok(w), tok(w), tok(ng)],
        out_shape=[jax.ShapeDtypeStruct((bsz, s, w), F32),
                   jax.ShapeDtypeStruct((bsz, s, w), BF16),
                   jax.ShapeDtypeStruct((bsz, s, w), BF16),
                   jax.ShapeDtypeStruct((bsz, s, ng), BF16)],
        compiler_params=_cparams(("parallel", "parallel")),
        name="in_proj",
    )(x, mod, *args)


def _s5_params(a_re, a_im, b_re, b_im, c_re, c_im, log_step):
    g, n = a_re.shape
    nblk = g // GROUPS_PER_BLOCK
    lam_re = jnp.minimum(a_re.astype(F32), -1e-4)
    lam_im = a_im.astype(F32)
    dt = jnp.exp(log_step.astype(F32))[:, None]
    mag = jnp.exp(lam_re * dt)
    ab_re = mag * jnp.cos(lam_im * dt)
    ab_im = mag * jnp.sin(lam_im * dt)
    den = lam_re * lam_re + lam_im * lam_im
    nr = ab_re - 1.0
    f_re = (nr * lam_re + ab_im * lam_im) / den
    f_im = (ab_im * lam_re - nr * lam_im) / den
    bt_re = b_re.astype(F32).transpose(0, 2, 1)
    bt_im = b_im.astype(F32).transpose(0, 2, 1)
    w_re = f_re[:, None, :] * bt_re - f_im[:, None, :] * bt_im
    w_im = f_re[:, None, :] * bt_im + f_im[:, None, :] * bt_re
    eye = jnp.eye(GROUPS_PER_BLOCK, dtype=F32)

    def bdiag_in(wt):
        t = wt.reshape(nblk, GROUPS_PER_BLOCK, SSM_GROUP_CH, 1, n) * eye[None, :, None, :, None]
        return t.reshape(nblk, GROUPS_PER_BLOCK * SSM_GROUP_CH, GROUPS_PER_BLOCK * n)

    def bdiag_out(ct):
        t = ct.transpose(0, 2, 1).reshape(nblk, GROUPS_PER_BLOCK, n, 1, SSM_GROUP_CH)
        t = t * eye[None, :, None, :, None]
        return t.reshape(nblk, GROUPS_PER_BLOCK * n, GROUPS_PER_BLOCK * SSM_GROUP_CH)

    w_in = jnp.concatenate([bdiag_in(w_re), bdiag_in(w_im)], axis=2).astype(BF16)
    w_out = jnp.concatenate([bdiag_out(c_re.astype(F32)), -bdiag_out(c_im.astype(F32))],
                            axis=1).astype(BF16)
    a_row = jnp.concatenate([ab_re.reshape(nblk, -1), ab_im.reshape(nblk, -1)], axis=1).reshape(1, -1)
    return a_row, w_in, w_out


def _s5_body(z_ref, a_ref, win_ref, wout_ref, d_ref, wglu_ref, bglu_ref, o_ref,
             zslab, zsb, xs, hstate, yslab):
    nb, ts, w = z_ref.shape
    rows = nb * ts
    nblk = w // LANES
    half = GROUPS_PER_BLOCK * SSM_STATE
    sw = 2 * half

    @pl.when(pl.program_id(1) == 0)
    def _():
        hstate[...] = jnp.zeros_like(hstate)

    for b in range(nb):
        for c in range(nblk):
            zslab[c, b * SLAB_PITCH:b * SLAB_PITCH + ts, :] = z_ref[b, :, c * LANES:(c + 1) * LANES]

    for s in range(ts):
        for c in range(nblk):
            zsb[s * nb:(s + 1) * nb, c * LANES:(c + 1) * LANES] = zslab.at[c][pl.ds(s, nb, stride=SLAB_PITCH), :]

    def in_map(j):
        xs[:, j * sw:(j + 1) * sw] = _dot(zsb[:, j * LANES:(j + 1) * LANES].astype(BF16), win_ref[j])

    def scan(j):
        re_sl = slice(j * sw, j * sw + half)
        im_sl = slice(j * sw + half, (j + 1) * sw)
        a_r = jnp.broadcast_to(a_ref[:, re_sl], (nb, half))
        a_i = jnp.broadcast_to(a_ref[:, im_sl], (nb, half))
        h_r, h_i = hstate[:, re_sl], hstate[:, im_sl]
        for s in range(ts):
            x_r = xs[s * nb:(s + 1) * nb, re_sl]
            x_i = xs[s * nb:(s + 1) * nb, im_sl]
            h_r, h_i = a_r * h_r - a_i * h_i + x_r, a_r * h_i + a_i * h_r + x_i
            xs[s * nb:(s + 1) * nb, re_sl] = h_r
            xs[s * nb:(s + 1) * nb, im_sl] = h_i
        hstate[:, re_sl] = h_r
        hstate[:, im_sl] = h_i

    def out_map(j):
        y = _dot(xs[:, j * sw:(j + 1) * sw].astype(BF16), wout_ref[j])
        y = y + d_ref[:, j * LANES:(j + 1) * LANES] * zsb[:, j * LANES:(j + 1) * LANES]
        return _gelu(y)

    ys = []
    for step in range(nblk + 2):
        if step < nblk:
            in_map(step)
        if 0 <= step - 1 < nblk:
            scan(step - 1)
        if 0 <= step - 2 < nblk:
            ys.append(out_map(step - 2))
    g = jnp.concatenate(ys, axis=1)
    g = g * jax.nn.sigmoid(_dot(g.astype(BF16), wglu_ref[...]) + bglu_ref[...])
    for c in range(nblk):
        yslab[c] = g[:, c * LANES:(c + 1) * LANES]

    for b in range(nb):
        for c in range(nblk):
            o_ref[b, :, c * LANES:(c + 1) * LANES] = yslab.at[c][pl.ds(b, ts, stride=nb), :].astype(BF16)


def _s5(za, a_row, w_in, w_out, d_skip, w_glu, b_glu):
    bsz, s, w = za.shape
    nb, ts = SCAN_BATCHES, SCAN_TS
    rows = nb * ts
    nblk = w // LANES
    nstate = a_row.shape[1]
    full = lambda a: pl.BlockSpec(a.shape, lambda b, i: (0,) * a.ndim)
    args = (a_row, w_in, w_out, d_skip.reshape(1, w).astype(F32), w_glu.astype(BF16), b_glu.reshape(1, w))
    return pl.pallas_call(
        _s5_body,
        grid=(bsz // nb, s // ts),
        in_specs=[pl.BlockSpec((nb, ts, w), lambda b, i: (b, i, 0))] + [full(a) for a in args],
        out_specs=pl.BlockSpec((nb, ts, w), lambda b, i: (b, i, 0)),
        out_shape=jax.ShapeDtypeStruct((bsz, s, w), BF16),
        scratch_shapes=[pltpu.VMEM((nblk, nb * SLAB_PITCH, LANES), F32),
                        pltpu.VMEM((rows, w), F32),
                        pltpu.VMEM((rows, nstate), F32),
                        pltpu.VMEM((nb, nstate), F32),
                        pltpu.VMEM((nblk, rows, LANES), F32)],
        compiler_params=_cparams(("parallel", "arbitrary")),
        name="s5",
    )(za, *args)


def _mix_body(x_ref, u_ref, v_ref, ya_ref, gates_ref, mod_ref, wp_ref, sb_ref, wba_ref, wbb_ref,
              wout_ref, g2_ref, wr_ref, br_ref, h1_ref, xp_ref, route_ref):
    tm_full = x_ref.shape[1]
    npair = u_ref.shape[2] // LANES
    hd = LANES // 2

    t_idx = lax.broadcasted_iota(I32, (SGU_CHUNK, 2 * SGU_CHUNK), 0)
    s_idx = lax.broadcasted_iota(I32, (SGU_CHUNK, 2 * SGU_CHUNK), 1) & (SGU_CHUNK - 1)
    causal = s_idx <= t_idx
    lane = lax.broadcasted_iota(I32, (SGU_CHUNK, LANES), 1)
    first_head = lane < hd
    w_causal = [jnp.where(causal, wp_ref[j], jnp.zeros_like(wp_ref[j])) for j in range(npair)]
    for r0 in range(0, tm_full, MIX_SUBTILE):
        _mix_rows(slice(r0, r0 + MIX_SUBTILE), w_causal, first_head, x_ref, u_ref, v_ref, ya_ref, gates_ref,
                  mod_ref, sb_ref, wba_ref, wbb_ref, wout_ref, g2_ref, wr_ref, br_ref, h1_ref, xp_ref,
                  route_ref)


def _mix_rows(rs, w_causal, first_head, x_ref, u_ref, v_ref, ya_ref, gates_ref, mod_ref, sb_ref, wba_ref,
              wbb_ref, wout_ref, g2_ref, wr_ref, br_ref, h1_ref, xp_ref, route_ref):
    tm = rs.stop - rs.start
    d = x_ref.shape[2]
    npair = len(w_causal)
    mixed_rows = []
    for c in range(rs.start, rs.stop, SGU_CHUNK):
        vc = v_ref[0, c:c + SGU_CHUNK, :]
        blocks = []
        for j in range(npair):
            vb = vc[:, j * LANES:(j + 1) * LANES]
            zero = jnp.zeros_like(vb)
            rhs = jnp.concatenate([jnp.where(first_head, vb, zero), jnp.where(first_head, zero, vb)], axis=0)
            blocks.append(_dot(w_causal[j], rhs))
        mixed_rows.append(jnp.concatenate(blocks, axis=1) + sb_ref[...])
    mixed = jnp.concatenate(mixed_rows, axis=0)
    yb = (u_ref[0, rs, :].astype(F32) * mixed).astype(BF16)

    pa = _dot(ya_ref[0, rs, :], wba_ref[...])
    pb = _dot(yb, wbb_ref[...])
    merged = gates_ref[0, rs, :d].astype(F32) * pa + gates_ref[0, rs, d:].astype(F32) * pb
    o = _dot(merged.astype(BF16), wout_ref[...])
    h1 = x_ref[0, rs, :] + mod_ref[0, 2:3, :] * o
    h1_ref[0, rs, :] = h1

    ms = jnp.mean(h1 * h1, axis=-1, keepdims=True)
    xn = h1 * lax.rsqrt(ms + RMS_EPS) * g2_ref[...]
    xn = xn * (1.0 + mod_ref[0, 4:5, :]) + mod_ref[0, 3:4, :]
    x_hi = xn.astype(BF16)

    logits = _dot(x_hi, wr_ref[...]) + br_ref[...]
    lt = logits.T
    best = lt[0:1, :]
    grp = jnp.zeros((1, tm), I32)
    for gi in range(1, N_GROUPS):
        better = lt[gi:gi + 1, :] > best
        grp = jnp.where(better, gi, grp)
        best = jnp.where(better, lt[gi:gi + 1, :], best)
    den = jnp.zeros((1, tm), F32)
    for gi in range(N_GROUPS):
        den = den + jnp.exp(lt[gi:gi + 1, :] - best)
    pg = 1.0 / den
    le = lt[SUBLANES:2 * SUBLANES, :]
    for gi in range(1, N_GROUPS):
        le = jnp.where(grp == gi, lt[SUBLANES * (gi + 1):SUBLANES * (gi + 2), :], le)
    eidx = lax.broadcasted_iota(I32, (EXPERTS_PER_GROUP, tm), 0).astype(F32)
    none = float(EXPERTS_PER_GROUP)
    v1 = jnp.max(le, axis=0, keepdims=True)
    i1 = jnp.min(jnp.where(le == v1, eidx, none), axis=0, keepdims=True)
    rest = jnp.where(eidx == i1, -jnp.inf, le)
    v2 = jnp.max(rest, axis=0, keepdims=True)
    i2 = jnp.min(jnp.where(rest == v2, eidx, none), axis=0, keepdims=True)
    ex = jnp.exp(v2 - v1)
    p1 = 1.0 / (1.0 + ex)
    wt1 = pg * p1
    wt2 = pg * (ex * p1)
    first_lo = i1 < i2
    lo = jnp.where(first_lo, i1, i2)
    hi = jnp.where(first_lo, i2, i1)
    pair = lo * (2 * EXPERTS_PER_GROUP - 1.0 - lo) * 0.5 + (hi - lo - 1.0)
    cls = grp.astype(F32) * PAIRS_PER_GROUP + pair
    info = jnp.concatenate([cls, jnp.where(first_lo, wt1, wt2), jnp.where(first_lo, wt2, wt1)], axis=0)
    route_ref[:, rs] = jnp.concatenate([info, jnp.zeros((SUBLANES - 3, tm), F32)], axis=0)

    bits = lax.bitcast_convert_type(x_hi.astype(F32), I32)
    packed = lax.shift_right_logical(bits[:, :d // 2], 16) | (bits[:, d // 2:] & jnp.int32(-65536))
    info_cols = jnp.concatenate([info, jnp.zeros((LANES - 3, tm), F32)], axis=0).T
    xp_ref[0, rs, :] = jnp.concatenate([packed, lax.bitcast_convert_type(info_cols, I32)], axis=1)


def _mix(x, u, v, ya, gates, mod, sgu_w, sgu_b, w_ba, w_bb, w_out, g2, w_rg, b_rg, w_re, b_re, tm):
    bsz, s, d = x.shape
    w = u.shape[2]
    nt = bsz * s
    wp = sgu_w.reshape(SGU_HEADS // 2, 2, SGU_CHUNK, SGU_CHUNK).transpose(0, 2, 1, 3)
    wp = wp.reshape(SGU_HEADS // 2, SGU_CHUNK, 2 * SGU_CHUNK).astype(BF16)
    sb = jnp.repeat(sgu_b.T, w // SGU_HEADS, axis=1).astype(F32)
    wr = jnp.zeros((d, LANES), F32)
    wr = wr.at[:, :N_GROUPS].set(w_rg)
    wr = wr.at[:, SUBLANES:SUBLANES + N_GROUPS * EXPERTS_PER_GROUP].set(
        w_re.transpose(1, 0, 2).reshape(d, N_GROUPS * EXPERTS_PER_GROUP))
    br = jnp.zeros((1, LANES), F32)
    br = br.at[0, :N_GROUPS].set(b_rg)
    br = br.at[0, SUBLANES:SUBLANES + N_GROUPS * EXPERTS_PER_GROUP].set(b_re.reshape(-1))
    tok = lambda n: pl.BlockSpec((1, tm, n), lambda b, i: (b, i, 0))
    full = lambda a: pl.BlockSpec(a.shape, lambda b, i: (0,) * a.ndim)
    args = (wp, sb, w_ba.astype(BF16), w_bb.astype(BF16), w_out.astype(BF16), g2.reshape(1, d),
            wr.astype(BF16), br)
    nsteps = s // tm
    return pl.pallas_call(
        _mix_body,
        grid=(bsz, nsteps),
        in_specs=[tok(d), tok(w), tok(w), tok(w), tok(2 * d),
                  pl.BlockSpec((1, 6, d), lambda b, i: (b, 0, 0))] + [full(a) for a in args],
        out_specs=[tok(d), tok(d // 2 + LANES),
                   pl.BlockSpec((SUBLANES, tm), lambda b, i: (0, b * nsteps + i))],
        out_shape=[jax.ShapeDtypeStruct((bsz, s, d), F32),
                   jax.ShapeDtypeStruct((bsz, s, d // 2 + LANES), I32),
                   jax.ShapeDtypeStruct((SUBLANES, nt), F32)],
        compiler_params=_cparams(("parallel", "parallel")),
        name="mix",
    )(x, u, v, ya, gates, mod, *args)


def _tile_lanes(a, reps):
    return jnp.concatenate([a] * reps, axis=1)


def _dispatch_body(route_ref, lo_ref, hi_ref, pos_ref, blk_ref, cls_ref, run, pstart, ranks, earlier):
    phase = pl.program_id(0)
    i = pl.program_id(1)
    n = pl.num_programs(1)
    tt = route_ref.shape[1]
    reps = tt // LANES
    cls = route_ref[0:1, :]
    cid = lax.broadcasted_iota(I32, (LANES, tt), 0).astype(F32)
    member = cls == cid

    @pl.when(jnp.logical_and(phase == 0, i == 0))
    def _():
        run[...] = jnp.zeros_like(run)
        earlier_tok = lax.broadcasted_iota(I32, (tt, tt), 0) < lax.broadcasted_iota(I32, (tt, tt), 1)
        earlier[...] = jnp.where(earlier_tok, 1.0, 0.0).astype(BF16)

    @pl.when(phase == 0)
    def _():
        onehot = jnp.where(member, 1.0, 0.0)
        before = _dot(onehot.astype(BF16), earlier[...])
        ranks[i] = jnp.sum(onehot * (before + _tile_lanes(run[...], reps)), axis=0, keepdims=True)
        run[...] = run[...] + jnp.sum(onehot, axis=1, keepdims=True)

    @pl.when(jnp.logical_and(phase == 0, i == n - 1))
    def _():
        counts = run[...]
        nblk = jnp.floor((counts + (EXPERT_ROWS - 1.0)) * (1.0 / EXPERT_ROWS))
        hi_part = jnp.floor(nblk * (1.0 / 16.0))
        lo_part = nblk - 16.0 * hi_part
        upto = lax.broadcasted_iota(I32, (LANES, LANES), 1) <= lax.broadcasted_iota(I32, (LANES, LANES), 0)
        upto = jnp.where(upto, 1.0, 0.0).astype(BF16)
        ends = 16.0 * _dot(upto, hi_part.astype(BF16)) + _dot(upto, lo_part.astype(BF16))
        pstart[...] = (ends - nblk) * EXPERT_ROWS
        diag = lax.broadcasted_iota(I32, (LANES, LANES), 0) == lax.broadcasted_iota(I32, (LANES, LANES), 1)
        on_lanes = lambda a: jnp.sum(jnp.where(diag, a, 0.0), axis=0, keepdims=True)
        cls_ref[...] = jnp.concatenate([on_lanes(counts), on_lanes(pstart[...]),
                                        jnp.zeros((SUBLANES - 2, LANES), F32)], axis=0)
        nb_lanes = blk_ref.shape[1]
        breps = nb_lanes // LANES
        n_used = _tile_lanes(ends[LANES - 1:LANES, :], breps)
        b_idx = lax.broadcasted_iota(I32, (LANES, nb_lanes), 1).astype(F32)
        b_idx = jnp.minimum(b_idx, n_used - 1.0)
        b_cls = jnp.sum(jnp.where(_tile_lanes(ends, breps) <= b_idx, 1.0, 0.0), axis=0, keepdims=True)
        b_member = b_cls == lax.broadcasted_iota(I32, (LANES, nb_lanes), 0).astype(F32)
        pick = lambda tab: jnp.sum(jnp.where(b_member, _tile_lanes(tab, breps), 0.0), axis=0, keepdims=True)
        blk_ref[...] = jnp.concatenate([pick(lo_ref[...]), pick(hi_ref[...]), n_used,
                                        jnp.zeros((SUBLANES - 3, nb_lanes), F32)], axis=0)

    @pl.when(phase == 1)
    def _():
        base = jnp.sum(jnp.where(member, _tile_lanes(pstart[...], reps), 0.0), axis=0, keepdims=True)
        pos_ref[0] = (base + ranks[i]).astype(I32)


def _dispatch(route, n_blocks, tt=1024):
    nt = route.shape[1]
    n = nt // tt
    nb_lanes = pl.cdiv(n_blocks, LANES) * LANES
    lo_tab, hi_tab = [], []
    for g in range(N_GROUPS):
        for a in range(EXPERTS_PER_GROUP):
            for b in range(a + 1, EXPERTS_PER_GROUP):
                lo_tab.append(g * EXPERTS_PER_GROUP + a)
                hi_tab.append(g * EXPERTS_PER_GROUP + b)
    pad = [0] * (LANES - N_CLASSES)
    lo_tile = jnp.broadcast_to(jnp.asarray(lo_tab + pad, F32)[:, None], (LANES, LANES))
    hi_tile = jnp.broadcast_to(jnp.asarray(hi_tab + pad, F32)[:, None], (LANES, LANES))
    const = lambda shape: pl.BlockSpec(shape, lambda p, i: (0, 0))
    pos, blk_tab, cls_tab = pl.pallas_call(
        _dispatch_body,
        grid=(2, n),
        in_specs=[pl.BlockSpec((SUBLANES, tt), lambda p, i: (0, i)),
                  const((LANES, LANES)), const((LANES, LANES))],
        out_specs=[pl.BlockSpec((1, 1, tt), lambda p, i: (p * i, 0, 0)),
                   const((SUBLANES, nb_lanes)), const((SUBLANES, LANES))],
        out_shape=[jax.ShapeDtypeStruct((n, 1, tt), I32),
                   jax.ShapeDtypeStruct((SUBLANES, nb_lanes), F32),
                   jax.ShapeDtypeStruct((SUBLANES, LANES), F32)],
        scratch_shapes=[pltpu.VMEM((LANES, LANES), F32), pltpu.VMEM((LANES, LANES), F32),
                        pltpu.VMEM((n, 1, tt), F32), pltpu.VMEM((tt, tt), BF16)],
        compiler_params=_cparams(("arbitrary", "arbitrary")),
        name="dispatch",
    )(route, lo_tile, hi_tile)
    blk = blk_tab.astype(I32)
    cls = cls_tab.astype(I32)
    return pos.reshape(nt), blk[0, :n_blocks], blk[1, :n_blocks], blk[2, :1], cls[0], cls[1]


ZERO_FILL_ROWS = tuple(1 << k for k in range(EXPERT_ROWS.bit_length() - 1))


def _tile_rows_wait(buf, sem):
    pltpu.make_async_copy(buf, buf, sem).wait()


def _scatter_body(cnt_ref, pst_ref, pos_ref, xp_ref, xs_hbm, ring, zeros, sem, zsem):
    i = pl.program_id(0)
    n = pl.num_programs(0)
    tm, wrow = xp_ref.shape
    half = tm // 2
    n_tiles = wrow // LANES

    @pl.when(i == 0)
    def _():
        ring[...] = jnp.zeros_like(ring)

    for h in range(2):
        @pl.when(i > 0)
        def _():
            _tile_rows_wait(ring.at[h], sem.at[h])

        for c in range(n_tiles):
            ring.at[h][pl.ds(c, half, stride=SUBLANES), :] = xp_ref[h * half:(h + 1) * half,
                                                                    c * LANES:(c + 1) * LANES]
        for r in range(half):
            pltpu.make_async_copy(ring.at[h, pl.ds(r * SUBLANES, SUBLANES)],
                                  xs_hbm.at[pos_ref[0, 0, h * half + r]], sem.at[h]).start()

    @pl.when(i == n - 1)
    def _():
        _tile_rows_wait(ring.at[0], sem.at[0])
        _tile_rows_wait(ring.at[1], sem.at[1])
        zeros[...] = jnp.zeros_like(zeros)

        def fill(wait):
            def per_class(k, carry):
                cnt = cnt_ref[k]
                n_pad = (-cnt) & (EXPERT_ROWS - 1)
                off = pst_ref[k] + cnt
                for size in ZERO_FILL_ROWS:
                    piece = pltpu.make_async_copy(zeros.at[pl.ds(0, size)], xs_hbm.at[pl.ds(off, size)], zsem)

                    @pl.when((n_pad & size) != 0)
                    def _():
                        piece.wait() if wait else piece.start()

                    off = off + (n_pad & size)
                return carry

            lax.fori_loop(0, N_CLASSES, per_class, 0)

            last = N_CLASSES - 1
            used_rows = pst_ref[last] + cnt_ref[last] + ((-cnt_ref[last]) & (EXPERT_ROWS - 1))
            tail = ZERO_FILL_ROWS[-1]

            def per_piece(j, carry):
                piece = pltpu.make_async_copy(zeros, xs_hbm.at[pl.ds(used_rows + j * tail, tail)], zsem)
                piece.wait() if wait else piece.start()
                return carry

            lax.fori_loop(0, (xs_hbm.shape[0] - used_rows) // tail, per_piece, 0)

        fill(False)
        fill(True)


def _scatter_rows(xp, pos, counts, pstarts, n_blocks, tm=512):
    nt, wrow = xp.shape
    n = nt // tm
    grid_spec = pltpu.PrefetchScalarGridSpec(
        num_scalar_prefetch=2,
        grid=(n,),
        in_specs=[pl.BlockSpec((1, 1, tm), lambda i, c, p: (i, 0, 0), memory_space=pltpu.SMEM),
                  pl.BlockSpec((tm, wrow), lambda i, c, p: (i, 0))],
        out_specs=pl.BlockSpec(memory_space=pl.ANY),
        scratch_shapes=[pltpu.VMEM((2, tm // 2 * SUBLANES, LANES), I32),
                        pltpu.VMEM((ZERO_FILL_ROWS[-1], SUBLANES, LANES), I32),
                        pltpu.SemaphoreType.DMA((2,)), pltpu.SemaphoreType.DMA(())],
    )
    return pl.pallas_call(
        _scatter_body,
        grid_spec=grid_spec,
        out_shape=jax.ShapeDtypeStruct((n_blocks * EXPERT_ROWS, SUBLANES, LANES), I32),
        compiler_params=_cparams(("arbitrary",)),
        name="scatter_rows",
    )(counts, pstarts, pos.reshape(n, 1, tm), xp)


def _unpack_rows(xp):
    left = lax.bitcast_convert_type(lax.shift_left(xp, 16), F32).astype(BF16)
    right = lax.bitcast_convert_type(xp & jnp.int32(-65536), F32).astype(BF16)
    return jnp.concatenate([left, right], axis=1)


def _experts_body(elo_ref, ehi_ref, nu_ref, xs_ref, wa_ref, wb_ref, o_ref):
    i = pl.program_id(0)
    rows = xs_ref.shape[0] // SUBLANES
    d = wa_ref.shape[2]
    ff = wa_ref.shape[1] - d
    n_packed = d // 2 // LANES
    lane_tile = lambda ref, c: ref[pl.ds(c, rows, stride=SUBLANES), :]

    @pl.when(i < nu_ref[0])
    def _():
        x = _unpack_rows(jnp.concatenate([lane_tile(xs_ref, c) for c in range(n_packed)], axis=1))
        info = lax.bitcast_convert_type(lane_tile(xs_ref, n_packed), F32)

        def ffn(w_ref, weight):
            ab = _dot(x, w_ref[0, :d, :])
            a = ab[:, :ff]
            hid = (a * jax.nn.sigmoid(a)) * ab[:, ff:] * weight
            return _dot(hid.astype(BF16), w_ref[0, d:, :])

        y = ffn(wa_ref, info[:, 1:2]) + ffn(wb_ref, info[:, 2:3])
        for c in range(d // LANES):
            o_ref[pl.ds(c, rows, stride=SUBLANES), :] = y[:, c * LANES:(c + 1) * LANES]

    @pl.when(i >= nu_ref[0])
    def _():
        o_ref[...] = jnp.zeros_like(o_ref)


def _experts(xs, e_lo, e_hi, n_used, w1, w3, w2):
    n_rows = xs.shape[0]
    n_blocks = n_rows // EXPERT_ROWS
    n_exp, d, ff = w1.shape
    assert 2 * ff == d and d == SUBLANES * LANES
    wcat = jnp.concatenate([jnp.concatenate([w1, w3], axis=2), w2], axis=1).astype(BF16)
    blk = EXPERT_ROWS * SUBLANES
    grid_spec = pltpu.PrefetchScalarGridSpec(
        num_scalar_prefetch=3,
        grid=(n_blocks,),
        in_specs=[pl.BlockSpec((blk, LANES), lambda i, lo, hi, nu: (jnp.minimum(i, nu[0] - 1), 0)),
                  pl.BlockSpec((1, d + ff, d), lambda i, lo, hi, nu: (lo[i], 0, 0)),
                  pl.BlockSpec((1, d + ff, d), lambda i, lo, hi, nu: (hi[i], 0, 0))],
        out_specs=pl.BlockSpec((blk, LANES), lambda i, lo, hi, nu: (i, 0)),
    )
    y = pl.pallas_call(
        _experts_body,
        grid_spec=grid_spec,
        out_shape=jax.ShapeDtypeStruct((n_rows * SUBLANES, LANES), F32),
        compiler_params=_cparams(("arbitrary",)),
        name="experts",
    )(e_lo, e_hi, n_used, xs.reshape(n_rows * SUBLANES, LANES), wcat, wcat)
    return y.reshape(n_rows, SUBLANES, LANES)


def _final_body(posc_ref, posn_ref, y_hbm, h1_ref, mod_ref, gf_ref, o_ref, ybuf, sem):
    i = pl.program_id(0)
    n = pl.num_programs(0)
    tm = h1_ref.shape[0]
    half = tm // 2

    def fetch(idx_ref, h):
        for r in range(half):
            pltpu.make_async_copy(y_hbm.at[idx_ref[0, 0, h * half + r]],
                                  ybuf.at[h, pl.ds(r * SUBLANES, SUBLANES)], sem.at[h]).start()

    @pl.when(i == 0)
    def _():
        fetch(posc_ref, 0)
        fetch(posc_ref, 1)

    for h in range(2):
        rows = slice(h * half, (h + 1) * half)
        _tile_rows_wait(ybuf.at[h], sem.at[h])
        y = jnp.concatenate([ybuf.at[h][pl.ds(c, half, stride=SUBLANES), :] for c in range(SUBLANES)], axis=1)
        hres = h1_ref[rows, :] + mod_ref[0, 5:6, :] * y
        ms = jnp.mean(hres * hres, axis=-1, keepdims=True)
        o_ref[rows, :] = hres * lax.rsqrt(ms + RMS_EPS) * gf_ref[...]
        fetch(posn_ref, h)

    @pl.when(i == n - 1)
    def _():
        _tile_rows_wait(ybuf.at[0], sem.at[0])
        _tile_rows_wait(ybuf.at[1], sem.at[1])


def _final(pos, y_sorted, h1, mod, gf, seq, tm):
    nt, d = h1.shape
    n = nt // tm
    pos3 = pos.reshape(n, 1, tm)
    idx_spec = lambda f: pl.BlockSpec((1, 1, tm), f, memory_space=pltpu.SMEM)
    return pl.pallas_call(
        _final_body,
        grid=(n,),
        in_specs=[idx_spec(lambda i: (i, 0, 0)),
                  idx_spec(lambda i: (jnp.minimum(i + 1, n - 1), 0, 0)),
                  pl.BlockSpec(memory_space=pl.ANY),
                  pl.BlockSpec((tm, d), lambda i: (i, 0)),
                  pl.BlockSpec((1, 6, d), lambda i: (i * tm // seq, 0, 0)),
                  pl.BlockSpec((1, d), lambda i: (0, 0))],
        out_specs=pl.BlockSpec((tm, d), lambda i: (i, 0)),
        out_shape=jax.ShapeDtypeStruct((nt, d), F32),
        scratch_shapes=[pltpu.VMEM((2, tm // 2 * SUBLANES, LANES), F32), pltpu.SemaphoreType.DMA((2,))],
        compiler_params=_cparams(("arbitrary",)),
        name="final",
    )(pos3, pos3, y_sorted, h1, mod, gf.reshape(1, d))


def kernel(x, c, w_ada, b_ada, norm1_g, w_in, w_gate, b_gate, ssm_a_re, ssm_a_im, ssm_b_re, ssm_b_im, ssm_c_re, ssm_c_im, ssm_d, ssm_log_step, w_glu, b_glu, sgu_ln_g, sgu_ln_b, sgu_w, sgu_b, w_branch_a, w_branch_b, w_out, norm2_g, w_router_group, b_router_group, w_router_expert, b_router_expert, w1, w3, w2, norm_f_g):
    bsz, seq, d = x.shape
    depth = w_ada.shape[0]
    assert depth == 1 and bsz % SCAN_BATCHES == 0 and seq % 512 == 0
    l = 0
    mod = _adaln(c, w_ada[l], b_ada[l]).reshape(bsz, 6, d)
    za, u, v, gates = _inproj(x, mod, norm1_g[l], w_in[l], w_gate[l], b_gate[l],
                              sgu_ln_g[l], sgu_ln_b[l], tm=512)
    a_row, s5_in, s5_out = _s5_params(ssm_a_re[l], ssm_a_im[l], ssm_b_re[l], ssm_b_im[l],
                                      ssm_c_re[l], ssm_c_im[l], ssm_log_step[l])
    ya = _s5(za, a_row, s5_in, s5_out, ssm_d[l], w_glu[l], b_glu[l])
    h1, xp, route = _mix(x, u, v, ya, gates, mod, sgu_w[l], sgu_b[l], w_branch_a[l], w_branch_b[l],
                         w_out[l], norm2_g[l], w_router_group[l], b_router_group[l],
                         w_router_expert[l], b_router_expert[l], tm=512)
    nt = bsz * seq
    n_blocks = nt // EXPERT_ROWS + N_CLASSES
    pos, e_lo, e_hi, n_used, counts, pstarts = _dispatch(route, n_blocks)
    xs = _scatter_rows(xp.reshape(nt, xp.shape[-1]), pos, counts, pstarts, n_blocks)
    y_sorted = _experts(xs, e_lo, e_hi, n_used, w1[l], w3[l], w2[l])
    out = _final(pos, y_sorted, h1.reshape(nt, d), mod, norm_f_g, seq, tm=512)
    return out.reshape(bsz, seq, d)
```

```python
import functools
import math

import jax
import jax.numpy as jnp
from jax import lax
from jax.experimental import pallas as pl
from jax.experimental.pallas import tpu as pltpu

F32 = jnp.float32
BF16 = jnp.bfloat16
I32 = jnp.int32

LANES = 128
SUBLANES = 8
VMEM_LIMIT = 56 * 1024 * 1024

RMS_EPS = 1e-6
LN_EPS = 1e-5

SSM_GROUP_CH = 16
SSM_STATE = 64
GROUPS_PER_BLOCK = LANES // SSM_GROUP_CH
SGU_HEADS = 8
SGU_CHUNK = 128
N_GROUPS = 4
EXPERTS_PER_GROUP = 8
PAIRS_PER_GROUP = EXPERTS_PER_GROUP * (EXPERTS_PER_GROUP - 1) // 2
N_CLASSES = N_GROUPS * PAIRS_PER_GROUP
EXPERT_ROWS = 256

SCAN_BATCHES = SUBLANES
SCAN_TS = 128
SLAB_PITCH = SCAN_TS + 8
INPROJ_SUBTILE = 256
MIX_SUBTILE = 256


def _gelu(x):
    return 0.5 * x * (1.0 + jnp.tanh(math.sqrt(2.0 / math.pi) * (x + 0.044715 * (x * x * x))))


def _dot(a, b):
    return jnp.dot(a, b, preferred_element_type=F32)


def _split_bf16(a):
    hi = a.astype(BF16)
    lo = (a - hi.astype(F32)).astype(BF16)
    return hi, lo


def _cparams(sem):
    return pltpu.CompilerParams(dimension_semantics=sem, vmem_limit_bytes=VMEM_LIMIT)


def _adaln_body(c_ref, w_ref, b_ref, o_ref):
    c = c_ref[...]
    act = c * jax.nn.sigmoid(c)
    a_hi, a_lo = _split_bf16(act)
    w_hi, w_lo = _split_bf16(w_ref[...])
    o_ref[...] = _dot(a_hi, w_hi) + _dot(a_hi, w_lo) + _dot(a_lo, w_hi) + b_ref[...]


def _adaln(c, w, b):
    bsz, d = c.shape
    n = w.shape[1]
    tn = 1024
    return pl.pallas_call(
        _adaln_body,
        grid=(n // tn,),
        in_specs=[pl.BlockSpec((bsz, d), lambda j: (0, 0)),
                  pl.BlockSpec((d, tn), lambda j: (0, j)),
                  pl.BlockSpec((1, tn), lambda j: (0, j))],
        out_specs=pl.BlockSpec((bsz, tn), lambda j: (0, j)),
        out_shape=jax.ShapeDtypeStruct((bsz, n), F32),
        compiler_params=_cparams(("arbitrary",)),
        name="adaln",
    )(c, w, b.reshape(1, n))


def _inproj_body(x_ref, mod_ref, g1_ref, win_ref, wgate_ref, bgate_ref, lng_ref, lnb_ref,
                 w1_ref, w3_ref, w2_ref, za_ref, u_ref, v_ref, gates_ref, w13_ref, w2b_ref):
    w = za_ref.shape[-1]
    tm = x_ref.shape[1]
    ff = w1_ref.shape[2]
    w13_ref[0, :, :ff] = w1_ref[0].astype(BF16)
    w13_ref[0, :, ff:] = w3_ref[0].astype(BF16)
    w2b_ref[0] = w2_ref[0].astype(BF16)
    for r0 in range(0, tm, INPROJ_SUBTILE):
        rs = slice(r0, r0 + INPROJ_SUBTILE)
        x = x_ref[0, rs, :]
        ms = jnp.mean(x * x, axis=-1, keepdims=True)
        xn = x * lax.rsqrt(ms + RMS_EPS) * g1_ref[...]
        xn = xn * (1.0 + mod_ref[0, 1:2, :]) + mod_ref[0, 0:1, :]
        xb = xn.astype(BF16)
        proj = _dot(xb, win_ref[...])
        za_ref[0, rs, :] = proj[:, :w]
        u_ref[0, rs, :] = _gelu(proj[:, w:2 * w]).astype(BF16)
        gv = _gelu(proj[:, 2 * w:])
        mu = jnp.mean(gv, axis=-1, keepdims=True)
        cen = gv - mu
        var = jnp.mean(cen * cen, axis=-1, keepdims=True)
        v_ref[0, rs, :] = (cen * lax.rsqrt(var + LN_EPS) * lng_ref[...] + lnb_ref[...]).astype(BF16)
        gates_ref[0, rs, :] = jax.nn.sigmoid(_dot(xb, wgate_ref[...]) + bgate_ref[...]).astype(BF16)


def _inproj(x, mod, g1, w_in, w_gate, b_gate, ln_g, ln_b, w1, w3, w2, tm):
    bsz, s, d = x.shape
    w = w_in.shape[1] // 3
    ng = w_gate.shape[1]
    n_exp, _, ff = w1.shape
    nsteps = s // tm
    parts = bsz * nsteps // n_exp
    assert bsz * nsteps == n_exp * parts and d % (parts * SUBLANES) == 0 and ff % (parts * SUBLANES) == 0
    tok = lambda n: pl.BlockSpec((1, tm, n), lambda b, i: (b, i, 0))
    full = lambda a: pl.BlockSpec(a.shape, lambda b, i: (0,) * a.ndim)
    wslice = lambda rows, cols: pl.BlockSpec(
        (1, rows // parts, cols), lambda b, i: ((b * nsteps + i) // parts, (b * nsteps + i) % parts, 0))
    args = (g1.reshape(1, d), w_in.astype(BF16), w_gate.astype(BF16), b_gate.reshape(1, ng),
            ln_g.reshape(1, w), ln_b.reshape(1, w))
    return pl.pallas_call(
        _inproj_body,
        grid=(bsz, nsteps),
        in_specs=[tok(d), pl.BlockSpec((1, 6, d), lambda b, i: (b, 0, 0))] + [full(a) for a in args]
                 + [wslice(d, ff), wslice(d, ff), wslice(ff, d)],
        out_specs=[tok(w), tok(w), tok(w), tok(ng), wslice(d, 2 * ff), wslice(ff, d)],
        out_shape=[jax.ShapeDtypeStruct((bsz, s, w), F32),
                   jax.ShapeDtypeStruct((bsz, s, w), BF16),
                   jax.ShapeDtypeStruct((bsz, s, w), BF16),
                   jax.ShapeDtypeStruct((bsz, s, ng), BF16),
                   jax.ShapeDtypeStruct((n_exp, d, 2 * ff), BF16),
                   jax.ShapeDtypeStruct((n_exp, ff, d), BF16)],
        compiler_params=_cparams(("parallel", "parallel")),
        name="in_proj",
    )(x, mod, *args, w1, w3, w2)


def _s5_params(a_re, a_im, b_re, b_im, c_re, c_im, log_step):
    g, n = a_re.shape
    nblk = g // GROUPS_PER_BLOCK
    lam_re = jnp.minimum(a_re.astype(F32), -1e-4)
    lam_im = a_im.astype(F32)
    dt = jnp.exp(log_step.astype(F32))[:, None]
    mag = jnp.exp(lam_re * dt)
    ab_re = mag * jnp.cos(lam_im * dt)
    ab_im = mag * jnp.sin(lam_im * dt)
    den = lam_re * lam_re + lam_im * lam_im
    nr = ab_re - 1.0
    f_re = (nr * lam_re + ab_im * lam_im) / den
    f_im = (ab_im * lam_re - nr * lam_im) / den
    bt_re = b_re.astype(F32).transpose(0, 2, 1)
    bt_im = b_im.astype(F32).transpose(0, 2, 1)
    w_re = f_re[:, None, :] * bt_re - f_im[:, None, :] * bt_im
    w_im = f_re[:, None, :] * bt_im + f_im[:, None, :] * bt_re
    eye = jnp.eye(GROUPS_PER_BLOCK, dtype=F32)

    def bdiag_in(wt):
        t = wt.reshape(nblk, GROUPS_PER_BLOCK, SSM_GROUP_CH, 1, n) * eye[None, :, None, :, None]
        return t.reshape(nblk, GROUPS_PER_BLOCK * SSM_GROUP_CH, GROUPS_PER_BLOCK * n)

    def bdiag_out(ct):
        t = ct.transpose(0, 2, 1).reshape(nblk, GROUPS_PER_BLOCK, n, 1, SSM_GROUP_CH)
        t = t * eye[None, :, None, :, None]
        return t.reshape(nblk, GROUPS_PER_BLOCK * n, GROUPS_PER_BLOCK * SSM_GROUP_CH)

    w_in = jnp.concatenate([bdiag_in(w_re), bdiag_in(w_im)], axis=2).astype(BF16)
    w_out = jnp.concatenate([bdiag_out(c_re.astype(F32)), -bdiag_out(c_im.astype(F32))],
                            axis=1).astype(BF16)
    a_row = jnp.concatenate([ab_re.reshape(nblk, -1), ab_im.reshape(nblk, -1)], axis=1).reshape(1, -1)
    return a_row, w_in, w_out


def _s5_body(z_ref, a_ref, win_ref, wout_ref, d_ref, wglu_ref, bglu_ref, o_ref,
             zslab, zsb, xs, hstate, yslab):
    nb, ts, w = z_ref.shape
    rows = nb * ts
    nblk = w // LANES
    half = GROUPS_PER_BLOCK * SSM_STATE
    sw = 2 * half

    @pl.when(pl.program_id(1) == 0)
    def _():
        hstate[...] = jnp.zeros_like(hstate)

    for b in range(nb):
        for c in range(nblk):
            zslab[c, b * SLAB_PITCH:b * SLAB_PITCH + ts, :] = z_ref[b, :, c * LANES:(c + 1) * LANES]

    for s in range(ts):
        for c in range(nblk):
            zsb[s * nb:(s + 1) * nb, c * LANES:(c + 1) * LANES] = zslab.at[c][pl.ds(s, nb, stride=SLAB_PITCH), :]

    def in_map(j):
        xs[:, j * sw:(j + 1) * sw] = _dot(zsb[:, j * LANES:(j + 1) * LANES].astype(BF16), win_ref[j])

    def scan(j):
        re_sl = slice(j * sw, j * sw + half)
        im_sl = slice(j * sw + half, (j + 1) * sw)
        a_r = jnp.broadcast_to(a_ref[:, re_sl], (nb, half))
        a_i = jnp.broadcast_to(a_ref[:, im_sl], (nb, half))
        h_r, h_i = hstate[:, re_sl], hstate[:, im_sl]
        for s in range(ts):
            x_r = xs[s * nb:(s + 1) * nb, re_sl]
            x_i = xs[s * nb:(s + 1) * nb, im_sl]
            h_r, h_i = a_r * h_r - a_i * h_i + x_r, a_r * h_i + a_i * h_r + x_i
            xs[s * nb:(s + 1) * nb, re_sl] = h_r
            xs[s * nb:(s + 1) * nb, im_sl] = h_i
        hstate[:, re_sl] = h_r
        hstate[:, im_sl] = h_i

    def out_map(j):
        y = _dot(xs[:, j * sw:(j + 1) * sw].astype(BF16), wout_ref[j])
        y = y + d_ref[:, j * LANES:(j + 1) * LANES] * zsb[:, j * LANES:(j + 1) * LANES]
        return _gelu(y)

    ys = []
    for step in range(nblk + 2):
        if step < nblk:
            in_map(step)
        if 0 <= step - 1 < nblk:
            scan(step - 1)
        if 0 <= step - 2 < nblk:
            ys.append(out_map(step - 2))
    g = jnp.concatenate(ys, axis=1)
    g = g * jax.nn.sigmoid(_dot(g.astype(BF16), wglu_ref[...]) + bglu_ref[...])
    for c in range(nblk):
        yslab[c] = g[:, c * LANES:(c + 1) * LANES]

    for b in range(nb):
        for c in range(nblk):
            o_ref[b, :, c * LANES:(c + 1) * LANES] = yslab.at[c][pl.ds(b, ts, stride=nb), :].astype(BF16)


def _s5(za, a_row, w_in, w_out, d_skip, w_glu, b_glu):
    bsz, s, w = za.shape
    nb, ts = SCAN_BATCHES, SCAN_TS
    rows = nb * ts
    nblk = w // LANES
    nstate = a_row.shape[1]
    full = lambda a: pl.BlockSpec(a.shape, lambda b, i: (0,) * a.ndim)
    args = (a_row, w_in, w_out, d_skip.reshape(1, w).astype(F32), w_glu.astype(BF16), b_glu.reshape(1, w))
    return pl.pallas_call(
        _s5_body,
        grid=(bsz // nb, s // ts),
        in_specs=[pl.BlockSpec((nb, ts, w), lambda b, i: (b, i, 0))] + [full(a) for a in args],
        out_specs=pl.BlockSpec((nb, ts, w), lambda b, i: (b, i, 0)),
        out_shape=jax.ShapeDtypeStruct((bsz, s, w), BF16),
        scratch_shapes=[pltpu.VMEM((nblk, nb * SLAB_PITCH, LANES), F32),
                        pltpu.VMEM((rows, w), F32),
                        pltpu.VMEM((rows, nstate), F32),
                        pltpu.VMEM((nb, nstate), F32),
                        pltpu.VMEM((nblk, rows, LANES), F32)],
        compiler_params=_cparams(("parallel", "arbitrary")),
        name="s5",
    )(za, *args)


def _mix_body(x_ref, u_ref, v_ref, ya_ref, gates_ref, mod_ref, wp_ref, sb_ref, wba_ref, wbb_ref,
              wout_ref, g2_ref, wr_ref, br_ref, h1_ref, xp_ref, route_ref):
    tm_full = x_ref.shape[1]
    npair = u_ref.shape[2] // LANES
    hd = LANES // 2

    t_idx = lax.broadcasted_iota(I32, (SGU_CHUNK, 2 * SGU_CHUNK), 0)
    s_idx = lax.broadcasted_iota(I32, (SGU_CHUNK, 2 * SGU_CHUNK), 1) & (SGU_CHUNK - 1)
    causal = s_idx <= t_idx
    lane = lax.broadcasted_iota(I32, (SGU_CHUNK, LANES), 1)
    first_head = lane < hd
    w_causal = [jnp.where(causal, wp_ref[j], jnp.zeros_like(wp_ref[j])) for j in range(npair)]
    for r0 in range(0, tm_full, MIX_SUBTILE):
        _mix_rows(slice(r0, r0 + MIX_SUBTILE), w_causal, first_head, x_ref, u_ref, v_ref, ya_ref, gates_ref,
                  mod_ref, sb_ref, wba_ref, wbb_ref, wout_ref, g2_ref, wr_ref, br_ref, h1_ref, xp_ref,
                  route_ref)


def _mix_rows(rs, w_causal, first_head, x_ref, u_ref, v_ref, ya_ref, gates_ref, mod_ref, sb_ref, wba_ref,
              wbb_ref, wout_ref, g2_ref, wr_ref, br_ref, h1_ref, xp_ref, route_ref):
    tm = rs.stop - rs.start
    d = x_ref.shape[2]
    npair = len(w_causal)
    mixed_rows = []
    for c in range(rs.start, rs.stop, SGU_CHUNK):
        vc = v_ref[0, c:c + SGU_CHUNK, :]
        blocks = []
        for j in range(npair):
            vb = vc[:, j * LANES:(j + 1) * LANES]
            zero = jnp.zeros_like(vb)
            rhs = jnp.concatenate([jnp.where(first_head, vb, zero), jnp.where(first_head, zero, vb)], axis=0)
            blocks.append(_dot(w_causal[j], rhs))
        mixed_rows.append(jnp.concatenate(blocks, axis=1) + sb_ref[...])
    mixed = jnp.concatenate(mixed_rows, axis=0)
    yb = (u_ref[0, rs, :].astype(F32) * mixed).astype(BF16)

    pa = _dot(ya_ref[0, rs, :], wba_ref[...])
    pb = _dot(yb, wbb_ref[...])
    merged = gates_ref[0, rs, :d].astype(F32) * pa + gates_ref[0, rs, d:].astype(F32) * pb
    o = _dot(merged.astype(BF16), wout_ref[...])
    h1 = x_ref[0, rs, :] + mod_ref[0, 2:3, :] * o
    h1_ref[0, rs, :] = h1

    ms = jnp.mean(h1 * h1, axis=-1, keepdims=True)
    xn = h1 * lax.rsqrt(ms + RMS_EPS) * g2_ref[...]
    xn = xn * (1.0 + mod_ref[0, 4:5, :]) + mod_ref[0, 3:4, :]
    x_hi = xn.astype(BF16)

    logits = _dot(x_hi, wr_ref[...]) + br_ref[...]
    lt = logits.T
    best = lt[0:1, :]
    grp = jnp.zeros((1, tm), I32)
    for gi in range(1, N_GROUPS):
        better = lt[gi:gi + 1, :] > best
        grp = jnp.where(better, gi, grp)
        best = jnp.where(better, lt[gi:gi + 1, :], best)
    den = jnp.zeros((1, tm), F32)
    for gi in range(N_GROUPS):
        den = den + jnp.exp(lt[gi:gi + 1, :] - best)
    pg = 1.0 / den
    le = lt[SUBLANES:2 * SUBLANES, :]
    for gi in range(1, N_GROUPS):
        le = jnp.where(grp == gi, lt[SUBLANES * (gi + 1):SUBLANES * (gi + 2), :], le)
    eidx = lax.broadcasted_iota(I32, (EXPERTS_PER_GROUP, tm), 0).astype(F32)
    none = float(EXPERTS_PER_GROUP)
    v1 = jnp.max(le, axis=0, keepdims=True)
    i1 = jnp.min(jnp.where(le == v1, eidx, none), axis=0, keepdims=True)
    rest = jnp.where(eidx == i1, -jnp.inf, le)
    v2 = jnp.max(rest, axis=0, keepdims=True)
    i2 = jnp.min(jnp.where(rest == v2, eidx, none), axis=0, keepdims=True)
    ex = jnp.exp(v2 - v1)
    p1 = 1.0 / (1.0 + ex)
    wt1 = pg * p1
    wt2 = pg * (ex * p1)
    first_lo = i1 < i2
    lo = jnp.where(first_lo, i1, i2)
    hi = jnp.where(first_lo, i2, i1)
    pair = lo * (2 * EXPERTS_PER_GROUP - 1.0 - lo) * 0.5 + (hi - lo - 1.0)
    cls = grp.astype(F32) * PAIRS_PER_GROUP + pair
    info = jnp.concatenate([cls, jnp.where(first_lo, wt1, wt2), jnp.where(first_lo, wt2, wt1)], axis=0)
    route_ref[:, rs] = jnp.concatenate([info, jnp.zeros((SUBLANES - 3, tm), F32)], axis=0)

    bits = lax.bitcast_convert_type(x_hi.astype(F32), I32)
    packed = lax.shift_right_logical(bits[:, :d // 2], 16) | (bits[:, d // 2:] & jnp.int32(-65536))
    info_cols = jnp.concatenate([info, jnp.zeros((LANES - 3, tm), F32)], axis=0).T
    xp_ref[0, rs, :] = jnp.concatenate([packed, lax.bitcast_convert_type(info_cols, I32)], axis=1)


def _mix(x, u, v, ya, gates, mod, sgu_w, sgu_b, w_ba, w_bb, w_out, g2, w_rg, b_rg, w_re, b_re, tm):
    bsz, s, d = x.shape
    w = u.shape[2]
    nt = bsz * s
    wp = sgu_w.reshape(SGU_HEADS // 2, 2, SGU_CHUNK, SGU_CHUNK).transpose(0, 2, 1, 3)
    wp = wp.reshape(SGU_HEADS // 2, SGU_CHUNK, 2 * SGU_CHUNK).astype(BF16)
    sb = jnp.repeat(sgu_b.T, w // SGU_HEADS, axis=1).astype(F32)
    wr = jnp.zeros((d, LANES), F32)
    wr = wr.at[:, :N_GROUPS].set(w_rg)
    wr = wr.at[:, SUBLANES:SUBLANES + N_GROUPS * EXPERTS_PER_GROUP].set(
        w_re.transpose(1, 0, 2).reshape(d, N_GROUPS * EXPERTS_PER_GROUP))
    br = jnp.zeros((1, LANES), F32)
    br = br.at[0, :N_GROUPS].set(b_rg)
    br = br.at[0, SUBLANES:SUBLANES + N_GROUPS * EXPERTS_PER_GROUP].set(b_re.reshape(-1))
    tok = lambda n: pl.BlockSpec((1, tm, n), lambda b, i: (b, i, 0))
    full = lambda a: pl.BlockSpec(a.shape, lambda b, i: (0,) * a.ndim)
    args = (wp, sb, w_ba.astype(BF16), w_bb.astype(BF16), w_out.astype(BF16), g2.reshape(1, d),
            wr.astype(BF16), br)
    nsteps = s // tm
    return pl.pallas_call(
        _mix_body,
        grid=(bsz, nsteps),
        in_specs=[tok(d), tok(w), tok(w), tok(w), tok(2 * d),
                  pl.BlockSpec((1, 6, d), lambda b, i: (b, 0, 0))] + [full(a) for a in args],
        out_specs=[tok(d), tok(d // 2 + LANES),
                   pl.BlockSpec((SUBLANES, tm), lambda b, i: (0, b * nsteps + i))],
        out_shape=[jax.ShapeDtypeStruct((bsz, s, d), F32),
                   jax.ShapeDtypeStruct((bsz, s, d // 2 + LANES), I32),
                   jax.ShapeDtypeStruct((SUBLANES, nt), F32)],
        compiler_params=_cparams(("parallel", "parallel")),
        name="mix",
    )(x, u, v, ya, gates, mod, *args)


def _tile_lanes(a, reps):
    return jnp.concatenate([a] * reps, axis=1)


def _dispatch_body(route_ref, lo_ref, hi_ref, pos_ref, blk_ref, cls_ref, run, pstart, ranks, earlier):
    phase = pl.program_id(0)
    i = pl.program_id(1)
    n = pl.num_programs(1)
    tt = route_ref.shape[1]
    reps = tt // LANES
    cls = route_ref[0:1, :]
    cid = lax.broadcasted_iota(I32, (LANES, tt), 0).astype(F32)
    member = cls == cid

    @pl.when(jnp.logical_and(phase == 0, i == 0))
    def _():
        run[...] = jnp.zeros_like(run)
        earlier_tok = lax.broadcasted_iota(I32, (tt, tt), 0) < lax.broadcasted_iota(I32, (tt, tt), 1)
        earlier[...] = jnp.where(earlier_tok, 1.0, 0.0).astype(BF16)

    @pl.when(phase == 0)
    def _():
        onehot = jnp.where(member, 1.0, 0.0)
        before = _dot(onehot.astype(BF16), earlier[...])
        ranks[i] = jnp.sum(onehot * (before + _tile_lanes(run[...], reps)), axis=0, keepdims=True)
        run[...] = run[...] + jnp.sum(onehot, axis=1, keepdims=True)

    @pl.when(jnp.logical_and(phase == 0, i == n - 1))
    def _():
        counts = run[...]
        nblk = jnp.floor((counts + (EXPERT_ROWS - 1.0)) * (1.0 / EXPERT_ROWS))
        hi_part = jnp.floor(nblk * (1.0 / 16.0))
        lo_part = nblk - 16.0 * hi_part
        upto = lax.broadcasted_iota(I32, (LANES, LANES), 1) <= lax.broadcasted_iota(I32, (LANES, LANES), 0)
        upto = jnp.where(upto, 1.0, 0.0).astype(BF16)
        ends = 16.0 * _dot(upto, hi_part.astype(BF16)) + _dot(upto, lo_part.astype(BF16))
        pstart[...] = (ends - nblk) * EXPERT_ROWS
        diag = lax.broadcasted_iota(I32, (LANES, LANES), 0) == lax.broadcasted_iota(I32, (LANES, LANES), 1)
        on_lanes = lambda a: jnp.sum(jnp.where(diag, a, 0.0), axis=0, keepdims=True)
        cls_ref[...] = jnp.concatenate([on_lanes(counts), on_lanes(pstart[...]),
                                        jnp.zeros((SUBLANES - 2, LANES), F32)], axis=0)
        nb_lanes = blk_ref.shape[1]
        breps = nb_lanes // LANES
        n_used = _tile_lanes(ends[LANES - 1:LANES, :], breps)
        b_idx = lax.broadcasted_iota(I32, (LANES, nb_lanes), 1).astype(F32)
        b_idx = jnp.minimum(b_idx, n_used - 1.0)
        b_cls = jnp.sum(jnp.where(_tile_lanes(ends, breps) <= b_idx, 1.0, 0.0), axis=0, keepdims=True)
        b_member = b_cls == lax.broadcasted_iota(I32, (LANES, nb_lanes), 0).astype(F32)
        pick = lambda tab: jnp.sum(jnp.where(b_member, _tile_lanes(tab, breps), 0.0), axis=0, keepdims=True)
        blk_ref[...] = jnp.concatenate([pick(lo_ref[...]), pick(hi_ref[...]), n_used,
                                        jnp.zeros((SUBLANES - 3, nb_lanes), F32)], axis=0)

    @pl.when(phase == 1)
    def _():
        base = jnp.sum(jnp.where(member, _tile_lanes(pstart[...], reps), 0.0), axis=0, keepdims=True)
        pos_ref[0] = (base + ranks[i]).astype(I32)


def _dispatch(route, n_blocks, tt=1024):
    nt = route.shape[1]
    n = nt // tt
    nb_lanes = pl.cdiv(n_blocks, LANES) * LANES
    lo_tab, hi_tab = [], []
    for g in range(N_GROUPS):
        for a in range(EXPERTS_PER_GROUP):
            for b in range(a + 1, EXPERTS_PER_GROUP):
                lo_tab.append(g * EXPERTS_PER_GROUP + a)
                hi_tab.append(g * EXPERTS_PER_GROUP + b)
    pad = [0] * (LANES - N_CLASSES)
    lo_tile = jnp.broadcast_to(jnp.asarray(lo_tab + pad, F32)[:, None], (LANES, LANES))
    hi_tile = jnp.broadcast_to(jnp.asarray(hi_tab + pad, F32)[:, None], (LANES, LANES))
    const = lambda shape: pl.BlockSpec(shape, lambda p, i: (0, 0))
    pos, blk_tab, cls_tab = pl.pallas_call(
        _dispatch_body,
        grid=(2, n),
        in_specs=[pl.BlockSpec((SUBLANES, tt), lambda p, i: (0, i)),
                  const((LANES, LANES)), const((LANES, LANES))],
        out_specs=[pl.BlockSpec((1, 1, tt), lambda p, i: (p * i, 0, 0)),
                   const((SUBLANES, nb_lanes)), const((SUBLANES, LANES))],
        out_shape=[jax.ShapeDtypeStruct((n, 1, tt), I32),
                   jax.ShapeDtypeStruct((SUBLANES, nb_lanes), F32),
                   jax.ShapeDtypeStruct((SUBLANES, LANES), F32)],
        scratch_shapes=[pltpu.VMEM((LANES, LANES), F32), pltpu.VMEM((LANES, LANES), F32),
                        pltpu.VMEM((n, 1, tt), F32), pltpu.VMEM((tt, tt), BF16)],
        compiler_params=_cparams(("arbitrary", "arbitrary")),
        name="dispatch",
    )(route, lo_tile, hi_tile)
    blk = blk_tab.astype(I32)
    cls = cls_tab.astype(I32)
    return pos.reshape(nt), blk[0, :n_blocks], blk[1, :n_blocks], blk[2, :1], cls[0], cls[1]


ZERO_FILL_ROWS = tuple(SUBLANES << k for k in range((EXPERT_ROWS // SUBLANES).bit_length() - 1))


def _scatter_body(cnt_ref, pst_ref, pos_ref, xp_ref, xs_hbm, ring, zeros, sem, zsem):
    i = pl.program_id(0)
    n = pl.num_programs(0)
    tm = xp_ref.shape[0]
    half = tm // 2

    def rows_done(h):
        pltpu.make_async_copy(ring.at[pl.ds(h * half, half)], xs_hbm.at[pl.ds(0, half)], sem.at[h]).wait()

    for h in range(2):
        @pl.when(i > 0)
        def _():
            rows_done(h)

        ring[h * half:(h + 1) * half, :] = xp_ref[h * half:(h + 1) * half, :]
        for r in range(h * half, (h + 1) * half):
            pltpu.make_async_copy(ring.at[pl.ds(r, 1)], xs_hbm.at[pl.ds(pos_ref[0, 0, r], 1)],
                                  sem.at[h]).start()

    @pl.when(i == n - 1)
    def _():
        rows_done(0)
        rows_done(1)
        zeros[...] = jnp.zeros_like(zeros)

        def fill(wait):
            def per_class(k, carry):
                cnt = cnt_ref[k]
                n_pad = (-cnt) & (EXPERT_ROWS - 1)
                off = pst_ref[k] + cnt
                for r in range(SUBLANES - 1):
                    row = pltpu.make_async_copy(zeros.at[pl.ds(0, 1)], xs_hbm.at[pl.ds(off + r, 1)], zsem)

                    @pl.when(r < (n_pad & (SUBLANES - 1)))
                    def _():
                        row.wait() if wait else row.start()

                off = off + (n_pad & (SUBLANES - 1))
                for size in ZERO_FILL_ROWS:
                    piece = pltpu.make_async_copy(zeros.at[pl.ds(0, size)],
                                                  xs_hbm.at[pl.ds(pl.multiple_of(off, size), size)], zsem)

                    @pl.when((n_pad & size) != 0)
                    def _():
                        piece.wait() if wait else piece.start()

                    off = off + (n_pad & size)
                return carry

            lax.fori_loop(0, N_CLASSES, per_class, 0)

            last = N_CLASSES - 1
            used_rows = pst_ref[last] + cnt_ref[last] + ((-cnt_ref[last]) & (EXPERT_ROWS - 1))
            tail = ZERO_FILL_ROWS[-1]

            def per_piece(j, carry):
                piece = pltpu.make_async_copy(
                    zeros, xs_hbm.at[pl.ds(pl.multiple_of(used_rows + j * tail, tail), tail)], zsem)
                piece.wait() if wait else piece.start()
                return carry

            lax.fori_loop(0, (xs_hbm.shape[0] - used_rows) // tail, per_piece, 0)

        fill(False)
        fill(True)


def _scatter_rows(xp, pos, counts, pstarts, n_blocks, tm=512):
    nt, wrow = xp.shape
    n = nt // tm
    grid_spec = pltpu.PrefetchScalarGridSpec(
        num_scalar_prefetch=2,
        grid=(n,),
        in_specs=[pl.BlockSpec((1, 1, tm), lambda i, c, p: (i, 0, 0), memory_space=pltpu.SMEM),
                  pl.BlockSpec((tm, wrow), lambda i, c, p: (i, 0))],
        out_specs=pl.BlockSpec(memory_space=pl.ANY),
        scratch_shapes=[pltpu.VMEM((tm, wrow), I32), pltpu.VMEM((ZERO_FILL_ROWS[-1], wrow), I32),
                        pltpu.SemaphoreType.DMA((2,)), pltpu.SemaphoreType.DMA(())],
    )
    return pl.pallas_call(
        _scatter_body,
        grid_spec=grid_spec,
        out_shape=jax.ShapeDtypeStruct((n_blocks * EXPERT_ROWS, wrow), I32),
        compiler_params=_cparams(("arbitrary",)),
        name="scatter_rows",
    )(counts, pstarts, pos.reshape(n, 1, tm), xp)


def _wait_rows(n, src_hbm, dst, sem):
    pltpu.make_async_copy(src_hbm.at[pl.ds(0, n)], dst, sem).wait()


def _unpack_rows(xp):
    left = lax.bitcast_convert_type(lax.shift_left(xp, 16), F32).astype(BF16)
    right = lax.bitcast_convert_type(xp & jnp.int32(-65536), F32).astype(BF16)
    return jnp.concatenate([left, right], axis=1)


def _experts_body(elo_ref, ehi_ref, nu_ref, xs_ref, w13a_ref, w2a_ref, w13b_ref, w2b_ref, o_ref):
    i = pl.program_id(0)
    dh = xs_ref.shape[1] - LANES
    ff = w2a_ref.shape[1]

    @pl.when(i < nu_ref[0])
    def _():
        x = _unpack_rows(xs_ref[:, :dh])
        info = lax.bitcast_convert_type(xs_ref[:, dh:], F32)

        def ffn(w13_ref, w2_ref, weight):
            ab = _dot(x, w13_ref[0])
            a = ab[:, :ff]
            hid = (a * jax.nn.sigmoid(a)) * ab[:, ff:] * weight
            return _dot(hid.astype(BF16), w2_ref[0])

        o_ref[...] = ffn(w13a_ref, w2a_ref, info[:, 1:2]) + ffn(w13b_ref, w2b_ref, info[:, 2:3])

    @pl.when(i >= nu_ref[0])
    def _():
        o_ref[...] = jnp.zeros_like(o_ref)


def _experts(xs, e_lo, e_hi, n_used, w13, w2):
    n_rows, wrow = xs.shape
    n_blocks = n_rows // EXPERT_ROWS
    n_exp, ff, d = w2.shape
    pick = lambda a, use_hi: pl.BlockSpec(
        (1,) + a.shape[1:],
        (lambda i, lo, hi, nu: (hi[i], 0, 0)) if use_hi else (lambda i, lo, hi, nu: (lo[i], 0, 0)))
    grid_spec = pltpu.PrefetchScalarGridSpec(
        num_scalar_prefetch=3,
        grid=(n_blocks,),
        in_specs=[pl.BlockSpec((EXPERT_ROWS, wrow), lambda i, lo, hi, nu: (jnp.minimum(i, nu[0] - 1), 0)),
                  pick(w13, False), pick(w2, False), pick(w13, True), pick(w2, True)],
        out_specs=pl.BlockSpec((EXPERT_ROWS, d), lambda i, lo, hi, nu: (i, 0)),
    )
    return pl.pallas_call(
        _experts_body,
        grid_spec=grid_spec,
        out_shape=jax.ShapeDtypeStruct((n_rows, d), F32),
        compiler_params=_cparams(("arbitrary",)),
        name="experts",
    )(e_lo, e_hi, n_used, xs, w13, w2, w13, w2)


def _final_body(posc_ref, posn_ref, y_hbm, h1_ref, mod_ref, gf_ref, o_ref, ybuf, sem):
    i = pl.program_id(0)
    n = pl.num_programs(0)
    tm = h1_ref.shape[0]
    half = tm // 2

    def fetch(idx_ref, h):
        for r in range(half):
            pltpu.make_async_copy(y_hbm.at[pl.ds(idx_ref[0, 0, h * half + r], 1)],
                                  ybuf.at[h, pl.ds(r, 1)], sem.at[h]).start()

    @pl.when(i == 0)
    def _():
        fetch(posc_ref, 0)
        fetch(posc_ref, 1)

    for h in range(2):
        rows = slice(h * half, (h + 1) * half)
        _wait_rows(half, y_hbm, ybuf.at[h], sem.at[h])
        hres = h1_ref[rows, :] + mod_ref[0, 5:6, :] * ybuf[h]
        ms = jnp.mean(hres * hres, axis=-1, keepdims=True)
        o_ref[rows, :] = hres * lax.rsqrt(ms + RMS_EPS) * gf_ref[...]
        fetch(posn_ref, h)

    @pl.when(i == n - 1)
    def _():
        _wait_rows(half, y_hbm, ybuf.at[0], sem.at[0])
        _wait_rows(half, y_hbm, ybuf.at[1], sem.at[1])


def _final(pos, y_sorted, h1, mod, gf, seq, tm):
    nt, d = h1.shape
    n = nt // tm
    pos3 = pos.reshape(n, 1, tm)
    idx_spec = lambda f: pl.BlockSpec((1, 1, tm), f, memory_space=pltpu.SMEM)
    return pl.pallas_call(
        _final_body,
        grid=(n,),
        in_specs=[idx_spec(lambda i: (i, 0, 0)),
                  idx_spec(lambda i: (jnp.minimum(i + 1, n - 1), 0, 0)),
                  pl.BlockSpec(memory_space=pl.ANY),
                  pl.BlockSpec((tm, d), lambda i: (i, 0)),
                  pl.BlockSpec((1, 6, d), lambda i: (i * tm // seq, 0, 0)),
                  pl.BlockSpec((1, d), lambda i: (0, 0))],
        out_specs=pl.BlockSpec((tm, d), lambda i: (i, 0)),
        out_shape=jax.ShapeDtypeStruct((nt, d), F32),
        scratch_shapes=[pltpu.VMEM((2, tm // 2, d), F32), pltpu.SemaphoreType.DMA((2,))],
        compiler_params=_cparams(("arbitrary",)),
        name="final",
    )(pos3, pos3, y_sorted, h1, mod, gf.reshape(1, d))


def kernel(x, c, w_ada, b_ada, norm1_g, w_in, w_gate, b_gate, ssm_a_re, ssm_a_im, ssm_b_re, ssm_b_im, ssm_c_re, ssm_c_im, ssm_d, ssm_log_step, w_glu, b_glu, sgu_ln_g, sgu_ln_b, sgu_w, sgu_b, w_branch_a, w_branch_b, w_out, norm2_g, w_router_group, b_router_group, w_router_expert, b_router_expert, w1, w3, w2, norm_f_g):
    bsz, seq, d = x.shape
    depth = w_ada.shape[0]
    assert depth == 1 and bsz % SCAN_BATCHES == 0 and seq % 512 == 0
    l = 0
    mod = _adaln(c, w_ada[l], b_ada[l]).reshape(bsz, 6, d)
    za, u, v, gates, w13_bf, w2_bf = _inproj(x, mod, norm1_g[l], w_in[l], w_gate[l], b_gate[l],
                                             sgu_ln_g[l], sgu_ln_b[l], w1[l], w3[l], w2[l], tm=512)
    a_row, s5_in, s5_out = _s5_params(ssm_a_re[l], ssm_a_im[l], ssm_b_re[l], ssm_b_im[l],
                                      ssm_c_re[l], ssm_c_im[l], ssm_log_step[l])
    ya = _s5(za, a_row, s5_in, s5_out, ssm_d[l], w_glu[l], b_glu[l])
    h1, xp, route = _mix(x, u, v, ya, gates, mod, sgu_w[l], sgu_b[l], w_branch_a[l], w_branch_b[l],
                         w_out[l], norm2_g[l], w_router_group[l], b_router_group[l],
                         w_router_expert[l], b_router_expert[l], tm=512)
    nt = bsz * seq
    n_blocks = nt // EXPERT_ROWS + N_CLASSES
    pos, e_lo, e_hi, n_used, counts, pstarts = _dispatch(route, n_blocks)
    xs = _scatter_rows(xp.reshape(nt, xp.shape[-1]), pos, counts, pstarts, n_blocks)
    y_sorted = _experts(xs, e_lo, e_hi, n_used, w13_bf, w2_bf)
    out = _final(pos, y_sorted, h1.reshape(nt, d), mod, norm_f_g, seq, tm=512)
    return out.reshape(bsz, seq, d)
```

```python
import functools
import math

import jax
import jax.numpy as jnp
from jax import lax
from jax.experimental import pallas as pl
from jax.experimental.pallas import tpu as pltpu

F32 = jnp.float32
BF16 = jnp.bfloat16
I32 = jnp.int32

LANES = 128
SUBLANES = 8
VMEM_LIMIT = 56 * 1024 * 1024

RMS_EPS = 1e-6
LN_EPS = 1e-5

SSM_GROUP_CH = 16
SSM_STATE = 64
GROUPS_PER_BLOCK = LANES // SSM_GROUP_CH
SGU_HEADS = 8
SGU_CHUNK = 128
N_GROUPS = 4
EXPERTS_PER_GROUP = 8
PAIRS_PER_GROUP = EXPERTS_PER_GROUP * (EXPERTS_PER_GROUP - 1) // 2
N_CLASSES = N_GROUPS * PAIRS_PER_GROUP
EXPERT_ROWS = 256

SCAN_BATCHES = SUBLANES
SCAN_TS = 128
SLAB_PITCH = SCAN_TS + 8
INPROJ_SUBTILE = 256
MIX_SUBTILE = 256


def _gelu(x):
    return 0.5 * x * (1.0 + jnp.tanh(math.sqrt(2.0 / math.pi) * (x + 0.044715 * (x * x * x))))


def _dot(a, b):
    return jnp.dot(a, b, preferred_element_type=F32)


def _split_bf16(a):
    hi = a.astype(BF16)
    lo = (a - hi.astype(F32)).astype(BF16)
    return hi, lo


def _cparams(sem):
    return pltpu.CompilerParams(dimension_semantics=sem, vmem_limit_bytes=VMEM_LIMIT)


def _adaln_body(c_ref, w_ref, b_ref, o_ref):
    c = c_ref[...]
    act = c * jax.nn.sigmoid(c)
    a_hi, a_lo = _split_bf16(act)
    w_hi, w_lo = _split_bf16(w_ref[...])
    o_ref[...] = _dot(a_hi, w_hi) + _dot(a_hi, w_lo) + _dot(a_lo, w_hi) + b_ref[...]


def _adaln(c, w, b):
    bsz, d = c.shape
    n = w.shape[1]
    tn = 1024
    return pl.pallas_call(
        _adaln_body,
        grid=(n // tn,),
        in_specs=[pl.BlockSpec((bsz, d), lambda j: (0, 0)),
                  pl.BlockSpec((d, tn), lambda j: (0, j)),
                  pl.BlockSpec((1, tn), lambda j: (0, j))],
        out_specs=pl.BlockSpec((bsz, tn), lambda j: (0, j)),
        out_shape=jax.ShapeDtypeStruct((bsz, n), F32),
        compiler_params=_cparams(("arbitrary",)),
        name="adaln",
    )(c, w, b.reshape(1, n))


def _inproj_body(x_ref, mod_ref, g1_ref, win_ref, wgate_ref, bgate_ref, lng_ref, lnb_ref,
                 w1_ref, w3_ref, w2_ref, za_ref, u_ref, v_ref, gates_ref, w13_ref, w2b_ref):
    w = za_ref.shape[-1]
    tm = x_ref.shape[1]
    ff = w1_ref.shape[2]
    w13_ref[0, :, :ff] = w1_ref[0].astype(BF16)
    w13_ref[0, :, ff:] = w3_ref[0].astype(BF16)
    w2b_ref[0] = w2_ref[0].astype(BF16)
    for r0 in range(0, tm, INPROJ_SUBTILE):
        rs = slice(r0, r0 + INPROJ_SUBTILE)
        x = x_ref[0, rs, :]
        ms = jnp.mean(x * x, axis=-1, keepdims=True)
        xn = x * lax.rsqrt(ms + RMS_EPS) * g1_ref[...]
        xn = xn * (1.0 + mod_ref[0, 1:2, :]) + mod_ref[0, 0:1, :]
        xb = xn.astype(BF16)
        proj = _dot(xb, win_ref[...])
        za_ref[0, rs, :] = proj[:, :w]
        u_ref[0, rs, :] = _gelu(proj[:, w:2 * w]).astype(BF16)
        gv = _gelu(proj[:, 2 * w:])
        mu = jnp.mean(gv, axis=-1, keepdims=True)
        cen = gv - mu
        var = jnp.mean(cen * cen, axis=-1, keepdims=True)
        v_ref[0, rs, :] = (cen * lax.rsqrt(var + LN_EPS) * lng_ref[...] + lnb_ref[...]).astype(BF16)
        gates_ref[0, rs, :] = jax.nn.sigmoid(_dot(xb, wgate_ref[...]) + bgate_ref[...]).astype(BF16)


def _inproj(x, mod, g1, w_in, w_gate, b_gate, ln_g, ln_b, w1, w3, w2, tm):
    bsz, s, d = x.shape
    w = w_in.shape[1] // 3
    ng = w_gate.shape[1]
    n_exp, _, ff = w1.shape
    nsteps = s // tm
    parts = bsz * nsteps // n_exp
    assert bsz * nsteps == n_exp * parts and d % (parts * SUBLANES) == 0 and ff % (parts * SUBLANES) == 0
    tok = lambda n: pl.BlockSpec((1, tm, n), lambda b, i: (b, i, 0))
    full = lambda a: pl.BlockSpec(a.shape, lambda b, i: (0,) * a.ndim)
    wslice = lambda rows, cols: pl.BlockSpec(
        (1, rows // parts, cols), lambda b, i: ((b * nsteps + i) // parts, (b * nsteps + i) % parts, 0))
    args = (g1.reshape(1, d), w_in.astype(BF16), w_gate.astype(BF16), b_gate.reshape(1, ng),
            ln_g.reshape(1, w), ln_b.reshape(1, w))
    return pl.pallas_call(
        _inproj_body,
        grid=(bsz, nsteps),
        in_specs=[tok(d), pl.BlockSpec((1, 6, d), lambda b, i: (b, 0, 0))] + [full(a) for a in args]
                 + [wslice(d, ff), wslice(d, ff), wslice(ff, d)],
        out_specs=[tok(w), tok(w), tok(w), tok(ng), wslice(d, 2 * ff), wslice(ff, d)],
        out_shape=[jax.ShapeDtypeStruct((bsz, s, w), F32),
                   jax.ShapeDtypeStruct((bsz, s, w), BF16),
                   jax.ShapeDtypeStruct((bsz, s, w), BF16),
                   jax.ShapeDtypeStruct((bsz, s, ng), BF16),
                   jax.ShapeDtypeStruct((n_exp, d, 2 * ff), BF16),
                   jax.ShapeDtypeStruct((n_exp, ff, d), BF16)],
        compiler_params=_cparams(("parallel", "parallel")),
        name="in_proj",
    )(x, mod, *args, w1, w3, w2)


def _s5_params(a_re, a_im, b_re, b_im, c_re, c_im, log_step):
    g, n = a_re.shape
    nblk = g // GROUPS_PER_BLOCK
    lam_re = jnp.minimum(a_re.astype(F32), -1e-4)
    lam_im = a_im.astype(F32)
    dt = jnp.exp(log_step.astype(F32))[:, None]
    mag = jnp.exp(lam_re * dt)
    ab_re = mag * jnp.cos(lam_im * dt)
    ab_im = mag * jnp.sin(lam_im * dt)
    den = lam_re * lam_re + lam_im * lam_im
    nr = ab_re - 1.0
    f_re = (nr * lam_re + ab_im * lam_im) / den
    f_im = (ab_im * lam_re - nr * lam_im) / den
    bt_re = b_re.astype(F32).transpose(0, 2, 1)
    bt_im = b_im.astype(F32).transpose(0, 2, 1)
    w_re = f_re[:, None, :] * bt_re - f_im[:, None, :] * bt_im
    w_im = f_re[:, None, :] * bt_im + f_im[:, None, :] * bt_re
    eye = jnp.eye(GROUPS_PER_BLOCK, dtype=F32)

    def bdiag_in(wt):
        t = wt.reshape(nblk, GROUPS_PER_BLOCK, SSM_GROUP_CH, 1, n) * eye[None, :, None, :, None]
        return t.reshape(nblk, GROUPS_PER_BLOCK * SSM_GROUP_CH, GROUPS_PER_BLOCK * n)

    def bdiag_out(ct):
        t = ct.transpose(0, 2, 1).reshape(nblk, GROUPS_PER_BLOCK, n, 1, SSM_GROUP_CH)
        t = t * eye[None, :, None, :, None]
        return t.reshape(nblk, GROUPS_PER_BLOCK * n, GROUPS_PER_BLOCK * SSM_GROUP_CH)

    w_in = jnp.concatenate([bdiag_in(w_re), bdiag_in(w_im)], axis=2).astype(BF16)
    w_out = jnp.concatenate([bdiag_out(c_re.astype(F32)), -bdiag_out(c_im.astype(F32))],
                            axis=1).astype(BF16)
    a_row = jnp.concatenate([ab_re.reshape(nblk, -1), ab_im.reshape(nblk, -1)], axis=1).reshape(1, -1)
    return a_row, w_in, w_out


def _s5_body(z_ref, a_ref, win_ref, wout_ref, d_ref, wglu_ref, bglu_ref, o_ref,
             zslab, zsb, xs, hstate, yslab):
    nb, ts, w = z_ref.shape
    rows = nb * ts
    nblk = w // LANES
    half = GROUPS_PER_BLOCK * SSM_STATE
    sw = 2 * half

    @pl.when(pl.program_id(1) == 0)
    def _():
        hstate[...] = jnp.zeros_like(hstate)

    for b in range(nb):
        for c in range(nblk):
            zslab[c, b * SLAB_PITCH:b * SLAB_PITCH + ts, :] = z_ref[b, :, c * LANES:(c + 1) * LANES]

    for s in range(ts):
        for c in range(nblk):
            zsb[s * nb:(s + 1) * nb, c * LANES:(c + 1) * LANES] = zslab.at[c][pl.ds(s, nb, stride=SLAB_PITCH), :]

    def in_map(j):
        xs[:, j * sw:(j + 1) * sw] = _dot(zsb[:, j * LANES:(j + 1) * LANES].astype(BF16), win_ref[j])

    def scan(j):
        re_sl = slice(j * sw, j * sw + half)
        im_sl = slice(j * sw + half, (j + 1) * sw)
        a_r = jnp.broadcast_to(a_ref[:, re_sl], (nb, half))
        a_i = jnp.broadcast_to(a_ref[:, im_sl], (nb, half))
        h_r, h_i = hstate[:, re_sl], hstate[:, im_sl]
        for s in range(ts):
            x_r = xs[s * nb:(s + 1) * nb, re_sl]
            x_i = xs[s * nb:(s + 1) * nb, im_sl]
            h_r, h_i = a_r * h_r - a_i * h_i + x_r, a_r * h_i + a_i * h_r + x_i
            xs[s * nb:(s + 1) * nb, re_sl] = h_r
            xs[s * nb:(s + 1) * nb, im_sl] = h_i
        hstate[:, re_sl] = h_r
        hstate[:, im_sl] = h_i

    def out_map(j):
        y = _dot(xs[:, j * sw:(j + 1) * sw].astype(BF16), wout_ref[j])
        y = y + d_ref[:, j * LANES:(j + 1) * LANES] * zsb[:, j * LANES:(j + 1) * LANES]
        return _gelu(y)

    ys = []
    for step in range(nblk + 2):
        if step < nblk:
            in_map(step)
        if 0 <= step - 1 < nblk:
            scan(step - 1)
        if 0 <= step - 2 < nblk:
            ys.append(out_map(step - 2))
    g = jnp.concatenate(ys, axis=1)
    g = g * jax.nn.sigmoid(_dot(g.astype(BF16), wglu_ref[...]) + bglu_ref[...])
    for c in range(nblk):
        yslab[c] = g[:, c * LANES:(c + 1) * LANES]

    for b in range(nb):
        for c in range(nblk):
            o_ref[b, :, c * LANES:(c + 1) * LANES] = yslab.at[c][pl.ds(b, ts, stride=nb), :].astype(BF16)


def _s5(za, a_row, w_in, w_out, d_skip, w_glu, b_glu):
    bsz, s, w = za.shape
    nb, ts = SCAN_BATCHES, SCAN_TS
    rows = nb * ts
    nblk = w // LANES
    nstate = a_row.shape[1]
    full = lambda a: pl.BlockSpec(a.shape, lambda b, i: (0,) * a.ndim)
    args = (a_row, w_in, w_out, d_skip.reshape(1, w).astype(F32), w_glu.astype(BF16), b_glu.reshape(1, w))
    return pl.pallas_call(
        _s5_body,
        grid=(bsz // nb, s // ts),
        in_specs=[pl.BlockSpec((nb, ts, w), lambda b, i: (b, i, 0))] + [full(a) for a in args],
        out_specs=pl.BlockSpec((nb, ts, w), lambda b, i: (b, i, 0)),
        out_shape=jax.ShapeDtypeStruct((bsz, s, w), BF16),
        scratch_shapes=[pltpu.VMEM((nblk, nb * SLAB_PITCH, LANES), F32),
                        pltpu.VMEM((rows, w), F32),
                        pltpu.VMEM((rows, nstate), F32),
                        pltpu.VMEM((nb, nstate), F32),
                        pltpu.VMEM((nblk, rows, LANES), F32)],
        compiler_params=_cparams(("parallel", "arbitrary")),
        name="s5",
    )(za, *args)


def _mix_body(x_ref, u_ref, v_ref, ya_ref, gates_ref, mod_ref, wp_ref, sb_ref, wba_ref, wbb_ref,
              wout_ref, g2_ref, wr_ref, br_ref, h1_ref, xp_ref, route_ref):
    tm_full = x_ref.shape[1]
    npair = u_ref.shape[2] // LANES
    hd = LANES // 2

    t_idx = lax.broadcasted_iota(I32, (SGU_CHUNK, 2 * SGU_CHUNK), 0)
    s_idx = lax.broadcasted_iota(I32, (SGU_CHUNK, 2 * SGU_CHUNK), 1) & (SGU_CHUNK - 1)
    causal = s_idx <= t_idx
    lane = lax.broadcasted_iota(I32, (SGU_CHUNK, LANES), 1)
    first_head = lane < hd
    w_causal = [jnp.where(causal, wp_ref[j], jnp.zeros_like(wp_ref[j])) for j in range(npair)]
    for r0 in range(0, tm_full, MIX_SUBTILE):
        _mix_rows(slice(r0, r0 + MIX_SUBTILE), w_causal, first_head, x_ref, u_ref, v_ref, ya_ref, gates_ref,
                  mod_ref, sb_ref, wba_ref, wbb_ref, wout_ref, g2_ref, wr_ref, br_ref, h1_ref, xp_ref,
                  route_ref)


def _mix_rows(rs, w_causal, first_head, x_ref, u_ref, v_ref, ya_ref, gates_ref, mod_ref, sb_ref, wba_ref,
              wbb_ref, wout_ref, g2_ref, wr_ref, br_ref, h1_ref, xp_ref, route_ref):
    tm = rs.stop - rs.start
    d = x_ref.shape[2]
    npair = len(w_causal)
    mixed_rows = []
    for c in range(rs.start, rs.stop, SGU_CHUNK):
        vc = v_ref[0, c:c + SGU_CHUNK, :]
        blocks = []
        for j in range(npair):
            vb = vc[:, j * LANES:(j + 1) * LANES]
            zero = jnp.zeros_like(vb)
            rhs = jnp.concatenate([jnp.where(first_head, vb, zero), jnp.where(first_head, zero, vb)], axis=0)
            blocks.append(_dot(w_causal[j], rhs))
        mixed_rows.append(jnp.concatenate(blocks, axis=1) + sb_ref[...])
    mixed = jnp.concatenate(mixed_rows, axis=0)
    yb = (u_ref[0, rs, :].astype(F32) * mixed).astype(BF16)

    pa = _dot(ya_ref[0, rs, :], wba_ref[...])
    pb = _dot(yb, wbb_ref[...])
    merged = gates_ref[0, rs, :d].astype(F32) * pa + gates_ref[0, rs, d:].astype(F32) * pb
    o = _dot(merged.astype(BF16), wout_ref[...])
    h1 = x_ref[0, rs, :] + mod_ref[0, 2:3, :] * o
    h1_ref[0, rs, :] = h1

    ms = jnp.mean(h1 * h1, axis=-1, keepdims=True)
    xn = h1 * lax.rsqrt(ms + RMS_EPS) * g2_ref[...]
    xn = xn * (1.0 + mod_ref[0, 4:5, :]) + mod_ref[0, 3:4, :]
    x_hi = xn.astype(BF16)

    logits = _dot(x_hi, wr_ref[...]) + br_ref[...]
    lt = logits.T
    best = lt[0:1, :]
    grp = jnp.zeros((1, tm), I32)
    for gi in range(1, N_GROUPS):
        better = lt[gi:gi + 1, :] > best
        grp = jnp.where(better, gi, grp)
        best = jnp.where(better, lt[gi:gi + 1, :], best)
    den = jnp.zeros((1, tm), F32)
    for gi in range(N_GROUPS):
        den = den + jnp.exp(lt[gi:gi + 1, :] - best)
    pg = 1.0 / den
    le = lt[SUBLANES:2 * SUBLANES, :]
    for gi in range(1, N_GROUPS):
        le = jnp.where(grp == gi, lt[SUBLANES * (gi + 1):SUBLANES * (gi + 2), :], le)
    eidx = lax.broadcasted_iota(I32, (EXPERTS_PER_GROUP, tm), 0).astype(F32)
    none = float(EXPERTS_PER_GROUP)
    v1 = jnp.max(le, axis=0, keepdims=True)
    i1 = jnp.min(jnp.where(le == v1, eidx, none), axis=0, keepdims=True)
    rest = jnp.where(eidx == i1, -jnp.inf, le)
    v2 = jnp.max(rest, axis=0, keepdims=True)
    i2 = jnp.min(jnp.where(rest == v2, eidx, none), axis=0, keepdims=True)
    ex = jnp.exp(v2 - v1)
    p1 = 1.0 / (1.0 + ex)
    wt1 = pg * p1
    wt2 = pg * (ex * p1)
    first_lo = i1 < i2
    lo = jnp.where(first_lo, i1, i2)
    hi = jnp.where(first_lo, i2, i1)
    pair = lo * (2 * EXPERTS_PER_GROUP - 1.0 - lo) * 0.5 + (hi - lo - 1.0)
    cls = grp.astype(F32) * PAIRS_PER_GROUP + pair
    info = jnp.concatenate([cls, jnp.where(first_lo, wt1, wt2), jnp.where(first_lo, wt2, wt1)], axis=0)
    route_ref[:, rs] = jnp.concatenate([info, jnp.zeros((SUBLANES - 3, tm), F32)], axis=0)

    bits = lax.bitcast_convert_type(x_hi.astype(F32), I32)
    packed = lax.shift_right_logical(bits[:, :d // 2], 16) | (bits[:, d // 2:] & jnp.int32(-65536))
    info_cols = jnp.concatenate([info, jnp.zeros((LANES - 3, tm), F32)], axis=0).T
    xp_ref[0, rs, :] = jnp.concatenate([packed, lax.bitcast_convert_type(info_cols, I32)], axis=1)


def _mix(x, u, v, ya, gates, mod, sgu_w, sgu_b, w_ba, w_bb, w_out, g2, w_rg, b_rg, w_re, b_re, tm):
    bsz, s, d = x.shape
    w = u.shape[2]
    nt = bsz * s
    wp = sgu_w.reshape(SGU_HEADS // 2, 2, SGU_CHUNK, SGU_CHUNK).transpose(0, 2, 1, 3)
    wp = wp.reshape(SGU_HEADS // 2, SGU_CHUNK, 2 * SGU_CHUNK).astype(BF16)
    sb = jnp.repeat(sgu_b.T, w // SGU_HEADS, axis=1).astype(F32)
    wr = jnp.zeros((d, LANES), F32)
    wr = wr.at[:, :N_GROUPS].set(w_rg)
    wr = wr.at[:, SUBLANES:SUBLANES + N_GROUPS * EXPERTS_PER_GROUP].set(
        w_re.transpose(1, 0, 2).reshape(d, N_GROUPS * EXPERTS_PER_GROUP))
    br = jnp.zeros((1, LANES), F32)
    br = br.at[0, :N_GROUPS].set(b_rg)
    br = br.at[0, SUBLANES:SUBLANES + N_GROUPS * EXPERTS_PER_GROUP].set(b_re.reshape(-1))
    tok = lambda n: pl.BlockSpec((1, tm, n), lambda b, i: (b, i, 0))
    full = lambda a: pl.BlockSpec(a.shape, lambda b, i: (0,) * a.ndim)
    args = (wp, sb, w_ba.astype(BF16), w_bb.astype(BF16), w_out.astype(BF16), g2.reshape(1, d),
            wr.astype(BF16), br)
    nsteps = s // tm
    return pl.pallas_call(
        _mix_body,
        grid=(bsz, nsteps),
        in_specs=[tok(d), tok(w), tok(w), tok(w), tok(2 * d),
                  pl.BlockSpec((1, 6, d), lambda b, i: (b, 0, 0))] + [full(a) for a in args],
        out_specs=[tok(d), tok(d // 2 + LANES),
                   pl.BlockSpec((SUBLANES, tm), lambda b, i: (0, b * nsteps + i))],
        out_shape=[jax.ShapeDtypeStruct((bsz, s, d), F32),
                   jax.ShapeDtypeStruct((bsz, s, d // 2 + LANES), I32),
                   jax.ShapeDtypeStruct((SUBLANES, nt), F32)],
        compiler_params=_cparams(("parallel", "parallel")),
        name="mix",
    )(x, u, v, ya, gates, mod, *args)


def _tile_lanes(a, reps):
    return jnp.concatenate([a] * reps, axis=1)


def _dispatch_body(route_ref, lo_ref, hi_ref, pos_ref, blk_ref, cls_ref, run, pstart, ranks, classes,
                   earlier):
    i = pl.program_id(0)
    n = pl.num_programs(0)
    tt = route_ref.shape[1]
    reps = tt // LANES
    cid = lax.broadcasted_iota(I32, (LANES, tt), 0).astype(F32)

    @pl.when(i == 0)
    def _():
        run[...] = jnp.zeros_like(run)
        earlier_tok = lax.broadcasted_iota(I32, (tt, tt), 0) < lax.broadcasted_iota(I32, (tt, tt), 1)
        earlier[...] = jnp.where(earlier_tok, 1.0, 0.0).astype(BF16)

    cls = route_ref[0:1, :]
    classes[i] = cls
    onehot = jnp.where(cls == cid, 1.0, 0.0)
    before = _dot(onehot.astype(BF16), earlier[...])
    ranks[i] = jnp.sum(onehot * (before + _tile_lanes(run[...], reps)), axis=0, keepdims=True)
    run[...] = run[...] + jnp.sum(onehot, axis=1, keepdims=True)

    @pl.when(i == n - 1)
    def _():
        counts = run[...]
        nblk = jnp.floor((counts + (EXPERT_ROWS - 1.0)) * (1.0 / EXPERT_ROWS))
        hi_part = jnp.floor(nblk * (1.0 / 16.0))
        lo_part = nblk - 16.0 * hi_part
        upto = lax.broadcasted_iota(I32, (LANES, LANES), 1) <= lax.broadcasted_iota(I32, (LANES, LANES), 0)
        upto = jnp.where(upto, 1.0, 0.0).astype(BF16)
        ends = 16.0 * _dot(upto, hi_part.astype(BF16)) + _dot(upto, lo_part.astype(BF16))
        pstart[...] = (ends - nblk) * EXPERT_ROWS
        diag = lax.broadcasted_iota(I32, (LANES, LANES), 0) == lax.broadcasted_iota(I32, (LANES, LANES), 1)
        on_lanes = lambda a: jnp.sum(jnp.where(diag, a, 0.0), axis=0, keepdims=True)
        cls_ref[...] = jnp.concatenate([on_lanes(counts), on_lanes(pstart[...]),
                                        jnp.zeros((SUBLANES - 2, LANES), F32)], axis=0)
        nb_lanes = blk_ref.shape[1]
        breps = nb_lanes // LANES
        n_used = _tile_lanes(ends[LANES - 1:LANES, :], breps)
        b_idx = lax.broadcasted_iota(I32, (LANES, nb_lanes), 1).astype(F32)
        b_idx = jnp.minimum(b_idx, n_used - 1.0)
        b_cls = jnp.sum(jnp.where(_tile_lanes(ends, breps) <= b_idx, 1.0, 0.0), axis=0, keepdims=True)
        b_member = b_cls == lax.broadcasted_iota(I32, (LANES, nb_lanes), 0).astype(F32)
        pick = lambda tab: jnp.sum(jnp.where(b_member, _tile_lanes(tab, breps), 0.0), axis=0, keepdims=True)
        live = pick(counts) - (b_idx[0:1, :] * EXPERT_ROWS - pick(pstart[...]))
        half_full = jnp.where(live <= EXPERT_ROWS // 2, 1.0, 0.0)
        blk_ref[...] = jnp.concatenate([pick(lo_ref[...]), pick(hi_ref[...]), n_used, half_full,
                                        jnp.zeros((SUBLANES - 4, nb_lanes), F32)], axis=0)
        starts = _tile_lanes(pstart[...], reps)

        def tile_positions(k, carry):
            base = jnp.sum(jnp.where(classes[k] == cid, starts, 0.0), axis=0, keepdims=True)
            pos_ref[k] = (base + ranks[k]).astype(I32)
            return carry

        lax.fori_loop(0, n, tile_positions, 0)


def _dispatch(route, n_blocks, tt=1024):
    nt = route.shape[1]
    n = nt // tt
    nb_lanes = pl.cdiv(n_blocks, LANES) * LANES
    lo_tab, hi_tab = [], []
    for g in range(N_GROUPS):
        for a in range(EXPERTS_PER_GROUP):
            for b in range(a + 1, EXPERTS_PER_GROUP):
                lo_tab.append(g * EXPERTS_PER_GROUP + a)
                hi_tab.append(g * EXPERTS_PER_GROUP + b)
    pad = [0] * (LANES - N_CLASSES)
    lo_tile = jnp.broadcast_to(jnp.asarray(lo_tab + pad, F32)[:, None], (LANES, LANES))
    hi_tile = jnp.broadcast_to(jnp.asarray(hi_tab + pad, F32)[:, None], (LANES, LANES))
    const = lambda shape: pl.BlockSpec(shape, lambda i: (0,) * len(shape))
    pos, blk_tab, cls_tab = pl.pallas_call(
        _dispatch_body,
        grid=(n,),
        in_specs=[pl.BlockSpec((SUBLANES, tt), lambda i: (0, i)),
                  const((LANES, LANES)), const((LANES, LANES))],
        out_specs=[const((n, 1, tt)), const((SUBLANES, nb_lanes)), const((SUBLANES, LANES))],
        out_shape=[jax.ShapeDtypeStruct((n, 1, tt), I32),
                   jax.ShapeDtypeStruct((SUBLANES, nb_lanes), F32),
                   jax.ShapeDtypeStruct((SUBLANES, LANES), F32)],
        scratch_shapes=[pltpu.VMEM((LANES, LANES), F32), pltpu.VMEM((LANES, LANES), F32),
                        pltpu.VMEM((n, 1, tt), F32), pltpu.VMEM((n, 1, tt), F32),
                        pltpu.VMEM((tt, tt), BF16)],
        compiler_params=_cparams(("arbitrary",)),
        name="dispatch",
    )(route, lo_tile, hi_tile)
    blk = blk_tab.astype(I32)
    cls = cls_tab.astype(I32)
    return pos.reshape(nt), blk[0, :n_blocks], blk[1, :n_blocks], blk[2, :1], blk[3, :n_blocks], cls[0], cls[1]


ZERO_FILL_ROWS = tuple(SUBLANES << k for k in range((EXPERT_ROWS // SUBLANES).bit_length() - 1))


def _scatter_body(cnt_ref, pst_ref, pos_ref, xp_ref, xs_hbm, ring, zeros, sem, zsem):
    i = pl.program_id(0)
    n = pl.num_programs(0)
    tm = xp_ref.shape[0]
    half = tm // 2

    def rows_done(h):
        pltpu.make_async_copy(ring.at[pl.ds(h * half, half)], xs_hbm.at[pl.ds(0, half)], sem.at[h]).wait()

    for h in range(2):
        @pl.when(i > 0)
        def _():
            rows_done(h)

        ring[h * half:(h + 1) * half, :] = xp_ref[h * half:(h + 1) * half, :]
        for r in range(h * half, (h + 1) * half):
            pltpu.make_async_copy(ring.at[pl.ds(r, 1)], xs_hbm.at[pl.ds(pos_ref[0, 0, r], 1)],
                                  sem.at[h]).start()

    @pl.when(i == n - 1)
    def _():
        rows_done(0)
        rows_done(1)
        zeros[...] = jnp.zeros_like(zeros)

        def fill(wait):
            def per_class(k, carry):
                cnt = cnt_ref[k]
                n_pad = (-cnt) & (EXPERT_ROWS - 1)
                off = pst_ref[k] + cnt
                for r in range(SUBLANES - 1):
                    row = pltpu.make_async_copy(zeros.at[pl.ds(0, 1)], xs_hbm.at[pl.ds(off + r, 1)], zsem)

                    @pl.when(r < (n_pad & (SUBLANES - 1)))
                    def _():
                        row.wait() if wait else row.start()

                off = off + (n_pad & (SUBLANES - 1))
                for size in ZERO_FILL_ROWS:
                    piece = pltpu.make_async_copy(zeros.at[pl.ds(0, size)],
                                                  xs_hbm.at[pl.ds(pl.multiple_of(off, size), size)], zsem)

                    @pl.when((n_pad & size) != 0)
                    def _():
                        piece.wait() if wait else piece.start()

                    off = off + (n_pad & size)
                return carry

            lax.fori_loop(0, N_CLASSES, per_class, 0)

            last = N_CLASSES - 1
            used_rows = pst_ref[last] + cnt_ref[last] + ((-cnt_ref[last]) & (EXPERT_ROWS - 1))
            tail = ZERO_FILL_ROWS[-1]

            def per_piece(j, carry):
                piece = pltpu.make_async_copy(
                    zeros, xs_hbm.at[pl.ds(pl.multiple_of(used_rows + j * tail, tail), tail)], zsem)
                piece.wait() if wait else piece.start()
                return carry

            lax.fori_loop(0, (xs_hbm.shape[0] - used_rows) // tail, per_piece, 0)

        fill(False)
        fill(True)


def _scatter_rows(xp, pos, counts, pstarts, n_blocks, tm=1024):
    nt, wrow = xp.shape
    n = nt // tm
    grid_spec = pltpu.PrefetchScalarGridSpec(
        num_scalar_prefetch=2,
        grid=(n,),
        in_specs=[pl.BlockSpec((1, 1, tm), lambda i, c, p: (i, 0, 0), memory_space=pltpu.SMEM),
                  pl.BlockSpec((tm, wrow), lambda i, c, p: (i, 0))],
        out_specs=pl.BlockSpec(memory_space=pl.ANY),
        scratch_shapes=[pltpu.VMEM((tm, wrow), I32), pltpu.VMEM((ZERO_FILL_ROWS[-1], wrow), I32),
                        pltpu.SemaphoreType.DMA((2,)), pltpu.SemaphoreType.DMA(())],
    )
    return pl.pallas_call(
        _scatter_body,
        grid_spec=grid_spec,
        out_shape=jax.ShapeDtypeStruct((n_blocks * EXPERT_ROWS, wrow), I32),
        compiler_params=_cparams(("arbitrary",)),
        name="scatter_rows",
    )(counts, pstarts, pos.reshape(n, 1, tm), xp)


def _wait_rows(n, src_hbm, dst, sem):
    pltpu.make_async_copy(src_hbm.at[pl.ds(0, n)], dst, sem).wait()


def _unpack_rows(xp):
    left = lax.bitcast_convert_type(lax.shift_left(xp, 16), F32).astype(BF16)
    right = lax.bitcast_convert_type(xp & jnp.int32(-65536), F32).astype(BF16)
    return jnp.concatenate([left, right], axis=1)


def _experts_body(elo_ref, ehi_ref, nu_ref, half_ref, xs_ref, w13a_ref, w2a_ref, w13b_ref, w2b_ref, o_ref):
    i = pl.program_id(0)
    rows = xs_ref.shape[0]
    dh = xs_ref.shape[1] - LANES
    ff = w2a_ref.shape[1]
    used = i < nu_ref[0]
    half_full = half_ref[i] != 0

    def both_experts(rs):
        x = _unpack_rows(xs_ref[rs, :dh])
        info = lax.bitcast_convert_type(xs_ref[rs, dh:], F32)

        def ffn(w13_ref, w2_ref, weight):
            ab = _dot(x, w13_ref[0])
            a = ab[:, :ff]
            hid = (a * jax.nn.sigmoid(a)) * ab[:, ff:] * weight
            return _dot(hid.astype(BF16), w2_ref[0])

        return ffn(w13a_ref, w2a_ref, info[:, 1:2]) + ffn(w13b_ref, w2b_ref, info[:, 2:3])

    @pl.when(jnp.logical_and(used, jnp.logical_not(half_full)))
    def _():
        o_ref[...] = both_experts(slice(0, rows))

    @pl.when(jnp.logical_and(used, half_full))
    def _():
        o_ref[:rows // 2, :] = both_experts(slice(0, rows // 2))
        o_ref[rows // 2:, :] = jnp.zeros((rows - rows // 2, o_ref.shape[1]), o_ref.dtype)

    @pl.when(jnp.logical_not(used))
    def _():
        o_ref[...] = jnp.zeros_like(o_ref)


def _experts(xs, e_lo, e_hi, n_used, half_full, w13, w2):
    n_rows, wrow = xs.shape
    n_blocks = n_rows // EXPERT_ROWS
    n_exp, ff, d = w2.shape
    pick = lambda a, use_hi: pl.BlockSpec(
        (1,) + a.shape[1:],
        (lambda i, lo, hi, nu, hf: (hi[i], 0, 0)) if use_hi else (lambda i, lo, hi, nu, hf: (lo[i], 0, 0)))
    grid_spec = pltpu.PrefetchScalarGridSpec(
        num_scalar_prefetch=4,
        grid=(n_blocks,),
        in_specs=[pl.BlockSpec((EXPERT_ROWS, wrow), lambda i, lo, hi, nu, hf: (jnp.minimum(i, nu[0] - 1), 0)),
                  pick(w13, False), pick(w2, False), pick(w13, True), pick(w2, True)],
        out_specs=pl.BlockSpec((EXPERT_ROWS, d), lambda i, lo, hi, nu, hf: (i, 0)),
    )
    return pl.pallas_call(
        _experts_body,
        grid_spec=grid_spec,
        out_shape=jax.ShapeDtypeStruct((n_rows, d), F32),
        compiler_params=_cparams(("arbitrary",)),
        name="experts",
    )(e_lo, e_hi, n_used, half_full, xs, w13, w2, w13, w2)


def _final_body(posc_ref, posn_ref, y_hbm, h1_ref, mod_ref, gf_ref, o_ref, ybuf, sem):
    i = pl.program_id(0)
    n = pl.num_programs(0)
    tm = h1_ref.shape[0]
    half = tm // 2

    def fetch(idx_ref, h):
        for r in range(half):
            pltpu.make_async_copy(y_hbm.at[pl.ds(idx_ref[0, 0, h * half + r], 1)],
                                  ybuf.at[h, pl.ds(r, 1)], sem.at[h]).start()

    @pl.when(i == 0)
    def _():
        fetch(posc_ref, 0)
        fetch(posc_ref, 1)

    for h in range(2):
        rows = slice(h * half, (h + 1) * half)
        _wait_rows(half, y_hbm, ybuf.at[h], sem.at[h])
        hres = h1_ref[rows, :] + mod_ref[0, 5:6, :] * ybuf[h]
        ms = jnp.mean(hres * hres, axis=-1, keepdims=True)
        o_ref[rows, :] = hres * lax.rsqrt(ms + RMS_EPS) * gf_ref[...]
        fetch(posn_ref, h)

    @pl.when(i == n - 1)
    def _():
        _wait_rows(half, y_hbm, ybuf.at[0], sem.at[0])
        _wait_rows(half, y_hbm, ybuf.at[1], sem.at[1])


def _final(pos, y_sorted, h1, mod, gf, seq, tm):
    nt, d = h1.shape
    n = nt // tm
    pos3 = pos.reshape(n, 1, tm)
    idx_spec = lambda f: pl.BlockSpec((1, 1, tm), f, memory_space=pltpu.SMEM)
    return pl.pallas_call(
        _final_body,
        grid=(n,),
        in_specs=[idx_spec(lambda i: (i, 0, 0)),
                  idx_spec(lambda i: (jnp.minimum(i + 1, n - 1), 0, 0)),
                  pl.BlockSpec(memory_space=pl.ANY),
                  pl.BlockSpec((tm, d), lambda i: (i, 0)),
                  pl.BlockSpec((1, 6, d), lambda i: (i * tm // seq, 0, 0)),
                  pl.BlockSpec((1, d), lambda i: (0, 0))],
        out_specs=pl.BlockSpec((tm, d), lambda i: (i, 0)),
        out_shape=jax.ShapeDtypeStruct((nt, d), F32),
        scratch_shapes=[pltpu.VMEM((2, tm // 2, d), F32), pltpu.SemaphoreType.DMA((2,))],
        compiler_params=_cparams(("arbitrary",)),
        name="final",
    )(pos3, pos3, y_sorted, h1, mod, gf.reshape(1, d))


def kernel(x, c, w_ada, b_ada, norm1_g, w_in, w_gate, b_gate, ssm_a_re, ssm_a_im, ssm_b_re, ssm_b_im, ssm_c_re, ssm_c_im, ssm_d, ssm_log_step, w_glu, b_glu, sgu_ln_g, sgu_ln_b, sgu_w, sgu_b, w_branch_a, w_branch_b, w_out, norm2_g, w_router_group, b_router_group, w_router_expert, b_router_expert, w1, w3, w2, norm_f_g):
    bsz, seq, d = x.shape
    depth = w_ada.shape[0]
    assert depth == 1 and bsz % SCAN_BATCHES == 0 and seq % 512 == 0
    l = 0
    mod = _adaln(c, w_ada[l], b_ada[l]).reshape(bsz, 6, d)
    za, u, v, gates, w13_bf, w2_bf = _inproj(x, mod, norm1_g[l], w_in[l], w_gate[l], b_gate[l],
                                             sgu_ln_g[l], sgu_ln_b[l], w1[l], w3[l], w2[l], tm=1024)
    a_row, s5_in, s5_out = _s5_params(ssm_a_re[l], ssm_a_im[l], ssm_b_re[l], ssm_b_im[l],
                                      ssm_c_re[l], ssm_c_im[l], ssm_log_step[l])
    ya = _s5(za, a_row, s5_in, s5_out, ssm_d[l], w_glu[l], b_glu[l])
    h1, xp, route = _mix(x, u, v, ya, gates, mod, sgu_w[l], sgu_b[l], w_branch_a[l], w_branch_b[l],
                         w_out[l], norm2_g[l], w_router_group[l], b_router_group[l],
                         w_router_expert[l], b_router_expert[l], tm=1024)
    nt = bsz * seq
    n_blocks = nt // EXPERT_ROWS + N_CLASSES
    pos, e_lo, e_hi, n_used, half_full, counts, pstarts = _dispatch(route, n_blocks)
    xs = _scatter_rows(xp.reshape(nt, xp.shape[-1]), pos, counts, pstarts, n_blocks)
    y_sorted = _experts(xs, e_lo, e_hi, n_used, half_full, w13_bf, w2_bf)
    out = _final(pos, y_sorted, h1.reshape(nt, d), mod, norm_f_g, seq, tm=1024)
    return out.reshape(bsz, seq, d)
```

```python
import functools
import math

import jax
import jax.numpy as jnp
from jax import lax
from jax.experimental import pallas as pl
from jax.experimental.pallas import tpu as pltpu

F32 = jnp.float32
BF16 = jnp.bfloat16
I32 = jnp.int32

LANES = 128
SUBLANES = 8
VMEM_LIMIT = 56 * 1024 * 1024

RMS_EPS = 1e-6
LN_EPS = 1e-5

SSM_GROUP_CH = 16
SSM_STATE = 64
GROUPS_PER_BLOCK = LANES // SSM_GROUP_CH
SGU_HEADS = 8
SGU_CHUNK = 128
N_GROUPS = 4
EXPERTS_PER_GROUP = 8
PAIRS_PER_GROUP = EXPERTS_PER_GROUP * (EXPERTS_PER_GROUP - 1) // 2
N_CLASSES = N_GROUPS * PAIRS_PER_GROUP
EXPERT_ROWS = 256

SCAN_BATCHES = SUBLANES
SCAN_TS = 128
SLAB_PITCH = SCAN_TS + 8
SCAN_CHUNK = 64
INPROJ_SUBTILE = 256
MIX_SUBTILE = 256


def _gelu(x):
    return 0.5 * x * (1.0 + jnp.tanh(math.sqrt(2.0 / math.pi) * (x + 0.044715 * (x * x * x))))


def _dot(a, b):
    return jnp.dot(a, b, preferred_element_type=F32)


def _split_bf16(a):
    hi = a.astype(BF16)
    lo = (a - hi.astype(F32)).astype(BF16)
    return hi, lo


def _cparams(sem):
    return pltpu.CompilerParams(dimension_semantics=sem, vmem_limit_bytes=VMEM_LIMIT)


def _adaln_body(c_ref, w_ref, b_ref, o_ref):
    c = c_ref[...]
    act = c * jax.nn.sigmoid(c)
    a_hi, a_lo = _split_bf16(act)
    w_hi, w_lo = _split_bf16(w_ref[...])
    o_ref[...] = _dot(a_hi, w_hi) + _dot(a_hi, w_lo) + _dot(a_lo, w_hi) + b_ref[...]


def _adaln(c, w, b):
    bsz, d = c.shape
    n = w.shape[1]
    tn = 1024
    return pl.pallas_call(
        _adaln_body,
        grid=(n // tn,),
        in_specs=[pl.BlockSpec((bsz, d), lambda j: (0, 0)),
                  pl.BlockSpec((d, tn), lambda j: (0, j)),
                  pl.BlockSpec((1, tn), lambda j: (0, j))],
        out_specs=pl.BlockSpec((bsz, tn), lambda j: (0, j)),
        out_shape=jax.ShapeDtypeStruct((bsz, n), F32),
        compiler_params=_cparams(("arbitrary",)),
        name="adaln",
    )(c, w, b.reshape(1, n))


def _inproj_body(x_ref, mod_ref, g1_ref, win_ref, wgate_ref, bgate_ref, lng_ref, lnb_ref,
                 w1_ref, w3_ref, w2_ref, za_ref, u_ref, v_ref, gates_ref, w13_ref, w2b_ref):
    w = za_ref.shape[-1]
    tm = x_ref.shape[1]
    ff = w1_ref.shape[2]
    w13_ref[0, :, :ff] = w1_ref[0].astype(BF16)
    w13_ref[0, :, ff:] = w3_ref[0].astype(BF16)
    w2b_ref[0] = w2_ref[0].astype(BF16)
    for r0 in range(0, tm, INPROJ_SUBTILE):
        rs = slice(r0, r0 + INPROJ_SUBTILE)
        x = x_ref[0, rs, :]
        ms = jnp.mean(x * x, axis=-1, keepdims=True)
        xn = x * lax.rsqrt(ms + RMS_EPS) * g1_ref[...]
        xn = xn * (1.0 + mod_ref[0, 1:2, :]) + mod_ref[0, 0:1, :]
        xb = xn.astype(BF16)
        proj = _dot(xb, win_ref[...])
        za_ref[0, rs, :] = proj[:, :w]
        u_ref[0, rs, :] = _gelu(proj[:, w:2 * w]).astype(BF16)
        gv = _gelu(proj[:, 2 * w:])
        mu = jnp.mean(gv, axis=-1, keepdims=True)
        cen = gv - mu
        var = jnp.mean(cen * cen, axis=-1, keepdims=True)
        v_ref[0, rs, :] = (cen * lax.rsqrt(var + LN_EPS) * lng_ref[...] + lnb_ref[...]).astype(BF16)
        gates_ref[0, rs, :] = jax.nn.sigmoid(_dot(xb, wgate_ref[...]) + bgate_ref[...]).astype(BF16)


def _inproj(x, mod, g1, w_in, w_gate, b_gate, ln_g, ln_b, w1, w3, w2, tm):
    bsz, s, d = x.shape
    w = w_in.shape[1] // 3
    ng = w_gate.shape[1]
    n_exp, _, ff = w1.shape
    nsteps = s // tm
    parts = bsz * nsteps // n_exp
    assert bsz * nsteps == n_exp * parts and d % (parts * SUBLANES) == 0 and ff % (parts * SUBLANES) == 0
    tok = lambda n: pl.BlockSpec((1, tm, n), lambda b, i: (b, i, 0))
    full = lambda a: pl.BlockSpec(a.shape, lambda b, i: (0,) * a.ndim)
    wslice = lambda rows, cols: pl.BlockSpec(
        (1, rows // parts, cols), lambda b, i: ((b * nsteps + i) // parts, (b * nsteps + i) % parts, 0))
    args = (g1.reshape(1, d), w_in.astype(BF16), w_gate.astype(BF16), b_gate.reshape(1, ng),
            ln_g.reshape(1, w), ln_b.reshape(1, w))
    return pl.pallas_call(
        _inproj_body,
        grid=(bsz, nsteps),
        in_specs=[tok(d), pl.BlockSpec((1, 6, d), lambda b, i: (b, 0, 0))] + [full(a) for a in args]
                 + [wslice(d, ff), wslice(d, ff), wslice(ff, d)],
        out_specs=[tok(w), tok(w), tok(w), tok(ng), wslice(d, 2 * ff), wslice(ff, d)],
        out_shape=[jax.ShapeDtypeStruct((bsz, s, w), F32),
                   jax.ShapeDtypeStruct((bsz, s, w), BF16),
                   jax.ShapeDtypeStruct((bsz, s, w), BF16),
                   jax.ShapeDtypeStruct((bsz, s, ng), BF16),
                   jax.ShapeDtypeStruct((n_exp, d, 2 * ff), BF16),
                   jax.ShapeDtypeStruct((n_exp, ff, d), BF16)],
        compiler_params=_cparams(("parallel", "parallel")),
        name="in_proj",
    )(x, mod, *args, w1, w3, w2)


def _s5_params(a_re, a_im, b_re, b_im, c_re, c_im, log_step):
    g, n = a_re.shape
    nblk = g // GROUPS_PER_BLOCK
    lam_re = jnp.minimum(a_re.astype(F32), -1e-4)
    lam_im = a_im.astype(F32)
    dt = jnp.exp(log_step.astype(F32))[:, None]
    mag = jnp.exp(lam_re * dt)
    ab_re = mag * jnp.cos(lam_im * dt)
    ab_im = mag * jnp.sin(lam_im * dt)
    den = lam_re * lam_re + lam_im * lam_im
    nr = ab_re - 1.0
    f_re = (nr * lam_re + ab_im * lam_im) / den
    f_im = (ab_im * lam_re - nr * lam_im) / den
    bt_re = b_re.astype(F32).transpose(0, 2, 1)
    bt_im = b_im.astype(F32).transpose(0, 2, 1)
    w_re = f_re[:, None, :] * bt_re - f_im[:, None, :] * bt_im
    w_im = f_re[:, None, :] * bt_im + f_im[:, None, :] * bt_re
    eye = jnp.eye(GROUPS_PER_BLOCK, dtype=F32)

    def bdiag_in(wt):
        t = wt.reshape(nblk, GROUPS_PER_BLOCK, SSM_GROUP_CH, 1, n) * eye[None, :, None, :, None]
        return t.reshape(nblk, GROUPS_PER_BLOCK * SSM_GROUP_CH, GROUPS_PER_BLOCK * n)

    def bdiag_out(ct):
        t = ct.transpose(0, 2, 1).reshape(nblk, GROUPS_PER_BLOCK, n, 1, SSM_GROUP_CH)
        t = t * eye[None, :, None, :, None]
        return t.reshape(nblk, GROUPS_PER_BLOCK * n, GROUPS_PER_BLOCK * SSM_GROUP_CH)

    w_in = jnp.concatenate([bdiag_in(w_re), bdiag_in(w_im)], axis=2).astype(BF16)
    w_out = jnp.concatenate([bdiag_out(c_re.astype(F32)), -bdiag_out(c_im.astype(F32))],
                            axis=1).astype(BF16)
    a_row = jnp.concatenate([ab_re.reshape(nblk, -1), ab_im.reshape(nblk, -1)], axis=1).reshape(1, -1)
    return a_row, w_in, w_out


def _s5_body(z_ref, a_ref, win_ref, wout_ref, d_ref, wglu_ref, bglu_ref, o_ref,
             zslab, zsb, xs, hstate, yslab):
    nb, ts, w = z_ref.shape
    rows = nb * ts
    nblk = w // LANES
    half = GROUPS_PER_BLOCK * SSM_STATE
    sw = 2 * half

    @pl.when(pl.program_id(1) == 0)
    def _():
        hstate[...] = jnp.zeros_like(hstate)

    for b in range(nb):
        for c in range(nblk):
            zslab[c, b * SLAB_PITCH:b * SLAB_PITCH + ts, :] = z_ref[b, :, c * LANES:(c + 1) * LANES]

    for s in range(ts):
        for c in range(nblk):
            zsb[s * nb:(s + 1) * nb, c * LANES:(c + 1) * LANES] = zslab.at[c][pl.ds(s, nb, stride=SLAB_PITCH), :]

    n_chunks = ts // SCAN_CHUNK
    crows = SCAN_CHUNK * nb
    units = [(j, q) for j in range(nblk) for q in range(n_chunks)]

    def in_map(j, q):
        rs = slice(q * crows, (q + 1) * crows)
        xs[rs, j * sw:(j + 1) * sw] = _dot(zsb[rs, j * LANES:(j + 1) * LANES].astype(BF16), win_ref[j])

    carry = {}

    def scan(j, q):
        re_sl = slice(j * sw, j * sw + half)
        im_sl = slice(j * sw + half, (j + 1) * sw)
        a_r = jnp.broadcast_to(a_ref[:, re_sl], (nb, half))
        a_i = jnp.broadcast_to(a_ref[:, im_sl], (nb, half))
        h_r, h_i = carry[j] if q else (hstate[:, re_sl], hstate[:, im_sl])
        for s in range(q * SCAN_CHUNK, (q + 1) * SCAN_CHUNK):
            x_r = xs[s * nb:(s + 1) * nb, re_sl]
            x_i = xs[s * nb:(s + 1) * nb, im_sl]
            h_r, h_i = a_r * h_r - a_i * h_i + x_r, a_r * h_i + a_i * h_r + x_i
            xs[s * nb:(s + 1) * nb, re_sl] = h_r
            xs[s * nb:(s + 1) * nb, im_sl] = h_i
        carry[j] = (h_r, h_i)
        if q == n_chunks - 1:
            hstate[:, re_sl] = h_r
            hstate[:, im_sl] = h_i

    gel = {}

    def out_map(j, q):
        rs = slice(q * crows, (q + 1) * crows)
        y = _dot(xs[rs, j * sw:(j + 1) * sw].astype(BF16), wout_ref[j])
        y = y + d_ref[:, j * LANES:(j + 1) * LANES] * zsb[rs, j * LANES:(j + 1) * LANES]
        gel[j, q] = _gelu(y)

    for k in range(len(units) + 2):
        if k < len(units):
            in_map(*units[k])
        if 0 <= k - 1 < len(units):
            scan(*units[k - 1])
        if 0 <= k - 2 < len(units):
            out_map(*units[k - 2])

    for q in range(n_chunks):
        g = jnp.concatenate([gel[j, q] for j in range(nblk)], axis=1)
        g = g * jax.nn.sigmoid(_dot(g.astype(BF16), wglu_ref[...]) + bglu_ref[...])
        for c in range(nblk):
            yslab[c, q * crows:(q + 1) * crows, :] = g[:, c * LANES:(c + 1) * LANES]

    for b in range(nb):
        for c in range(nblk):
            o_ref[b, :, c * LANES:(c + 1) * LANES] = yslab.at[c][pl.ds(b, ts, stride=nb), :].astype(BF16)


def _s5(za, a_row, w_in, w_out, d_skip, w_glu, b_glu):
    bsz, s, w = za.shape
    nb, ts = SCAN_BATCHES, SCAN_TS
    rows = nb * ts
    nblk = w // LANES
    nstate = a_row.shape[1]
    full = lambda a: pl.BlockSpec(a.shape, lambda b, i: (0,) * a.ndim)
    args = (a_row, w_in, w_out, d_skip.reshape(1, w).astype(F32), w_glu.astype(BF16), b_glu.reshape(1, w))
    return pl.pallas_call(
        _s5_body,
        grid=(bsz // nb, s // ts),
        in_specs=[pl.BlockSpec((nb, ts, w), lambda b, i: (b, i, 0))] + [full(a) for a in args],
        out_specs=pl.BlockSpec((nb, ts, w), lambda b, i: (b, i, 0)),
        out_shape=jax.ShapeDtypeStruct((bsz, s, w), BF16),
        scratch_shapes=[pltpu.VMEM((nblk, nb * SLAB_PITCH, LANES), F32),
                        pltpu.VMEM((rows, w), F32),
                        pltpu.VMEM((rows, nstate), F32),
                        pltpu.VMEM((nb, nstate), F32),
                        pltpu.VMEM((nblk, rows, LANES), F32)],
        compiler_params=_cparams(("parallel", "arbitrary")),
        name="s5",
    )(za, *args)


def _mix_body(x_ref, u_ref, v_ref, ya_ref, gates_ref, mod_ref, wp_ref, sb_ref, wba_ref, wbb_ref,
              wout_ref, g2_ref, wr_ref, br_ref, h1_ref, xp_ref, route_ref):
    tm_full = x_ref.shape[1]
    npair = u_ref.shape[2] // LANES
    hd = LANES // 2

    t_idx = lax.broadcasted_iota(I32, (SGU_CHUNK, 2 * SGU_CHUNK), 0)
    s_idx = lax.broadcasted_iota(I32, (SGU_CHUNK, 2 * SGU_CHUNK), 1) & (SGU_CHUNK - 1)
    causal = s_idx <= t_idx
    lane = lax.broadcasted_iota(I32, (SGU_CHUNK, LANES), 1)
    first_head = lane < hd
    w_causal = [jnp.where(causal, wp_ref[j], jnp.zeros_like(wp_ref[j])) for j in range(npair)]
    for r0 in range(0, tm_full, MIX_SUBTILE):
        _mix_rows(slice(r0, r0 + MIX_SUBTILE), w_causal, first_head, x_ref, u_ref, v_ref, ya_ref, gates_ref,
                  mod_ref, sb_ref, wba_ref, wbb_ref, wout_ref, g2_ref, wr_ref, br_ref, h1_ref, xp_ref,
                  route_ref)


def _mix_rows(rs, w_causal, first_head, x_ref, u_ref, v_ref, ya_ref, gates_ref, mod_ref, sb_ref, wba_ref,
              wbb_ref, wout_ref, g2_ref, wr_ref, br_ref, h1_ref, xp_ref, route_ref):
    tm = rs.stop - rs.start
    d = x_ref.shape[2]
    npair = len(w_causal)
    mixed_rows = []
    for c in range(rs.start, rs.stop, SGU_CHUNK):
        vc = v_ref[0, c:c + SGU_CHUNK, :]
        blocks = []
        for j in range(npair):
            vb = vc[:, j * LANES:(j + 1) * LANES]
            zero = jnp.zeros_like(vb)
            rhs = jnp.concatenate([jnp.where(first_head, vb, zero), jnp.where(first_head, zero, vb)], axis=0)
            blocks.append(_dot(w_causal[j], rhs))
        mixed_rows.append(jnp.concatenate(blocks, axis=1) + sb_ref[...])
    mixed = jnp.concatenate(mixed_rows, axis=0)
    yb = (u_ref[0, rs, :].astype(F32) * mixed).astype(BF16)

    pa = _dot(ya_ref[0, rs, :], wba_ref[...])
    pb = _dot(yb, wbb_ref[...])
    merged = gates_ref[0, rs, :d].astype(F32) * pa + gates_ref[0, rs, d:].astype(F32) * pb
    o = _dot(merged.astype(BF16), wout_ref[...])
    h1 = x_ref[0, rs, :] + mod_ref[0, 2:3, :] * o
    h1_ref[0, rs, :] = h1

    ms = jnp.mean(h1 * h1, axis=-1, keepdims=True)
    xn = h1 * lax.rsqrt(ms + RMS_EPS) * g2_ref[...]
    xn = xn * (1.0 + mod_ref[0, 4:5, :]) + mod_ref[0, 3:4, :]
    x_hi = xn.astype(BF16)

    logits = _dot(x_hi, wr_ref[...]) + br_ref[...]
    lt = logits.T
    best = lt[0:1, :]
    grp = jnp.zeros((1, tm), I32)
    for gi in range(1, N_GROUPS):
        better = lt[gi:gi + 1, :] > best
        grp = jnp.where(better, gi, grp)
        best = jnp.where(better, lt[gi:gi + 1, :], best)
    den = jnp.zeros((1, tm), F32)
    for gi in range(N_GROUPS):
        den = den + jnp.exp(lt[gi:gi + 1, :] - best)
    pg = 1.0 / den
    le = lt[SUBLANES:2 * SUBLANES, :]
    for gi in range(1, N_GROUPS):
        le = jnp.where(grp == gi, lt[SUBLANES * (gi + 1):SUBLANES * (gi + 2), :], le)
    eidx = lax.broadcasted_iota(I32, (EXPERTS_PER_GROUP, tm), 0).astype(F32)
    none = float(EXPERTS_PER_GROUP)
    v1 = jnp.max(le, axis=0, keepdims=True)
    i1 = jnp.min(jnp.where(le == v1, eidx, none), axis=0, keepdims=True)
    rest = jnp.where(eidx == i1, -jnp.inf, le)
    v2 = jnp.max(rest, axis=0, keepdims=True)
    i2 = jnp.min(jnp.where(rest == v2, eidx, none), axis=0, keepdims=True)
    ex = jnp.exp(v2 - v1)
    p1 = 1.0 / (1.0 + ex)
    wt1 = pg * p1
    wt2 = pg * (ex * p1)
    first_lo = i1 < i2
    lo = jnp.where(first_lo, i1, i2)
    hi = jnp.where(first_lo, i2, i1)
    pair = lo * (2 * EXPERTS_PER_GROUP - 1.0 - lo) * 0.5 + (hi - lo - 1.0)
    cls = grp.astype(F32) * PAIRS_PER_GROUP + pair
    info = jnp.concatenate([cls, jnp.where(first_lo, wt1, wt2), jnp.where(first_lo, wt2, wt1)], axis=0)
    route_ref[:, rs] = jnp.concatenate([info, jnp.zeros((SUBLANES - 3, tm), F32)], axis=0)

    bits = lax.bitcast_convert_type(x_hi.astype(F32), I32)
    packed = lax.shift_right_logical(bits[:, :d // 2], 16) | (bits[:, d // 2:] & jnp.int32(-65536))
    info_cols = jnp.concatenate([info, jnp.zeros((LANES - 3, tm), F32)], axis=0).T
    xp_ref[0, rs, :] = jnp.concatenate([packed, lax.bitcast_convert_type(info_cols, I32)], axis=1)


def _mix(x, u, v, ya, gates, mod, sgu_w, sgu_b, w_ba, w_bb, w_out, g2, w_rg, b_rg, w_re, b_re, tm):
    bsz, s, d = x.shape
    w = u.shape[2]
    nt = bsz * s
    wp = sgu_w.reshape(SGU_HEADS // 2, 2, SGU_CHUNK, SGU_CHUNK).transpose(0, 2, 1, 3)
    wp = wp.reshape(SGU_HEADS // 2, SGU_CHUNK, 2 * SGU_CHUNK).astype(BF16)
    sb = jnp.repeat(sgu_b.T, w // SGU_HEADS, axis=1).astype(F32)
    wr = jnp.zeros((d, LANES), F32)
    wr = wr.at[:, :N_GROUPS].set(w_rg)
    wr = wr.at[:, SUBLANES:SUBLANES + N_GROUPS * EXPERTS_PER_GROUP].set(
        w_re.transpose(1, 0, 2).reshape(d, N_GROUPS * EXPERTS_PER_GROUP))
    br = jnp.zeros((1, LANES), F32)
    br = br.at[0, :N_GROUPS].set(b_rg)
    br = br.at[0, SUBLANES:SUBLANES + N_GROUPS * EXPERTS_PER_GROUP].set(b_re.reshape(-1))
    tok = lambda n: pl.BlockSpec((1, tm, n), lambda b, i: (b, i, 0))
    full = lambda a: pl.BlockSpec(a.shape, lambda b, i: (0,) * a.ndim)
    args = (wp, sb, w_ba.astype(BF16), w_bb.astype(BF16), w_out.astype(BF16), g2.reshape(1, d),
            wr.astype(BF16), br)
    nsteps = s // tm
    return pl.pallas_call(
        _mix_body,
        grid=(bsz, nsteps),
        in_specs=[tok(d), tok(w), tok(w), tok(w), tok(2 * d),
                  pl.BlockSpec((1, 6, d), lambda b, i: (b, 0, 0))] + [full(a) for a in args],
        out_specs=[tok(d), tok(d // 2 + LANES),
                   pl.BlockSpec((SUBLANES, tm), lambda b, i: (0, b * nsteps + i))],
        out_shape=[jax.ShapeDtypeStruct((bsz, s, d), F32),
                   jax.ShapeDtypeStruct((bsz, s, d // 2 + LANES), I32),
                   jax.ShapeDtypeStruct((SUBLANES, nt), F32)],
        compiler_params=_cparams(("parallel", "parallel")),
        name="mix",
    )(x, u, v, ya, gates, mod, *args)


def _tile_lanes(a, reps):
    return jnp.concatenate([a] * reps, axis=1)


def _dispatch_body(route_ref, lo_ref, hi_ref, pos_ref, blk_ref, cls_ref, run, pstart, ranks, classes,
                   earlier):
    i = pl.program_id(0)
    n = pl.num_programs(0)
    tt = route_ref.shape[1]
    reps = tt // LANES
    cid = lax.broadcasted_iota(I32, (LANES, tt), 0).astype(F32)

    @pl.when(i == 0)
    def _():
        run[...] = jnp.zeros_like(run)
        earlier_tok = lax.broadcasted_iota(I32, (tt, tt), 0) < lax.broadcasted_iota(I32, (tt, tt), 1)
        earlier[...] = jnp.where(earlier_tok, 1.0, 0.0).astype(BF16)

    cls = route_ref[0:1, :]
    classes[i] = cls
    onehot = jnp.where(cls == cid, 1.0, 0.0)
    before = _dot(onehot.astype(BF16), earlier[...])
    ranks[i] = jnp.sum(onehot * (before + _tile_lanes(run[...], reps)), axis=0, keepdims=True)
    run[...] = run[...] + jnp.sum(onehot, axis=1, keepdims=True)

    @pl.when(i == n - 1)
    def _():
        counts = run[...]
        nblk = jnp.floor((counts + (EXPERT_ROWS - 1.0)) * (1.0 / EXPERT_ROWS))
        hi_part = jnp.floor(nblk * (1.0 / 16.0))
        lo_part = nblk - 16.0 * hi_part
        upto = lax.broadcasted_iota(I32, (LANES, LANES), 1) <= lax.broadcasted_iota(I32, (LANES, LANES), 0)
        upto = jnp.where(upto, 1.0, 0.0).astype(BF16)
        ends = 16.0 * _dot(upto, hi_part.astype(BF16)) + _dot(upto, lo_part.astype(BF16))
        pstart[...] = (ends - nblk) * EXPERT_ROWS
        diag = lax.broadcasted_iota(I32, (LANES, LANES), 0) == lax.broadcasted_iota(I32, (LANES, LANES), 1)
        on_lanes = lambda a: jnp.sum(jnp.where(diag, a, 0.0), axis=0, keepdims=True)
        cls_ref[...] = jnp.concatenate([on_lanes(counts), on_lanes(pstart[...]),
                                        jnp.zeros((SUBLANES - 2, LANES), F32)], axis=0)
        nb_lanes = blk_ref.shape[1]
        breps = nb_lanes // LANES
        n_used = _tile_lanes(ends[LANES - 1:LANES, :], breps)
        b_idx = lax.broadcasted_iota(I32, (LANES, nb_lanes), 1).astype(F32)
        b_idx = jnp.minimum(b_idx, n_used - 1.0)
        b_cls = jnp.sum(jnp.where(_tile_lanes(ends, breps) <= b_idx, 1.0, 0.0), axis=0, keepdims=True)
        b_member = b_cls == lax.broadcasted_iota(I32, (LANES, nb_lanes), 0).astype(F32)
        pick = lambda tab: jnp.sum(jnp.where(b_member, _tile_lanes(tab, breps), 0.0), axis=0, keepdims=True)
        blk_ref[...] = jnp.concatenate([pick(lo_ref[...]), pick(hi_ref[...]), n_used,
                                        jnp.zeros((SUBLANES - 3, nb_lanes), F32)], axis=0)
        starts = _tile_lanes(pstart[...], reps)

        def tile_positions(k, carry):
            base = jnp.sum(jnp.where(classes[k] == cid, starts, 0.0), axis=0, keepdims=True)
            pos_ref[k] = (base + ranks[k]).astype(I32)
            return carry

        lax.fori_loop(0, n, tile_positions, 0)


def _dispatch(route, n_blocks, tt=1024):
    nt = route.shape[1]
    n = nt // tt
    nb_lanes = pl.cdiv(n_blocks, LANES) * LANES
    lo_tab, hi_tab = [], []
    for g in range(N_GROUPS):
        for a in range(EXPERTS_PER_GROUP):
            for b in range(a + 1, EXPERTS_PER_GROUP):
                lo_tab.append(g * EXPERTS_PER_GROUP + a)
                hi_tab.append(g * EXPERTS_PER_GROUP + b)
    pad = [0] * (LANES - N_CLASSES)
    lo_tile = jnp.broadcast_to(jnp.asarray(lo_tab + pad, F32)[:, None], (LANES, LANES))
    hi_tile = jnp.broadcast_to(jnp.asarray(hi_tab + pad, F32)[:, None], (LANES, LANES))
    const = lambda shape: pl.BlockSpec(shape, lambda i: (0,) * len(shape))
    pos, blk_tab, cls_tab = pl.pallas_call(
        _dispatch_body,
        grid=(n,),
        in_specs=[pl.BlockSpec((SUBLANES, tt), lambda i: (0, i)),
                  const((LANES, LANES)), const((LANES, LANES))],
        out_specs=[const((n, 1, tt)), const((SUBLANES, nb_lanes)), const((SUBLANES, LANES))],
        out_shape=[jax.ShapeDtypeStruct((n, 1, tt), I32),
                   jax.ShapeDtypeStruct((SUBLANES, nb_lanes), F32),
                   jax.ShapeDtypeStruct((SUBLANES, LANES), F32)],
        scratch_shapes=[pltpu.VMEM((LANES, LANES), F32), pltpu.VMEM((LANES, LANES), F32),
                        pltpu.VMEM((n, 1, tt), F32), pltpu.VMEM((n, 1, tt), F32),
                        pltpu.VMEM((tt, tt), BF16)],
        compiler_params=_cparams(("arbitrary",)),
        name="dispatch",
    )(route, lo_tile, hi_tile)
    blk = blk_tab.astype(I32)
    cls = cls_tab.astype(I32)
    return pos.reshape(nt), blk[0, :n_blocks], blk[1, :n_blocks], blk[2, :1], cls[0], cls[1]


ZERO_FILL_ROWS = tuple(SUBLANES << k for k in range((EXPERT_ROWS // SUBLANES).bit_length() - 1))


def _scatter_body(cnt_ref, pst_ref, pos_ref, xp_ref, xs_hbm, ring, zeros, sem, zsem):
    i = pl.program_id(0)
    n = pl.num_programs(0)
    tm = xp_ref.shape[0]
    half = tm // 2

    def rows_done(h):
        pltpu.make_async_copy(ring.at[pl.ds(h * half, half)], xs_hbm.at[pl.ds(0, half)], sem.at[h]).wait()

    for h in range(2):
        @pl.when(i > 0)
        def _():
            rows_done(h)

        ring[h * half:(h + 1) * half, :] = xp_ref[h * half:(h + 1) * half, :]
        for r in range(h * half, (h + 1) * half):
            pltpu.make_async_copy(ring.at[pl.ds(r, 1)], xs_hbm.at[pl.ds(pos_ref[0, 0, r], 1)],
                                  sem.at[h]).start()

    @pl.when(i == n - 1)
    def _():
        rows_done(0)
        rows_done(1)
        zeros[...] = jnp.zeros_like(zeros)

        def fill(wait):
            def per_class(k, carry):
                cnt = cnt_ref[k]
                n_pad = (-cnt) & (EXPERT_ROWS - 1)
                off = pst_ref[k] + cnt
                for r in range(SUBLANES - 1):
                    row = pltpu.make_async_copy(zeros.at[pl.ds(0, 1)], xs_hbm.at[pl.ds(off + r, 1)], zsem)

                    @pl.when(r < (n_pad & (SUBLANES - 1)))
                    def _():
                        row.wait() if wait else row.start()

                off = off + (n_pad & (SUBLANES - 1))
                for size in ZERO_FILL_ROWS:
                    piece = pltpu.make_async_copy(zeros.at[pl.ds(0, size)],
                                                  xs_hbm.at[pl.ds(pl.multiple_of(off, size), size)], zsem)

                    @pl.when((n_pad & size) != 0)
                    def _():
                        piece.wait() if wait else piece.start()

                    off = off + (n_pad & size)
                return carry

            lax.fori_loop(0, N_CLASSES, per_class, 0)

            last = N_CLASSES - 1
            used_rows = pst_ref[last] + cnt_ref[last] + ((-cnt_ref[last]) & (EXPERT_ROWS - 1))
            tail = ZERO_FILL_ROWS[-1]

            def per_piece(j, carry):
                piece = pltpu.make_async_copy(
                    zeros, xs_hbm.at[pl.ds(pl.multiple_of(used_rows + j * tail, tail), tail)], zsem)
                piece.wait() if wait else piece.start()
                return carry

            lax.fori_loop(0, (xs_hbm.shape[0] - used_rows) // tail, per_piece, 0)

        fill(False)
        fill(True)


def _scatter_rows(xp, pos, counts, pstarts, n_blocks, tm=1024):
    nt, wrow = xp.shape
    n = nt // tm
    grid_spec = pltpu.PrefetchScalarGridSpec(
        num_scalar_prefetch=2,
        grid=(n,),
        in_specs=[pl.BlockSpec((1, 1, tm), lambda i, c, p: (i, 0, 0), memory_space=pltpu.SMEM),
                  pl.BlockSpec((tm, wrow), lambda i, c, p: (i, 0))],
        out_specs=pl.BlockSpec(memory_space=pl.ANY),
        scratch_shapes=[pltpu.VMEM((tm, wrow), I32), pltpu.VMEM((ZERO_FILL_ROWS[-1], wrow), I32),
                        pltpu.SemaphoreType.DMA((2,)), pltpu.SemaphoreType.DMA(())],
    )
    return pl.pallas_call(
        _scatter_body,
        grid_spec=grid_spec,
        out_shape=jax.ShapeDtypeStruct((n_blocks * EXPERT_ROWS, wrow), I32),
        compiler_params=_cparams(("arbitrary",)),
        name="scatter_rows",
    )(counts, pstarts, pos.reshape(n, 1, tm), xp)


def _wait_rows(n, src_hbm, dst, sem):
    pltpu.make_async_copy(src_hbm.at[pl.ds(0, n)], dst, sem).wait()


def _unpack_rows(xp):
    left = lax.bitcast_convert_type(lax.shift_left(xp, 16), F32).astype(BF16)
    right = lax.bitcast_convert_type(xp & jnp.int32(-65536), F32).astype(BF16)
    return jnp.concatenate([left, right], axis=1)


def _experts_body(elo_ref, ehi_ref, nu_ref, xs_ref, w13a_ref, w2a_ref, w13b_ref, w2b_ref, o_ref):
    i = pl.program_id(0)
    dh = xs_ref.shape[1] - LANES
    ff = w2a_ref.shape[1]

    @pl.when(i < nu_ref[0])
    def _():
        x = _unpack_rows(xs_ref[:, :dh])
        info = lax.bitcast_convert_type(xs_ref[:, dh:], F32)

        def ffn(w13_ref, w2_ref, weight):
            ab = _dot(x, w13_ref[0])
            a = ab[:, :ff]
            hid = (a * jax.nn.sigmoid(a)) * ab[:, ff:] * weight
            return _dot(hid.astype(BF16), w2_ref[0])

        o_ref[...] = ffn(w13a_ref, w2a_ref, info[:, 1:2]) + ffn(w13b_ref, w2b_ref, info[:, 2:3])

    @pl.when(i >= nu_ref[0])
    def _():
        o_ref[...] = jnp.zeros_like(o_ref)


def _experts(xs, e_lo, e_hi, n_used, w13, w2):
    n_rows, wrow = xs.shape
    n_blocks = n_rows // EXPERT_ROWS
    n_exp, ff, d = w2.shape
    pick = lambda a, use_hi: pl.BlockSpec(
        (1,) + a.shape[1:],
        (lambda i, lo, hi, nu: (hi[i], 0, 0)) if use_hi else (lambda i, lo, hi, nu: (lo[i], 0, 0)))
    grid_spec = pltpu.PrefetchScalarGridSpec(
        num_scalar_prefetch=3,
        grid=(n_blocks,),
        in_specs=[pl.BlockSpec((EXPERT_ROWS, wrow), lambda i, lo, hi, nu: (jnp.minimum(i, nu[0] - 1), 0)),
                  pick(w13, False), pick(w2, False), pick(w13, True), pick(w2, True)],
        out_specs=pl.BlockSpec((EXPERT_ROWS, d), lambda i, lo, hi, nu: (i, 0)),
    )
    return pl.pallas_call(
        _experts_body,
        grid_spec=grid_spec,
        out_shape=jax.ShapeDtypeStruct((n_rows, d), F32),
        compiler_params=_cparams(("arbitrary",)),
        name="experts",
    )(e_lo, e_hi, n_used, xs, w13, w2, w13, w2)


def _final_body(posc_ref, posn_ref, y_hbm, h1_ref, mod_ref, gf_ref, o_ref, ybuf, sem):
    i = pl.program_id(0)
    n = pl.num_programs(0)
    tm = h1_ref.shape[0]
    half = tm // 2

    def fetch(idx_ref, h):
        for r in range(half):
            pltpu.make_async_copy(y_hbm.at[pl.ds(idx_ref[0, 0, h * half + r], 1)],
                                  ybuf.at[h, pl.ds(r, 1)], sem.at[h]).start()

    @pl.when(i == 0)
    def _():
        fetch(posc_ref, 0)
        fetch(posc_ref, 1)

    for h in range(2):
        rows = slice(h * half, (h + 1) * half)
        _wait_rows(half, y_hbm, ybuf.at[h], sem.at[h])
        hres = h1_ref[rows, :] + mod_ref[0, 5:6, :] * ybuf[h]
        ms = jnp.mean(hres * hres, axis=-1, keepdims=True)
        o_ref[rows, :] = hres * lax.rsqrt(ms + RMS_EPS) * gf_ref[...]
        fetch(posn_ref, h)

    @pl.when(i == n - 1)
    def _():
        _wait_rows(half, y_hbm, ybuf.at[0], sem.at[0])
        _wait_rows(half, y_hbm, ybuf.at[1], sem.at[1])


def _final(pos, y_sorted, h1, mod, gf, seq, tm):
    nt, d = h1.shape
    n = nt // tm
    pos3 = pos.reshape(n, 1, tm)
    idx_spec = lambda f: pl.BlockSpec((1, 1, tm), f, memory_space=pltpu.SMEM)
    return pl.pallas_call(
        _final_body,
        grid=(n,),
        in_specs=[idx_spec(lambda i: (i, 0, 0)),
                  idx_spec(lambda i: (jnp.minimum(i + 1, n - 1), 0, 0)),
                  pl.BlockSpec(memory_space=pl.ANY),
                  pl.BlockSpec((tm, d), lambda i: (i, 0)),
                  pl.BlockSpec((1, 6, d), lambda i: (i * tm // seq, 0, 0)),
                  pl.BlockSpec((1, d), lambda i: (0, 0))],
        out_specs=pl.BlockSpec((tm, d), lambda i: (i, 0)),
        out_shape=jax.ShapeDtypeStruct((nt, d), F32),
        scratch_shapes=[pltpu.VMEM((2, tm // 2, d), F32), pltpu.SemaphoreType.DMA((2,))],
        compiler_params=_cparams(("arbitrary",)),
        name="final",
    )(pos3, pos3, y_sorted, h1, mod, gf.reshape(1, d))


def kernel(x, c, w_ada, b_ada, norm1_g, w_in, w_gate, b_gate, ssm_a_re, ssm_a_im, ssm_b_re, ssm_b_im, ssm_c_re, ssm_c_im, ssm_d, ssm_log_step, w_glu, b_glu, sgu_ln_g, sgu_ln_b, sgu_w, sgu_b, w_branch_a, w_branch_b, w_out, norm2_g, w_router_group, b_router_group, w_router_expert, b_router_expert, w1, w3, w2, norm_f_g):
    bsz, seq, d = x.shape
    depth = w_ada.shape[0]
    assert depth == 1 and bsz % SCAN_BATCHES == 0 and seq % 512 == 0
    l = 0
    mod = _adaln(c, w_ada[l], b_ada[l]).reshape(bsz, 6, d)
    za, u, v, gates, w13_bf, w2_bf = _inproj(x, mod, norm1_g[l], w_in[l], w_gate[l], b_gate[l],
                                             sgu_ln_g[l], sgu_ln_b[l], w1[l], w3[l], w2[l], tm=1024)
    a_row, s5_in, s5_out = _s5_params(ssm_a_re[l], ssm_a_im[l], ssm_b_re[l], ssm_b_im[l],
                                      ssm_c_re[l], ssm_c_im[l], ssm_log_step[l])
    ya = _s5(za, a_row, s5_in, s5_out, ssm_d[l], w_glu[l], b_glu[l])
    h1, xp, route = _mix(x, u, v, ya, gates, mod, sgu_w[l], sgu_b[l], w_branch_a[l], w_branch_b[l],
                         w_out[l], norm2_g[l], w_router_group[l], b_router_group[l],
                         w_router_expert[l], b_router_expert[l], tm=1024)
    nt = bsz * seq
    n_blocks = nt // EXPERT_ROWS + N_CLASSES
    pos, e_lo, e_hi, n_used, counts, pstarts = _dispatch(route, n_blocks)
    xs = _scatter_rows(xp.reshape(nt, xp.shape[-1]), pos, counts, pstarts, n_blocks)
    y_sorted = _experts(xs, e_lo, e_hi, n_used, w13_bf, w2_bf)
    out = _final(pos, y_sorted, h1.reshape(nt, d), mod, norm_f_g, seq, tm=1024)
    return out.reshape(bsz, seq, d)
```

```python
import functools
import math

import jax
import jax.numpy as jnp
from jax import lax
from jax.experimental import pallas as pl
from jax.experimental.pallas import tpu as pltpu

F32 = jnp.float32
BF16 = jnp.bfloat16
I32 = jnp.int32

LANES = 128
SUBLANES = 8
VMEM_LIMIT = 56 * 1024 * 1024

RMS_EPS = 1e-6
LN_EPS = 1e-5

SSM_GROUP_CH = 16
SSM_STATE = 64
GROUPS_PER_BLOCK = LANES // SSM_GROUP_CH
SGU_HEADS = 8
SGU_CHUNK = 128
N_GROUPS = 4
EXPERTS_PER_GROUP = 8
PAIRS_PER_GROUP = EXPERTS_PER_GROUP * (EXPERTS_PER_GROUP - 1) // 2
N_CLASSES = N_GROUPS * PAIRS_PER_GROUP
EXPERT_ROWS = 256

SCAN_BATCHES = SUBLANES
SCAN_TS = 128
SLAB_PITCH = SCAN_TS + 8
SCAN_CHUNK = 64
INPROJ_SUBTILE = 256
MIX_SUBTILE = 512


def _gelu(x):
    return 0.5 * x * (1.0 + jnp.tanh(math.sqrt(2.0 / math.pi) * (x + 0.044715 * (x * x * x))))


def _dot(a, b):
    return jnp.dot(a, b, preferred_element_type=F32)


def _split_bf16(a):
    hi = a.astype(BF16)
    lo = (a - hi.astype(F32)).astype(BF16)
    return hi, lo


def _cparams(sem):
    return pltpu.CompilerParams(dimension_semantics=sem, vmem_limit_bytes=VMEM_LIMIT)


def _adaln_body(c_ref, w_ref, b_ref, o_ref):
    c = c_ref[...]
    act = c * jax.nn.sigmoid(c)
    a_hi, a_lo = _split_bf16(act)
    w_hi, w_lo = _split_bf16(w_ref[...])
    o_ref[...] = _dot(a_hi, w_hi) + _dot(a_hi, w_lo) + _dot(a_lo, w_hi) + b_ref[...]


def _adaln(c, w, b):
    bsz, d = c.shape
    n = w.shape[1]
    tn = 1024
    return pl.pallas_call(
        _adaln_body,
        grid=(n // tn,),
        in_specs=[pl.BlockSpec((bsz, d), lambda j: (0, 0)),
                  pl.BlockSpec((d, tn), lambda j: (0, j)),
                  pl.BlockSpec((1, tn), lambda j: (0, j))],
        out_specs=pl.BlockSpec((bsz, tn), lambda j: (0, j)),
        out_shape=jax.ShapeDtypeStruct((bsz, n), F32),
        compiler_params=_cparams(("arbitrary",)),
        name="adaln",
    )(c, w, b.reshape(1, n))


def _inproj_body(x_ref, mod_ref, g1_ref, win_ref, wgate_ref, bgate_ref, lng_ref, lnb_ref,
                 w1_ref, w3_ref, w2_ref, za_ref, u_ref, v_ref, gates_ref, w13_ref, w2b_ref):
    w = za_ref.shape[-1]
    tm = x_ref.shape[1]
    ff = w1_ref.shape[2]
    w13_ref[0, :, :ff] = w1_ref[0].astype(BF16)
    w13_ref[0, :, ff:] = w3_ref[0].astype(BF16)
    w2b_ref[0] = w2_ref[0].astype(BF16)
    for r0 in range(0, tm, INPROJ_SUBTILE):
        rs = slice(r0, r0 + INPROJ_SUBTILE)
        x = x_ref[0, rs, :]
        ms = jnp.mean(x * x, axis=-1, keepdims=True)
        xn = x * lax.rsqrt(ms + RMS_EPS) * g1_ref[...]
        xn = xn * (1.0 + mod_ref[0, 1:2, :]) + mod_ref[0, 0:1, :]
        xb = xn.astype(BF16)
        proj = _dot(xb, win_ref[...])
        za_ref[0, rs, :] = proj[:, :w]
        u_ref[0, rs, :] = _gelu(proj[:, w:2 * w]).astype(BF16)
        gv = _gelu(proj[:, 2 * w:])
        mu = jnp.mean(gv, axis=-1, keepdims=True)
        cen = gv - mu
        var = jnp.mean(cen * cen, axis=-1, keepdims=True)
        v_ref[0, rs, :] = (cen * lax.rsqrt(var + LN_EPS) * lng_ref[...] + lnb_ref[...]).astype(BF16)
        gates_ref[0, rs, :] = jax.nn.sigmoid(_dot(xb, wgate_ref[...]) + bgate_ref[...]).astype(BF16)


def _inproj(x, mod, g1, w_in, w_gate, b_gate, ln_g, ln_b, w1, w3, w2, tm):
    bsz, s, d = x.shape
    w = w_in.shape[1] // 3
    ng = w_gate.shape[1]
    n_exp, _, ff = w1.shape
    nsteps = s // tm
    parts = bsz * nsteps // n_exp
    assert bsz * nsteps == n_exp * parts and d % (parts * SUBLANES) == 0 and ff % (parts * SUBLANES) == 0
    tok = lambda n: pl.BlockSpec((1, tm, n), lambda b, i: (b, i, 0))
    full = lambda a: pl.BlockSpec(a.shape, lambda b, i: (0,) * a.ndim)
    wslice = lambda rows, cols: pl.BlockSpec(
        (1, rows // parts, cols), lambda b, i: ((b * nsteps + i) // parts, (b * nsteps + i) % parts, 0))
    args = (g1.reshape(1, d), w_in.astype(BF16), w_gate.astype(BF16), b_gate.reshape(1, ng),
            ln_g.reshape(1, w), ln_b.reshape(1, w))
    return pl.pallas_call(
        _inproj_body,
        grid=(bsz, nsteps),
        in_specs=[tok(d), pl.BlockSpec((1, 6, d), lambda b, i: (b, 0, 0))] + [full(a) for a in args]
                 + [wslice(d, ff), wslice(d, ff), wslice(ff, d)],
        out_specs=[tok(w), tok(w), tok(w), tok(ng), wslice(d, 2 * ff), wslice(ff, d)],
        out_shape=[jax.ShapeDtypeStruct((bsz, s, w), F32),
                   jax.ShapeDtypeStruct((bsz, s, w), BF16),
                   jax.ShapeDtypeStruct((bsz, s, w), BF16),
                   jax.ShapeDtypeStruct((bsz, s, ng), BF16),
                   jax.ShapeDtypeStruct((n_exp, d, 2 * ff), BF16),
                   jax.ShapeDtypeStruct((n_exp, ff, d), BF16)],
        compiler_params=_cparams(("parallel", "parallel")),
        name="in_proj",
    )(x, mod, *args, w1, w3, w2)


def _s5_params(a_re, a_im, b_re, b_im, c_re, c_im, log_step):
    g, n = a_re.shape
    nblk = g // GROUPS_PER_BLOCK
    lam_re = jnp.minimum(a_re.astype(F32), -1e-4)
    lam_im = a_im.astype(F32)
    dt = jnp.exp(log_step.astype(F32))[:, None]
    mag = jnp.exp(lam_re * dt)
    ab_re = mag * jnp.cos(lam_im * dt)
    ab_im = mag * jnp.sin(lam_im * dt)
    den = lam_re * lam_re + lam_im * lam_im
    nr = ab_re - 1.0
    f_re = (nr * lam_re + ab_im * lam_im) / den
    f_im = (ab_im * lam_re - nr * lam_im) / den
    bt_re = b_re.astype(F32).transpose(0, 2, 1)
    bt_im = b_im.astype(F32).transpose(0, 2, 1)
    w_re = f_re[:, None, :] * bt_re - f_im[:, None, :] * bt_im
    w_im = f_re[:, None, :] * bt_im + f_im[:, None, :] * bt_re
    eye = jnp.eye(GROUPS_PER_BLOCK, dtype=F32)

    def bdiag_in(wt):
        t = wt.reshape(nblk, GROUPS_PER_BLOCK, SSM_GROUP_CH, 1, n) * eye[None, :, None, :, None]
        return t.reshape(nblk, GROUPS_PER_BLOCK * SSM_GROUP_CH, GROUPS_PER_BLOCK * n)

    def bdiag_out(ct):
        t = ct.transpose(0, 2, 1).reshape(nblk, GROUPS_PER_BLOCK, n, 1, SSM_GROUP_CH)
        t = t * eye[None, :, None, :, None]
        return t.reshape(nblk, GROUPS_PER_BLOCK * n, GROUPS_PER_BLOCK * SSM_GROUP_CH)

    w_in = jnp.concatenate([bdiag_in(w_re), bdiag_in(w_im)], axis=2).astype(BF16)
    w_out = jnp.concatenate([bdiag_out(c_re.astype(F32)), -bdiag_out(c_im.astype(F32))],
                            axis=1).astype(BF16)
    a_row = jnp.concatenate([ab_re.reshape(nblk, -1), ab_im.reshape(nblk, -1)], axis=1).reshape(1, -1)
    return a_row, w_in, w_out


def _s5_body(z_ref, a_ref, win_ref, wout_ref, d_ref, wglu_ref, bglu_ref, o_ref,
             zslab, zsb, xs, hstate, yslab):
    nb, ts, w = z_ref.shape
    rows = nb * ts
    nblk = w // LANES
    half = GROUPS_PER_BLOCK * SSM_STATE
    sw = 2 * half

    @pl.when(pl.program_id(1) == 0)
    def _():
        hstate[...] = jnp.zeros_like(hstate)

    for b in range(nb):
        for c in range(nblk):
            zslab[c, b * SLAB_PITCH:b * SLAB_PITCH + ts, :] = z_ref[b, :, c * LANES:(c + 1) * LANES]

    for s in range(ts):
        for c in range(nblk):
            zsb[s * nb:(s + 1) * nb, c * LANES:(c + 1) * LANES] = zslab.at[c][pl.ds(s, nb, stride=SLAB_PITCH), :]

    n_chunks = ts // SCAN_CHUNK
    crows = SCAN_CHUNK * nb
    units = [(j, q) for j in range(nblk) for q in range(n_chunks)]

    def in_map(j, q):
        rs = slice(q * crows, (q + 1) * crows)
        xs[rs, j * sw:(j + 1) * sw] = _dot(zsb[rs, j * LANES:(j + 1) * LANES].astype(BF16), win_ref[j])

    carry = {}

    def scan(j, q):
        re_sl = slice(j * sw, j * sw + half)
        im_sl = slice(j * sw + half, (j + 1) * sw)
        a_r = jnp.broadcast_to(a_ref[:, re_sl], (nb, half))
        a_i = jnp.broadcast_to(a_ref[:, im_sl], (nb, half))
        h_r, h_i = carry[j] if q else (hstate[:, re_sl], hstate[:, im_sl])
        for s in range(q * SCAN_CHUNK, (q + 1) * SCAN_CHUNK):
            x_r = xs[s * nb:(s + 1) * nb, re_sl]
            x_i = xs[s * nb:(s + 1) * nb, im_sl]
            h_r, h_i = a_r * h_r - a_i * h_i + x_r, a_r * h_i + a_i * h_r + x_i
            xs[s * nb:(s + 1) * nb, re_sl] = h_r
            xs[s * nb:(s + 1) * nb, im_sl] = h_i
        carry[j] = (h_r, h_i)
        if q == n_chunks - 1:
            hstate[:, re_sl] = h_r
            hstate[:, im_sl] = h_i

    gel = {}

    def out_map(j, q):
        rs = slice(q * crows, (q + 1) * crows)
        y = _dot(xs[rs, j * sw:(j + 1) * sw].astype(BF16), wout_ref[j])
        y = y + d_ref[:, j * LANES:(j + 1) * LANES] * zsb[rs, j * LANES:(j + 1) * LANES]
        gel[j, q] = _gelu(y)

    for k in range(len(units) + 2):
        if k < len(units):
            in_map(*units[k])
        if 0 <= k - 1 < len(units):
            scan(*units[k - 1])
        if 0 <= k - 2 < len(units):
            out_map(*units[k - 2])

    for q in range(n_chunks):
        g = jnp.concatenate([gel[j, q] for j in range(nblk)], axis=1)
        g = g * jax.nn.sigmoid(_dot(g.astype(BF16), wglu_ref[...]) + bglu_ref[...])
        for c in range(nblk):
            yslab[c, q * crows:(q + 1) * crows, :] = g[:, c * LANES:(c + 1) * LANES]

    for b in range(nb):
        for c in range(nblk):
            o_ref[b, :, c * LANES:(c + 1) * LANES] = yslab.at[c][pl.ds(b, ts, stride=nb), :].astype(BF16)


def _s5(za, a_row, w_in, w_out, d_skip, w_glu, b_glu):
    bsz, s, w = za.shape
    nb, ts = SCAN_BATCHES, SCAN_TS
    rows = nb * ts
    nblk = w // LANES
    nstate = a_row.shape[1]
    full = lambda a: pl.BlockSpec(a.shape, lambda b, i: (0,) * a.ndim)
    args = (a_row, w_in, w_out, d_skip.reshape(1, w).astype(F32), w_glu.astype(BF16), b_glu.reshape(1, w))
    return pl.pallas_call(
        _s5_body,
        grid=(bsz // nb, s // ts),
        in_specs=[pl.BlockSpec((nb, ts, w), lambda b, i: (b, i, 0))] + [full(a) for a in args],
        out_specs=pl.BlockSpec((nb, ts, w), lambda b, i: (b, i, 0)),
        out_shape=jax.ShapeDtypeStruct((bsz, s, w), BF16),
        scratch_shapes=[pltpu.VMEM((nblk, nb * SLAB_PITCH, LANES), F32),
                        pltpu.VMEM((rows, w), F32),
                        pltpu.VMEM((rows, nstate), F32),
                        pltpu.VMEM((nb, nstate), F32),
                        pltpu.VMEM((nblk, rows, LANES), F32)],
        compiler_params=_cparams(("parallel", "arbitrary")),
        name="s5",
    )(za, *args)


def _mix_body(x_ref, u_ref, v_ref, ya_ref, gates_ref, mod_ref, wp_ref, sb_ref, wba_ref, wbb_ref,
              wout_ref, g2_ref, wr_ref, br_ref, h1_ref, xp_ref, route_ref):
    tm_full = x_ref.shape[1]
    npair = u_ref.shape[2] // LANES
    hd = LANES // 2

    t_idx = lax.broadcasted_iota(I32, (SGU_CHUNK, 2 * SGU_CHUNK), 0)
    s_idx = lax.broadcasted_iota(I32, (SGU_CHUNK, 2 * SGU_CHUNK), 1) & (SGU_CHUNK - 1)
    causal = s_idx <= t_idx
    lane = lax.broadcasted_iota(I32, (SGU_CHUNK, LANES), 1)
    first_head = lane < hd
    w_causal = [jnp.where(causal, wp_ref[j], jnp.zeros_like(wp_ref[j])) for j in range(npair)]
    for r0 in range(0, tm_full, MIX_SUBTILE):
        _mix_rows(slice(r0, r0 + MIX_SUBTILE), w_causal, first_head, x_ref, u_ref, v_ref, ya_ref, gates_ref,
                  mod_ref, sb_ref, wba_ref, wbb_ref, wout_ref, g2_ref, wr_ref, br_ref, h1_ref, xp_ref,
                  route_ref)


def _mix_rows(rs, w_causal, first_head, x_ref, u_ref, v_ref, ya_ref, gates_ref, mod_ref, sb_ref, wba_ref,
              wbb_ref, wout_ref, g2_ref, wr_ref, br_ref, h1_ref, xp_ref, route_ref):
    tm = rs.stop - rs.start
    d = x_ref.shape[2]
    npair = len(w_causal)
    mixed_rows = []
    for c in range(rs.start, rs.stop, SGU_CHUNK):
        vc = v_ref[0, c:c + SGU_CHUNK, :]
        blocks = []
        for j in range(npair):
            vb = vc[:, j * LANES:(j + 1) * LANES]
            zero = jnp.zeros_like(vb)
            rhs = jnp.concatenate([jnp.where(first_head, vb, zero), jnp.where(first_head, zero, vb)], axis=0)
            blocks.append(_dot(w_causal[j], rhs))
        mixed_rows.append(jnp.concatenate(blocks, axis=1) + sb_ref[...])
    mixed = jnp.concatenate(mixed_rows, axis=0)
    yb = (u_ref[0, rs, :].astype(F32) * mixed).astype(BF16)

    pa = _dot(ya_ref[0, rs, :], wba_ref[...])
    pb = _dot(yb, wbb_ref[...])
    merged = gates_ref[0, rs, :d].astype(F32) * pa + gates_ref[0, rs, d:].astype(F32) * pb
    o = _dot(merged.astype(BF16), wout_ref[...])
    h1 = x_ref[0, rs, :] + mod_ref[0, 2:3, :] * o
    h1_ref[0, rs, :] = h1

    ms = jnp.mean(h1 * h1, axis=-1, keepdims=True)
    xn = h1 * lax.rsqrt(ms + RMS_EPS) * g2_ref[...]
    xn = xn * (1.0 + mod_ref[0, 4:5, :]) + mod_ref[0, 3:4, :]
    x_hi = xn.astype(BF16)

    logits = _dot(x_hi, wr_ref[...]) + br_ref[...]
    lt = logits.T
    best = lt[0:1, :]
    grp = jnp.zeros((1, tm), I32)
    for gi in range(1, N_GROUPS):
        better = lt[gi:gi + 1, :] > best
        grp = jnp.where(better, gi, grp)
        best = jnp.where(better, lt[gi:gi + 1, :], best)
    den = jnp.zeros((1, tm), F32)
    for gi in range(N_GROUPS):
        den = den + jnp.exp(lt[gi:gi + 1, :] - best)
    pg = 1.0 / den
    le = lt[SUBLANES:2 * SUBLANES, :]
    for gi in range(1, N_GROUPS):
        le = jnp.where(grp == gi, lt[SUBLANES * (gi + 1):SUBLANES * (gi + 2), :], le)
    eidx = lax.broadcasted_iota(I32, (EXPERTS_PER_GROUP, tm), 0).astype(F32)
    none = float(EXPERTS_PER_GROUP)
    v1 = jnp.max(le, axis=0, keepdims=True)
    i1 = jnp.min(jnp.where(le == v1, eidx, none), axis=0, keepdims=True)
    rest = jnp.where(eidx == i1, -jnp.inf, le)
    v2 = jnp.max(rest, axis=0, keepdims=True)
    i2 = jnp.min(jnp.where(rest == v2, eidx, none), axis=0, keepdims=True)
    ex = jnp.exp(v2 - v1)
    p1 = 1.0 / (1.0 + ex)
    wt1 = pg * p1
    wt2 = pg * (ex * p1)
    first_lo = i1 < i2
    lo = jnp.where(first_lo, i1, i2)
    hi = jnp.where(first_lo, i2, i1)
    pair = lo * (2 * EXPERTS_PER_GROUP - 1.0 - lo) * 0.5 + (hi - lo - 1.0)
    cls = grp.astype(F32) * PAIRS_PER_GROUP + pair
    info = jnp.concatenate([cls, jnp.where(first_lo, wt1, wt2), jnp.where(first_lo, wt2, wt1)], axis=0)
    route_ref[:, rs] = jnp.concatenate([info, jnp.zeros((SUBLANES - 3, tm), F32)], axis=0)

    bits = lax.bitcast_convert_type(x_hi.astype(F32), I32)
    packed = lax.shift_right_logical(bits[:, :d // 2], 16) | (bits[:, d // 2:] & jnp.int32(-65536))
    info_cols = jnp.concatenate([info, jnp.zeros((LANES - 3, tm), F32)], axis=0).T
    xp_ref[0, rs, :] = jnp.concatenate([packed, lax.bitcast_convert_type(info_cols, I32)], axis=1)


def _mix(x, u, v, ya, gates, mod, sgu_w, sgu_b, w_ba, w_bb, w_out, g2, w_rg, b_rg, w_re, b_re, tm):
    bsz, s, d = x.shape
    w = u.shape[2]
    nt = bsz * s
    wp = sgu_w.reshape(SGU_HEADS // 2, 2, SGU_CHUNK, SGU_CHUNK).transpose(0, 2, 1, 3)
    wp = wp.reshape(SGU_HEADS // 2, SGU_CHUNK, 2 * SGU_CHUNK).astype(BF16)
    sb = jnp.repeat(sgu_b.T, w // SGU_HEADS, axis=1).astype(F32)
    wr = jnp.zeros((d, LANES), F32)
    wr = wr.at[:, :N_GROUPS].set(w_rg)
    wr = wr.at[:, SUBLANES:SUBLANES + N_GROUPS * EXPERTS_PER_GROUP].set(
        w_re.transpose(1, 0, 2).reshape(d, N_GROUPS * EXPERTS_PER_GROUP))
    br = jnp.zeros((1, LANES), F32)
    br = br.at[0, :N_GROUPS].set(b_rg)
    br = br.at[0, SUBLANES:SUBLANES + N_GROUPS * EXPERTS_PER_GROUP].set(b_re.reshape(-1))
    tok = lambda n: pl.BlockSpec((1, tm, n), lambda b, i: (b, i, 0))
    full = lambda a: pl.BlockSpec(a.shape, lambda b, i: (0,) * a.ndim)
    args = (wp, sb, w_ba.astype(BF16), w_bb.astype(BF16), w_out.astype(BF16), g2.reshape(1, d),
            wr.astype(BF16), br)
    nsteps = s // tm
    return pl.pallas_call(
        _mix_body,
        grid=(bsz, nsteps),
        in_specs=[tok(d), tok(w), tok(w), tok(w), tok(2 * d),
                  pl.BlockSpec((1, 6, d), lambda b, i: (b, 0, 0))] + [full(a) for a in args],
        out_specs=[tok(d), tok(d // 2 + LANES),
                   pl.BlockSpec((SUBLANES, tm), lambda b, i: (0, b * nsteps + i))],
        out_shape=[jax.ShapeDtypeStruct((bsz, s, d), F32),
                   jax.ShapeDtypeStruct((bsz, s, d // 2 + LANES), I32),
                   jax.ShapeDtypeStruct((SUBLANES, nt), F32)],
        compiler_params=_cparams(("parallel", "parallel")),
        name="mix",
    )(x, u, v, ya, gates, mod, *args)


def _tile_lanes(a, reps):
    return jnp.concatenate([a] * reps, axis=1)


def _dispatch_body(route_ref, lo_ref, hi_ref, pos_ref, blk_ref, cls_ref, run, pstart, ranks, classes,
                   earlier):
    i = pl.program_id(0)
    n = pl.num_programs(0)
    tt = route_ref.shape[1]
    reps = tt // LANES
    cid = lax.broadcasted_iota(I32, (LANES, tt), 0).astype(F32)

    @pl.when(i == 0)
    def _():
        run[...] = jnp.zeros_like(run)
        earlier_tok = lax.broadcasted_iota(I32, (tt, tt), 0) < lax.broadcasted_iota(I32, (tt, tt), 1)
        earlier[...] = jnp.where(earlier_tok, 1.0, 0.0).astype(BF16)

    cls = route_ref[0:1, :]
    classes[i] = cls
    onehot = jnp.where(cls == cid, 1.0, 0.0)
    before = _dot(onehot.astype(BF16), earlier[...])
    ranks[i] = jnp.sum(onehot * (before + _tile_lanes(run[...], reps)), axis=0, keepdims=True)
    run[...] = run[...] + jnp.sum(onehot, axis=1, keepdims=True)

    @pl.when(i == n - 1)
    def _():
        counts = run[...]
        nblk = jnp.floor((counts + (EXPERT_ROWS - 1.0)) * (1.0 / EXPERT_ROWS))
        hi_part = jnp.floor(nblk * (1.0 / 16.0))
        lo_part = nblk - 16.0 * hi_part
        upto = lax.broadcasted_iota(I32, (LANES, LANES), 1) <= lax.broadcasted_iota(I32, (LANES, LANES), 0)
        upto = jnp.where(upto, 1.0, 0.0).astype(BF16)
        ends = 16.0 * _dot(upto, hi_part.astype(BF16)) + _dot(upto, lo_part.astype(BF16))
        pstart[...] = (ends - nblk) * EXPERT_ROWS
        diag = lax.broadcasted_iota(I32, (LANES, LANES), 0) == lax.broadcasted_iota(I32, (LANES, LANES), 1)
        on_lanes = lambda a: jnp.sum(jnp.where(diag, a, 0.0), axis=0, keepdims=True)
        cls_ref[...] = jnp.concatenate([on_lanes(counts), on_lanes(pstart[...]),
                                        jnp.zeros((SUBLANES - 2, LANES), F32)], axis=0)
        nb_lanes = blk_ref.shape[1]
        breps = nb_lanes // LANES
        n_used = _tile_lanes(ends[LANES - 1:LANES, :], breps)
        b_idx = lax.broadcasted_iota(I32, (LANES, nb_lanes), 1).astype(F32)
        b_idx = jnp.minimum(b_idx, n_used - 1.0)
        b_cls = jnp.sum(jnp.where(_tile_lanes(ends, breps) <= b_idx, 1.0, 0.0), axis=0, keepdims=True)
        b_member = b_cls == lax.broadcasted_iota(I32, (LANES, nb_lanes), 0).astype(F32)
        pick = lambda tab: jnp.sum(jnp.where(b_member, _tile_lanes(tab, breps), 0.0), axis=0, keepdims=True)
        blk_ref[...] = jnp.concatenate([pick(lo_ref[...]), pick(hi_ref[...]), n_used,
                                        jnp.zeros((SUBLANES - 3, nb_lanes), F32)], axis=0)
        starts = _tile_lanes(pstart[...], reps)

        def tile_positions(k, carry):
            base = jnp.sum(jnp.where(classes[k] == cid, starts, 0.0), axis=0, keepdims=True)
            pos_ref[k] = (base + ranks[k]).astype(I32)
            return carry

        lax.fori_loop(0, n, tile_positions, 0)


def _dispatch(route, n_blocks, tt=1024):
    nt = route.shape[1]
    n = nt // tt
    nb_lanes = pl.cdiv(n_blocks, LANES) * LANES
    lo_tab, hi_tab = [], []
    for g in range(N_GROUPS):
        for a in range(EXPERTS_PER_GROUP):
            for b in range(a + 1, EXPERTS_PER_GROUP):
                lo_tab.append(g * EXPERTS_PER_GROUP + a)
                hi_tab.append(g * EXPERTS_PER_GROUP + b)
    pad = [0] * (LANES - N_CLASSES)
    lo_tile = jnp.broadcast_to(jnp.asarray(lo_tab + pad, F32)[:, None], (LANES, LANES))
    hi_tile = jnp.broadcast_to(jnp.asarray(hi_tab + pad, F32)[:, None], (LANES, LANES))
    const = lambda shape: pl.BlockSpec(shape, lambda i: (0,) * len(shape))
    pos, blk_tab, cls_tab = pl.pallas_call(
        _dispatch_body,
        grid=(n,),
        in_specs=[pl.BlockSpec((SUBLANES, tt), lambda i: (0, i)),
                  const((LANES, LANES)), const((LANES, LANES))],
        out_specs=[const((n, 1, tt)), const((SUBLANES, nb_lanes)), const((SUBLANES, LANES))],
        out_shape=[jax.ShapeDtypeStruct((n, 1, tt), I32),
                   jax.ShapeDtypeStruct((SUBLANES, nb_lanes), F32),
                   jax.ShapeDtypeStruct((SUBLANES, LANES), F32)],
        scratch_shapes=[pltpu.VMEM((LANES, LANES), F32), pltpu.VMEM((LANES, LANES), F32),
                        pltpu.VMEM((n, 1, tt), F32), pltpu.VMEM((n, 1, tt), F32),
                        pltpu.VMEM((tt, tt), BF16)],
        compiler_params=_cparams(("arbitrary",)),
        name="dispatch",
    )(route, lo_tile, hi_tile)
    blk = blk_tab.astype(I32)
    cls = cls_tab.astype(I32)
    return pos.reshape(nt), blk[0, :n_blocks], blk[1, :n_blocks], blk[2, :1], cls[0], cls[1]


ZERO_FILL_ROWS = tuple(SUBLANES << k for k in range((EXPERT_ROWS // SUBLANES).bit_length() - 1))


def _scatter_body(cnt_ref, pst_ref, pos_ref, xp_ref, xs_hbm, ring, zeros, sem, zsem):
    i = pl.program_id(0)
    n = pl.num_programs(0)
    tm = xp_ref.shape[0]
    half = tm // 2

    def rows_done(h):
        pltpu.make_async_copy(ring.at[pl.ds(h * half, half)], xs_hbm.at[pl.ds(0, half)], sem.at[h]).wait()

    for h in range(2):
        @pl.when(i > 0)
        def _():
            rows_done(h)

        ring[h * half:(h + 1) * half, :] = xp_ref[h * half:(h + 1) * half, :]
        for r in range(h * half, (h + 1) * half):
            pltpu.make_async_copy(ring.at[pl.ds(r, 1)], xs_hbm.at[pl.ds(pos_ref[0, 0, r], 1)],
                                  sem.at[h]).start()

    @pl.when(i == n - 1)
    def _():
        rows_done(0)
        rows_done(1)
        zeros[...] = jnp.zeros_like(zeros)

        def fill(wait):
            def per_class(k, carry):
                cnt = cnt_ref[k]
                n_pad = (-cnt) & (EXPERT_ROWS - 1)
                off = pst_ref[k] + cnt
                for r in range(SUBLANES - 1):
                    row = pltpu.make_async_copy(zeros.at[pl.ds(0, 1)], xs_hbm.at[pl.ds(off + r, 1)], zsem)

                    @pl.when(r < (n_pad & (SUBLANES - 1)))
                    def _():
                        row.wait() if wait else row.start()

                off = off + (n_pad & (SUBLANES - 1))
                for size in ZERO_FILL_ROWS:
                    piece = pltpu.make_async_copy(zeros.at[pl.ds(0, size)],
                                                  xs_hbm.at[pl.ds(pl.multiple_of(off, size), size)], zsem)

                    @pl.when((n_pad & size) != 0)
                    def _():
                        piece.wait() if wait else piece.start()

                    off = off + (n_pad & size)
                return carry

            lax.fori_loop(0, N_CLASSES, per_class, 0)

            last = N_CLASSES - 1
            used_rows = pst_ref[last] + cnt_ref[last] + ((-cnt_ref[last]) & (EXPERT_ROWS - 1))
            tail = ZERO_FILL_ROWS[-1]

            def per_piece(j, carry):
                piece = pltpu.make_async_copy(
                    zeros, xs_hbm.at[pl.ds(pl.multiple_of(used_rows + j * tail, tail), tail)], zsem)
                piece.wait() if wait else piece.start()
                return carry

            lax.fori_loop(0, (xs_hbm.shape[0] - used_rows) // tail, per_piece, 0)

        fill(False)
        fill(True)


def _scatter_rows(xp, pos, counts, pstarts, n_blocks, tm=1024):
    nt, wrow = xp.shape
    n = nt // tm
    grid_spec = pltpu.PrefetchScalarGridSpec(
        num_scalar_prefetch=2,
        grid=(n,),
        in_specs=[pl.BlockSpec((1, 1, tm), lambda i, c, p: (i, 0, 0), memory_space=pltpu.SMEM),
                  pl.BlockSpec((tm, wrow), lambda i, c, p: (i, 0))],
        out_specs=pl.BlockSpec(memory_space=pl.ANY),
        scratch_shapes=[pltpu.VMEM((tm, wrow), I32), pltpu.VMEM((ZERO_FILL_ROWS[-1], wrow), I32),
                        pltpu.SemaphoreType.DMA((2,)), pltpu.SemaphoreType.DMA(())],
    )
    return pl.pallas_call(
        _scatter_body,
        grid_spec=grid_spec,
        out_shape=jax.ShapeDtypeStruct((n_blocks * EXPERT_ROWS, wrow), I32),
        compiler_params=_cparams(("arbitrary",)),
        name="scatter_rows",
    )(counts, pstarts, pos.reshape(n, 1, tm), xp)


def _wait_rows(n, src_hbm, dst, sem):
    pltpu.make_async_copy(src_hbm.at[pl.ds(0, n)], dst, sem).wait()


def _unpack_rows(xp):
    left = lax.bitcast_convert_type(lax.shift_left(xp, 16), F32).astype(BF16)
    right = lax.bitcast_convert_type(xp & jnp.int32(-65536), F32).astype(BF16)
    return jnp.concatenate([left, right], axis=1)


def _experts_body(elo_ref, ehi_ref, nu_ref, xs_ref, w13a_ref, w2a_ref, w13b_ref, w2b_ref, o_ref):
    i = pl.program_id(0)
    dh = xs_ref.shape[1] - LANES
    ff = w2a_ref.shape[1]

    @pl.when(i < nu_ref[0])
    def _():
        x = _unpack_rows(xs_ref[:, :dh])
        info = lax.bitcast_convert_type(xs_ref[:, dh:], F32)

        def ffn(w13_ref, w2_ref, weight):
            ab = _dot(x, w13_ref[0])
            a = ab[:, :ff]
            hid = (a * jax.nn.sigmoid(a)) * ab[:, ff:] * weight
            return _dot(hid.astype(BF16), w2_ref[0])

        o_ref[...] = ffn(w13a_ref, w2a_ref, info[:, 1:2]) + ffn(w13b_ref, w2b_ref, info[:, 2:3])

    @pl.when(i >= nu_ref[0])
    def _():
        o_ref[...] = jnp.zeros_like(o_ref)


def _experts(xs, e_lo, e_hi, n_used, w13, w2):
    n_rows, wrow = xs.shape
    n_blocks = n_rows // EXPERT_ROWS
    n_exp, ff, d = w2.shape
    pick = lambda a, use_hi: pl.BlockSpec(
        (1,) + a.shape[1:],
        (lambda i, lo, hi, nu: (hi[i], 0, 0)) if use_hi else (lambda i, lo, hi, nu: (lo[i], 0, 0)))
    grid_spec = pltpu.PrefetchScalarGridSpec(
        num_scalar_prefetch=3,
        grid=(n_blocks,),
        in_specs=[pl.BlockSpec((EXPERT_ROWS, wrow), lambda i, lo, hi, nu: (jnp.minimum(i, nu[0] - 1), 0)),
                  pick(w13, False), pick(w2, False), pick(w13, True), pick(w2, True)],
        out_specs=pl.BlockSpec((EXPERT_ROWS, d), lambda i, lo, hi, nu: (i, 0)),
    )
    return pl.pallas_call(
        _experts_body,
        grid_spec=grid_spec,
        out_shape=jax.ShapeDtypeStruct((n_rows, d), F32),
        compiler_params=_cparams(("arbitrary",)),
        name="experts",
    )(e_lo, e_hi, n_used, xs, w13, w2, w13, w2)


def _final_body(posc_ref, posn_ref, y_hbm, h1_ref, mod_ref, gf_ref, o_ref, ybuf, sem):
    i = pl.program_id(0)
    n = pl.num_programs(0)
    tm = h1_ref.shape[0]
    half = tm // 2

    def fetch(idx_ref, h):
        for r in range(half):
            pltpu.make_async_copy(y_hbm.at[pl.ds(idx_ref[0, 0, h * half + r], 1)],
                                  ybuf.at[h, pl.ds(r, 1)], sem.at[h]).start()

    @pl.when(i == 0)
    def _():
        fetch(posc_ref, 0)
        fetch(posc_ref, 1)

    for h in range(2):
        rows = slice(h * half, (h + 1) * half)
        _wait_rows(half, y_hbm, ybuf.at[h], sem.at[h])
        hres = h1_ref[rows, :] + mod_ref[0, 5:6, :] * ybuf[h]
        ms = jnp.mean(hres * hres, axis=-1, keepdims=True)
        o_ref[rows, :] = hres * lax.rsqrt(ms + RMS_EPS) * gf_ref[...]
        fetch(posn_ref, h)

    @pl.when(i == n - 1)
    def _():
        _wait_rows(half, y_hbm, ybuf.at[0], sem.at[0])
        _wait_rows(half, y_hbm, ybuf.at[1], sem.at[1])


def _final(pos, y_sorted, h1, mod, gf, seq, tm):
    nt, d = h1.shape
    n = nt // tm
    pos3 = pos.reshape(n, 1, tm)
    idx_spec = lambda f: pl.BlockSpec((1, 1, tm), f, memory_space=pltpu.SMEM)
    return pl.pallas_call(
        _final_body,
        grid=(n,),
        in_specs=[idx_spec(lambda i: (i, 0, 0)),
                  idx_spec(lambda i: (jnp.minimum(i + 1, n - 1), 0, 0)),
                  pl.BlockSpec(memory_space=pl.ANY),
                  pl.BlockSpec((tm, d), lambda i: (i, 0)),
                  pl.BlockSpec((1, 6, d), lambda i: (i * tm // seq, 0, 0)),
                  pl.BlockSpec((1, d), lambda i: (0, 0))],
        out_specs=pl.BlockSpec((tm, d), lambda i: (i, 0)),
        out_shape=jax.ShapeDtypeStruct((nt, d), F32),
        scratch_shapes=[pltpu.VMEM((2, tm // 2, d), F32), pltpu.SemaphoreType.DMA((2,))],
        compiler_params=_cparams(("arbitrary",)),
        name="final",
    )(pos3, pos3, y_sorted, h1, mod, gf.reshape(1, d))


def kernel(x, c, w_ada, b_ada, norm1_g, w_in, w_gate, b_gate, ssm_a_re, ssm_a_im, ssm_b_re, ssm_b_im, ssm_c_re, ssm_c_im, ssm_d, ssm_log_step, w_glu, b_glu, sgu_ln_g, sgu_ln_b, sgu_w, sgu_b, w_branch_a, w_branch_b, w_out, norm2_g, w_router_group, b_router_group, w_router_expert, b_router_expert, w1, w3, w2, norm_f_g):
    bsz, seq, d = x.shape
    depth = w_ada.shape[0]
    assert depth == 1 and bsz % SCAN_BATCHES == 0 and seq % 512 == 0
    l = 0
    mod = _adaln(c, w_ada[l], b_ada[l]).reshape(bsz, 6, d)
    za, u, v, gates, w13_bf, w2_bf = _inproj(x, mod, norm1_g[l], w_in[l], w_gate[l], b_gate[l],
                                             sgu_ln_g[l], sgu_ln_b[l], w1[l], w3[l], w2[l], tm=1024)
    a_row, s5_in, s5_out = _s5_params(ssm_a_re[l], ssm_a_im[l], ssm_b_re[l], ssm_b_im[l],
                                      ssm_c_re[l], ssm_c_im[l], ssm_log_step[l])
    ya = _s5(za, a_row, s5_in, s5_out, ssm_d[l], w_glu[l], b_glu[l])
    h1, xp, route = _mix(x, u, v, ya, gates, mod, sgu_w[l], sgu_b[l], w_branch_a[l], w_branch_b[l],
                         w_out[l], norm2_g[l], w_router_group[l], b_router_group[l],
                         w_router_expert[l], b_router_expert[l], tm=1024)
    nt = bsz * seq
    n_blocks = nt // EXPERT_ROWS + N_CLASSES
    pos, e_lo, e_hi, n_used, counts, pstarts = _dispatch(route, n_blocks)
    xs = _scatter_rows(xp.reshape(nt, xp.shape[-1]), pos, counts, pstarts, n_blocks)
    y_sorted = _experts(xs, e_lo, e_hi, n_used, w13_bf, w2_bf)
    out = _final(pos, y_sorted, h1.reshape(nt, d), mod, norm_f_g, seq, tm=1024)
    return out.reshape(bsz, seq, d)
```

```python
import functools
import math

import jax
import jax.numpy as jnp
from jax import lax
from jax.experimental import pallas as pl
from jax.experimental.pallas import tpu as pltpu

F32 = jnp.float32
BF16 = jnp.bfloat16
I32 = jnp.int32

LANES = 128
SUBLANES = 8
VMEM_LIMIT = 56 * 1024 * 1024

RMS_EPS = 1e-6
LN_EPS = 1e-5

SSM_GROUP_CH = 16
SSM_STATE = 64
GROUPS_PER_BLOCK = LANES // SSM_GROUP_CH
SGU_HEADS = 8
SGU_CHUNK = 128
N_GROUPS = 4
EXPERTS_PER_GROUP = 8
PAIRS_PER_GROUP = EXPERTS_PER_GROUP * (EXPERTS_PER_GROUP - 1) // 2
N_CLASSES = N_GROUPS * PAIRS_PER_GROUP
EXPERT_ROWS = 256

SCAN_BATCHES = SUBLANES
SCAN_TS = 128
SLAB_PITCH = SCAN_TS + 8
SCAN_CHUNK = 64
INPROJ_SUBTILE = 256
MIX_SUBTILE = 512


def _gelu(x):
    return 0.5 * x * (1.0 + jnp.tanh(math.sqrt(2.0 / math.pi) * (x + 0.044715 * (x * x * x))))


def _dot(a, b):
    return jnp.dot(a, b, preferred_element_type=F32)


def _split_bf16(a):
    hi = a.astype(BF16)
    lo = (a - hi.astype(F32)).astype(BF16)
    return hi, lo


def _cparams(sem):
    return pltpu.CompilerParams(dimension_semantics=sem, vmem_limit_bytes=VMEM_LIMIT)


def _adaln_body(c_ref, w_ref, b_ref, o_ref):
    c = c_ref[...]
    act = c * jax.nn.sigmoid(c)
    a_hi, a_lo = _split_bf16(act)
    w_hi, w_lo = _split_bf16(w_ref[...])
    o_ref[...] = _dot(a_hi, w_hi) + _dot(a_hi, w_lo) + _dot(a_lo, w_hi) + b_ref[...]


def _adaln(c, w, b):
    bsz, d = c.shape
    n = w.shape[1]
    tn = 1024
    return pl.pallas_call(
        _adaln_body,
        grid=(n // tn,),
        in_specs=[pl.BlockSpec((bsz, d), lambda j: (0, 0)),
                  pl.BlockSpec((d, tn), lambda j: (0, j)),
                  pl.BlockSpec((1, tn), lambda j: (0, j))],
        out_specs=pl.BlockSpec((bsz, tn), lambda j: (0, j)),
        out_shape=jax.ShapeDtypeStruct((bsz, n), F32),
        compiler_params=_cparams(("arbitrary",)),
        name="adaln",
    )(c, w, b.reshape(1, n))


def _inproj_body(x_ref, mod_ref, g1_ref, win_ref, bgate_ref, lng_ref, lnb_ref,
                 w1_ref, w3_ref, w2_ref, za_ref, u_ref, v_ref, gates_ref, w13_ref, w2b_ref):
    w = za_ref.shape[-1]
    tm = x_ref.shape[1]
    ff = w1_ref.shape[2]
    w13_ref[0, :, :ff] = w1_ref[0].astype(BF16)
    w13_ref[0, :, ff:] = w3_ref[0].astype(BF16)
    w2b_ref[0] = w2_ref[0].astype(BF16)
    for r0 in range(0, tm, INPROJ_SUBTILE):
        rs = slice(r0, r0 + INPROJ_SUBTILE)
        x = x_ref[0, rs, :]
        ms = jnp.mean(x * x, axis=-1, keepdims=True)
        xn = x * lax.rsqrt(ms + RMS_EPS) * g1_ref[...]
        xn = xn * (1.0 + mod_ref[0, 1:2, :]) + mod_ref[0, 0:1, :]
        xb = xn.astype(BF16)
        both = _dot(xb, win_ref[...])
        proj = both[:, :3 * w]
        za_ref[0, rs, :] = proj[:, :w]
        u_ref[0, rs, :] = _gelu(proj[:, w:2 * w]).astype(BF16)
        gv = _gelu(proj[:, 2 * w:])
        mu = jnp.mean(gv, axis=-1, keepdims=True)
        cen = gv - mu
        var = jnp.mean(cen * cen, axis=-1, keepdims=True)
        v_ref[0, rs, :] = (cen * lax.rsqrt(var + LN_EPS) * lng_ref[...] + lnb_ref[...]).astype(BF16)
        gates_ref[0, rs, :] = jax.nn.sigmoid(both[:, 3 * w:] + bgate_ref[...]).astype(BF16)


def _inproj(x, mod, g1, w_in, w_gate, b_gate, ln_g, ln_b, w1, w3, w2, tm):
    bsz, s, d = x.shape
    w = w_in.shape[1] // 3
    ng = w_gate.shape[1]
    n_exp, _, ff = w1.shape
    nsteps = s // tm
    parts = bsz * nsteps // n_exp
    assert bsz * nsteps == n_exp * parts and d % (parts * SUBLANES) == 0 and ff % (parts * SUBLANES) == 0
    tok = lambda n: pl.BlockSpec((1, tm, n), lambda b, i: (b, i, 0))
    full = lambda a: pl.BlockSpec(a.shape, lambda b, i: (0,) * a.ndim)
    wslice = lambda rows, cols: pl.BlockSpec(
        (1, rows // parts, cols), lambda b, i: ((b * nsteps + i) // parts, (b * nsteps + i) % parts, 0))
    w_both = jnp.concatenate([w_in, w_gate], axis=1).astype(BF16)
    args = (g1.reshape(1, d), w_both, b_gate.reshape(1, ng), ln_g.reshape(1, w), ln_b.reshape(1, w))
    return pl.pallas_call(
        _inproj_body,
        grid=(bsz, nsteps),
        in_specs=[tok(d), pl.BlockSpec((1, 6, d), lambda b, i: (b, 0, 0))] + [full(a) for a in args]
                 + [wslice(d, ff), wslice(d, ff), wslice(ff, d)],
        out_specs=[tok(w), tok(w), tok(w), tok(ng), wslice(d, 2 * ff), wslice(ff, d)],
        out_shape=[jax.ShapeDtypeStruct((bsz, s, w), F32),
                   jax.ShapeDtypeStruct((bsz, s, w), BF16),
                   jax.ShapeDtypeStruct((bsz, s, w), BF16),
                   jax.ShapeDtypeStruct((bsz, s, ng), BF16),
                   jax.ShapeDtypeStruct((n_exp, d, 2 * ff), BF16),
                   jax.ShapeDtypeStruct((n_exp, ff, d), BF16)],
        compiler_params=_cparams(("parallel", "parallel")),
        name="in_proj",
    )(x, mod, *args, w1, w3, w2)


def _s5_params(a_re, a_im, b_re, b_im, c_re, c_im, log_step):
    g, n = a_re.shape
    nblk = g // GROUPS_PER_BLOCK
    lam_re = jnp.minimum(a_re.astype(F32), -1e-4)
    lam_im = a_im.astype(F32)
    dt = jnp.exp(log_step.astype(F32))[:, None]
    mag = jnp.exp(lam_re * dt)
    ab_re = mag * jnp.cos(lam_im * dt)
    ab_im = mag * jnp.sin(lam_im * dt)
    den = lam_re * lam_re + lam_im * lam_im
    nr = ab_re - 1.0
    f_re = (nr * lam_re + ab_im * lam_im) / den
    f_im = (ab_im * lam_re - nr * lam_im) / den
    bt_re = b_re.astype(F32).transpose(0, 2, 1)
    bt_im = b_im.astype(F32).transpose(0, 2, 1)
    w_re = f_re[:, None, :] * bt_re - f_im[:, None, :] * bt_im
    w_im = f_re[:, None, :] * bt_im + f_im[:, None, :] * bt_re
    eye = jnp.eye(GROUPS_PER_BLOCK, dtype=F32)

    def bdiag_in(wt):
        t = wt.reshape(nblk, GROUPS_PER_BLOCK, SSM_GROUP_CH, 1, n) * eye[None, :, None, :, None]
        return t.reshape(nblk, GROUPS_PER_BLOCK * SSM_GROUP_CH, GROUPS_PER_BLOCK * n)

    def bdiag_out(ct):
        t = ct.transpose(0, 2, 1).reshape(nblk, GROUPS_PER_BLOCK, n, 1, SSM_GROUP_CH)
        t = t * eye[None, :, None, :, None]
        return t.reshape(nblk, GROUPS_PER_BLOCK * n, GROUPS_PER_BLOCK * SSM_GROUP_CH)

    w_in = jnp.concatenate([bdiag_in(w_re), bdiag_in(w_im)], axis=2).astype(BF16)
    w_out = jnp.concatenate([bdiag_out(c_re.astype(F32)), -bdiag_out(c_im.astype(F32))],
                            axis=1).astype(BF16)
    a_row = jnp.concatenate([ab_re.reshape(nblk, -1), ab_im.reshape(nblk, -1)], axis=1).reshape(1, -1)
    return a_row, w_in, w_out


def _s5_body(z_ref, a_ref, win_ref, wout_ref, d_ref, wglu_ref, bglu_ref, o_ref,
             zslab, zsb, xs, hstate, yslab):
    nb, ts, w = z_ref.shape
    rows = nb * ts
    nblk = w // LANES
    half = GROUPS_PER_BLOCK * SSM_STATE
    sw = 2 * half

    @pl.when(pl.program_id(1) == 0)
    def _():
        hstate[...] = jnp.zeros_like(hstate)

    for b in range(nb):
        for c in range(nblk):
            zslab[c, b * SLAB_PITCH:b * SLAB_PITCH + ts, :] = z_ref[b, :, c * LANES:(c + 1) * LANES]

    for s in range(ts):
        for c in range(nblk):
            zsb[s * nb:(s + 1) * nb, c * LANES:(c + 1) * LANES] = zslab.at[c][pl.ds(s, nb, stride=SLAB_PITCH), :]

    n_chunks = ts // SCAN_CHUNK
    crows = SCAN_CHUNK * nb
    units = [(j, q) for j in range(nblk) for q in range(n_chunks)]

    def in_map(j, q):
        rs = slice(q * crows, (q + 1) * crows)
        xs[rs, j * sw:(j + 1) * sw] = _dot(zsb[rs, j * LANES:(j + 1) * LANES].astype(BF16), win_ref[j])

    carry = {}

    def scan(j, q):
        re_sl = slice(j * sw, j * sw + half)
        im_sl = slice(j * sw + half, (j + 1) * sw)
        a_r = jnp.broadcast_to(a_ref[:, re_sl], (nb, half))
        a_i = jnp.broadcast_to(a_ref[:, im_sl], (nb, half))
        h_r, h_i = carry[j] if q else (hstate[:, re_sl], hstate[:, im_sl])
        for s in range(q * SCAN_CHUNK, (q + 1) * SCAN_CHUNK):
            x_r = xs[s * nb:(s + 1) * nb, re_sl]
            x_i = xs[s * nb:(s + 1) * nb, im_sl]
            h_r, h_i = a_r * h_r - a_i * h_i + x_r, a_r * h_i + a_i * h_r + x_i
            xs[s * nb:(s + 1) * nb, re_sl] = h_r
            xs[s * nb:(s + 1) * nb, im_sl] = h_i
        carry[j] = (h_r, h_i)
        if q == n_chunks - 1:
            hstate[:, re_sl] = h_r
            hstate[:, im_sl] = h_i

    gel = {}

    def out_map(j, q):
        rs = slice(q * crows, (q + 1) * crows)
        y = _dot(xs[rs, j * sw:(j + 1) * sw].astype(BF16), wout_ref[j])
        y = y + d_ref[:, j * LANES:(j + 1) * LANES] * zsb[rs, j * LANES:(j + 1) * LANES]
        gel[j, q] = _gelu(y)

    for k in range(len(units) + 2):
        if k < len(units):
            in_map(*units[k])
        if 0 <= k - 1 < len(units):
            scan(*units[k - 1])
        if 0 <= k - 2 < len(units):
            out_map(*units[k - 2])

    for q in range(n_chunks):
        g = jnp.concatenate([gel[j, q] for j in range(nblk)], axis=1)
        g = g * jax.nn.sigmoid(_dot(g.astype(BF16), wglu_ref[...]) + bglu_ref[...])
        for c in range(nblk):
            yslab[c, q * crows:(q + 1) * crows, :] = g[:, c * LANES:(c + 1) * LANES]

    for b in range(nb):
        for c in range(nblk):
            o_ref[b, :, c * LANES:(c + 1) * LANES] = yslab.at[c][pl.ds(b, ts, stride=nb), :].astype(BF16)


def _s5(za, a_row, w_in, w_out, d_skip, w_glu, b_glu):
    bsz, s, w = za.shape
    nb, ts = SCAN_BATCHES, SCAN_TS
    rows = nb * ts
    nblk = w // LANES
    nstate = a_row.shape[1]
    full = lambda a: pl.BlockSpec(a.shape, lambda b, i: (0,) * a.ndim)
    args = (a_row, w_in, w_out, d_skip.reshape(1, w).astype(F32), w_glu.astype(BF16), b_glu.reshape(1, w))
    return pl.pallas_call(
        _s5_body,
        grid=(bsz // nb, s // ts),
        in_specs=[pl.BlockSpec((nb, ts, w), lambda b, i: (b, i, 0))] + [full(a) for a in args],
        out_specs=pl.BlockSpec((nb, ts, w), lambda b, i: (b, i, 0)),
        out_shape=jax.ShapeDtypeStruct((bsz, s, w), BF16),
        scratch_shapes=[pltpu.VMEM((nblk, nb * SLAB_PITCH, LANES), F32),
                        pltpu.VMEM((rows, w), F32),
                        pltpu.VMEM((rows, nstate), F32),
                        pltpu.VMEM((nb, nstate), F32),
                        pltpu.VMEM((nblk, rows, LANES), F32)],
        compiler_params=_cparams(("parallel", "arbitrary")),
        name="s5",
    )(za, *args)


def _mix_body(x_ref, u_ref, v_ref, ya_ref, gates_ref, mod_ref, wp_ref, sb_ref, wba_ref, wbb_ref,
              wout_ref, g2_ref, wr_ref, br_ref, h1_ref, xp_ref, route_ref):
    tm_full = x_ref.shape[1]
    npair = u_ref.shape[2] // LANES
    hd = LANES // 2

    t_idx = lax.broadcasted_iota(I32, (SGU_CHUNK, 2 * SGU_CHUNK), 0)
    s_idx = lax.broadcasted_iota(I32, (SGU_CHUNK, 2 * SGU_CHUNK), 1) & (SGU_CHUNK - 1)
    causal = s_idx <= t_idx
    lane = lax.broadcasted_iota(I32, (SGU_CHUNK, LANES), 1)
    first_head = lane < hd
    w_causal = [jnp.where(causal, wp_ref[j], jnp.zeros_like(wp_ref[j])) for j in range(npair)]
    for r0 in range(0, tm_full, MIX_SUBTILE):
        _mix_rows(slice(r0, r0 + MIX_SUBTILE), w_causal, first_head, x_ref, u_ref, v_ref, ya_ref, gates_ref,
                  mod_ref, sb_ref, wba_ref, wbb_ref, wout_ref, g2_ref, wr_ref, br_ref, h1_ref, xp_ref,
                  route_ref)


def _mix_rows(rs, w_causal, first_head, x_ref, u_ref, v_ref, ya_ref, gates_ref, mod_ref, sb_ref, wba_ref,
              wbb_ref, wout_ref, g2_ref, wr_ref, br_ref, h1_ref, xp_ref, route_ref):
    tm = rs.stop - rs.start
    d = x_ref.shape[2]
    npair = len(w_causal)
    mixed_rows = []
    for c in range(rs.start, rs.stop, SGU_CHUNK):
        vc = v_ref[0, c:c + SGU_CHUNK, :]
        blocks = []
        for j in range(npair):
            vb = vc[:, j * LANES:(j + 1) * LANES]
            zero = jnp.zeros_like(vb)
            rhs = jnp.concatenate([jnp.where(first_head, vb, zero), jnp.where(first_head, zero, vb)], axis=0)
            blocks.append(_dot(w_causal[j], rhs))
        mixed_rows.append(jnp.concatenate(blocks, axis=1) + sb_ref[...])
    mixed = jnp.concatenate(mixed_rows, axis=0)
    yb = (u_ref[0, rs, :].astype(F32) * mixed).astype(BF16)

    pa = _dot(ya_ref[0, rs, :], wba_ref[...])
    pb = _dot(yb, wbb_ref[...])
    merged = gates_ref[0, rs, :d].astype(F32) * pa + gates_ref[0, rs, d:].astype(F32) * pb
    o = _dot(merged.astype(BF16), wout_ref[...])
    h1 = x_ref[0, rs, :] + mod_ref[0, 2:3, :] * o
    h1_ref[0, rs, :] = h1

    ms = jnp.mean(h1 * h1, axis=-1, keepdims=True)
    xn = h1 * lax.rsqrt(ms + RMS_EPS) * g2_ref[...]
    xn = xn * (1.0 + mod_ref[0, 4:5, :]) + mod_ref[0, 3:4, :]
    x_hi = xn.astype(BF16)

    logits = _dot(x_hi, wr_ref[...]) + br_ref[...]
    lt = logits.T
    best = lt[0:1, :]
    grp = jnp.zeros((1, tm), I32)
    for gi in range(1, N_GROUPS):
        better = lt[gi:gi + 1, :] > best
        grp = jnp.where(better, gi, grp)
        best = jnp.where(better, lt[gi:gi + 1, :], best)
    den = jnp.zeros((1, tm), F32)
    for gi in range(N_GROUPS):
        den = den + jnp.exp(lt[gi:gi + 1, :] - best)
    pg = 1.0 / den
    le = lt[SUBLANES:2 * SUBLANES, :]
    for gi in range(1, N_GROUPS):
        le = jnp.where(grp == gi, lt[SUBLANES * (gi + 1):SUBLANES * (gi + 2), :], le)
    eidx = lax.broadcasted_iota(I32, (EXPERTS_PER_GROUP, tm), 0).astype(F32)
    none = float(EXPERTS_PER_GROUP)
    v1 = jnp.max(le, axis=0, keepdims=True)
    i1 = jnp.min(jnp.where(le == v1, eidx, none), axis=0, keepdims=True)
    rest = jnp.where(eidx == i1, -jnp.inf, le)
    v2 = jnp.max(rest, axis=0, keepdims=True)
    i2 = jnp.min(jnp.where(rest == v2, eidx, none), axis=0, keepdims=True)
    ex = jnp.exp(v2 - v1)
    p1 = 1.0 / (1.0 + ex)
    wt1 = pg * p1
    wt2 = pg * (ex * p1)
    first_lo = i1 < i2
    lo = jnp.where(first_lo, i1, i2)
    hi = jnp.where(first_lo, i2, i1)
    pair = lo * (2 * EXPERTS_PER_GROUP - 1.0 - lo) * 0.5 + (hi - lo - 1.0)
    cls = grp.astype(F32) * PAIRS_PER_GROUP + pair
    info = jnp.concatenate([cls, jnp.where(first_lo, wt1, wt2), jnp.where(first_lo, wt2, wt1)], axis=0)
    route_ref[:, rs] = jnp.concatenate([info, jnp.zeros((SUBLANES - 3, tm), F32)], axis=0)

    bits = lax.bitcast_convert_type(x_hi.astype(F32), I32)
    packed = lax.shift_right_logical(bits[:, :d // 2], 16) | (bits[:, d // 2:] & jnp.int32(-65536))
    info_cols = jnp.concatenate([info, jnp.zeros((LANES - 3, tm), F32)], axis=0).T
    xp_ref[0, rs, :] = jnp.concatenate([packed, lax.bitcast_convert_type(info_cols, I32)], axis=1)


def _mix(x, u, v, ya, gates, mod, sgu_w, sgu_b, w_ba, w_bb, w_out, g2, w_rg, b_rg, w_re, b_re, tm):
    bsz, s, d = x.shape
    w = u.shape[2]
    nt = bsz * s
    wp = sgu_w.reshape(SGU_HEADS // 2, 2, SGU_CHUNK, SGU_CHUNK).transpose(0, 2, 1, 3)
    wp = wp.reshape(SGU_HEADS // 2, SGU_CHUNK, 2 * SGU_CHUNK).astype(BF16)
    sb = jnp.repeat(sgu_b.T, w // SGU_HEADS, axis=1).astype(F32)
    wr = jnp.zeros((d, LANES), F32)
    wr = wr.at[:, :N_GROUPS].set(w_rg)
    wr = wr.at[:, SUBLANES:SUBLANES + N_GROUPS * EXPERTS_PER_GROUP].set(
        w_re.transpose(1, 0, 2).reshape(d, N_GROUPS * EXPERTS_PER_GROUP))
    br = jnp.zeros((1, LANES), F32)
    br = br.at[0, :N_GROUPS].set(b_rg)
    br = br.at[0, SUBLANES:SUBLANES + N_GROUPS * EXPERTS_PER_GROUP].set(b_re.reshape(-1))
    tok = lambda n: pl.BlockSpec((1, tm, n), lambda b, i: (b, i, 0))
    full = lambda a: pl.BlockSpec(a.shape, lambda b, i: (0,) * a.ndim)
    args = (wp, sb, w_ba.astype(BF16), w_bb.astype(BF16), w_out.astype(BF16), g2.reshape(1, d),
            wr.astype(BF16), br)
    nsteps = s // tm
    return pl.pallas_call(
        _mix_body,
        grid=(bsz, nsteps),
        in_specs=[tok(d), tok(w), tok(w), tok(w), tok(2 * d),
                  pl.BlockSpec((1, 6, d), lambda b, i: (b, 0, 0))] + [full(a) for a in args],
        out_specs=[tok(d), tok(d // 2 + LANES),
                   pl.BlockSpec((SUBLANES, tm), lambda b, i: (0, b * nsteps + i))],
        out_shape=[jax.ShapeDtypeStruct((bsz, s, d), F32),
                   jax.ShapeDtypeStruct((bsz, s, d // 2 + LANES), I32),
                   jax.ShapeDtypeStruct((SUBLANES, nt), F32)],
        compiler_params=_cparams(("parallel", "parallel")),
        name="mix",
    )(x, u, v, ya, gates, mod, *args)


def _tile_lanes(a, reps):
    return jnp.concatenate([a] * reps, axis=1)


def _dispatch_body(route_ref, lo_ref, hi_ref, pos_ref, blk_ref, cls_ref, run, pstart, ranks, classes,
                   earlier):
    i = pl.program_id(0)
    n = pl.num_programs(0)
    tt = route_ref.shape[1]
    reps = tt // LANES
    cid = lax.broadcasted_iota(I32, (LANES, tt), 0).astype(F32)

    @pl.when(i == 0)
    def _():
        run[...] = jnp.zeros_like(run)
        earlier_tok = lax.broadcasted_iota(I32, (tt, tt), 0) < lax.broadcasted_iota(I32, (tt, tt), 1)
        earlier[...] = jnp.where(earlier_tok, 1.0, 0.0).astype(BF16)

    cls = route_ref[0:1, :]
    classes[i] = cls
    onehot = jnp.where(cls == cid, 1.0, 0.0)
    before = _dot(onehot.astype(BF16), earlier[...])
    ranks[i] = jnp.sum(onehot * (before + _tile_lanes(run[...], reps)), axis=0, keepdims=True)
    run[...] = run[...] + jnp.sum(onehot, axis=1, keepdims=True)

    @pl.when(i == n - 1)
    def _():
        counts = run[...]
        nblk = jnp.floor((counts + (EXPERT_ROWS - 1.0)) * (1.0 / EXPERT_ROWS))
        hi_part = jnp.floor(nblk * (1.0 / 16.0))
        lo_part = nblk - 16.0 * hi_part
        upto = lax.broadcasted_iota(I32, (LANES, LANES), 1) <= lax.broadcasted_iota(I32, (LANES, LANES), 0)
        upto = jnp.where(upto, 1.0, 0.0).astype(BF16)
        ends = 16.0 * _dot(upto, hi_part.astype(BF16)) + _dot(upto, lo_part.astype(BF16))
        pstart[...] = (ends - nblk) * EXPERT_ROWS
        diag = lax.broadcasted_iota(I32, (LANES, LANES), 0) == lax.broadcasted_iota(I32, (LANES, LANES), 1)
        on_lanes = lambda a: jnp.sum(jnp.where(diag, a, 0.0), axis=0, keepdims=True)
        cls_ref[...] = jnp.concatenate([on_lanes(counts), on_lanes(pstart[...]),
                                        jnp.zeros((SUBLANES - 2, LANES), F32)], axis=0)
        nb_lanes = blk_ref.shape[1]
        breps = nb_lanes // LANES
        n_used = _tile_lanes(ends[LANES - 1:LANES, :], breps)
        b_idx = lax.broadcasted_iota(I32, (LANES, nb_lanes), 1).astype(F32)
        b_idx = jnp.minimum(b_idx, n_used - 1.0)
        b_cls = jnp.sum(jnp.where(_tile_lanes(ends, breps) <= b_idx, 1.0, 0.0), axis=0, keepdims=True)
        b_member = b_cls == lax.broadcasted_iota(I32, (LANES, nb_lanes), 0).astype(F32)
        pick = lambda tab: jnp.sum(jnp.where(b_member, _tile_lanes(tab, breps), 0.0), axis=0, keepdims=True)
        blk_ref[...] = jnp.concatenate([pick(lo_ref[...]), pick(hi_ref[...]), n_used,
                                        jnp.zeros((SUBLANES - 3, nb_lanes), F32)], axis=0)
        starts = _tile_lanes(pstart[...], reps)

        def tile_positions(k, carry):
            base = jnp.sum(jnp.where(classes[k] == cid, starts, 0.0), axis=0, keepdims=True)
            pos_ref[k] = (base + ranks[k]).astype(I32)
            return carry

        lax.fori_loop(0, n, tile_positions, 0)


def _dispatch(route, n_blocks, tt=1024):
    nt = route.shape[1]
    n = nt // tt
    nb_lanes = pl.cdiv(n_blocks, LANES) * LANES
    lo_tab, hi_tab = [], []
    for g in range(N_GROUPS):
        for a in range(EXPERTS_PER_GROUP):
            for b in range(a + 1, EXPERTS_PER_GROUP):
                lo_tab.append(g * EXPERTS_PER_GROUP + a)
                hi_tab.append(g * EXPERTS_PER_GROUP + b)
    pad = [0] * (LANES - N_CLASSES)
    lo_tile = jnp.broadcast_to(jnp.asarray(lo_tab + pad, F32)[:, None], (LANES, LANES))
    hi_tile = jnp.broadcast_to(jnp.asarray(hi_tab + pad, F32)[:, None], (LANES, LANES))
    const = lambda shape: pl.BlockSpec(shape, lambda i: (0,) * len(shape))
    pos, blk_tab, cls_tab = pl.pallas_call(
        _dispatch_body,
        grid=(n,),
        in_specs=[pl.BlockSpec((SUBLANES, tt), lambda i: (0, i)),
                  const((LANES, LANES)), const((LANES, LANES))],
        out_specs=[const((n, 1, tt)), const((SUBLANES, nb_lanes)), const((SUBLANES, LANES))],
        out_shape=[jax.ShapeDtypeStruct((n, 1, tt), I32),
                   jax.ShapeDtypeStruct((SUBLANES, nb_lanes), F32),
                   jax.ShapeDtypeStruct((SUBLANES, LANES), F32)],
        scratch_shapes=[pltpu.VMEM((LANES, LANES), F32), pltpu.VMEM((LANES, LANES), F32),
                        pltpu.VMEM((n, 1, tt), F32), pltpu.VMEM((n, 1, tt), F32),
                        pltpu.VMEM((tt, tt), BF16)],
        compiler_params=_cparams(("arbitrary",)),
        name="dispatch",
    )(route, lo_tile, hi_tile)
    blk = blk_tab.astype(I32)
    cls = cls_tab.astype(I32)
    return pos.reshape(nt), blk[0, :n_blocks], blk[1, :n_blocks], blk[2, :1], cls[0], cls[1]


ZERO_FILL_ROWS = tuple(SUBLANES << k for k in range((EXPERT_ROWS // SUBLANES).bit_length() - 1))


def _scatter_body(cnt_ref, pst_ref, pos_ref, xp_ref, xs_hbm, ring, zeros, sem, zsem):
    i = pl.program_id(0)
    n = pl.num_programs(0)
    tm = xp_ref.shape[0]
    half = tm // 2

    def rows_done(h):
        pltpu.make_async_copy(ring.at[pl.ds(h * half, half)], xs_hbm.at[pl.ds(0, half)], sem.at[h]).wait()

    for h in range(2):
        @pl.when(i > 0)
        def _():
            rows_done(h)

        ring[h * half:(h + 1) * half, :] = xp_ref[h * half:(h + 1) * half, :]
        for r in range(h * half, (h + 1) * half):
            pltpu.make_async_copy(ring.at[pl.ds(r, 1)], xs_hbm.at[pl.ds(pos_ref[0, 0, r], 1)],
                                  sem.at[h]).start(priority=r % 2)

    @pl.when(i == n - 1)
    def _():
        rows_done(0)
        rows_done(1)
        zeros[...] = jnp.zeros_like(zeros)

        def fill(wait):
            def per_class(k, carry):
                cnt = cnt_ref[k]
                n_pad = (-cnt) & (EXPERT_ROWS - 1)
                off = pst_ref[k] + cnt
                for r in range(SUBLANES - 1):
                    row = pltpu.make_async_copy(zeros.at[pl.ds(0, 1)], xs_hbm.at[pl.ds(off + r, 1)], zsem)

                    @pl.when(r < (n_pad & (SUBLANES - 1)))
                    def _():
                        row.wait() if wait else row.start()

                off = off + (n_pad & (SUBLANES - 1))
                for size in ZERO_FILL_ROWS:
                    piece = pltpu.make_async_copy(zeros.at[pl.ds(0, size)],
                                                  xs_hbm.at[pl.ds(pl.multiple_of(off, size), size)], zsem)

                    @pl.when((n_pad & size) != 0)
                    def _():
                        piece.wait() if wait else piece.start()

                    off = off + (n_pad & size)
                return carry

            lax.fori_loop(0, N_CLASSES, per_class, 0)

            last = N_CLASSES - 1
            used_rows = pst_ref[last] + cnt_ref[last] + ((-cnt_ref[last]) & (EXPERT_ROWS - 1))
            tail = ZERO_FILL_ROWS[-1]

            def per_piece(j, carry):
                piece = pltpu.make_async_copy(
                    zeros, xs_hbm.at[pl.ds(pl.multiple_of(used_rows + j * tail, tail), tail)], zsem)
                piece.wait() if wait else piece.start()
                return carry

            lax.fori_loop(0, (xs_hbm.shape[0] - used_rows) // tail, per_piece, 0)

        fill(False)
        fill(True)


def _scatter_rows(xp, pos, counts, pstarts, n_blocks, tm=1024):
    nt, wrow = xp.shape
    n = nt // tm
    grid_spec = pltpu.PrefetchScalarGridSpec(
        num_scalar_prefetch=2,
        grid=(n,),
        in_specs=[pl.BlockSpec((1, 1, tm), lambda i, c, p: (i, 0, 0), memory_space=pltpu.SMEM),
                  pl.BlockSpec((tm, wrow), lambda i, c, p: (i, 0))],
        out_specs=pl.BlockSpec(memory_space=pl.ANY),
        scratch_shapes=[pltpu.VMEM((tm, wrow), I32), pltpu.VMEM((ZERO_FILL_ROWS[-1], wrow), I32),
                        pltpu.SemaphoreType.DMA((2,)), pltpu.SemaphoreType.DMA(())],
    )
    return pl.pallas_call(
        _scatter_body,
        grid_spec=grid_spec,
        out_shape=jax.ShapeDtypeStruct((n_blocks * EXPERT_ROWS, wrow), I32),
        compiler_params=_cparams(("arbitrary",)),
        name="scatter_rows",
    )(counts, pstarts, pos.reshape(n, 1, tm), xp)


def _wait_rows(n, src_hbm, dst, sem):
    pltpu.make_async_copy(src_hbm.at[pl.ds(0, n)], dst, sem).wait()


def _unpack_rows(xp):
    left = lax.bitcast_convert_type(lax.shift_left(xp, 16), F32).astype(BF16)
    right = lax.bitcast_convert_type(xp & jnp.int32(-65536), F32).astype(BF16)
    return jnp.concatenate([left, right], axis=1)


def _experts_body(elo_ref, ehi_ref, nu_ref, xs_ref, w13a_ref, w2a_ref, w13b_ref, w2b_ref, o_ref):
    i = pl.program_id(0)
    dh = xs_ref.shape[1] - LANES
    ff = w2a_ref.shape[1]

    @pl.when(i < nu_ref[0])
    def _():
        x = _unpack_rows(xs_ref[:, :dh])
        info = lax.bitcast_convert_type(xs_ref[:, dh:], F32)

        def ffn(w13_ref, w2_ref, weight):
            ab = _dot(x, w13_ref[0])
            a = ab[:, :ff]
            hid = (a * jax.nn.sigmoid(a)) * ab[:, ff:] * weight
            return _dot(hid.astype(BF16), w2_ref[0])

        o_ref[...] = ffn(w13a_ref, w2a_ref, info[:, 1:2]) + ffn(w13b_ref, w2b_ref, info[:, 2:3])

    @pl.when(i >= nu_ref[0])
    def _():
        o_ref[...] = jnp.zeros_like(o_ref)


def _experts(xs, e_lo, e_hi, n_used, w13, w2):
    n_rows, wrow = xs.shape
    n_blocks = n_rows // EXPERT_ROWS
    n_exp, ff, d = w2.shape
    pick = lambda a, use_hi: pl.BlockSpec(
        (1,) + a.shape[1:],
        (lambda i, lo, hi, nu: (hi[i], 0, 0)) if use_hi else (lambda i, lo, hi, nu: (lo[i], 0, 0)))
    grid_spec = pltpu.PrefetchScalarGridSpec(
        num_scalar_prefetch=3,
        grid=(n_blocks,),
        in_specs=[pl.BlockSpec((EXPERT_ROWS, wrow), lambda i, lo, hi, nu: (jnp.minimum(i, nu[0] - 1), 0)),
                  pick(w13, False), pick(w2, False), pick(w13, True), pick(w2, True)],
        out_specs=pl.BlockSpec((EXPERT_ROWS, d), lambda i, lo, hi, nu: (i, 0)),
    )
    return pl.pallas_call(
        _experts_body,
        grid_spec=grid_spec,
        out_shape=jax.ShapeDtypeStruct((n_rows, d), F32),
        compiler_params=_cparams(("arbitrary",)),
        name="experts",
    )(e_lo, e_hi, n_used, xs, w13, w2, w13, w2)


def _final_body(posc_ref, posn_ref, y_hbm, h1_ref, mod_ref, gf_ref, o_ref, ybuf, sem):
    i = pl.program_id(0)
    n = pl.num_programs(0)
    tm = h1_ref.shape[0]
    half = tm // 2

    def fetch(idx_ref, h):
        for r in range(half):
            pltpu.make_async_copy(y_hbm.at[pl.ds(idx_ref[0, 0, h * half + r], 1)],
                                  ybuf.at[h, pl.ds(r, 1)], sem.at[h]).start(priority=r % 2)

    @pl.when(i == 0)
    def _():
        fetch(posc_ref, 0)
        fetch(posc_ref, 1)

    for h in range(2):
        rows = slice(h * half, (h + 1) * half)
        _wait_rows(half, y_hbm, ybuf.at[h], sem.at[h])
        hres = h1_ref[rows, :] + mod_ref[0, 5:6, :] * ybuf[h]
        ms = jnp.mean(hres * hres, axis=-1, keepdims=True)
        o_ref[rows, :] = hres * lax.rsqrt(ms + RMS_EPS) * gf_ref[...]
        fetch(posn_ref, h)

    @pl.when(i == n - 1)
    def _():
        _wait_rows(half, y_hbm, ybuf.at[0], sem.at[0])
        _wait_rows(half, y_hbm, ybuf.at[1], sem.at[1])


def _final(pos, y_sorted, h1, mod, gf, seq, tm):
    nt, d = h1.shape
    n = nt // tm
    pos3 = pos.reshape(n, 1, tm)
    idx_spec = lambda f: pl.BlockSpec((1, 1, tm), f, memory_space=pltpu.SMEM)
    return pl.pallas_call(
        _final_body,
        grid=(n,),
        in_specs=[idx_spec(lambda i: (i, 0, 0)),
                  idx_spec(lambda i: (jnp.minimum(i + 1, n - 1), 0, 0)),
                  pl.BlockSpec(memory_space=pl.ANY),
                  pl.BlockSpec((tm, d), lambda i: (i, 0)),
                  pl.BlockSpec((1, 6, d), lambda i: (i * tm // seq, 0, 0)),
                  pl.BlockSpec((1, d), lambda i: (0, 0))],
        out_specs=pl.BlockSpec((tm, d), lambda i: (i, 0)),
        out_shape=jax.ShapeDtypeStruct((nt, d), F32),
        scratch_shapes=[pltpu.VMEM((2, tm // 2, d), F32), pltpu.SemaphoreType.DMA((2,))],
        compiler_params=_cparams(("arbitrary",)),
        name="final",
    )(pos3, pos3, y_sorted, h1, mod, gf.reshape(1, d))


def kernel(x, c, w_ada, b_ada, norm1_g, w_in, w_gate, b_gate, ssm_a_re, ssm_a_im, ssm_b_re, ssm_b_im, ssm_c_re, ssm_c_im, ssm_d, ssm_log_step, w_glu, b_glu, sgu_ln_g, sgu_ln_b, sgu_w, sgu_b, w_branch_a, w_branch_b, w_out, norm2_g, w_router_group, b_router_group, w_router_expert, b_router_expert, w1, w3, w2, norm_f_g):
    bsz, seq, d = x.shape
    depth = w_ada.shape[0]
    assert depth == 1 and bsz % SCAN_BATCHES == 0 and seq % 512 == 0
    l = 0
    mod = _adaln(c, w_ada[l], b_ada[l]).reshape(bsz, 6, d)
    za, u, v, gates, w13_bf, w2_bf = _inproj(x, mod, norm1_g[l], w_in[l], w_gate[l], b_gate[l],
                                             sgu_ln_g[l], sgu_ln_b[l], w1[l], w3[l], w2[l], tm=1024)
    a_row, s5_in, s5_out = _s5_params(ssm_a_re[l], ssm_a_im[l], ssm_b_re[l], ssm_b_im[l],
                                      ssm_c_re[l], ssm_c_im[l], ssm_log_step[l])
    ya = _s5(za, a_row, s5_in, s5_out, ssm_d[l], w_glu[l], b_glu[l])
    h1, xp, route = _mix(x, u, v, ya, gates, mod, sgu_w[l], sgu_b[l], w_branch_a[l], w_branch_b[l],
                         w_out[l], norm2_g[l], w_router_group[l], b_router_group[l],
                         w_router_expert[l], b_router_expert[l], tm=1024)
    nt = bsz * seq
    n_blocks = nt // EXPERT_ROWS + N_CLASSES
    pos, e_lo, e_hi, n_used, counts, pstarts = _dispatch(route, n_blocks)
    xs = _scatter_rows(xp.reshape(nt, xp.shape[-1]), pos, counts, pstarts, n_blocks)
    y_sorted = _experts(xs, e_lo, e_hi, n_used, w13_bf, w2_bf)
    out = _final(pos, y_sorted, h1.reshape(nt, d), mod, norm_f_g, seq, tm=1024)
    return out.reshape(bsz, seq, d)
```

```python
import functools
import math

import jax
import jax.numpy as jnp
from jax import lax
from jax.experimental import pallas as pl
from jax.experimental.pallas import tpu as pltpu

F32 = jnp.float32
BF16 = jnp.bfloat16
I32 = jnp.int32

LANES = 128
SUBLANES = 8
VMEM_LIMIT = 56 * 1024 * 1024

RMS_EPS = 1e-6
LN_EPS = 1e-5

SSM_GROUP_CH = 16
SSM_STATE = 64
GROUPS_PER_BLOCK = LANES // SSM_GROUP_CH
SGU_HEADS = 8
SGU_CHUNK = 128
N_GROUPS = 4
EXPERTS_PER_GROUP = 8
PAIRS_PER_GROUP = EXPERTS_PER_GROUP * (EXPERTS_PER_GROUP - 1) // 2
N_CLASSES = N_GROUPS * PAIRS_PER_GROUP
EXPERT_ROWS = 256

SCAN_BATCHES = SUBLANES
SCAN_TS = 128
SLAB_PITCH = SCAN_TS + 8
SCAN_CHUNK = 64
INPROJ_SUBTILE = 256
MIX_SUBTILE = 512


def _gelu(x):
    return 0.5 * x * (1.0 + jnp.tanh(math.sqrt(2.0 / math.pi) * (x + 0.044715 * (x * x * x))))


def _dot(a, b):
    return jnp.dot(a, b, preferred_element_type=F32)


def _split_bf16(a):
    hi = a.astype(BF16)
    lo = (a - hi.astype(F32)).astype(BF16)
    return hi, lo


def _cparams(sem):
    return pltpu.CompilerParams(dimension_semantics=sem, vmem_limit_bytes=VMEM_LIMIT)


def _adaln_body(c_ref, w_ref, b_ref, o_ref):
    c = c_ref[...]
    act = c * jax.nn.sigmoid(c)
    a_hi, a_lo = _split_bf16(act)
    w_hi, w_lo = _split_bf16(w_ref[...])
    o_ref[...] = _dot(a_hi, w_hi) + _dot(a_hi, w_lo) + _dot(a_lo, w_hi) + b_ref[...]


def _adaln(c, w, b):
    bsz, d = c.shape
    n = w.shape[1]
    tn = 1024
    return pl.pallas_call(
        _adaln_body,
        grid=(n // tn,),
        in_specs=[pl.BlockSpec((bsz, d), lambda j: (0, 0)),
                  pl.BlockSpec((d, tn), lambda j: (0, j)),
                  pl.BlockSpec((1, tn), lambda j: (0, j))],
        out_specs=pl.BlockSpec((bsz, tn), lambda j: (0, j)),
        out_shape=jax.ShapeDtypeStruct((bsz, n), F32),
        compiler_params=_cparams(("arbitrary",)),
        name="adaln",
    )(c, w, b.reshape(1, n))


def _inproj_body(x_ref, mod_ref, g1_ref, win_ref, bgate_ref, lng_ref, lnb_ref,
                 w1_ref, w3_ref, w2_ref, za_ref, u_ref, v_ref, gates_ref, w13_ref, w2b_ref):
    w = za_ref.shape[-1]
    tm = x_ref.shape[1]
    ff = w1_ref.shape[2]
    w13_ref[0, :, :ff] = w1_ref[0].astype(BF16)
    w13_ref[0, :, ff:] = w3_ref[0].astype(BF16)
    w2b_ref[0] = w2_ref[0].astype(BF16)
    for r0 in range(0, tm, INPROJ_SUBTILE):
        rs = slice(r0, r0 + INPROJ_SUBTILE)
        x = x_ref[0, rs, :]
        ms = jnp.mean(x * x, axis=-1, keepdims=True)
        xn = x * lax.rsqrt(ms + RMS_EPS) * g1_ref[...]
        xn = xn * (1.0 + mod_ref[0, 1:2, :]) + mod_ref[0, 0:1, :]
        xb = xn.astype(BF16)
        both = _dot(xb, win_ref[...])
        proj = both[:, :3 * w]
        za_ref[0, rs, :] = proj[:, :w]
        u_ref[0, rs, :] = _gelu(proj[:, w:2 * w]).astype(BF16)
        gv = _gelu(proj[:, 2 * w:])
        mu = jnp.mean(gv, axis=-1, keepdims=True)
        cen = gv - mu
        var = jnp.mean(cen * cen, axis=-1, keepdims=True)
        v_ref[0, rs, :] = (cen * lax.rsqrt(var + LN_EPS) * lng_ref[...] + lnb_ref[...]).astype(BF16)
        gates_ref[0, rs, :] = jax.nn.sigmoid(both[:, 3 * w:] + bgate_ref[...]).astype(BF16)


def _inproj(x, mod, g1, w_in, w_gate, b_gate, ln_g, ln_b, w1, w3, w2, tm):
    bsz, s, d = x.shape
    w = w_in.shape[1] // 3
    ng = w_gate.shape[1]
    n_exp, _, ff = w1.shape
    nsteps = s // tm
    parts = bsz * nsteps // n_exp
    assert bsz * nsteps == n_exp * parts and d % (parts * SUBLANES) == 0 and ff % (parts * SUBLANES) == 0
    tok = lambda n: pl.BlockSpec((1, tm, n), lambda b, i: (b, i, 0))
    full = lambda a: pl.BlockSpec(a.shape, lambda b, i: (0,) * a.ndim)
    wslice = lambda rows, cols: pl.BlockSpec(
        (1, rows // parts, cols), lambda b, i: ((b * nsteps + i) // parts, (b * nsteps + i) % parts, 0))
    w_both = jnp.concatenate([w_in, w_gate], axis=1).astype(BF16)
    args = (g1.reshape(1, d), w_both, b_gate.reshape(1, ng), ln_g.reshape(1, w), ln_b.reshape(1, w))
    return pl.pallas_call(
        _inproj_body,
        grid=(bsz, nsteps),
        in_specs=[tok(d), pl.BlockSpec((1, 6, d), lambda b, i: (b, 0, 0))] + [full(a) for a in args]
                 + [wslice(d, ff), wslice(d, ff), wslice(ff, d)],
        out_specs=[tok(w), tok(w), tok(w), tok(ng), wslice(d, 2 * ff), wslice(ff, d)],
        out_shape=[jax.ShapeDtypeStruct((bsz, s, w), F32),
                   jax.ShapeDtypeStruct((bsz, s, w), BF16),
                   jax.ShapeDtypeStruct((bsz, s, w), BF16),
                   jax.ShapeDtypeStruct((bsz, s, ng), BF16),
                   jax.ShapeDtypeStruct((n_exp, d, 2 * ff), BF16),
                   jax.ShapeDtypeStruct((n_exp, ff, d), BF16)],
        compiler_params=_cparams(("parallel", "parallel")),
        name="in_proj",
    )(x, mod, *args, w1, w3, w2)


def _s5_params(a_re, a_im, b_re, b_im, c_re, c_im, log_step):
    g, n = a_re.shape
    nblk = g // GROUPS_PER_BLOCK
    lam_re = jnp.minimum(a_re.astype(F32), -1e-4)
    lam_im = a_im.astype(F32)
    dt = jnp.exp(log_step.astype(F32))[:, None]
    mag = jnp.exp(lam_re * dt)
    ab_re = mag * jnp.cos(lam_im * dt)
    ab_im = mag * jnp.sin(lam_im * dt)
    den = lam_re * lam_re + lam_im * lam_im
    nr = ab_re - 1.0
    f_re = (nr * lam_re + ab_im * lam_im) / den
    f_im = (ab_im * lam_re - nr * lam_im) / den
    bt_re = b_re.astype(F32).transpose(0, 2, 1)
    bt_im = b_im.astype(F32).transpose(0, 2, 1)
    w_re = f_re[:, None, :] * bt_re - f_im[:, None, :] * bt_im
    w_im = f_re[:, None, :] * bt_im + f_im[:, None, :] * bt_re
    eye = jnp.eye(GROUPS_PER_BLOCK, dtype=F32)

    def bdiag_in(wt):
        t = wt.reshape(nblk, GROUPS_PER_BLOCK, SSM_GROUP_CH, 1, n) * eye[None, :, None, :, None]
        return t.reshape(nblk, GROUPS_PER_BLOCK * SSM_GROUP_CH, GROUPS_PER_BLOCK * n)

    def bdiag_out(ct):
        t = ct.transpose(0, 2, 1).reshape(nblk, GROUPS_PER_BLOCK, n, 1, SSM_GROUP_CH)
        t = t * eye[None, :, None, :, None]
        return t.reshape(nblk, GROUPS_PER_BLOCK * n, GROUPS_PER_BLOCK * SSM_GROUP_CH)

    w_in = jnp.concatenate([bdiag_in(w_re), bdiag_in(w_im)], axis=2).astype(BF16)
    w_out = jnp.concatenate([bdiag_out(c_re.astype(F32)), -bdiag_out(c_im.astype(F32))],
                            axis=1).astype(BF16)
    a_row = jnp.concatenate([ab_re.reshape(nblk, -1), ab_im.reshape(nblk, -1)], axis=1).reshape(1, -1)
    return a_row, w_in, w_out


def _s5_body(z_ref, a_ref, win_ref, wout_ref, d_ref, wglu_ref, bglu_ref, o_ref,
             zslab, zsb, xs, hstate, yslab):
    nb, ts, w = z_ref.shape
    rows = nb * ts
    nblk = w // LANES
    half = GROUPS_PER_BLOCK * SSM_STATE
    sw = 2 * half

    @pl.when(pl.program_id(1) == 0)
    def _():
        hstate[...] = jnp.zeros_like(hstate)

    for b in range(nb):
        for c in range(nblk):
            zslab[c, b * SLAB_PITCH:b * SLAB_PITCH + ts, :] = z_ref[b, :, c * LANES:(c + 1) * LANES]

    for s in range(ts):
        for c in range(nblk):
            zsb[s * nb:(s + 1) * nb, c * LANES:(c + 1) * LANES] = zslab.at[c][pl.ds(s, nb, stride=SLAB_PITCH), :]

    n_chunks = ts // SCAN_CHUNK
    crows = SCAN_CHUNK * nb
    units = [(j, q) for j in range(nblk) for q in range(n_chunks)]

    def in_map(j, q):
        rs = slice(q * crows, (q + 1) * crows)
        xs[rs, j * sw:(j + 1) * sw] = _dot(zsb[rs, j * LANES:(j + 1) * LANES].astype(BF16), win_ref[j])

    carry = {}

    def scan(j, q):
        re_sl = slice(j * sw, j * sw + half)
        im_sl = slice(j * sw + half, (j + 1) * sw)
        a_r = jnp.broadcast_to(a_ref[:, re_sl], (nb, half))
        a_i = jnp.broadcast_to(a_ref[:, im_sl], (nb, half))
        h_r, h_i = carry[j] if q else (hstate[:, re_sl], hstate[:, im_sl])
        for s in range(q * SCAN_CHUNK, (q + 1) * SCAN_CHUNK):
            x_r = xs[s * nb:(s + 1) * nb, re_sl]
            x_i = xs[s * nb:(s + 1) * nb, im_sl]
            h_r, h_i = a_r * h_r - a_i * h_i + x_r, a_r * h_i + a_i * h_r + x_i
            xs[s * nb:(s + 1) * nb, re_sl] = h_r
            xs[s * nb:(s + 1) * nb, im_sl] = h_i
        carry[j] = (h_r, h_i)
        if q == n_chunks - 1:
            hstate[:, re_sl] = h_r
            hstate[:, im_sl] = h_i

    gel = {}

    def out_map(j, q):
        rs = slice(q * crows, (q + 1) * crows)
        y = _dot(xs[rs, j * sw:(j + 1) * sw].astype(BF16), wout_ref[j])
        y = y + d_ref[:, j * LANES:(j + 1) * LANES] * zsb[rs, j * LANES:(j + 1) * LANES]
        gel[j, q] = _gelu(y)

    for k in range(len(units) + 2):
        if k < len(units):
            in_map(*units[k])
        if 0 <= k - 1 < len(units):
            scan(*units[k - 1])
        if 0 <= k - 2 < len(units):
            out_map(*units[k - 2])

    for q in range(n_chunks):
        g = jnp.concatenate([gel[j, q] for j in range(nblk)], axis=1)
        g = g * jax.nn.sigmoid(_dot(g.astype(BF16), wglu_ref[...]) + bglu_ref[...])
        for c in range(nblk):
            yslab[c, q * crows:(q + 1) * crows, :] = g[:, c * LANES:(c + 1) * LANES]

    for b in range(nb):
        for c in range(nblk):
            o_ref[b, :, c * LANES:(c + 1) * LANES] = yslab.at[c][pl.ds(b, ts, stride=nb), :].astype(BF16)


def _s5(za, a_row, w_in, w_out, d_skip, w_glu, b_glu):
    bsz, s, w = za.shape
    nb, ts = SCAN_BATCHES, SCAN_TS
    rows = nb * ts
    nblk = w // LANES
    nstate = a_row.shape[1]
    full = lambda a: pl.BlockSpec(a.shape, lambda b, i: (0,) * a.ndim)
    args = (a_row, w_in, w_out, d_skip.reshape(1, w).astype(F32), w_glu.astype(BF16), b_glu.reshape(1, w))
    return pl.pallas_call(
        _s5_body,
        grid=(bsz // nb, s // ts),
        in_specs=[pl.BlockSpec((nb, ts, w), lambda b, i: (b, i, 0))] + [full(a) for a in args],
        out_specs=pl.BlockSpec((nb, ts, w), lambda b, i: (b, i, 0)),
        out_shape=jax.ShapeDtypeStruct((bsz, s, w), BF16),
        scratch_shapes=[pltpu.VMEM((nblk, nb * SLAB_PITCH, LANES), F32),
                        pltpu.VMEM((rows, w), F32),
                        pltpu.VMEM((rows, nstate), F32),
                        pltpu.VMEM((nb, nstate), F32),
                        pltpu.VMEM((nblk, rows, LANES), F32)],
        compiler_params=_cparams(("parallel", "arbitrary")),
        name="s5",
    )(za, *args)


def _mix_body(x_ref, u_ref, v_ref, ya_ref, gates_ref, mod_ref, wp_ref, sb_ref, wba_ref, wbb_ref,
              wout_ref, g2_ref, wr_ref, br_ref, h1_ref, xp_ref, route_ref):
    tm_full = x_ref.shape[1]
    npair = u_ref.shape[2] // LANES
    hd = LANES // 2

    t_idx = lax.broadcasted_iota(I32, (SGU_CHUNK, 2 * SGU_CHUNK), 0)
    s_idx = lax.broadcasted_iota(I32, (SGU_CHUNK, 2 * SGU_CHUNK), 1) & (SGU_CHUNK - 1)
    causal = s_idx <= t_idx
    lane = lax.broadcasted_iota(I32, (SGU_CHUNK, LANES), 1)
    first_head = lane < hd
    w_causal = [jnp.where(causal, wp_ref[j], jnp.zeros_like(wp_ref[j])) for j in range(npair)]
    for r0 in range(0, tm_full, MIX_SUBTILE):
        _mix_rows(slice(r0, r0 + MIX_SUBTILE), w_causal, first_head, x_ref, u_ref, v_ref, ya_ref, gates_ref,
                  mod_ref, sb_ref, wba_ref, wbb_ref, wout_ref, g2_ref, wr_ref, br_ref, h1_ref, xp_ref,
                  route_ref)


def _mix_rows(rs, w_causal, first_head, x_ref, u_ref, v_ref, ya_ref, gates_ref, mod_ref, sb_ref, wba_ref,
              wbb_ref, wout_ref, g2_ref, wr_ref, br_ref, h1_ref, xp_ref, route_ref):
    tm = rs.stop - rs.start
    d = x_ref.shape[2]
    npair = len(w_causal)
    mixed_rows = []
    for c in range(rs.start, rs.stop, SGU_CHUNK):
        vc = v_ref[0, c:c + SGU_CHUNK, :]
        blocks = []
        for j in range(npair):
            vb = vc[:, j * LANES:(j + 1) * LANES]
            zero = jnp.zeros_like(vb)
            rhs = jnp.concatenate([jnp.where(first_head, vb, zero), jnp.where(first_head, zero, vb)], axis=0)
            blocks.append(_dot(w_causal[j], rhs))
        mixed_rows.append(jnp.concatenate(blocks, axis=1) + sb_ref[...])
    mixed = jnp.concatenate(mixed_rows, axis=0)
    yb = (u_ref[0, rs, :].astype(F32) * mixed).astype(BF16)

    pa = _dot(ya_ref[0, rs, :], wba_ref[...])
    pb = _dot(yb, wbb_ref[...])
    merged = gates_ref[0, rs, :d].astype(F32) * pa + gates_ref[0, rs, d:].astype(F32) * pb
    o = _dot(merged.astype(BF16), wout_ref[...])
    h1 = x_ref[0, rs, :] + mod_ref[0, 2:3, :] * o
    h1_ref[0, rs, :] = h1

    ms = jnp.mean(h1 * h1, axis=-1, keepdims=True)
    xn = h1 * lax.rsqrt(ms + RMS_EPS) * g2_ref[...]
    xn = xn * (1.0 + mod_ref[0, 4:5, :]) + mod_ref[0, 3:4, :]
    x_hi = xn.astype(BF16)

    logits = _dot(x_hi, wr_ref[...]) + br_ref[...]
    lt = logits.T
    best = lt[0:1, :]
    grp = jnp.zeros((1, tm), I32)
    for gi in range(1, N_GROUPS):
        better = lt[gi:gi + 1, :] > best
        grp = jnp.where(better, gi, grp)
        best = jnp.where(better, lt[gi:gi + 1, :], best)
    den = jnp.zeros((1, tm), F32)
    for gi in range(N_GROUPS):
        den = den + jnp.exp(lt[gi:gi + 1, :] - best)
    pg = 1.0 / den
    le = lt[SUBLANES:2 * SUBLANES, :]
    for gi in range(1, N_GROUPS):
        le = jnp.where(grp == gi, lt[SUBLANES * (gi + 1):SUBLANES * (gi + 2), :], le)
    eidx = lax.broadcasted_iota(I32, (EXPERTS_PER_GROUP, tm), 0).astype(F32)
    none = float(EXPERTS_PER_GROUP)
    v1 = jnp.max(le, axis=0, keepdims=True)
    i1 = jnp.min(jnp.where(le == v1, eidx, none), axis=0, keepdims=True)
    rest = jnp.where(eidx == i1, -jnp.inf, le)
    v2 = jnp.max(rest, axis=0, keepdims=True)
    i2 = jnp.min(jnp.where(rest == v2, eidx, none), axis=0, keepdims=True)
    ex = jnp.exp(v2 - v1)
    p1 = 1.0 / (1.0 + ex)
    wt1 = pg * p1
    wt2 = pg * (ex * p1)
    first_lo = i1 < i2
    lo = jnp.where(first_lo, i1, i2)
    hi = jnp.where(first_lo, i2, i1)
    pair = lo * (2 * EXPERTS_PER_GROUP - 1.0 - lo) * 0.5 + (hi - lo - 1.0)
    cls = grp.astype(F32) * PAIRS_PER_GROUP + pair
    info = jnp.concatenate([cls, jnp.where(first_lo, wt1, wt2), jnp.where(first_lo, wt2, wt1)], axis=0)
    route_ref[:, rs] = jnp.concatenate([info, jnp.zeros((SUBLANES - 3, tm), F32)], axis=0)

    bits = lax.bitcast_convert_type(x_hi.astype(F32), I32)
    packed = lax.shift_right_logical(bits[:, :d // 2], 16) | (bits[:, d // 2:] & jnp.int32(-65536))
    info_cols = jnp.concatenate([info, jnp.zeros((LANES - 3, tm), F32)], axis=0).T
    xp_ref[0, rs, :] = jnp.concatenate([packed, lax.bitcast_convert_type(info_cols, I32)], axis=1)


def _mix(x, u, v, ya, gates, mod, sgu_w, sgu_b, w_ba, w_bb, w_out, g2, w_rg, b_rg, w_re, b_re, tm):
    bsz, s, d = x.shape
    w = u.shape[2]
    nt = bsz * s
    wp = sgu_w.reshape(SGU_HEADS // 2, 2, SGU_CHUNK, SGU_CHUNK).transpose(0, 2, 1, 3)
    wp = wp.reshape(SGU_HEADS // 2, SGU_CHUNK, 2 * SGU_CHUNK).astype(BF16)
    sb = jnp.repeat(sgu_b.T, w // SGU_HEADS, axis=1).astype(F32)
    wr = jnp.zeros((d, LANES), F32)
    wr = wr.at[:, :N_GROUPS].set(w_rg)
    wr = wr.at[:, SUBLANES:SUBLANES + N_GROUPS * EXPERTS_PER_GROUP].set(
        w_re.transpose(1, 0, 2).reshape(d, N_GROUPS * EXPERTS_PER_GROUP))
    br = jnp.zeros((1, LANES), F32)
    br = br.at[0, :N_GROUPS].set(b_rg)
    br = br.at[0, SUBLANES:SUBLANES + N_GROUPS * EXPERTS_PER_GROUP].set(b_re.reshape(-1))
    tok = lambda n: pl.BlockSpec((1, tm, n), lambda b, i: (b, i, 0))
    full = lambda a: pl.BlockSpec(a.shape, lambda b, i: (0,) * a.ndim)
    args = (wp, sb, w_ba.astype(BF16), w_bb.astype(BF16), w_out.astype(BF16), g2.reshape(1, d),
            wr.astype(BF16), br)
    nsteps = s // tm
    return pl.pallas_call(
        _mix_body,
        grid=(bsz, nsteps),
        in_specs=[tok(d), tok(w), tok(w), tok(w), tok(2 * d),
                  pl.BlockSpec((1, 6, d), lambda b, i: (b, 0, 0))] + [full(a) for a in args],
        out_specs=[tok(d), tok(d // 2 + LANES),
                   pl.BlockSpec((SUBLANES, tm), lambda b, i: (0, b * nsteps + i))],
        out_shape=[jax.ShapeDtypeStruct((bsz, s, d), F32),
                   jax.ShapeDtypeStruct((bsz, s, d // 2 + LANES), I32),
                   jax.ShapeDtypeStruct((SUBLANES, nt), F32)],
        compiler_params=_cparams(("parallel", "parallel")),
        name="mix",
    )(x, u, v, ya, gates, mod, *args)


def _tile_lanes(a, reps):
    return jnp.concatenate([a] * reps, axis=1)


def _dispatch_body(route_ref, lo_ref, hi_ref, pos_ref, blk_ref, cls_ref, run, pstart, ranks, classes,
                   earlier):
    i = pl.program_id(0)
    n = pl.num_programs(0)
    tt = route_ref.shape[1]
    reps = tt // LANES
    cid = lax.broadcasted_iota(I32, (LANES, tt), 0).astype(F32)

    @pl.when(i == 0)
    def _():
        run[...] = jnp.zeros_like(run)
        earlier_tok = lax.broadcasted_iota(I32, (tt, tt), 0) < lax.broadcasted_iota(I32, (tt, tt), 1)
        earlier[...] = jnp.where(earlier_tok, 1.0, 0.0).astype(BF16)

    cls = route_ref[0:1, :]
    classes[i] = cls
    onehot = jnp.where(cls == cid, 1.0, 0.0)
    before = _dot(onehot.astype(BF16), earlier[...])
    ranks[i] = jnp.sum(onehot * (before + _tile_lanes(run[...], reps)), axis=0, keepdims=True)
    run[...] = run[...] + jnp.sum(onehot, axis=1, keepdims=True)

    @pl.when(i == n - 1)
    def _():
        counts = run[...]
        nblk = jnp.floor((counts + (EXPERT_ROWS - 1.0)) * (1.0 / EXPERT_ROWS))
        hi_part = jnp.floor(nblk * (1.0 / 16.0))
        lo_part = nblk - 16.0 * hi_part
        upto = lax.broadcasted_iota(I32, (LANES, LANES), 1) <= lax.broadcasted_iota(I32, (LANES, LANES), 0)
        upto = jnp.where(upto, 1.0, 0.0).astype(BF16)
        ends = 16.0 * _dot(upto, hi_part.astype(BF16)) + _dot(upto, lo_part.astype(BF16))
        pstart[...] = (ends - nblk) * EXPERT_ROWS
        diag = lax.broadcasted_iota(I32, (LANES, LANES), 0) == lax.broadcasted_iota(I32, (LANES, LANES), 1)
        on_lanes = lambda a: jnp.sum(jnp.where(diag, a, 0.0), axis=0, keepdims=True)
        cls_ref[...] = jnp.concatenate([on_lanes(counts), on_lanes(pstart[...]),
                                        jnp.zeros((SUBLANES - 2, LANES), F32)], axis=0)
        nb_lanes = blk_ref.shape[1]
        breps = nb_lanes // LANES
        n_used = _tile_lanes(ends[LANES - 1:LANES, :], breps)
        b_idx = lax.broadcasted_iota(I32, (LANES, nb_lanes), 1).astype(F32)
        b_idx = jnp.minimum(b_idx, n_used - 1.0)
        b_cls = jnp.sum(jnp.where(_tile_lanes(ends, breps) <= b_idx, 1.0, 0.0), axis=0, keepdims=True)
        b_member = b_cls == lax.broadcasted_iota(I32, (LANES, nb_lanes), 0).astype(F32)
        pick = lambda tab: jnp.sum(jnp.where(b_member, _tile_lanes(tab, breps), 0.0), axis=0, keepdims=True)
        blk_ref[...] = jnp.concatenate([pick(lo_ref[...]), pick(hi_ref[...]), n_used,
                                        jnp.zeros((SUBLANES - 3, nb_lanes), F32)], axis=0)
        starts = _tile_lanes(pstart[...], reps)

        def tile_positions(k, carry):
            base = jnp.sum(jnp.where(classes[k] == cid, starts, 0.0), axis=0, keepdims=True)
            pos_ref[k] = (base + ranks[k]).astype(I32)
            return carry

        lax.fori_loop(0, n, tile_positions, 0)


def _dispatch(route, n_blocks, tt=1024):
    nt = route.shape[1]
    n = nt // tt
    nb_lanes = pl.cdiv(n_blocks, LANES) * LANES
    lo_tab, hi_tab = [], []
    for g in range(N_GROUPS):
        for a in range(EXPERTS_PER_GROUP):
            for b in range(a + 1, EXPERTS_PER_GROUP):
                lo_tab.append(g * EXPERTS_PER_GROUP + a)
                hi_tab.append(g * EXPERTS_PER_GROUP + b)
    pad = [0] * (LANES - N_CLASSES)
    lo_tile = jnp.broadcast_to(jnp.asarray(lo_tab + pad, F32)[:, None], (LANES, LANES))
    hi_tile = jnp.broadcast_to(jnp.asarray(hi_tab + pad, F32)[:, None], (LANES, LANES))
    const = lambda shape: pl.BlockSpec(shape, lambda i: (0,) * len(shape))
    pos, blk_tab, cls_tab = pl.pallas_call(
        _dispatch_body,
        grid=(n,),
        in_specs=[pl.BlockSpec((SUBLANES, tt), lambda i: (0, i)),
                  const((LANES, LANES)), const((LANES, LANES))],
        out_specs=[const((n, 1, tt)), const((SUBLANES, nb_lanes)), const((SUBLANES, LANES))],
        out_shape=[jax.ShapeDtypeStruct((n, 1, tt), I32),
                   jax.ShapeDtypeStruct((SUBLANES, nb_lanes), F32),
                   jax.ShapeDtypeStruct((SUBLANES, LANES), F32)],
        scratch_shapes=[pltpu.VMEM((LANES, LANES), F32), pltpu.VMEM((LANES, LANES), F32),
                        pltpu.VMEM((n, 1, tt), F32), pltpu.VMEM((n, 1, tt), F32),
                        pltpu.VMEM((tt, tt), BF16)],
        compiler_params=_cparams(("arbitrary",)),
        name="dispatch",
    )(route, lo_tile, hi_tile)
    blk = blk_tab.astype(I32)
    cls = cls_tab.astype(I32)
    return pos.reshape(nt), blk[0, :n_blocks], blk[1, :n_blocks], blk[2, :1], cls[0], cls[1]


ZERO_FILL_ROWS = tuple(1 << k for k in range(EXPERT_ROWS.bit_length() - 1))


def _tile_rows_wait(buf, sem):
    pltpu.make_async_copy(buf, buf, sem).wait()


def _scatter_body(cnt_ref, pst_ref, pos_ref, xp_ref, xs_hbm, ring, zeros, sem, zsem):
    i = pl.program_id(0)
    n = pl.num_programs(0)
    tm, wrow = xp_ref.shape
    half = tm // 2
    n_tiles = wrow // LANES

    @pl.when(i == 0)
    def _():
        ring[...] = jnp.zeros_like(ring)

    for h in range(2):
        @pl.when(i > 0)
        def _():
            _tile_rows_wait(ring.at[h], sem.at[h])

        for c in range(n_tiles):
            ring.at[h][pl.ds(c, half, stride=SUBLANES), :] = xp_ref[h * half:(h + 1) * half,
                                                                    c * LANES:(c + 1) * LANES]
        for r in range(half):
            pltpu.make_async_copy(ring.at[h, pl.ds(r * SUBLANES, SUBLANES)],
                                  xs_hbm.at[pos_ref[0, 0, h * half + r]], sem.at[h]).start(priority=r % 2)

    @pl.when(i == n - 1)
    def _():
        _tile_rows_wait(ring.at[0], sem.at[0])
        _tile_rows_wait(ring.at[1], sem.at[1])
        zeros[...] = jnp.zeros_like(zeros)

        def fill(wait):
            def per_class(k, carry):
                cnt = cnt_ref[k]
                n_pad = (-cnt) & (EXPERT_ROWS - 1)
                off = pst_ref[k] + cnt
                for size in ZERO_FILL_ROWS:
                    piece = pltpu.make_async_copy(zeros.at[pl.ds(0, size)], xs_hbm.at[pl.ds(off, size)], zsem)

                    @pl.when((n_pad & size) != 0)
                    def _():
                        piece.wait() if wait else piece.start()

                    off = off + (n_pad & size)
                return carry

            lax.fori_loop(0, N_CLASSES, per_class, 0)

            last = N_CLASSES - 1
            used_rows = pst_ref[last] + cnt_ref[last] + ((-cnt_ref[last]) & (EXPERT_ROWS - 1))
            tail = ZERO_FILL_ROWS[-1]

            def per_piece(j, carry):
                piece = pltpu.make_async_copy(zeros, xs_hbm.at[pl.ds(used_rows + j * tail, tail)], zsem)
                piece.wait() if wait else piece.start()
                return carry

            lax.fori_loop(0, (xs_hbm.shape[0] - used_rows) // tail, per_piece, 0)

        fill(False)
        fill(True)


def _scatter_rows(xp, pos, counts, pstarts, n_blocks, tm=1024):
    nt, wrow = xp.shape
    n = nt // tm
    grid_spec = pltpu.PrefetchScalarGridSpec(
        num_scalar_prefetch=2,
        grid=(n,),
        in_specs=[pl.BlockSpec((1, 1, tm), lambda i, c, p: (i, 0, 0), memory_space=pltpu.SMEM),
                  pl.BlockSpec((tm, wrow), lambda i, c, p: (i, 0))],
        out_specs=pl.BlockSpec(memory_space=pl.ANY),
        scratch_shapes=[pltpu.VMEM((2, tm // 2 * SUBLANES, LANES), I32),
                        pltpu.VMEM((ZERO_FILL_ROWS[-1], SUBLANES, LANES), I32),
                        pltpu.SemaphoreType.DMA((2,)), pltpu.SemaphoreType.DMA(())],
    )
    return pl.pallas_call(
        _scatter_body,
        grid_spec=grid_spec,
        out_shape=jax.ShapeDtypeStruct((n_blocks * EXPERT_ROWS, SUBLANES, LANES), I32),
        compiler_params=_cparams(("arbitrary",)),
        name="scatter_rows",
    )(counts, pstarts, pos.reshape(n, 1, tm), xp)


def _unpack_rows(xp):
    left = lax.bitcast_convert_type(lax.shift_left(xp, 16), F32).astype(BF16)
    right = lax.bitcast_convert_type(xp & jnp.int32(-65536), F32).astype(BF16)
    return jnp.concatenate([left, right], axis=1)


def _experts_body(elo_ref, ehi_ref, nu_ref, xs_ref, w13a_ref, w2a_ref, w13b_ref, w2b_ref, o_ref):
    i = pl.program_id(0)
    rows = xs_ref.shape[0] // SUBLANES
    ff, d = w2a_ref.shape[1], w2a_ref.shape[2]
    n_packed = d // 2 // LANES
    lane_tile = lambda ref, c: ref[pl.ds(c, rows, stride=SUBLANES), :]

    @pl.when(i < nu_ref[0])
    def _():
        x = _unpack_rows(jnp.concatenate([lane_tile(xs_ref, c) for c in range(n_packed)], axis=1))
        info = lax.bitcast_convert_type(lane_tile(xs_ref, n_packed), F32)

        def ffn(w13_ref, w2_ref, weight):
            ab = _dot(x, w13_ref[0])
            a = ab[:, :ff]
            hid = (a * jax.nn.sigmoid(a)) * ab[:, ff:] * weight
            return _dot(hid.astype(BF16), w2_ref[0])

        y = ffn(w13a_ref, w2a_ref, info[:, 1:2]) + ffn(w13b_ref, w2b_ref, info[:, 2:3])
        for c in range(d // LANES):
            o_ref[pl.ds(c, rows, stride=SUBLANES), :] = y[:, c * LANES:(c + 1) * LANES]

    @pl.when(i >= nu_ref[0])
    def _():
        o_ref[...] = jnp.zeros_like(o_ref)


def _experts(xs, e_lo, e_hi, n_used, w13, w2):
    n_rows = xs.shape[0]
    n_blocks = n_rows // EXPERT_ROWS
    n_exp, ff, d = w2.shape
    assert d == SUBLANES * LANES
    blk = EXPERT_ROWS * SUBLANES
    pick = lambda a, use_hi: pl.BlockSpec(
        (1,) + a.shape[1:],
        (lambda i, lo, hi, nu: (hi[i], 0, 0)) if use_hi else (lambda i, lo, hi, nu: (lo[i], 0, 0)))
    grid_spec = pltpu.PrefetchScalarGridSpec(
        num_scalar_prefetch=3,
        grid=(n_blocks,),
        in_specs=[pl.BlockSpec((blk, LANES), lambda i, lo, hi, nu: (jnp.minimum(i, nu[0] - 1), 0)),
                  pick(w13, False), pick(w2, False), pick(w13, True), pick(w2, True)],
        out_specs=pl.BlockSpec((blk, LANES), lambda i, lo, hi, nu: (i, 0)),
    )
    y = pl.pallas_call(
        _experts_body,
        grid_spec=grid_spec,
        out_shape=jax.ShapeDtypeStruct((n_rows * SUBLANES, LANES), F32),
        compiler_params=_cparams(("arbitrary",)),
        name="experts",
    )(e_lo, e_hi, n_used, xs.reshape(n_rows * SUBLANES, LANES), w13, w2, w13, w2)
    return y.reshape(n_rows, SUBLANES, LANES)


def _final_body(posc_ref, posn_ref, y_hbm, h1_ref, mod_ref, gf_ref, o_ref, ybuf, sem):
    i = pl.program_id(0)
    n = pl.num_programs(0)
    tm = h1_ref.shape[0]
    half = tm // 2

    def fetch(idx_ref, h):
        for r in range(half):
            pltpu.make_async_copy(y_hbm.at[idx_ref[0, 0, h * half + r]],
                                  ybuf.at[h, pl.ds(r * SUBLANES, SUBLANES)], sem.at[h]).start(priority=r % 2)

    @pl.when(i == 0)
    def _():
        fetch(posc_ref, 0)
        fetch(posc_ref, 1)

    for h in range(2):
        rows = slice(h * half, (h + 1) * half)
        _tile_rows_wait(ybuf.at[h], sem.at[h])
        y = jnp.concatenate([ybuf.at[h][pl.ds(c, half, stride=SUBLANES), :] for c in range(SUBLANES)], axis=1)
        hres = h1_ref[rows, :] + mod_ref[0, 5:6, :] * y
        ms = jnp.mean(hres * hres, axis=-1, keepdims=True)
        o_ref[rows, :] = hres * lax.rsqrt(ms + RMS_EPS) * gf_ref[...]
        fetch(posn_ref, h)

    @pl.when(i == n - 1)
    def _():
        _tile_rows_wait(ybuf.at[0], sem.at[0])
        _tile_rows_wait(ybuf.at[1], sem.at[1])


def _final(pos, y_sorted, h1, mod, gf, seq, tm):
    nt, d = h1.shape
    n = nt // tm
    pos3 = pos.reshape(n, 1, tm)
    idx_spec = lambda f: pl.BlockSpec((1, 1, tm), f, memory_space=pltpu.SMEM)
    return pl.pallas_call(
        _final_body,
        grid=(n,),
        in_specs=[idx_spec(lambda i: (i, 0, 0)),
                  idx_spec(lambda i: (jnp.minimum(i + 1, n - 1), 0, 0)),
                  pl.BlockSpec(memory_space=pl.ANY),
                  pl.BlockSpec((tm, d), lambda i: (i, 0)),
                  pl.BlockSpec((1, 6, d), lambda i: (i * tm // seq, 0, 0)),
                  pl.BlockSpec((1, d), lambda i: (0, 0))],
        out_specs=pl.BlockSpec((tm, d), lambda i: (i, 0)),
        out_shape=jax.ShapeDtypeStruct((nt, d), F32),
        scratch_shapes=[pltpu.VMEM((2, tm // 2 * SUBLANES, LANES), F32), pltpu.SemaphoreType.DMA((2,))],
        compiler_params=_cparams(("arbitrary",)),
        name="final",
    )(pos3, pos3, y_sorted, h1, mod, gf.reshape(1, d))


def kernel(x, c, w_ada, b_ada, norm1_g, w_in, w_gate, b_gate, ssm_a_re, ssm_a_im, ssm_b_re, ssm_b_im, ssm_c_re, ssm_c_im, ssm_d, ssm_log_step, w_glu, b_glu, sgu_ln_g, sgu_ln_b, sgu_w, sgu_b, w_branch_a, w_branch_b, w_out, norm2_g, w_router_group, b_router_group, w_router_expert, b_router_expert, w1, w3, w2, norm_f_g):
    bsz, seq, d = x.shape
    depth = w_ada.shape[0]
    assert depth == 1 and bsz % SCAN_BATCHES == 0 and seq % 512 == 0
    l = 0
    mod = _adaln(c, w_ada[l], b_ada[l]).reshape(bsz, 6, d)
    za, u, v, gates, w13_bf, w2_bf = _inproj(x, mod, norm1_g[l], w_in[l], w_gate[l], b_gate[l],
                                             sgu_ln_g[l], sgu_ln_b[l], w1[l], w3[l], w2[l], tm=1024)
    a_row, s5_in, s5_out = _s5_params(ssm_a_re[l], ssm_a_im[l], ssm_b_re[l], ssm_b_im[l],
                                      ssm_c_re[l], ssm_c_im[l], ssm_log_step[l])
    ya = _s5(za, a_row, s5_in, s5_out, ssm_d[l], w_glu[l], b_glu[l])
    h1, xp, route = _mix(x, u, v, ya, gates, mod, sgu_w[l], sgu_b[l], w_branch_a[l], w_branch_b[l],
                         w_out[l], norm2_g[l], w_router_group[l], b_router_group[l],
                         w_router_expert[l], b_router_expert[l], tm=1024)
    nt = bsz * seq
    n_blocks = nt // EXPERT_ROWS + N_CLASSES
    pos, e_lo, e_hi, n_used, counts, pstarts = _dispatch(route, n_blocks)
    xs = _scatter_rows(xp.reshape(nt, xp.shape[-1]), pos, counts, pstarts, n_blocks)
    y_sorted = _experts(xs, e_lo, e_hi, n_used, w13_bf, w2_bf)
    out = _final(pos, y_sorted, h1.reshape(nt, d), mod, norm_f_g, seq, tm=1024)
    return out.reshape(bsz, seq, d)
```

```python
import math

import jax
import jax.numpy as jnp
from jax import lax
from jax.experimental import pallas as pl
from jax.experimental.pallas import tpu as pltpu

F32 = jnp.float32
BF16 = jnp.bfloat16
I32 = jnp.int32

LANES = 128
SUBLANES = 8
VMEM_LIMIT = 56 * 1024 * 1024

RMS_EPS = 1e-6
LN_EPS = 1e-5

SSM_GROUP_CH = 16
SSM_STATE = 64
GROUPS_PER_BLOCK = LANES // SSM_GROUP_CH
SGU_HEADS = 8
SGU_CHUNK = 128
N_GROUPS = 4
EXPERTS_PER_GROUP = 8
PAIRS_PER_GROUP = EXPERTS_PER_GROUP * (EXPERTS_PER_GROUP - 1) // 2
N_CLASSES = N_GROUPS * PAIRS_PER_GROUP
EXPERT_ROWS = 256

SCAN_BATCHES = SUBLANES
SCAN_TS = 128
SLAB_PITCH = SCAN_TS + 8
SCAN_CHUNK = 64
INPROJ_SUBTILE = 256
MIX_SUBTILE = 512


def _gelu(x):
    return 0.5 * x * (1.0 + jnp.tanh(math.sqrt(2.0 / math.pi) * (x + 0.044715 * (x * x * x))))


def _dot(a, b):
    return jnp.dot(a, b, preferred_element_type=F32)


def _split_bf16(a):
    hi = a.astype(BF16)
    lo = (a - hi.astype(F32)).astype(BF16)
    return hi, lo


def _cparams(sem):
    return pltpu.CompilerParams(dimension_semantics=sem, vmem_limit_bytes=VMEM_LIMIT)


def _adaln_body(c_ref, w_ref, b_ref, o_ref):
    c = c_ref[...]
    act = c * jax.nn.sigmoid(c)
    a_hi, a_lo = _split_bf16(act)
    w_hi, w_lo = _split_bf16(w_ref[...])
    o_ref[...] = _dot(a_hi, w_hi) + _dot(a_hi, w_lo) + _dot(a_lo, w_hi) + b_ref[...]


def _adaln(c, w, b):
    bsz, d = c.shape
    n = w.shape[1]
    tn = 1024
    return pl.pallas_call(
        _adaln_body,
        grid=(n // tn,),
        in_specs=[pl.BlockSpec((bsz, d), lambda j: (0, 0)),
                  pl.BlockSpec((d, tn), lambda j: (0, j)),
                  pl.BlockSpec((1, tn), lambda j: (0, j))],
        out_specs=pl.BlockSpec((bsz, tn), lambda j: (0, j)),
        out_shape=jax.ShapeDtypeStruct((bsz, n), F32),
        compiler_params=_cparams(("arbitrary",)),
        name="adaln",
    )(c, w, b.reshape(1, n))


def _inproj_body(x_ref, mod_ref, g1_ref, win_ref, bgate_ref, lng_ref, lnb_ref,
                 w1_ref, w3_ref, w2_ref, za_ref, u_ref, v_ref, gates_ref, w13_ref, w2b_ref):
    w = za_ref.shape[-1]
    tm = x_ref.shape[1]
    ff = w1_ref.shape[2]
    w13_ref[0, :, :ff] = w1_ref[0].astype(BF16)
    w13_ref[0, :, ff:] = w3_ref[0].astype(BF16)
    w2b_ref[0] = w2_ref[0].astype(BF16)
    for r0 in range(0, tm, INPROJ_SUBTILE):
        rs = slice(r0, r0 + INPROJ_SUBTILE)
        x = x_ref[0, rs, :]
        ms = jnp.mean(x * x, axis=-1, keepdims=True)
        xn = x * lax.rsqrt(ms + RMS_EPS) * g1_ref[...]
        xn = xn * (1.0 + mod_ref[0, 1:2, :]) + mod_ref[0, 0:1, :]
        xb = xn.astype(BF16)
        both = _dot(xb, win_ref[...])
        proj = both[:, :3 * w]
        za_ref[0, rs, :] = proj[:, :w]
        u_ref[0, rs, :] = _gelu(proj[:, w:2 * w]).astype(BF16)
        gv = _gelu(proj[:, 2 * w:])
        mu = jnp.mean(gv, axis=-1, keepdims=True)
        cen = gv - mu
        var = jnp.mean(cen * cen, axis=-1, keepdims=True)
        v_ref[0, rs, :] = (cen * lax.rsqrt(var + LN_EPS) * lng_ref[...] + lnb_ref[...]).astype(BF16)
        gates_ref[0, rs, :] = jax.nn.sigmoid(both[:, 3 * w:] + bgate_ref[...]).astype(BF16)


def _inproj(x, mod, g1, w_in, w_gate, b_gate, ln_g, ln_b, w1, w3, w2, tm):
    bsz, s, d = x.shape
    w = w_in.shape[1] // 3
    ng = w_gate.shape[1]
    n_exp, _, ff = w1.shape
    nsteps = s // tm
    parts = bsz * nsteps // n_exp
    assert bsz * nsteps == n_exp * parts and d % (parts * SUBLANES) == 0 and ff % (parts * SUBLANES) == 0
    tok = lambda n: pl.BlockSpec((1, tm, n), lambda b, i: (b, i, 0))
    full = lambda a: pl.BlockSpec(a.shape, lambda b, i: (0,) * a.ndim)
    wslice = lambda rows, cols: pl.BlockSpec(
        (1, rows // parts, cols), lambda b, i: ((b * nsteps + i) // parts, (b * nsteps + i) % parts, 0))
    w_both = jnp.concatenate([w_in, w_gate], axis=1).astype(BF16)
    args = (g1.reshape(1, d), w_both, b_gate.reshape(1, ng), ln_g.reshape(1, w), ln_b.reshape(1, w))
    return pl.pallas_call(
        _inproj_body,
        grid=(bsz, nsteps),
        in_specs=[tok(d), pl.BlockSpec((1, 6, d), lambda b, i: (b, 0, 0))] + [full(a) for a in args]
                 + [wslice(d, ff), wslice(d, ff), wslice(ff, d)],
        out_specs=[tok(w), tok(w), tok(w), tok(ng), wslice(d, 2 * ff), wslice(ff, d)],
        out_shape=[jax.ShapeDtypeStruct((bsz, s, w), F32),
                   jax.ShapeDtypeStruct((bsz, s, w), BF16),
                   jax.ShapeDtypeStruct((bsz, s, w), BF16),
                   jax.ShapeDtypeStruct((bsz, s, ng), BF16),
                   jax.ShapeDtypeStruct((n_exp, d, 2 * ff), BF16),
                   jax.ShapeDtypeStruct((n_exp, ff, d), BF16)],
        compiler_params=_cparams(("parallel", "parallel")),
        name="in_proj",
    )(x, mod, *args, w1, w3, w2)


def _s5_params(a_re, a_im, b_re, b_im, c_re, c_im, log_step):
    g, n = a_re.shape
    nblk = g // GROUPS_PER_BLOCK
    lam_re = jnp.minimum(a_re.astype(F32), -1e-4)
    lam_im = a_im.astype(F32)
    dt = jnp.exp(log_step.astype(F32))[:, None]
    mag = jnp.exp(lam_re * dt)
    ab_re = mag * jnp.cos(lam_im * dt)
    ab_im = mag * jnp.sin(lam_im * dt)
    den = lam_re * lam_re + lam_im * lam_im
    nr = ab_re - 1.0
    f_re = (nr * lam_re + ab_im * lam_im) / den
    f_im = (ab_im * lam_re - nr * lam_im) / den
    bt_re = b_re.astype(F32).transpose(0, 2, 1)
    bt_im = b_im.astype(F32).transpose(0, 2, 1)
    w_re = f_re[:, None, :] * bt_re - f_im[:, None, :] * bt_im
    w_im = f_re[:, None, :] * bt_im + f_im[:, None, :] * bt_re
    eye = jnp.eye(GROUPS_PER_BLOCK, dtype=F32)

    def bdiag_in(wt):
        t = wt.reshape(nblk, GROUPS_PER_BLOCK, SSM_GROUP_CH, 1, n) * eye[None, :, None, :, None]
        return t.reshape(nblk, GROUPS_PER_BLOCK * SSM_GROUP_CH, GROUPS_PER_BLOCK * n)

    def bdiag_out(ct):
        t = ct.transpose(0, 2, 1).reshape(nblk, GROUPS_PER_BLOCK, n, 1, SSM_GROUP_CH)
        t = t * eye[None, :, None, :, None]
        return t.reshape(nblk, GROUPS_PER_BLOCK * n, GROUPS_PER_BLOCK * SSM_GROUP_CH)

    w_in = jnp.concatenate([bdiag_in(w_re), bdiag_in(w_im)], axis=2).astype(BF16)
    w_out = jnp.concatenate([bdiag_out(c_re.astype(F32)), -bdiag_out(c_im.astype(F32))],
                            axis=1).astype(BF16)
    a_row = jnp.concatenate([ab_re.reshape(nblk, -1), ab_im.reshape(nblk, -1)], axis=1).reshape(1, -1)
    return a_row, w_in, w_out


def _s5_body(z_ref, a_ref, win_ref, wout_ref, d_ref, wglu_ref, bglu_ref, o_ref,
             zslab, zsb, xs, hstate, yslab):
    nb, ts, w = z_ref.shape
    rows = nb * ts
    nblk = w // LANES
    half = GROUPS_PER_BLOCK * SSM_STATE
    sw = 2 * half

    @pl.when(pl.program_id(1) == 0)
    def _():
        hstate[...] = jnp.zeros_like(hstate)

    for b in range(nb):
        for c in range(nblk):
            zslab[c, b * SLAB_PITCH:b * SLAB_PITCH + ts, :] = z_ref[b, :, c * LANES:(c + 1) * LANES]

    for s in range(ts):
        for c in range(nblk):
            zsb[s * nb:(s + 1) * nb, c * LANES:(c + 1) * LANES] = zslab.at[c][pl.ds(s, nb, stride=SLAB_PITCH), :]

    n_chunks = ts // SCAN_CHUNK
    crows = SCAN_CHUNK * nb
    units = [(j, q) for j in range(nblk) for q in range(n_chunks)]

    def in_map(j, q):
        rs = slice(q * crows, (q + 1) * crows)
        xs[rs, j * sw:(j + 1) * sw] = _dot(zsb[rs, j * LANES:(j + 1) * LANES].astype(BF16), win_ref[j])

    carry = {}

    def scan(j, q):
        re_sl = slice(j * sw, j * sw + half)
        im_sl = slice(j * sw + half, (j + 1) * sw)
        a_r = jnp.broadcast_to(a_ref[:, re_sl], (nb, half))
        a_i = jnp.broadcast_to(a_ref[:, im_sl], (nb, half))
        h_r, h_i = carry[j] if q else (hstate[:, re_sl], hstate[:, im_sl])
        for s in range(q * SCAN_CHUNK, (q + 1) * SCAN_CHUNK):
            x_r = xs[s * nb:(s + 1) * nb, re_sl]
            x_i = xs[s * nb:(s + 1) * nb, im_sl]
            h_r, h_i = a_r * h_r - a_i * h_i + x_r, a_r * h_i + a_i * h_r + x_i
            xs[s * nb:(s + 1) * nb, re_sl] = h_r
            xs[s * nb:(s + 1) * nb, im_sl] = h_i
        carry[j] = (h_r, h_i)
        if q == n_chunks - 1:
            hstate[:, re_sl] = h_r
            hstate[:, im_sl] = h_i

    gel = {}

    def out_map(j, q):
        rs = slice(q * crows, (q + 1) * crows)
        y = _dot(xs[rs, j * sw:(j + 1) * sw].astype(BF16), wout_ref[j])
        y = y + d_ref[:, j * LANES:(j + 1) * LANES] * zsb[rs, j * LANES:(j + 1) * LANES]
        gel[j, q] = _gelu(y)

    for k in range(len(units) + 2):
        if k < len(units):
            in_map(*units[k])
        if 0 <= k - 1 < len(units):
            scan(*units[k - 1])
        if 0 <= k - 2 < len(units):
            out_map(*units[k - 2])

    for q in range(n_chunks):
        g = jnp.concatenate([gel[j, q] for j in range(nblk)], axis=1)
        g = g * jax.nn.sigmoid(_dot(g.astype(BF16), wglu_ref[...]) + bglu_ref[...])
        for c in range(nblk):
            yslab[c, q * crows:(q + 1) * crows, :] = g[:, c * LANES:(c + 1) * LANES]

    for b in range(nb):
        for c in range(nblk):
            o_ref[b, :, c * LANES:(c + 1) * LANES] = yslab.at[c][pl.ds(b, ts, stride=nb), :].astype(BF16)


def _s5(za, a_row, w_in, w_out, d_skip, w_glu, b_glu):
    bsz, s, w = za.shape
    nb, ts = SCAN_BATCHES, SCAN_TS
    rows = nb * ts
    nblk = w // LANES
    nstate = a_row.shape[1]
    full = lambda a: pl.BlockSpec(a.shape, lambda b, i: (0,) * a.ndim)
    args = (a_row, w_in, w_out, d_skip.reshape(1, w).astype(F32), w_glu.astype(BF16), b_glu.reshape(1, w))
    return pl.pallas_call(
        _s5_body,
        grid=(bsz // nb, s // ts),
        in_specs=[pl.BlockSpec((nb, ts, w), lambda b, i: (b, i, 0))] + [full(a) for a in args],
        out_specs=pl.BlockSpec((nb, ts, w), lambda b, i: (b, i, 0)),
        out_shape=jax.ShapeDtypeStruct((bsz, s, w), BF16),
        scratch_shapes=[pltpu.VMEM((nblk, nb * SLAB_PITCH, LANES), F32),
                        pltpu.VMEM((rows, w), F32),
                        pltpu.VMEM((rows, nstate), F32),
                        pltpu.VMEM((nb, nstate), F32),
                        pltpu.VMEM((nblk, rows, LANES), F32)],
        compiler_params=_cparams(("parallel", "arbitrary")),
        name="s5",
    )(za, *args)


def _mix_body(x_ref, u_ref, v_ref, ya_ref, gates_ref, mod_ref, wp_ref, sb_ref, wba_ref, wbb_ref,
              wout_ref, g2_ref, wr_ref, br_ref, h1_ref, xp_ref, route_ref):
    tm_full = x_ref.shape[1]
    npair = u_ref.shape[2] // LANES
    hd = LANES // 2

    t_idx = lax.broadcasted_iota(I32, (SGU_CHUNK, 2 * SGU_CHUNK), 0)
    s_idx = lax.broadcasted_iota(I32, (SGU_CHUNK, 2 * SGU_CHUNK), 1) & (SGU_CHUNK - 1)
    causal = s_idx <= t_idx
    lane = lax.broadcasted_iota(I32, (SGU_CHUNK, LANES), 1)
    first_head = lane < hd
    w_causal = [jnp.where(causal, wp_ref[j], jnp.zeros_like(wp_ref[j])) for j in range(npair)]
    for r0 in range(0, tm_full, MIX_SUBTILE):
        _mix_rows(slice(r0, r0 + MIX_SUBTILE), w_causal, first_head, x_ref, u_ref, v_ref, ya_ref, gates_ref,
                  mod_ref, sb_ref, wba_ref, wbb_ref, wout_ref, g2_ref, wr_ref, br_ref, h1_ref, xp_ref,
                  route_ref)


def _mix_rows(rs, w_causal, first_head, x_ref, u_ref, v_ref, ya_ref, gates_ref, mod_ref, sb_ref, wba_ref,
              wbb_ref, wout_ref, g2_ref, wr_ref, br_ref, h1_ref, xp_ref, route_ref):
    tm = rs.stop - rs.start
    d = x_ref.shape[2]
    npair = len(w_causal)
    mixed_rows = []
    for c in range(rs.start, rs.stop, SGU_CHUNK):
        vc = v_ref[0, c:c + SGU_CHUNK, :]
        blocks = []
        for j in range(npair):
            vb = vc[:, j * LANES:(j + 1) * LANES]
            zero = jnp.zeros_like(vb)
            rhs = jnp.concatenate([jnp.where(first_head, vb, zero), jnp.where(first_head, zero, vb)], axis=0)
            blocks.append(_dot(w_causal[j], rhs))
        mixed_rows.append(jnp.concatenate(blocks, axis=1) + sb_ref[...])
    mixed = jnp.concatenate(mixed_rows, axis=0)
    yb = (u_ref[0, rs, :].astype(F32) * mixed).astype(BF16)

    pa = _dot(ya_ref[0, rs, :], wba_ref[...])
    pb = _dot(yb, wbb_ref[...])
    merged = gates_ref[0, rs, :d].astype(F32) * pa + gates_ref[0, rs, d:].astype(F32) * pb
    o = _dot(merged.astype(BF16), wout_ref[...])
    h1 = x_ref[0, rs, :] + mod_ref[0, 2:3, :] * o
    h1_ref[0, rs, :] = h1

    ms = jnp.mean(h1 * h1, axis=-1, keepdims=True)
    xn = h1 * lax.rsqrt(ms + RMS_EPS) * g2_ref[...]
    xn = xn * (1.0 + mod_ref[0, 4:5, :]) + mod_ref[0, 3:4, :]
    x_hi = xn.astype(BF16)

    logits = _dot(x_hi, wr_ref[...]) + br_ref[...]
    lt = logits.T
    best = lt[0:1, :]
    grp = jnp.zeros((1, tm), I32)
    for gi in range(1, N_GROUPS):
        better = lt[gi:gi + 1, :] > best
        grp = jnp.where(better, gi, grp)
        best = jnp.where(better, lt[gi:gi + 1, :], best)
    den = jnp.zeros((1, tm), F32)
    for gi in range(N_GROUPS):
        den = den + jnp.exp(lt[gi:gi + 1, :] - best)
    pg = 1.0 / den
    le = lt[SUBLANES:2 * SUBLANES, :]
    for gi in range(1, N_GROUPS):
        le = jnp.where(grp == gi, lt[SUBLANES * (gi + 1):SUBLANES * (gi + 2), :], le)
    eidx = lax.broadcasted_iota(I32, (EXPERTS_PER_GROUP, tm), 0).astype(F32)
    none = float(EXPERTS_PER_GROUP)
    v1 = jnp.max(le, axis=0, keepdims=True)
    i1 = jnp.min(jnp.where(le == v1, eidx, none), axis=0, keepdims=True)
    rest = jnp.where(eidx == i1, -jnp.inf, le)
    v2 = jnp.max(rest, axis=0, keepdims=True)
    i2 = jnp.min(jnp.where(rest == v2, eidx, none), axis=0, keepdims=True)
    ex = jnp.exp(v2 - v1)
    p1 = 1.0 / (1.0 + ex)
    wt1 = pg * p1
    wt2 = pg * (ex * p1)
    first_lo = i1 < i2
    lo = jnp.where(first_lo, i1, i2)
    hi = jnp.where(first_lo, i2, i1)
    pair = lo * (2 * EXPERTS_PER_GROUP - 1.0 - lo) * 0.5 + (hi - lo - 1.0)
    cls = grp.astype(F32) * PAIRS_PER_GROUP + pair
    info = jnp.concatenate([cls, jnp.where(first_lo, wt1, wt2), jnp.where(first_lo, wt2, wt1)], axis=0)
    route_ref[:, rs] = jnp.concatenate([info, jnp.zeros((SUBLANES - 3, tm), F32)], axis=0)

    bits = lax.bitcast_convert_type(x_hi.astype(F32), I32)
    packed = lax.shift_right_logical(bits[:, :d // 2], 16) | (bits[:, d // 2:] & jnp.int32(-65536))
    info_cols = jnp.concatenate([info, jnp.zeros((LANES - 3, tm), F32)], axis=0).T
    xp_ref[0, rs, :] = jnp.concatenate([packed, lax.bitcast_convert_type(info_cols, I32)], axis=1)


def _mix(x, u, v, ya, gates, mod, sgu_w, sgu_b, w_ba, w_bb, w_out, g2, w_rg, b_rg, w_re, b_re, tm):
    bsz, s, d = x.shape
    w = u.shape[2]
    nt = bsz * s
    wp = sgu_w.reshape(SGU_HEADS // 2, 2, SGU_CHUNK, SGU_CHUNK).transpose(0, 2, 1, 3)
    wp = wp.reshape(SGU_HEADS // 2, SGU_CHUNK, 2 * SGU_CHUNK).astype(BF16)
    sb = jnp.repeat(sgu_b.T, w // SGU_HEADS, axis=1).astype(F32)
    wr = jnp.zeros((d, LANES), F32)
    wr = wr.at[:, :N_GROUPS].set(w_rg)
    wr = wr.at[:, SUBLANES:SUBLANES + N_GROUPS * EXPERTS_PER_GROUP].set(
        w_re.transpose(1, 0, 2).reshape(d, N_GROUPS * EXPERTS_PER_GROUP))
    br = jnp.zeros((1, LANES), F32)
    br = br.at[0, :N_GROUPS].set(b_rg)
    br = br.at[0, SUBLANES:SUBLANES + N_GROUPS * EXPERTS_PER_GROUP].set(b_re.reshape(-1))
    tok = lambda n: pl.BlockSpec((1, tm, n), lambda b, i: (b, i, 0))
    full = lambda a: pl.BlockSpec(a.shape, lambda b, i: (0,) * a.ndim)
    args = (wp, sb, w_ba.astype(BF16), w_bb.astype(BF16), w_out.astype(BF16), g2.reshape(1, d),
            wr.astype(BF16), br)
    nsteps = s // tm
    return pl.pallas_call(
        _mix_body,
        grid=(bsz, nsteps),
        in_specs=[tok(d), tok(w), tok(w), tok(w), tok(2 * d),
                  pl.BlockSpec((1, 6, d), lambda b, i: (b, 0, 0))] + [full(a) for a in args],
        out_specs=[tok(d), tok(d // 2 + LANES),
                   pl.BlockSpec((SUBLANES, tm), lambda b, i: (0, b * nsteps + i))],
        out_shape=[jax.ShapeDtypeStruct((bsz, s, d), F32),
                   jax.ShapeDtypeStruct((bsz, s, d // 2 + LANES), I32),
                   jax.ShapeDtypeStruct((SUBLANES, nt), F32)],
        compiler_params=_cparams(("parallel", "parallel")),
        name="mix",
    )(x, u, v, ya, gates, mod, *args)


def _tile_lanes(a, reps):
    return jnp.concatenate([a] * reps, axis=1)


def _dispatch_body(route_ref, lo_ref, hi_ref, pos_ref, blk_ref, cls_ref, run, pstart, ranks, classes,
                   earlier):
    i = pl.program_id(0)
    n = pl.num_programs(0)
    tt = route_ref.shape[1]
    reps = tt // LANES
    cid = lax.broadcasted_iota(I32, (LANES, tt), 0).astype(F32)

    @pl.when(i == 0)
    def _():
        run[...] = jnp.zeros_like(run)
        earlier_tok = lax.broadcasted_iota(I32, (tt, tt), 0) < lax.broadcasted_iota(I32, (tt, tt), 1)
        earlier[...] = jnp.where(earlier_tok, 1.0, 0.0).astype(BF16)

    cls = route_ref[0:1, :]
    classes[i] = cls
    onehot = jnp.where(cls == cid, 1.0, 0.0)
    before = _dot(onehot.astype(BF16), earlier[...])
    ranks[i] = jnp.sum(onehot * (before + _tile_lanes(run[...], reps)), axis=0, keepdims=True)
    run[...] = run[...] + jnp.sum(onehot, axis=1, keepdims=True)

    @pl.when(i == n - 1)
    def _():
        counts = run[...]
        nblk = jnp.floor((counts + (EXPERT_ROWS - 1.0)) * (1.0 / EXPERT_ROWS))
        hi_part = jnp.floor(nblk * (1.0 / 16.0))
        lo_part = nblk - 16.0 * hi_part
        upto = lax.broadcasted_iota(I32, (LANES, LANES), 1) <= lax.broadcasted_iota(I32, (LANES, LANES), 0)
        upto = jnp.where(upto, 1.0, 0.0).astype(BF16)
        ends = 16.0 * _dot(upto, hi_part.astype(BF16)) + _dot(upto, lo_part.astype(BF16))
        pstart[...] = (ends - nblk) * EXPERT_ROWS
        diag = lax.broadcasted_iota(I32, (LANES, LANES), 0) == lax.broadcasted_iota(I32, (LANES, LANES), 1)
        on_lanes = lambda a: jnp.sum(jnp.where(diag, a, 0.0), axis=0, keepdims=True)
        cls_ref[...] = jnp.concatenate([on_lanes(counts), on_lanes(pstart[...]),
                                        jnp.zeros((SUBLANES - 2, LANES), F32)], axis=0)
        nb_lanes = blk_ref.shape[1]
        breps = nb_lanes // LANES
        n_used = _tile_lanes(ends[LANES - 1:LANES, :], breps)
        b_idx = lax.broadcasted_iota(I32, (LANES, nb_lanes), 1).astype(F32)
        b_idx = jnp.minimum(b_idx, n_used - 1.0)
        b_cls = jnp.sum(jnp.where(_tile_lanes(ends, breps) <= b_idx, 1.0, 0.0), axis=0, keepdims=True)
        b_member = b_cls == lax.broadcasted_iota(I32, (LANES, nb_lanes), 0).astype(F32)
        pick = lambda tab: jnp.sum(jnp.where(b_member, _tile_lanes(tab, breps), 0.0), axis=0, keepdims=True)
        blk_ref[...] = jnp.concatenate([pick(lo_ref[...]), pick(hi_ref[...]), n_used,
                                        jnp.zeros((SUBLANES - 3, nb_lanes), F32)], axis=0)
        starts = _tile_lanes(pstart[...], reps)

        def tile_positions(k, carry):
            base = jnp.sum(jnp.where(classes[k] == cid, starts, 0.0), axis=0, keepdims=True)
            pos_ref[k] = (base + ranks[k]).astype(I32)
            return carry

        lax.fori_loop(0, n, tile_positions, 0)


def _dispatch(route, n_blocks, tt=1024):
    nt = route.shape[1]
    n = nt // tt
    nb_lanes = pl.cdiv(n_blocks, LANES) * LANES
    lo_tab, hi_tab = [], []
    for g in range(N_GROUPS):
        for a in range(EXPERTS_PER_GROUP):
            for b in range(a + 1, EXPERTS_PER_GROUP):
                lo_tab.append(g * EXPERTS_PER_GROUP + a)
                hi_tab.append(g * EXPERTS_PER_GROUP + b)
    pad = [0] * (LANES - N_CLASSES)
    lo_tile = jnp.broadcast_to(jnp.asarray(lo_tab + pad, F32)[:, None], (LANES, LANES))
    hi_tile = jnp.broadcast_to(jnp.asarray(hi_tab + pad, F32)[:, None], (LANES, LANES))
    const = lambda shape: pl.BlockSpec(shape, lambda i: (0,) * len(shape))
    pos, blk_tab, cls_tab = pl.pallas_call(
        _dispatch_body,
        grid=(n,),
        in_specs=[pl.BlockSpec((SUBLANES, tt), lambda i: (0, i)),
                  const((LANES, LANES)), const((LANES, LANES))],
        out_specs=[const((n, 1, tt)), const((SUBLANES, nb_lanes)), const((SUBLANES, LANES))],
        out_shape=[jax.ShapeDtypeStruct((n, 1, tt), I32),
                   jax.ShapeDtypeStruct((SUBLANES, nb_lanes), F32),
                   jax.ShapeDtypeStruct((SUBLANES, LANES), F32)],
        scratch_shapes=[pltpu.VMEM((LANES, LANES), F32), pltpu.VMEM((LANES, LANES), F32),
                        pltpu.VMEM((n, 1, tt), F32), pltpu.VMEM((n, 1, tt), F32),
                        pltpu.VMEM((tt, tt), BF16)],
        compiler_params=_cparams(("arbitrary",)),
        name="dispatch",
    )(route, lo_tile, hi_tile)
    blk = blk_tab.astype(I32)
    cls = cls_tab.astype(I32)
    return pos.reshape(nt), blk[0, :n_blocks], blk[1, :n_blocks], blk[2, :1], cls[0], cls[1]


ZERO_FILL_ROWS = tuple(1 << k for k in range(EXPERT_ROWS.bit_length() - 1))


def _tile_rows_wait(buf, sem):
    pltpu.make_async_copy(buf, buf, sem).wait()


def _scatter_body(cnt_ref, pst_ref, pos_ref, xp_ref, xs_hbm, ring, zeros, sem, zsem):
    i = pl.program_id(0)
    n = pl.num_programs(0)
    tm, wrow = xp_ref.shape
    half = tm // 2
    n_tiles = wrow // LANES

    @pl.when(i == 0)
    def _():
        ring[...] = jnp.zeros_like(ring)

    for h in range(2):
        @pl.when(i > 0)
        def _():
            _tile_rows_wait(ring.at[h], sem.at[h])

        for c in range(n_tiles):
            ring.at[h][pl.ds(c, half, stride=SUBLANES), :] = xp_ref[h * half:(h + 1) * half,
                                                                    c * LANES:(c + 1) * LANES]
        for r in range(half):
            pltpu.make_async_copy(ring.at[h, pl.ds(r * SUBLANES, SUBLANES)],
                                  xs_hbm.at[pos_ref[0, 0, h * half + r]], sem.at[h]).start(priority=r % 2)

    @pl.when(i == n - 1)
    def _():
        _tile_rows_wait(ring.at[0], sem.at[0])
        _tile_rows_wait(ring.at[1], sem.at[1])
        zeros[...] = jnp.zeros_like(zeros)

        def fill(wait):
            def per_class(k, carry):
                cnt = cnt_ref[k]
                n_pad = (-cnt) & (EXPERT_ROWS - 1)
                off = pst_ref[k] + cnt
                for size in ZERO_FILL_ROWS:
                    piece = pltpu.make_async_copy(zeros.at[pl.ds(0, size)], xs_hbm.at[pl.ds(off, size)], zsem)

                    @pl.when((n_pad & size) != 0)
                    def _():
                        piece.wait() if wait else piece.start()

                    off = off + (n_pad & size)
                return carry

            lax.fori_loop(0, N_CLASSES, per_class, 0)

            last = N_CLASSES - 1
            used_rows = pst_ref[last] + cnt_ref[last] + ((-cnt_ref[last]) & (EXPERT_ROWS - 1))
            tail = ZERO_FILL_ROWS[-1]

            def per_piece(j, carry):
                piece = pltpu.make_async_copy(zeros, xs_hbm.at[pl.ds(used_rows + j * tail, tail)], zsem)
                piece.wait() if wait else piece.start()
                return carry

            lax.fori_loop(0, (xs_hbm.shape[0] - used_rows) // tail, per_piece, 0)

        fill(False)
        fill(True)


def _scatter_rows(xp, pos, counts, pstarts, n_blocks, tm=2048):
    nt, wrow = xp.shape
    n = nt // tm
    grid_spec = pltpu.PrefetchScalarGridSpec(
        num_scalar_prefetch=2,
        grid=(n,),
        in_specs=[pl.BlockSpec((1, 1, tm), lambda i, c, p: (i, 0, 0), memory_space=pltpu.SMEM),
                  pl.BlockSpec((tm, wrow), lambda i, c, p: (i, 0))],
        out_specs=pl.BlockSpec(memory_space=pl.ANY),
        scratch_shapes=[pltpu.VMEM((2, tm // 2 * SUBLANES, LANES), I32),
                        pltpu.VMEM((ZERO_FILL_ROWS[-1], SUBLANES, LANES), I32),
                        pltpu.SemaphoreType.DMA((2,)), pltpu.SemaphoreType.DMA(())],
    )
    return pl.pallas_call(
        _scatter_body,
        grid_spec=grid_spec,
        out_shape=jax.ShapeDtypeStruct((n_blocks * EXPERT_ROWS, SUBLANES, LANES), I32),
        compiler_params=_cparams(("arbitrary",)),
        name="scatter_rows",
    )(counts, pstarts, pos.reshape(n, 1, tm), xp)


def _unpack_rows(xp):
    left = lax.bitcast_convert_type(lax.shift_left(xp, 16), F32).astype(BF16)
    right = lax.bitcast_convert_type(xp & jnp.int32(-65536), F32).astype(BF16)
    return jnp.concatenate([left, right], axis=1)


def _experts_body(elo_ref, ehi_ref, nu_ref, xs_ref, w13a_ref, w2a_ref, w13b_ref, w2b_ref, o_ref):
    i = pl.program_id(0)
    rows = xs_ref.shape[0] // SUBLANES
    ff, d = w2a_ref.shape[1], w2a_ref.shape[2]
    n_packed = d // 2 // LANES
    lane_tile = lambda ref, c: ref[pl.ds(c, rows, stride=SUBLANES), :]

    @pl.when(i < nu_ref[0])
    def _():
        x = _unpack_rows(jnp.concatenate([lane_tile(xs_ref, c) for c in range(n_packed)], axis=1))
        info = lax.bitcast_convert_type(lane_tile(xs_ref, n_packed), F32)

        def ffn(w13_ref, w2_ref, weight):
            ab = _dot(x, w13_ref[0])
            a = ab[:, :ff]
            hid = (a * jax.nn.sigmoid(a)) * ab[:, ff:] * weight
            return _dot(hid.astype(BF16), w2_ref[0])

        y = ffn(w13a_ref, w2a_ref, info[:, 1:2]) + ffn(w13b_ref, w2b_ref, info[:, 2:3])
        for c in range(d // LANES):
            o_ref[pl.ds(c, rows, stride=SUBLANES), :] = y[:, c * LANES:(c + 1) * LANES]

    @pl.when(i >= nu_ref[0])
    def _():
        o_ref[...] = jnp.zeros_like(o_ref)


def _experts(xs, e_lo, e_hi, n_used, w13, w2):
    n_rows = xs.shape[0]
    n_blocks = n_rows // EXPERT_ROWS
    n_exp, ff, d = w2.shape
    assert d == SUBLANES * LANES
    blk = EXPERT_ROWS * SUBLANES
    pick = lambda a, use_hi: pl.BlockSpec(
        (1,) + a.shape[1:],
        (lambda i, lo, hi, nu: (hi[i], 0, 0)) if use_hi else (lambda i, lo, hi, nu: (lo[i], 0, 0)))
    grid_spec = pltpu.PrefetchScalarGridSpec(
        num_scalar_prefetch=3,
        grid=(n_blocks,),
        in_specs=[pl.BlockSpec((blk, LANES), lambda i, lo, hi, nu: (jnp.minimum(i, nu[0] - 1), 0)),
                  pick(w13, False), pick(w2, False), pick(w13, True), pick(w2, True)],
        out_specs=pl.BlockSpec((blk, LANES), lambda i, lo, hi, nu: (i, 0)),
    )
    y = pl.pallas_call(
        _experts_body,
        grid_spec=grid_spec,
        out_shape=jax.ShapeDtypeStruct((n_rows * SUBLANES, LANES), F32),
        compiler_params=_cparams(("arbitrary",)),
        name="experts",
    )(e_lo, e_hi, n_used, xs.reshape(n_rows * SUBLANES, LANES), w13, w2, w13, w2)
    return y.reshape(n_rows, SUBLANES, LANES)


def _final_body(posc_ref, posn_ref, y_hbm, h1_ref, mod_ref, gf_ref, o_ref, ybuf, sem):
    i = pl.program_id(0)
    n = pl.num_programs(0)
    tm = h1_ref.shape[0]
    half = tm // 2

    def fetch(idx_ref, h):
        for r in range(half):
            pltpu.make_async_copy(y_hbm.at[idx_ref[0, 0, h * half + r]],
                                  ybuf.at[h, pl.ds(r * SUBLANES, SUBLANES)], sem.at[h]).start(priority=r % 2)

    @pl.when(i == 0)
    def _():
        fetch(posc_ref, 0)
        fetch(posc_ref, 1)

    for h in range(2):
        rows = slice(h * half, (h + 1) * half)
        _tile_rows_wait(ybuf.at[h], sem.at[h])
        y = jnp.concatenate([ybuf.at[h][pl.ds(c, half, stride=SUBLANES), :] for c in range(SUBLANES)], axis=1)
        hres = h1_ref[rows, :] + mod_ref[0, 5:6, :] * y
        ms = jnp.mean(hres * hres, axis=-1, keepdims=True)
        o_ref[rows, :] = hres * lax.rsqrt(ms + RMS_EPS) * gf_ref[...]
        fetch(posn_ref, h)

    @pl.when(i == n - 1)
    def _():
        _tile_rows_wait(ybuf.at[0], sem.at[0])
        _tile_rows_wait(ybuf.at[1], sem.at[1])


def _final(pos, y_sorted, h1, mod, gf, seq, tm):
    nt, d = h1.shape
    n = nt // tm
    pos3 = pos.reshape(n, 1, tm)
    idx_spec = lambda f: pl.BlockSpec((1, 1, tm), f, memory_space=pltpu.SMEM)
    return pl.pallas_call(
        _final_body,
        grid=(n,),
        in_specs=[idx_spec(lambda i: (i, 0, 0)),
                  idx_spec(lambda i: (jnp.minimum(i + 1, n - 1), 0, 0)),
                  pl.BlockSpec(memory_space=pl.ANY),
                  pl.BlockSpec((tm, d), lambda i: (i, 0)),
                  pl.BlockSpec((1, 6, d), lambda i: (i * tm // seq, 0, 0)),
                  pl.BlockSpec((1, d), lambda i: (0, 0))],
        out_specs=pl.BlockSpec((tm, d), lambda i: (i, 0)),
        out_shape=jax.ShapeDtypeStruct((nt, d), F32),
        scratch_shapes=[pltpu.VMEM((2, tm // 2 * SUBLANES, LANES), F32), pltpu.SemaphoreType.DMA((2,))],
        compiler_params=_cparams(("arbitrary",)),
        name="final",
    )(pos3, pos3, y_sorted, h1, mod, gf.reshape(1, d))


def kernel(x, c, w_ada, b_ada, norm1_g, w_in, w_gate, b_gate, ssm_a_re, ssm_a_im, ssm_b_re, ssm_b_im, ssm_c_re, ssm_c_im, ssm_d, ssm_log_step, w_glu, b_glu, sgu_ln_g, sgu_ln_b, sgu_w, sgu_b, w_branch_a, w_branch_b, w_out, norm2_g, w_router_group, b_router_group, w_router_expert, b_router_expert, w1, w3, w2, norm_f_g):
    bsz, seq, d = x.shape
    depth = w_ada.shape[0]
    assert depth == 1 and bsz % SCAN_BATCHES == 0 and seq % 512 == 0
    l = 0
    mod = _adaln(c, w_ada[l], b_ada[l]).reshape(bsz, 6, d)
    za, u, v, gates, w13_bf, w2_bf = _inproj(x, mod, norm1_g[l], w_in[l], w_gate[l], b_gate[l],
                                             sgu_ln_g[l], sgu_ln_b[l], w1[l], w3[l], w2[l], tm=1024)
    a_row, s5_in, s5_out = _s5_params(ssm_a_re[l], ssm_a_im[l], ssm_b_re[l], ssm_b_im[l],
                                      ssm_c_re[l], ssm_c_im[l], ssm_log_step[l])
    ya = _s5(za, a_row, s5_in, s5_out, ssm_d[l], w_glu[l], b_glu[l])
    h1, xp, route = _mix(x, u, v, ya, gates, mod, sgu_w[l], sgu_b[l], w_branch_a[l], w_branch_b[l],
                         w_out[l], norm2_g[l], w_router_group[l], b_router_group[l],
                         w_router_expert[l], b_router_expert[l], tm=1024)
    nt = bsz * seq
    n_blocks = nt // EXPERT_ROWS + N_CLASSES
    pos, e_lo, e_hi, n_used, counts, pstarts = _dispatch(route, n_blocks)
    xs = _scatter_rows(xp.reshape(nt, xp.shape[-1]), pos, counts, pstarts, n_blocks)
    y_sorted = _experts(xs, e_lo, e_hi, n_used, w13_bf, w2_bf)
    out = _final(pos, y_sorted, h1.reshape(nt, d), mod, norm_f_g, seq, tm=2048)
    return out.reshape(bsz, seq, d)
```

```python
import math

import jax
import jax.numpy as jnp
from jax import lax
from jax.experimental import pallas as pl
from jax.experimental.pallas import tpu as pltpu

F32 = jnp.float32
BF16 = jnp.bfloat16
I32 = jnp.int32

LANES = 128
SUBLANES = 8
VMEM_LIMIT = 56 * 1024 * 1024

RMS_EPS = 1e-6
LN_EPS = 1e-5

SSM_GROUP_CH = 16
SSM_STATE = 64
GROUPS_PER_BLOCK = LANES // SSM_GROUP_CH
SGU_HEADS = 8
SGU_CHUNK = 128
N_GROUPS = 4
EXPERTS_PER_GROUP = 8
PAIRS_PER_GROUP = EXPERTS_PER_GROUP * (EXPERTS_PER_GROUP - 1) // 2
N_CLASSES = N_GROUPS * PAIRS_PER_GROUP
EXPERT_ROWS = 256

SCAN_BATCHES = SUBLANES
SCAN_TS = 128
SLAB_PITCH = SCAN_TS + 8
SCAN_CHUNK = 64
INPROJ_SUBTILE = 256
MIX_SUBTILE = 512
DISPATCH_SUB = 256


def _gelu(x):
    return 0.5 * x * (1.0 + jnp.tanh(math.sqrt(2.0 / math.pi) * (x + 0.044715 * (x * x * x))))


def _dot(a, b):
    return jnp.dot(a, b, preferred_element_type=F32)


def _split_bf16(a):
    hi = a.astype(BF16)
    lo = (a - hi.astype(F32)).astype(BF16)
    return hi, lo


def _cparams(sem):
    return pltpu.CompilerParams(dimension_semantics=sem, vmem_limit_bytes=VMEM_LIMIT)


def _adaln_body(c_ref, w_ref, b_ref, o_ref):
    c = c_ref[...]
    act = c * jax.nn.sigmoid(c)
    a_hi, a_lo = _split_bf16(act)
    w_hi, w_lo = _split_bf16(w_ref[...])
    o_ref[...] = _dot(a_hi, w_hi) + _dot(a_hi, w_lo) + _dot(a_lo, w_hi) + b_ref[...]


def _adaln(c, w, b):
    bsz, d = c.shape
    n = w.shape[1]
    tn = 1024
    return pl.pallas_call(
        _adaln_body,
        grid=(n // tn,),
        in_specs=[pl.BlockSpec((bsz, d), lambda j: (0, 0)),
                  pl.BlockSpec((d, tn), lambda j: (0, j)),
                  pl.BlockSpec((1, tn), lambda j: (0, j))],
        out_specs=pl.BlockSpec((bsz, tn), lambda j: (0, j)),
        out_shape=jax.ShapeDtypeStruct((bsz, n), F32),
        compiler_params=_cparams(("arbitrary",)),
        name="adaln",
    )(c, w, b.reshape(1, n))


def _inproj_body(x_ref, mod_ref, g1_ref, win_ref, bgate_ref, lng_ref, lnb_ref,
                 w1_ref, w3_ref, w2_ref, za_ref, u_ref, v_ref, gates_ref, w13_ref, w2b_ref):
    w = za_ref.shape[-1]
    tm = x_ref.shape[1]
    ff = w1_ref.shape[2]
    w13_ref[0, :, :ff] = w1_ref[0].astype(BF16)
    w13_ref[0, :, ff:] = w3_ref[0].astype(BF16)
    w2b_ref[0] = w2_ref[0].astype(BF16)
    for r0 in range(0, tm, INPROJ_SUBTILE):
        rs = slice(r0, r0 + INPROJ_SUBTILE)
        x = x_ref[0, rs, :]
        ms = jnp.mean(x * x, axis=-1, keepdims=True)
        xn = x * lax.rsqrt(ms + RMS_EPS) * g1_ref[...]
        xn = xn * (1.0 + mod_ref[0, 1:2, :]) + mod_ref[0, 0:1, :]
        xb = xn.astype(BF16)
        both = _dot(xb, win_ref[...])
        proj = both[:, :3 * w]
        za_ref[0, rs, :] = proj[:, :w]
        u_ref[0, rs, :] = _gelu(proj[:, w:2 * w]).astype(BF16)
        gv = _gelu(proj[:, 2 * w:])
        mu = jnp.mean(gv, axis=-1, keepdims=True)
        cen = gv - mu
        var = jnp.mean(cen * cen, axis=-1, keepdims=True)
        v_ref[0, rs, :] = (cen * lax.rsqrt(var + LN_EPS) * lng_ref[...] + lnb_ref[...]).astype(BF16)
        gates_ref[0, rs, :] = jax.nn.sigmoid(both[:, 3 * w:] + bgate_ref[...]).astype(BF16)


def _inproj(x, mod, g1, w_in, w_gate, b_gate, ln_g, ln_b, w1, w3, w2, tm):
    bsz, s, d = x.shape
    w = w_in.shape[1] // 3
    ng = w_gate.shape[1]
    n_exp, _, ff = w1.shape
    nsteps = s // tm
    parts = bsz * nsteps // n_exp
    assert bsz * nsteps == n_exp * parts and d % (parts * SUBLANES) == 0 and ff % (parts * SUBLANES) == 0
    tok = lambda n: pl.BlockSpec((1, tm, n), lambda b, i: (b, i, 0))
    full = lambda a: pl.BlockSpec(a.shape, lambda b, i: (0,) * a.ndim)
    wslice = lambda rows, cols: pl.BlockSpec(
        (1, rows // parts, cols), lambda b, i: ((b * nsteps + i) // parts, (b * nsteps + i) % parts, 0))
    w_both = jnp.concatenate([w_in, w_gate], axis=1).astype(BF16)
    args = (g1.reshape(1, d), w_both, b_gate.reshape(1, ng), ln_g.reshape(1, w), ln_b.reshape(1, w))
    return pl.pallas_call(
        _inproj_body,
        grid=(bsz, nsteps),
        in_specs=[tok(d), pl.BlockSpec((1, 6, d), lambda b, i: (b, 0, 0))] + [full(a) for a in args]
                 + [wslice(d, ff), wslice(d, ff), wslice(ff, d)],
        out_specs=[tok(w), tok(w), tok(w), tok(ng), wslice(d, 2 * ff), wslice(ff, d)],
        out_shape=[jax.ShapeDtypeStruct((bsz, s, w), F32),
                   jax.ShapeDtypeStruct((bsz, s, w), BF16),
                   jax.ShapeDtypeStruct((bsz, s, w), BF16),
                   jax.ShapeDtypeStruct((bsz, s, ng), BF16),
                   jax.ShapeDtypeStruct((n_exp, d, 2 * ff), BF16),
                   jax.ShapeDtypeStruct((n_exp, ff, d), BF16)],
        compiler_params=_cparams(("parallel", "parallel")),
        name="in_proj",
    )(x, mod, *args, w1, w3, w2)


def _s5_params(a_re, a_im, b_re, b_im, c_re, c_im, log_step):
    g, n = a_re.shape
    nblk = g // GROUPS_PER_BLOCK
    lam_re = jnp.minimum(a_re.astype(F32), -1e-4)
    lam_im = a_im.astype(F32)
    dt = jnp.exp(log_step.astype(F32))[:, None]
    mag = jnp.exp(lam_re * dt)
    ab_re = mag * jnp.cos(lam_im * dt)
    ab_im = mag * jnp.sin(lam_im * dt)
    den = lam_re * lam_re + lam_im * lam_im
    nr = ab_re - 1.0
    f_re = (nr * lam_re + ab_im * lam_im) / den
    f_im = (ab_im * lam_re - nr * lam_im) / den
    bt_re = b_re.astype(F32).transpose(0, 2, 1)
    bt_im = b_im.astype(F32).transpose(0, 2, 1)
    w_re = f_re[:, None, :] * bt_re - f_im[:, None, :] * bt_im
    w_im = f_re[:, None, :] * bt_im + f_im[:, None, :] * bt_re
    eye = jnp.eye(GROUPS_PER_BLOCK, dtype=F32)

    def bdiag_in(wt):
        t = wt.reshape(nblk, GROUPS_PER_BLOCK, SSM_GROUP_CH, 1, n) * eye[None, :, None, :, None]
        return t.reshape(nblk, GROUPS_PER_BLOCK * SSM_GROUP_CH, GROUPS_PER_BLOCK * n)

    def bdiag_out(ct):
        t = ct.transpose(0, 2, 1).reshape(nblk, GROUPS_PER_BLOCK, n, 1, SSM_GROUP_CH)
        t = t * eye[None, :, None, :, None]
        return t.reshape(nblk, GROUPS_PER_BLOCK * n, GROUPS_PER_BLOCK * SSM_GROUP_CH)

    w_in = jnp.concatenate([bdiag_in(w_re), bdiag_in(w_im)], axis=2).astype(BF16)
    w_out = jnp.concatenate([bdiag_out(c_re.astype(F32)), -bdiag_out(c_im.astype(F32))],
                            axis=1).astype(BF16)
    a_row = jnp.concatenate([ab_re.reshape(nblk, -1), ab_im.reshape(nblk, -1)], axis=1).reshape(1, -1)
    return a_row, w_in, w_out


def _s5_body(z_ref, a_ref, win_ref, wout_ref, d_ref, wglu_ref, bglu_ref, o_ref,
             zslab, zsb, xs, hstate, yslab):
    nb, ts, w = z_ref.shape
    rows = nb * ts
    nblk = w // LANES
    half = GROUPS_PER_BLOCK * SSM_STATE
    sw = 2 * half

    @pl.when(pl.program_id(1) == 0)
    def _():
        hstate[...] = jnp.zeros_like(hstate)

    for b in range(nb):
        for c in range(nblk):
            zslab[c, b * SLAB_PITCH:b * SLAB_PITCH + ts, :] = z_ref[b, :, c * LANES:(c + 1) * LANES]

    for s in range(ts):
        for c in range(nblk):
            zsb[s * nb:(s + 1) * nb, c * LANES:(c + 1) * LANES] = zslab.at[c][pl.ds(s, nb, stride=SLAB_PITCH), :]

    n_chunks = ts // SCAN_CHUNK
    crows = SCAN_CHUNK * nb
    units = [(j, q) for j in range(nblk) for q in range(n_chunks)]

    def in_map(j, q):
        rs = slice(q * crows, (q + 1) * crows)
        xs[rs, j * sw:(j + 1) * sw] = _dot(zsb[rs, j * LANES:(j + 1) * LANES].astype(BF16), win_ref[j])

    carry = {}

    def scan(j, q):
        re_sl = slice(j * sw, j * sw + half)
        im_sl = slice(j * sw + half, (j + 1) * sw)
        a_r = jnp.broadcast_to(a_ref[:, re_sl], (nb, half))
        a_i = jnp.broadcast_to(a_ref[:, im_sl], (nb, half))
        h_r, h_i = carry[j] if q else (hstate[:, re_sl], hstate[:, im_sl])
        for s in range(q * SCAN_CHUNK, (q + 1) * SCAN_CHUNK):
            x_r = xs[s * nb:(s + 1) * nb, re_sl]
            x_i = xs[s * nb:(s + 1) * nb, im_sl]
            h_r, h_i = a_r * h_r - a_i * h_i + x_r, a_r * h_i + a_i * h_r + x_i
            xs[s * nb:(s + 1) * nb, re_sl] = h_r
            xs[s * nb:(s + 1) * nb, im_sl] = h_i
        carry[j] = (h_r, h_i)
        if q == n_chunks - 1:
            hstate[:, re_sl] = h_r
            hstate[:, im_sl] = h_i

    gel = {}

    def out_map(j, q):
        rs = slice(q * crows, (q + 1) * crows)
        y = _dot(xs[rs, j * sw:(j + 1) * sw].astype(BF16), wout_ref[j])
        y = y + d_ref[:, j * LANES:(j + 1) * LANES] * zsb[rs, j * LANES:(j + 1) * LANES]
        gel[j, q] = _gelu(y)

    for k in range(len(units) + 2):
        if k < len(units):
            in_map(*units[k])
        if 0 <= k - 1 < len(units):
            scan(*units[k - 1])
        if 0 <= k - 2 < len(units):
            out_map(*units[k - 2])

    for q in range(n_chunks):
        g = jnp.concatenate([gel[j, q] for j in range(nblk)], axis=1)
        g = g * jax.nn.sigmoid(_dot(g.astype(BF16), wglu_ref[...]) + bglu_ref[...])
        for c in range(nblk):
            yslab[c, q * crows:(q + 1) * crows, :] = g[:, c * LANES:(c + 1) * LANES]

    for b in range(nb):
        for c in range(nblk):
            o_ref[b, :, c * LANES:(c + 1) * LANES] = yslab.at[c][pl.ds(b, ts, stride=nb), :].astype(BF16)


def _s5(za, a_row, w_in, w_out, d_skip, w_glu, b_glu):
    bsz, s, w = za.shape
    nb, ts = SCAN_BATCHES, SCAN_TS
    rows = nb * ts
    nblk = w // LANES
    nstate = a_row.shape[1]
    full = lambda a: pl.BlockSpec(a.shape, lambda b, i: (0,) * a.ndim)
    args = (a_row, w_in, w_out, d_skip.reshape(1, w).astype(F32), w_glu.astype(BF16), b_glu.reshape(1, w))
    return pl.pallas_call(
        _s5_body,
        grid=(bsz // nb, s // ts),
        in_specs=[pl.BlockSpec((nb, ts, w), lambda b, i: (b, i, 0))] + [full(a) for a in args],
        out_specs=pl.BlockSpec((nb, ts, w), lambda b, i: (b, i, 0)),
        out_shape=jax.ShapeDtypeStruct((bsz, s, w), BF16),
        scratch_shapes=[pltpu.VMEM((nblk, nb * SLAB_PITCH, LANES), F32),
                        pltpu.VMEM((rows, w), F32),
                        pltpu.VMEM((rows, nstate), F32),
                        pltpu.VMEM((nb, nstate), F32),
                        pltpu.VMEM((nblk, rows, LANES), F32)],
        compiler_params=_cparams(("parallel", "arbitrary")),
        name="s5",
    )(za, *args)


def _mix_body(x_ref, u_ref, v_ref, ya_ref, gates_ref, mod_ref, wp_ref, sb_ref, wba_ref, wbb_ref,
              wout_ref, g2_ref, wr_ref, br_ref, h1_ref, xp_ref, route_ref):
    tm_full = x_ref.shape[1]
    npair = u_ref.shape[2] // LANES
    hd = LANES // 2

    t_idx = lax.broadcasted_iota(I32, (SGU_CHUNK, 2 * SGU_CHUNK), 0)
    s_idx = lax.broadcasted_iota(I32, (SGU_CHUNK, 2 * SGU_CHUNK), 1) & (SGU_CHUNK - 1)
    causal = s_idx <= t_idx
    lane = lax.broadcasted_iota(I32, (SGU_CHUNK, LANES), 1)
    first_head = lane < hd
    w_causal = [jnp.where(causal, wp_ref[j], jnp.zeros_like(wp_ref[j])) for j in range(npair)]
    for r0 in range(0, tm_full, MIX_SUBTILE):
        _mix_rows(slice(r0, r0 + MIX_SUBTILE), w_causal, first_head, x_ref, u_ref, v_ref, ya_ref, gates_ref,
                  mod_ref, sb_ref, wba_ref, wbb_ref, wout_ref, g2_ref, wr_ref, br_ref, h1_ref, xp_ref,
                  route_ref)


def _mix_rows(rs, w_causal, first_head, x_ref, u_ref, v_ref, ya_ref, gates_ref, mod_ref, sb_ref, wba_ref,
              wbb_ref, wout_ref, g2_ref, wr_ref, br_ref, h1_ref, xp_ref, route_ref):
    tm = rs.stop - rs.start
    d = x_ref.shape[2]
    npair = len(w_causal)
    mixed_rows = []
    for c in range(rs.start, rs.stop, SGU_CHUNK):
        vc = v_ref[0, c:c + SGU_CHUNK, :]
        blocks = []
        for j in range(npair):
            vb = vc[:, j * LANES:(j + 1) * LANES]
            zero = jnp.zeros_like(vb)
            rhs = jnp.concatenate([jnp.where(first_head, vb, zero), jnp.where(first_head, zero, vb)], axis=0)
            blocks.append(_dot(w_causal[j], rhs))
        mixed_rows.append(jnp.concatenate(blocks, axis=1) + sb_ref[...])
    mixed = jnp.concatenate(mixed_rows, axis=0)
    yb = (u_ref[0, rs, :].astype(F32) * mixed).astype(BF16)

    pa = _dot(ya_ref[0, rs, :], wba_ref[...])
    pb = _dot(yb, wbb_ref[...])
    merged = gates_ref[0, rs, :d].astype(F32) * pa + gates_ref[0, rs, d:].astype(F32) * pb
    o = _dot(merged.astype(BF16), wout_ref[...])
    h1 = x_ref[0, rs, :] + mod_ref[0, 2:3, :] * o
    h1_ref[0, rs, :] = h1

    ms = jnp.mean(h1 * h1, axis=-1, keepdims=True)
    xn = h1 * lax.rsqrt(ms + RMS_EPS) * g2_ref[...]
    xn = xn * (1.0 + mod_ref[0, 4:5, :]) + mod_ref[0, 3:4, :]
    x_hi = xn.astype(BF16)

    logits = _dot(x_hi, wr_ref[...]) + br_ref[...]
    lt = logits.T
    best = lt[0:1, :]
    grp = jnp.zeros((1, tm), I32)
    for gi in range(1, N_GROUPS):
        better = lt[gi:gi + 1, :] > best
        grp = jnp.where(better, gi, grp)
        best = jnp.where(better, lt[gi:gi + 1, :], best)
    den = jnp.zeros((1, tm), F32)
    for gi in range(N_GROUPS):
        den = den + jnp.exp(lt[gi:gi + 1, :] - best)
    pg = 1.0 / den
    le = lt[SUBLANES:2 * SUBLANES, :]
    for gi in range(1, N_GROUPS):
        le = jnp.where(grp == gi, lt[SUBLANES * (gi + 1):SUBLANES * (gi + 2), :], le)
    eidx = lax.broadcasted_iota(I32, (EXPERTS_PER_GROUP, tm), 0).astype(F32)
    none = float(EXPERTS_PER_GROUP)
    v1 = jnp.max(le, axis=0, keepdims=True)
    i1 = jnp.min(jnp.where(le == v1, eidx, none), axis=0, keepdims=True)
    rest = jnp.where(eidx == i1, -jnp.inf, le)
    v2 = jnp.max(rest, axis=0, keepdims=True)
    i2 = jnp.min(jnp.where(rest == v2, eidx, none), axis=0, keepdims=True)
    ex = jnp.exp(v2 - v1)
    p1 = 1.0 / (1.0 + ex)
    wt1 = pg * p1
    wt2 = pg * (ex * p1)
    first_lo = i1 < i2
    lo = jnp.where(first_lo, i1, i2)
    hi = jnp.where(first_lo, i2, i1)
    pair = lo * (2 * EXPERTS_PER_GROUP - 1.0 - lo) * 0.5 + (hi - lo - 1.0)
    cls = grp.astype(F32) * PAIRS_PER_GROUP + pair
    info = jnp.concatenate([cls, jnp.where(first_lo, wt1, wt2), jnp.where(first_lo, wt2, wt1)], axis=0)
    route_ref[:, rs] = jnp.concatenate([info, jnp.zeros((SUBLANES - 3, tm), F32)], axis=0)

    bits = lax.bitcast_convert_type(x_hi.astype(F32), I32)
    packed = lax.shift_right_logical(bits[:, :d // 2], 16) | (bits[:, d // 2:] & jnp.int32(-65536))
    info_cols = jnp.concatenate([info, jnp.zeros((LANES - 3, tm), F32)], axis=0).T
    xp_ref[0, rs, :] = jnp.concatenate([packed, lax.bitcast_convert_type(info_cols, I32)], axis=1)


def _mix(x, u, v, ya, gates, mod, sgu_w, sgu_b, w_ba, w_bb, w_out, g2, w_rg, b_rg, w_re, b_re, tm):
    bsz, s, d = x.shape
    w = u.shape[2]
    nt = bsz * s
    wp = sgu_w.reshape(SGU_HEADS // 2, 2, SGU_CHUNK, SGU_CHUNK).transpose(0, 2, 1, 3)
    wp = wp.reshape(SGU_HEADS // 2, SGU_CHUNK, 2 * SGU_CHUNK).astype(BF16)
    sb = jnp.repeat(sgu_b.T, w // SGU_HEADS, axis=1).astype(F32)
    wr = jnp.zeros((d, LANES), F32)
    wr = wr.at[:, :N_GROUPS].set(w_rg)
    wr = wr.at[:, SUBLANES:SUBLANES + N_GROUPS * EXPERTS_PER_GROUP].set(
        w_re.transpose(1, 0, 2).reshape(d, N_GROUPS * EXPERTS_PER_GROUP))
    br = jnp.zeros((1, LANES), F32)
    br = br.at[0, :N_GROUPS].set(b_rg)
    br = br.at[0, SUBLANES:SUBLANES + N_GROUPS * EXPERTS_PER_GROUP].set(b_re.reshape(-1))
    tok = lambda n: pl.BlockSpec((1, tm, n), lambda b, i: (b, i, 0))
    full = lambda a: pl.BlockSpec(a.shape, lambda b, i: (0,) * a.ndim)
    args = (wp, sb, w_ba.astype(BF16), w_bb.astype(BF16), w_out.astype(BF16), g2.reshape(1, d),
            wr.astype(BF16), br)
    nsteps = s // tm
    return pl.pallas_call(
        _mix_body,
        grid=(bsz, nsteps),
        in_specs=[tok(d), tok(w), tok(w), tok(w), tok(2 * d),
                  pl.BlockSpec((1, 6, d), lambda b, i: (b, 0, 0))] + [full(a) for a in args],
        out_specs=[tok(d), tok(d // 2 + LANES),
                   pl.BlockSpec((SUBLANES, tm), lambda b, i: (0, b * nsteps + i))],
        out_shape=[jax.ShapeDtypeStruct((bsz, s, d), F32),
                   jax.ShapeDtypeStruct((bsz, s, d // 2 + LANES), I32),
                   jax.ShapeDtypeStruct((SUBLANES, nt), F32)],
        compiler_params=_cparams(("parallel", "parallel")),
        name="mix",
    )(x, u, v, ya, gates, mod, *args)


def _tile_lanes(a, reps):
    return jnp.concatenate([a] * reps, axis=1)


def _dispatch_body(route_ref, lo_ref, hi_ref, pos_ref, blk_ref, cls_ref, run, pstart, ranks, classes,
                   earlier):
    i = pl.program_id(0)
    n = pl.num_programs(0)
    tt = route_ref.shape[1]
    reps = tt // LANES
    cid = lax.broadcasted_iota(I32, (LANES, tt), 0).astype(F32)

    sub = earlier.shape[0]

    @pl.when(i == 0)
    def _():
        run[...] = jnp.zeros_like(run)
        earlier_tok = lax.broadcasted_iota(I32, (sub, sub), 0) < lax.broadcasted_iota(I32, (sub, sub), 1)
        earlier[...] = jnp.where(earlier_tok, 1.0, 0.0).astype(BF16)

    cls = route_ref[0:1, :]
    classes[i] = cls
    onehot = jnp.where(cls == cid, 1.0, 0.0)
    seen = run[...]
    before = []
    for k in range(tt // sub):
        oh = onehot[:, k * sub:(k + 1) * sub]
        before.append(_dot(oh.astype(BF16), earlier[...]) + _tile_lanes(seen, sub // LANES))
        seen = seen + jnp.sum(oh, axis=1, keepdims=True)
    ranks[i] = jnp.sum(onehot * jnp.concatenate(before, axis=1), axis=0, keepdims=True)
    run[...] = seen

    @pl.when(i == n - 1)
    def _():
        counts = run[...]
        nblk = jnp.floor((counts + (EXPERT_ROWS - 1.0)) * (1.0 / EXPERT_ROWS))
        hi_part = jnp.floor(nblk * (1.0 / 16.0))
        lo_part = nblk - 16.0 * hi_part
        upto = lax.broadcasted_iota(I32, (LANES, LANES), 1) <= lax.broadcasted_iota(I32, (LANES, LANES), 0)
        upto = jnp.where(upto, 1.0, 0.0).astype(BF16)
        ends = 16.0 * _dot(upto, hi_part.astype(BF16)) + _dot(upto, lo_part.astype(BF16))
        pstart[...] = (ends - nblk) * EXPERT_ROWS
        diag = lax.broadcasted_iota(I32, (LANES, LANES), 0) == lax.broadcasted_iota(I32, (LANES, LANES), 1)
        on_lanes = lambda a: jnp.sum(jnp.where(diag, a, 0.0), axis=0, keepdims=True)
        cls_ref[...] = jnp.concatenate([on_lanes(counts), on_lanes(pstart[...]),
                                        jnp.zeros((SUBLANES - 2, LANES), F32)], axis=0)
        nb_lanes = blk_ref.shape[1]
        breps = nb_lanes // LANES
        n_used = _tile_lanes(ends[LANES - 1:LANES, :], breps)
        b_idx = lax.broadcasted_iota(I32, (LANES, nb_lanes), 1).astype(F32)
        b_idx = jnp.minimum(b_idx, n_used - 1.0)
        b_cls = jnp.sum(jnp.where(_tile_lanes(ends, breps) <= b_idx, 1.0, 0.0), axis=0, keepdims=True)
        b_member = b_cls == lax.broadcasted_iota(I32, (LANES, nb_lanes), 0).astype(F32)
        pick = lambda tab: jnp.sum(jnp.where(b_member, _tile_lanes(tab, breps), 0.0), axis=0, keepdims=True)
        blk_ref[...] = jnp.concatenate([pick(lo_ref[...]), pick(hi_ref[...]), n_used,
                                        jnp.zeros((SUBLANES - 3, nb_lanes), F32)], axis=0)
        starts = _tile_lanes(pstart[...], reps)

        def tile_positions(k, carry):
            base = jnp.sum(jnp.where(classes[k] == cid, starts, 0.0), axis=0, keepdims=True)
            pos_ref[k] = (base + ranks[k]).astype(I32)
            return carry

        lax.fori_loop(0, n, tile_positions, 0)


def _dispatch(route, n_blocks, tt=1024):
    nt = route.shape[1]
    n = nt // tt
    nb_lanes = pl.cdiv(n_blocks, LANES) * LANES
    lo_tab, hi_tab = [], []
    for g in range(N_GROUPS):
        for a in range(EXPERTS_PER_GROUP):
            for b in range(a + 1, EXPERTS_PER_GROUP):
                lo_tab.append(g * EXPERTS_PER_GROUP + a)
                hi_tab.append(g * EXPERTS_PER_GROUP + b)
    pad = [0] * (LANES - N_CLASSES)
    lo_tile = jnp.broadcast_to(jnp.asarray(lo_tab + pad, F32)[:, None], (LANES, LANES))
    hi_tile = jnp.broadcast_to(jnp.asarray(hi_tab + pad, F32)[:, None], (LANES, LANES))
    const = lambda shape: pl.BlockSpec(shape, lambda i: (0,) * len(shape))
    pos, blk_tab, cls_tab = pl.pallas_call(
        _dispatch_body,
        grid=(n,),
        in_specs=[pl.BlockSpec((SUBLANES, tt), lambda i: (0, i)),
                  const((LANES, LANES)), const((LANES, LANES))],
        out_specs=[const((n, 1, tt)), const((SUBLANES, nb_lanes)), const((SUBLANES, LANES))],
        out_shape=[jax.ShapeDtypeStruct((n, 1, tt), I32),
                   jax.ShapeDtypeStruct((SUBLANES, nb_lanes), F32),
                   jax.ShapeDtypeStruct((SUBLANES, LANES), F32)],
        scratch_shapes=[pltpu.VMEM((LANES, LANES), F32), pltpu.VMEM((LANES, LANES), F32),
                        pltpu.VMEM((n, 1, tt), F32), pltpu.VMEM((n, 1, tt), F32),
                        pltpu.VMEM((DISPATCH_SUB, DISPATCH_SUB), BF16)],
        compiler_params=_cparams(("arbitrary",)),
        name="dispatch",
    )(route, lo_tile, hi_tile)
    blk = blk_tab.astype(I32)
    cls = cls_tab.astype(I32)
    return pos.reshape(nt), blk[0, :n_blocks], blk[1, :n_blocks], blk[2, :1], cls[0], cls[1]


ZERO_FILL_ROWS = tuple(1 << k for k in range(EXPERT_ROWS.bit_length() - 1))


def _tile_rows_wait(buf, sem):
    pltpu.make_async_copy(buf, buf, sem).wait()


def _scatter_body(cnt_ref, pst_ref, pos_ref, xp_ref, xs_hbm, ring, zeros, sem, zsem):
    i = pl.program_id(0)
    n = pl.num_programs(0)
    tm, wrow = xp_ref.shape
    half = tm // 2
    n_tiles = wrow // LANES

    @pl.when(i == 0)
    def _():
        ring[...] = jnp.zeros_like(ring)

    for h in range(2):
        @pl.when(i > 0)
        def _():
            _tile_rows_wait(ring.at[h], sem.at[h])

        for c in range(n_tiles):
            ring.at[h][pl.ds(c, half, stride=SUBLANES), :] = xp_ref[h * half:(h + 1) * half,
                                                                    c * LANES:(c + 1) * LANES]
        for r in range(half):
            pltpu.make_async_copy(ring.at[h, pl.ds(r * SUBLANES, SUBLANES)],
                                  xs_hbm.at[pos_ref[0, 0, h * half + r]], sem.at[h]).start(priority=r % 2)

    @pl.when(i == n - 1)
    def _():
        _tile_rows_wait(ring.at[0], sem.at[0])
        _tile_rows_wait(ring.at[1], sem.at[1])
        zeros[...] = jnp.zeros_like(zeros)

        def fill(wait):
            def per_class(k, carry):
                cnt = cnt_ref[k]
                n_pad = (-cnt) & (EXPERT_ROWS - 1)
                off = pst_ref[k] + cnt
                for size in ZERO_FILL_ROWS:
                    piece = pltpu.make_async_copy(zeros.at[pl.ds(0, size)], xs_hbm.at[pl.ds(off, size)], zsem)

                    @pl.when((n_pad & size) != 0)
                    def _():
                        piece.wait() if wait else piece.start()

                    off = off + (n_pad & size)
                return carry

            lax.fori_loop(0, N_CLASSES, per_class, 0)

            last = N_CLASSES - 1
            used_rows = pst_ref[last] + cnt_ref[last] + ((-cnt_ref[last]) & (EXPERT_ROWS - 1))
            tail = ZERO_FILL_ROWS[-1]

            def per_piece(j, carry):
                piece = pltpu.make_async_copy(zeros, xs_hbm.at[pl.ds(used_rows + j * tail, tail)], zsem)
                piece.wait() if wait else piece.start()
                return carry

            lax.fori_loop(0, (xs_hbm.shape[0] - used_rows) // tail, per_piece, 0)

        fill(False)
        fill(True)


def _scatter_rows(xp, pos, counts, pstarts, n_blocks, tm=1024):
    nt, wrow = xp.shape
    n = nt // tm
    grid_spec = pltpu.PrefetchScalarGridSpec(
        num_scalar_prefetch=2,
        grid=(n,),
        in_specs=[pl.BlockSpec((1, 1, tm), lambda i, c, p: (i, 0, 0), memory_space=pltpu.SMEM),
                  pl.BlockSpec((tm, wrow), lambda i, c, p: (i, 0))],
        out_specs=pl.BlockSpec(memory_space=pl.ANY),
        scratch_shapes=[pltpu.VMEM((2, tm // 2 * SUBLANES, LANES), I32),
                        pltpu.VMEM((ZERO_FILL_ROWS[-1], SUBLANES, LANES), I32),
                        pltpu.SemaphoreType.DMA((2,)), pltpu.SemaphoreType.DMA(())],
    )
    return pl.pallas_call(
        _scatter_body,
        grid_spec=grid_spec,
        out_shape=jax.ShapeDtypeStruct((n_blocks * EXPERT_ROWS, SUBLANES, LANES), I32),
        compiler_params=_cparams(("arbitrary",)),
        name="scatter_rows",
    )(counts, pstarts, pos.reshape(n, 1, tm), xp)


def _unpack_rows(xp):
    left = lax.bitcast_convert_type(lax.shift_left(xp, 16), F32).astype(BF16)
    right = lax.bitcast_convert_type(xp & jnp.int32(-65536), F32).astype(BF16)
    return jnp.concatenate([left, right], axis=1)


def _experts_body(elo_ref, ehi_ref, nu_ref, xs_ref, w13a_ref, w2a_ref, w13b_ref, w2b_ref, o_ref):
    i = pl.program_id(0)
    rows = xs_ref.shape[0] // SUBLANES
    ff, d = w2a_ref.shape[1], w2a_ref.shape[2]
    n_packed = d // 2 // LANES
    lane_tile = lambda ref, c: ref[pl.ds(c, rows, stride=SUBLANES), :]

    @pl.when(i < nu_ref[0])
    def _():
        x = _unpack_rows(jnp.concatenate([lane_tile(xs_ref, c) for c in range(n_packed)], axis=1))
        info = lax.bitcast_convert_type(lane_tile(xs_ref, n_packed), F32)

        def ffn(w13_ref, w2_ref, weight):
            ab = _dot(x, w13_ref[0])
            a = ab[:, :ff]
            hid = (a * jax.nn.sigmoid(a)) * ab[:, ff:] * weight
            return _dot(hid.astype(BF16), w2_ref[0])

        y = ffn(w13a_ref, w2a_ref, info[:, 1:2]) + ffn(w13b_ref, w2b_ref, info[:, 2:3])
        for c in range(d // LANES):
            o_ref[pl.ds(c, rows, stride=SUBLANES), :] = y[:, c * LANES:(c + 1) * LANES]

    @pl.when(i >= nu_ref[0])
    def _():
        o_ref[...] = jnp.zeros_like(o_ref)


def _experts(xs, e_lo, e_hi, n_used, w13, w2):
    n_rows = xs.shape[0]
    n_blocks = n_rows // EXPERT_ROWS
    n_exp, ff, d = w2.shape
    assert d == SUBLANES * LANES
    blk = EXPERT_ROWS * SUBLANES
    pick = lambda a, use_hi: pl.BlockSpec(
        (1,) + a.shape[1:],
        (lambda i, lo, hi, nu: (hi[i], 0, 0)) if use_hi else (lambda i, lo, hi, nu: (lo[i], 0, 0)))
    grid_spec = pltpu.PrefetchScalarGridSpec(
        num_scalar_prefetch=3,
        grid=(n_blocks,),
        in_specs=[pl.BlockSpec((blk, LANES), lambda i, lo, hi, nu: (jnp.minimum(i, nu[0] - 1), 0)),
                  pick(w13, False), pick(w2, False), pick(w13, True), pick(w2, True)],
        out_specs=pl.BlockSpec((blk, LANES), lambda i, lo, hi, nu: (i, 0)),
    )
    y = pl.pallas_call(
        _experts_body,
        grid_spec=grid_spec,
        out_shape=jax.ShapeDtypeStruct((n_rows * SUBLANES, LANES), F32),
        compiler_params=_cparams(("arbitrary",)),
        name="experts",
    )(e_lo, e_hi, n_used, xs.reshape(n_rows * SUBLANES, LANES), w13, w2, w13, w2)
    return y.reshape(n_rows, SUBLANES, LANES)


def _final_body(posc_ref, posn_ref, y_hbm, h1_ref, mod_ref, gf_ref, o_ref, ybuf, sem):
    i = pl.program_id(0)
    n = pl.num_programs(0)
    tm = h1_ref.shape[0]
    half = tm // 2

    def fetch(idx_ref, h):
        for r in range(half):
            pltpu.make_async_copy(y_hbm.at[idx_ref[0, 0, h * half + r]],
                                  ybuf.at[h, pl.ds(r * SUBLANES, SUBLANES)], sem.at[h]).start(priority=r % 2)

    @pl.when(i == 0)
    def _():
        fetch(posc_ref, 0)
        fetch(posc_ref, 1)

    for h in range(2):
        rows = slice(h * half, (h + 1) * half)
        _tile_rows_wait(ybuf.at[h], sem.at[h])
        y = jnp.concatenate([ybuf.at[h][pl.ds(c, half, stride=SUBLANES), :] for c in range(SUBLANES)], axis=1)
        hres = h1_ref[rows, :] + mod_ref[0, 5:6, :] * y
        ms = jnp.mean(hres * hres, axis=-1, keepdims=True)
        o_ref[rows, :] = hres * lax.rsqrt(ms + RMS_EPS) * gf_ref[...]
        fetch(posn_ref, h)

    @pl.when(i == n - 1)
    def _():
        _tile_rows_wait(ybuf.at[0], sem.at[0])
        _tile_rows_wait(ybuf.at[1], sem.at[1])


def _final(pos, y_sorted, h1, mod, gf, seq, tm):
    nt, d = h1.shape
    n = nt // tm
    pos3 = pos.reshape(n, 1, tm)
    idx_spec = lambda f: pl.BlockSpec((1, 1, tm), f, memory_space=pltpu.SMEM)
    return pl.pallas_call(
        _final_body,
        grid=(n,),
        in_specs=[idx_spec(lambda i: (i, 0, 0)),
                  idx_spec(lambda i: (jnp.minimum(i + 1, n - 1), 0, 0)),
                  pl.BlockSpec(memory_space=pl.ANY),
                  pl.BlockSpec((tm, d), lambda i: (i, 0)),
                  pl.BlockSpec((1, 6, d), lambda i: (i * tm // seq, 0, 0)),
                  pl.BlockSpec((1, d), lambda i: (0, 0))],
        out_specs=pl.BlockSpec((tm, d), lambda i: (i, 0)),
        out_shape=jax.ShapeDtypeStruct((nt, d), F32),
        scratch_shapes=[pltpu.VMEM((2, tm // 2 * SUBLANES, LANES), F32), pltpu.SemaphoreType.DMA((2,))],
        compiler_params=_cparams(("arbitrary",)),
        name="final",
    )(pos3, pos3, y_sorted, h1, mod, gf.reshape(1, d))


def kernel(x, c, w_ada, b_ada, norm1_g, w_in, w_gate, b_gate, ssm_a_re, ssm_a_im, ssm_b_re, ssm_b_im, ssm_c_re, ssm_c_im, ssm_d, ssm_log_step, w_glu, b_glu, sgu_ln_g, sgu_ln_b, sgu_w, sgu_b, w_branch_a, w_branch_b, w_out, norm2_g, w_router_group, b_router_group, w_router_expert, b_router_expert, w1, w3, w2, norm_f_g):
    bsz, seq, d = x.shape
    depth = w_ada.shape[0]
    assert depth == 1 and bsz % SCAN_BATCHES == 0 and seq % 512 == 0
    l = 0
    mod = _adaln(c, w_ada[l], b_ada[l]).reshape(bsz, 6, d)
    za, u, v, gates, w13_bf, w2_bf = _inproj(x, mod, norm1_g[l], w_in[l], w_gate[l], b_gate[l],
                                             sgu_ln_g[l], sgu_ln_b[l], w1[l], w3[l], w2[l], tm=1024)
    a_row, s5_in, s5_out = _s5_params(ssm_a_re[l], ssm_a_im[l], ssm_b_re[l], ssm_b_im[l],
                                      ssm_c_re[l], ssm_c_im[l], ssm_log_step[l])
    ya = _s5(za, a_row, s5_in, s5_out, ssm_d[l], w_glu[l], b_glu[l])
    h1, xp, route = _mix(x, u, v, ya, gates, mod, sgu_w[l], sgu_b[l], w_branch_a[l], w_branch_b[l],
                         w_out[l], norm2_g[l], w_router_group[l], b_router_group[l],
                         w_router_expert[l], b_router_expert[l], tm=1024)
    nt = bsz * seq
    n_blocks = nt // EXPERT_ROWS + N_CLASSES
    pos, e_lo, e_hi, n_used, counts, pstarts = _dispatch(route, n_blocks)
    xs = _scatter_rows(xp.reshape(nt, xp.shape[-1]), pos, counts, pstarts, n_blocks)
    y_sorted = _experts(xs, e_lo, e_hi, n_used, w13_bf, w2_bf)
    out = _final(pos, y_sorted, h1.reshape(nt, d), mod, norm_f_g, seq, tm=1024)
    return out.reshape(bsz, seq, d)
```

```python
import math

import jax
import jax.numpy as jnp
from jax import lax
from jax.experimental import pallas as pl
from jax.experimental.pallas import tpu as pltpu

F32 = jnp.float32
BF16 = jnp.bfloat16
I32 = jnp.int32

LANES = 128
SUBLANES = 8
VMEM_LIMIT = 56 * 1024 * 1024

RMS_EPS = 1e-6
LN_EPS = 1e-5

SSM_GROUP_CH = 16
SSM_STATE = 64
GROUPS_PER_BLOCK = LANES // SSM_GROUP_CH
SGU_HEADS = 8
SGU_CHUNK = 128
N_GROUPS = 4
EXPERTS_PER_GROUP = 8
PAIRS_PER_GROUP = EXPERTS_PER_GROUP * (EXPERTS_PER_GROUP - 1) // 2
N_CLASSES = N_GROUPS * PAIRS_PER_GROUP
EXPERT_ROWS = 256

SCAN_BATCHES = SUBLANES
SCAN_TS = 128
SLAB_PITCH = SCAN_TS + 8
SCAN_CHUNK = 64
INPROJ_SUBTILE = 256
MIX_SUBTILE = 512
DISPATCH_SUB = 256


def _gelu(x):
    return 0.5 * x * (1.0 + jnp.tanh(math.sqrt(2.0 / math.pi) * (x + 0.044715 * (x * x * x))))


def _dot(a, b):
    return jnp.dot(a, b, preferred_element_type=F32)


def _split_bf16(a):
    hi = a.astype(BF16)
    lo = (a - hi.astype(F32)).astype(BF16)
    return hi, lo


def _cparams(sem):
    return pltpu.CompilerParams(dimension_semantics=sem, vmem_limit_bytes=VMEM_LIMIT)


def _adaln_body(c_ref, w_ref, b_ref, o_ref):
    c = c_ref[...]
    act = c * jax.nn.sigmoid(c)
    a_hi, a_lo = _split_bf16(act)
    w_hi, w_lo = _split_bf16(w_ref[...])
    o_ref[...] = _dot(a_hi, w_hi) + _dot(a_hi, w_lo) + _dot(a_lo, w_hi) + b_ref[...]


def _adaln(c, w, b):
    bsz, d = c.shape
    n = w.shape[1]
    tn = 2048
    return pl.pallas_call(
        _adaln_body,
        grid=(n // tn,),
        in_specs=[pl.BlockSpec((bsz, d), lambda j: (0, 0)),
                  pl.BlockSpec((d, tn), lambda j: (0, j)),
                  pl.BlockSpec((1, tn), lambda j: (0, j))],
        out_specs=pl.BlockSpec((bsz, tn), lambda j: (0, j)),
        out_shape=jax.ShapeDtypeStruct((bsz, n), F32),
        compiler_params=_cparams(("arbitrary",)),
        name="adaln",
    )(c, w, b.reshape(1, n))


def _inproj_body(x_ref, mod_ref, g1_ref, win_ref, bgate_ref, lng_ref, lnb_ref,
                 w1_ref, w3_ref, w2_ref, za_ref, u_ref, v_ref, gates_ref, w13_ref, w2b_ref):
    w = za_ref.shape[-1]
    tm = x_ref.shape[1]
    ff = w1_ref.shape[2]
    w13_ref[0, :, :ff] = w1_ref[0].astype(BF16)
    w13_ref[0, :, ff:] = w3_ref[0].astype(BF16)
    w2b_ref[0] = w2_ref[0].astype(BF16)
    for r0 in range(0, tm, INPROJ_SUBTILE):
        rs = slice(r0, r0 + INPROJ_SUBTILE)
        x = x_ref[0, rs, :]
        ms = jnp.mean(x * x, axis=-1, keepdims=True)
        xn = x * lax.rsqrt(ms + RMS_EPS) * g1_ref[...]
        xn = xn * (1.0 + mod_ref[0, 1:2, :]) + mod_ref[0, 0:1, :]
        xb = xn.astype(BF16)
        both = _dot(xb, win_ref[...])
        proj = both[:, :3 * w]
        za_ref[0, rs, :] = proj[:, :w]
        u_ref[0, rs, :] = _gelu(proj[:, w:2 * w]).astype(BF16)
        gv = _gelu(proj[:, 2 * w:])
        mu = jnp.mean(gv, axis=-1, keepdims=True)
        cen = gv - mu
        var = jnp.mean(cen * cen, axis=-1, keepdims=True)
        v_ref[0, rs, :] = (cen * lax.rsqrt(var + LN_EPS) * lng_ref[...] + lnb_ref[...]).astype(BF16)
        gates_ref[0, rs, :] = jax.nn.sigmoid(both[:, 3 * w:] + bgate_ref[...]).astype(BF16)


def _inproj(x, mod, g1, w_in, w_gate, b_gate, ln_g, ln_b, w1, w3, w2, tm):
    bsz, s, d = x.shape
    w = w_in.shape[1] // 3
    ng = w_gate.shape[1]
    n_exp, _, ff = w1.shape
    nsteps = s // tm
    parts = bsz * nsteps // n_exp
    assert bsz * nsteps == n_exp * parts and d % (parts * SUBLANES) == 0 and ff % (parts * SUBLANES) == 0
    tok = lambda n: pl.BlockSpec((1, tm, n), lambda b, i: (b, i, 0))
    full = lambda a: pl.BlockSpec(a.shape, lambda b, i: (0,) * a.ndim)
    wslice = lambda rows, cols: pl.BlockSpec(
        (1, rows // parts, cols), lambda b, i: ((b * nsteps + i) // parts, (b * nsteps + i) % parts, 0))
    w_both = jnp.concatenate([w_in, w_gate], axis=1).astype(BF16)
    args = (g1.reshape(1, d), w_both, b_gate.reshape(1, ng), ln_g.reshape(1, w), ln_b.reshape(1, w))
    return pl.pallas_call(
        _inproj_body,
        grid=(bsz, nsteps),
        in_specs=[tok(d), pl.BlockSpec((1, 6, d), lambda b, i: (b, 0, 0))] + [full(a) for a in args]
                 + [wslice(d, ff), wslice(d, ff), wslice(ff, d)],
        out_specs=[tok(w), tok(w), tok(w), tok(ng), wslice(d, 2 * ff), wslice(ff, d)],
        out_shape=[jax.ShapeDtypeStruct((bsz, s, w), F32),
                   jax.ShapeDtypeStruct((bsz, s, w), BF16),
                   jax.ShapeDtypeStruct((bsz, s, w), BF16),
                   jax.ShapeDtypeStruct((bsz, s, ng), BF16),
                   jax.ShapeDtypeStruct((n_exp, d, 2 * ff), BF16),
                   jax.ShapeDtypeStruct((n_exp, ff, d), BF16)],
        compiler_params=_cparams(("parallel", "parallel")),
        name="in_proj",
    )(x, mod, *args, w1, w3, w2)


def _s5_params(a_re, a_im, b_re, b_im, c_re, c_im, log_step):
    g, n = a_re.shape
    nblk = g // GROUPS_PER_BLOCK
    lam_re = jnp.minimum(a_re.astype(F32), -1e-4)
    lam_im = a_im.astype(F32)
    dt = jnp.exp(log_step.astype(F32))[:, None]
    mag = jnp.exp(lam_re * dt)
    ab_re = mag * jnp.cos(lam_im * dt)
    ab_im = mag * jnp.sin(lam_im * dt)
    den = lam_re * lam_re + lam_im * lam_im
    nr = ab_re - 1.0
    f_re = (nr * lam_re + ab_im * lam_im) / den
    f_im = (ab_im * lam_re - nr * lam_im) / den
    bt_re = b_re.astype(F32).transpose(0, 2, 1)
    bt_im = b_im.astype(F32).transpose(0, 2, 1)
    w_re = f_re[:, None, :] * bt_re - f_im[:, None, :] * bt_im
    w_im = f_re[:, None, :] * bt_im + f_im[:, None, :] * bt_re
    eye = jnp.eye(GROUPS_PER_BLOCK, dtype=F32)

    def bdiag_in(wt):
        t = wt.reshape(nblk, GROUPS_PER_BLOCK, SSM_GROUP_CH, 1, n) * eye[None, :, None, :, None]
        return t.reshape(nblk, GROUPS_PER_BLOCK * SSM_GROUP_CH, GROUPS_PER_BLOCK * n)

    def bdiag_out(ct):
        t = ct.transpose(0, 2, 1).reshape(nblk, GROUPS_PER_BLOCK, n, 1, SSM_GROUP_CH)
        t = t * eye[None, :, None, :, None]
        return t.reshape(nblk, GROUPS_PER_BLOCK * n, GROUPS_PER_BLOCK * SSM_GROUP_CH)

    w_in = jnp.concatenate([bdiag_in(w_re), bdiag_in(w_im)], axis=2).astype(BF16)
    w_out = jnp.concatenate([bdiag_out(c_re.astype(F32)), -bdiag_out(c_im.astype(F32))],
                            axis=1).astype(BF16)
    a_row = jnp.concatenate([ab_re.reshape(nblk, -1), ab_im.reshape(nblk, -1)], axis=1).reshape(1, -1)
    return a_row, w_in, w_out


def _s5_body(z_ref, a_ref, win_ref, wout_ref, d_ref, wglu_ref, bglu_ref, o_ref,
             zslab, zsb, xs, hstate, yslab):
    nb, ts, w = z_ref.shape
    rows = nb * ts
    nblk = w // LANES
    half = GROUPS_PER_BLOCK * SSM_STATE
    sw = 2 * half

    @pl.when(pl.program_id(1) == 0)
    def _():
        hstate[...] = jnp.zeros_like(hstate)

    for b in range(nb):
        for c in range(nblk):
            zslab[c, b * SLAB_PITCH:b * SLAB_PITCH + ts, :] = z_ref[b, :, c * LANES:(c + 1) * LANES]

    for s in range(ts):
        for c in range(nblk):
            zsb[s * nb:(s + 1) * nb, c * LANES:(c + 1) * LANES] = zslab.at[c][pl.ds(s, nb, stride=SLAB_PITCH), :]

    n_chunks = ts // SCAN_CHUNK
    crows = SCAN_CHUNK * nb
    units = [(j, q) for j in range(nblk) for q in range(n_chunks)]

    def in_map(j, q):
        rs = slice(q * crows, (q + 1) * crows)
        xs[rs, j * sw:(j + 1) * sw] = _dot(zsb[rs, j * LANES:(j + 1) * LANES].astype(BF16), win_ref[j])

    carry = {}

    def scan(j, q):
        re_sl = slice(j * sw, j * sw + half)
        im_sl = slice(j * sw + half, (j + 1) * sw)
        a_r = jnp.broadcast_to(a_ref[:, re_sl], (nb, half))
        a_i = jnp.broadcast_to(a_ref[:, im_sl], (nb, half))
        h_r, h_i = carry[j] if q else (hstate[:, re_sl], hstate[:, im_sl])
        for s in range(q * SCAN_CHUNK, (q + 1) * SCAN_CHUNK):
            x_r = xs[s * nb:(s + 1) * nb, re_sl]
            x_i = xs[s * nb:(s + 1) * nb, im_sl]
            h_r, h_i = a_r * h_r - a_i * h_i + x_r, a_r * h_i + a_i * h_r + x_i
            xs[s * nb:(s + 1) * nb, re_sl] = h_r
            xs[s * nb:(s + 1) * nb, im_sl] = h_i
        carry[j] = (h_r, h_i)
        if q == n_chunks - 1:
            hstate[:, re_sl] = h_r
            hstate[:, im_sl] = h_i

    gel = {}

    def out_map(j, q):
        rs = slice(q * crows, (q + 1) * crows)
        y = _dot(xs[rs, j * sw:(j + 1) * sw].astype(BF16), wout_ref[j])
        y = y + d_ref[:, j * LANES:(j + 1) * LANES] * zsb[rs, j * LANES:(j + 1) * LANES]
        gel[j, q] = _gelu(y)

    for k in range(len(units) + 2):
        if k < len(units):
            in_map(*units[k])
        if 0 <= k - 1 < len(units):
            scan(*units[k - 1])
        if 0 <= k - 2 < len(units):
            out_map(*units[k - 2])

    for q in range(n_chunks):
        g = jnp.concatenate([gel[j, q] for j in range(nblk)], axis=1)
        g = g * jax.nn.sigmoid(_dot(g.astype(BF16), wglu_ref[...]) + bglu_ref[...])
        for c in range(nblk):
            yslab[c, q * crows:(q + 1) * crows, :] = g[:, c * LANES:(c + 1) * LANES]

    for b in range(nb):
        for c in range(nblk):
            o_ref[b, :, c * LANES:(c + 1) * LANES] = yslab.at[c][pl.ds(b, ts, stride=nb), :].astype(BF16)


def _s5(za, a_row, w_in, w_out, d_skip, w_glu, b_glu):
    bsz, s, w = za.shape
    nb, ts = SCAN_BATCHES, SCAN_TS
    rows = nb * ts
    nblk = w // LANES
    nstate = a_row.shape[1]
    full = lambda a: pl.BlockSpec(a.shape, lambda b, i: (0,) * a.ndim)
    args = (a_row, w_in, w_out, d_skip.reshape(1, w).astype(F32), w_glu.astype(BF16), b_glu.reshape(1, w))
    return pl.pallas_call(
        _s5_body,
        grid=(bsz // nb, s // ts),
        in_specs=[pl.BlockSpec((nb, ts, w), lambda b, i: (b, i, 0))] + [full(a) for a in args],
        out_specs=pl.BlockSpec((nb, ts, w), lambda b, i: (b, i, 0)),
        out_shape=jax.ShapeDtypeStruct((bsz, s, w), BF16),
        scratch_shapes=[pltpu.VMEM((nblk, nb * SLAB_PITCH, LANES), F32),
                        pltpu.VMEM((rows, w), F32),
                        pltpu.VMEM((rows, nstate), F32),
                        pltpu.VMEM((nb, nstate), F32),
                        pltpu.VMEM((nblk, rows, LANES), F32)],
        compiler_params=_cparams(("parallel", "arbitrary")),
        name="s5",
    )(za, *args)


def _mix_body(x_ref, u_ref, v_ref, ya_ref, gates_ref, mod_ref, wp_ref, sb_ref, wba_ref, wbb_ref,
              wout_ref, g2_ref, wr_ref, br_ref, h1_ref, xp_ref, route_ref):
    tm_full = x_ref.shape[1]
    npair = u_ref.shape[2] // LANES
    hd = LANES // 2

    t_idx = lax.broadcasted_iota(I32, (SGU_CHUNK, 2 * SGU_CHUNK), 0)
    s_idx = lax.broadcasted_iota(I32, (SGU_CHUNK, 2 * SGU_CHUNK), 1) & (SGU_CHUNK - 1)
    causal = s_idx <= t_idx
    lane = lax.broadcasted_iota(I32, (SGU_CHUNK, LANES), 1)
    first_head = lane < hd
    w_causal = [jnp.where(causal, wp_ref[j], jnp.zeros_like(wp_ref[j])) for j in range(npair)]
    for r0 in range(0, tm_full, MIX_SUBTILE):
        _mix_rows(slice(r0, r0 + MIX_SUBTILE), w_causal, first_head, x_ref, u_ref, v_ref, ya_ref, gates_ref,
                  mod_ref, sb_ref, wba_ref, wbb_ref, wout_ref, g2_ref, wr_ref, br_ref, h1_ref, xp_ref,
                  route_ref)


def _mix_rows(rs, w_causal, first_head, x_ref, u_ref, v_ref, ya_ref, gates_ref, mod_ref, sb_ref, wba_ref,
              wbb_ref, wout_ref, g2_ref, wr_ref, br_ref, h1_ref, xp_ref, route_ref):
    tm = rs.stop - rs.start
    d = x_ref.shape[2]
    npair = len(w_causal)
    mixed_rows = []
    for c in range(rs.start, rs.stop, SGU_CHUNK):
        vc = v_ref[0, c:c + SGU_CHUNK, :]
        blocks = []
        for j in range(npair):
            vb = vc[:, j * LANES:(j + 1) * LANES]
            zero = jnp.zeros_like(vb)
            rhs = jnp.concatenate([jnp.where(first_head, vb, zero), jnp.where(first_head, zero, vb)], axis=0)
            blocks.append(_dot(w_causal[j], rhs))
        mixed_rows.append(jnp.concatenate(blocks, axis=1) + sb_ref[...])
    mixed = jnp.concatenate(mixed_rows, axis=0)
    yb = (u_ref[0, rs, :].astype(F32) * mixed).astype(BF16)

    pa = _dot(ya_ref[0, rs, :], wba_ref[...])
    pb = _dot(yb, wbb_ref[...])
    merged = gates_ref[0, rs, :d].astype(F32) * pa + gates_ref[0, rs, d:].astype(F32) * pb
    o = _dot(merged.astype(BF16), wout_ref[...])
    h1 = x_ref[0, rs, :] + mod_ref[0, 2:3, :] * o
    h1_ref[0, rs, :] = h1

    ms = jnp.mean(h1 * h1, axis=-1, keepdims=True)
    xn = h1 * lax.rsqrt(ms + RMS_EPS) * g2_ref[...]
    xn = xn * (1.0 + mod_ref[0, 4:5, :]) + mod_ref[0, 3:4, :]
    x_hi = xn.astype(BF16)

    logits = _dot(x_hi, wr_ref[...]) + br_ref[...]
    lt = logits.T
    best = lt[0:1, :]
    grp = jnp.zeros((1, tm), I32)
    for gi in range(1, N_GROUPS):
        better = lt[gi:gi + 1, :] > best
        grp = jnp.where(better, gi, grp)
        best = jnp.where(better, lt[gi:gi + 1, :], best)
    den = jnp.zeros((1, tm), F32)
    for gi in range(N_GROUPS):
        den = den + jnp.exp(lt[gi:gi + 1, :] - best)
    pg = 1.0 / den
    le = lt[SUBLANES:2 * SUBLANES, :]
    for gi in range(1, N_GROUPS):
        le = jnp.where(grp == gi, lt[SUBLANES * (gi + 1):SUBLANES * (gi + 2), :], le)
    eidx = lax.broadcasted_iota(I32, (EXPERTS_PER_GROUP, tm), 0).astype(F32)
    none = float(EXPERTS_PER_GROUP)
    v1 = jnp.max(le, axis=0, keepdims=True)
    i1 = jnp.min(jnp.where(le == v1, eidx, none), axis=0, keepdims=True)
    rest = jnp.where(eidx == i1, -jnp.inf, le)
    v2 = jnp.max(rest, axis=0, keepdims=True)
    i2 = jnp.min(jnp.where(rest == v2, eidx, none), axis=0, keepdims=True)
    ex = jnp.exp(v2 - v1)
    p1 = 1.0 / (1.0 + ex)
    wt1 = pg * p1
    wt2 = pg * (ex * p1)
    first_lo = i1 < i2
    lo = jnp.where(first_lo, i1, i2)
    hi = jnp.where(first_lo, i2, i1)
    pair = lo * (2 * EXPERTS_PER_GROUP - 1.0 - lo) * 0.5 + (hi - lo - 1.0)
    cls = grp.astype(F32) * PAIRS_PER_GROUP + pair
    info = jnp.concatenate([cls, jnp.where(first_lo, wt1, wt2), jnp.where(first_lo, wt2, wt1)], axis=0)
    route_ref[:, rs] = jnp.concatenate([info, jnp.zeros((SUBLANES - 3, tm), F32)], axis=0)

    bits = lax.bitcast_convert_type(x_hi.astype(F32), I32)
    packed = lax.shift_right_logical(bits[:, :d // 2], 16) | (bits[:, d // 2:] & jnp.int32(-65536))
    info_cols = jnp.concatenate([info, jnp.zeros((LANES - 3, tm), F32)], axis=0).T
    xp_ref[0, rs, :] = jnp.concatenate([packed, lax.bitcast_convert_type(info_cols, I32)], axis=1)


def _mix(x, u, v, ya, gates, mod, sgu_w, sgu_b, w_ba, w_bb, w_out, g2, w_rg, b_rg, w_re, b_re, tm):
    bsz, s, d = x.shape
    w = u.shape[2]
    nt = bsz * s
    wp = sgu_w.reshape(SGU_HEADS // 2, 2, SGU_CHUNK, SGU_CHUNK).transpose(0, 2, 1, 3)
    wp = wp.reshape(SGU_HEADS // 2, SGU_CHUNK, 2 * SGU_CHUNK).astype(BF16)
    sb = jnp.repeat(sgu_b.T, w // SGU_HEADS, axis=1).astype(F32)
    wr = jnp.zeros((d, LANES), F32)
    wr = wr.at[:, :N_GROUPS].set(w_rg)
    wr = wr.at[:, SUBLANES:SUBLANES + N_GROUPS * EXPERTS_PER_GROUP].set(
        w_re.transpose(1, 0, 2).reshape(d, N_GROUPS * EXPERTS_PER_GROUP))
    br = jnp.zeros((1, LANES), F32)
    br = br.at[0, :N_GROUPS].set(b_rg)
    br = br.at[0, SUBLANES:SUBLANES + N_GROUPS * EXPERTS_PER_GROUP].set(b_re.reshape(-1))
    tok = lambda n: pl.BlockSpec((1, tm, n), lambda b, i: (b, i, 0))
    full = lambda a: pl.BlockSpec(a.shape, lambda b, i: (0,) * a.ndim)
    args = (wp, sb, w_ba.astype(BF16), w_bb.astype(BF16), w_out.astype(BF16), g2.reshape(1, d),
            wr.astype(BF16), br)
    nsteps = s // tm
    return pl.pallas_call(
        _mix_body,
        grid=(bsz, nsteps),
        in_specs=[tok(d), tok(w), tok(w), tok(w), tok(2 * d),
                  pl.BlockSpec((1, 6, d), lambda b, i: (b, 0, 0))] + [full(a) for a in args],
        out_specs=[tok(d), tok(d // 2 + LANES),
                   pl.BlockSpec((SUBLANES, tm), lambda b, i: (0, b * nsteps + i))],
        out_shape=[jax.ShapeDtypeStruct((bsz, s, d), F32),
                   jax.ShapeDtypeStruct((bsz, s, d // 2 + LANES), I32),
                   jax.ShapeDtypeStruct((SUBLANES, nt), F32)],
        compiler_params=_cparams(("parallel", "parallel")),
        name="mix",
    )(x, u, v, ya, gates, mod, *args)


def _tile_lanes(a, reps):
    return jnp.concatenate([a] * reps, axis=1)


def _dispatch_body(route_ref, lo_ref, hi_ref, pos_ref, blk_ref, cls_ref, run, pstart, ranks, classes,
                   earlier):
    i = pl.program_id(0)
    n = pl.num_programs(0)
    tt = route_ref.shape[1]
    reps = tt // LANES
    cid = lax.broadcasted_iota(I32, (LANES, tt), 0).astype(F32)

    sub = earlier.shape[0]

    @pl.when(i == 0)
    def _():
        run[...] = jnp.zeros_like(run)
        earlier_tok = lax.broadcasted_iota(I32, (sub, sub), 0) < lax.broadcasted_iota(I32, (sub, sub), 1)
        earlier[...] = jnp.where(earlier_tok, 1.0, 0.0).astype(BF16)

    cls = route_ref[0:1, :]
    classes[i] = cls
    onehot = jnp.where(cls == cid, 1.0, 0.0)
    seen = run[...]
    before = []
    for k in range(tt // sub):
        oh = onehot[:, k * sub:(k + 1) * sub]
        before.append(_dot(oh.astype(BF16), earlier[...]) + _tile_lanes(seen, sub // LANES))
        seen = seen + jnp.sum(oh, axis=1, keepdims=True)
    ranks[i] = jnp.sum(onehot * jnp.concatenate(before, axis=1), axis=0, keepdims=True)
    run[...] = seen

    @pl.when(i == n - 1)
    def _():
        counts = run[...]
        nblk = jnp.floor((counts + (EXPERT_ROWS - 1.0)) * (1.0 / EXPERT_ROWS))
        hi_part = jnp.floor(nblk * (1.0 / 16.0))
        lo_part = nblk - 16.0 * hi_part
        upto = lax.broadcasted_iota(I32, (LANES, LANES), 1) <= lax.broadcasted_iota(I32, (LANES, LANES), 0)
        upto = jnp.where(upto, 1.0, 0.0).astype(BF16)
        ends = 16.0 * _dot(upto, hi_part.astype(BF16)) + _dot(upto, lo_part.astype(BF16))
        pstart[...] = (ends - nblk) * EXPERT_ROWS
        diag = lax.broadcasted_iota(I32, (LANES, LANES), 0) == lax.broadcasted_iota(I32, (LANES, LANES), 1)
        on_lanes = lambda a: jnp.sum(jnp.where(diag, a, 0.0), axis=0, keepdims=True)
        cls_ref[...] = jnp.concatenate([on_lanes(counts), on_lanes(pstart[...]),
                                        jnp.zeros((SUBLANES - 2, LANES), F32)], axis=0)
        nb_lanes = blk_ref.shape[1]
        breps = nb_lanes // LANES
        n_used = _tile_lanes(ends[LANES - 1:LANES, :], breps)
        b_idx = lax.broadcasted_iota(I32, (LANES, nb_lanes), 1).astype(F32)
        b_idx = jnp.minimum(b_idx, n_used - 1.0)
        b_cls = jnp.sum(jnp.where(_tile_lanes(ends, breps) <= b_idx, 1.0, 0.0), axis=0, keepdims=True)
        b_member = b_cls == lax.broadcasted_iota(I32, (LANES, nb_lanes), 0).astype(F32)
        pick = lambda tab: jnp.sum(jnp.where(b_member, _tile_lanes(tab, breps), 0.0), axis=0, keepdims=True)
        blk_ref[...] = jnp.concatenate([pick(lo_ref[...]), pick(hi_ref[...]), n_used,
                                        jnp.zeros((SUBLANES - 3, nb_lanes), F32)], axis=0)
        starts = _tile_lanes(pstart[...], reps)

        def tile_positions(k, carry):
            base = jnp.sum(jnp.where(classes[k] == cid, starts, 0.0), axis=0, keepdims=True)
            pos_ref[k] = (base + ranks[k]).astype(I32)
            return carry

        lax.fori_loop(0, n, tile_positions, 0)


def _dispatch(route, n_blocks, tt=2048):
    nt = route.shape[1]
    n = nt // tt
    nb_lanes = pl.cdiv(n_blocks, LANES) * LANES
    lo_tab, hi_tab = [], []
    for g in range(N_GROUPS):
        for a in range(EXPERTS_PER_GROUP):
            for b in range(a + 1, EXPERTS_PER_GROUP):
                lo_tab.append(g * EXPERTS_PER_GROUP + a)
                hi_tab.append(g * EXPERTS_PER_GROUP + b)
    pad = [0] * (LANES - N_CLASSES)
    lo_tile = jnp.broadcast_to(jnp.asarray(lo_tab + pad, F32)[:, None], (LANES, LANES))
    hi_tile = jnp.broadcast_to(jnp.asarray(hi_tab + pad, F32)[:, None], (LANES, LANES))
    const = lambda shape: pl.BlockSpec(shape, lambda i: (0,) * len(shape))
    pos, blk_tab, cls_tab = pl.pallas_call(
        _dispatch_body,
        grid=(n,),
        in_specs=[pl.BlockSpec((SUBLANES, tt), lambda i: (0, i)),
                  const((LANES, LANES)), const((LANES, LANES))],
        out_specs=[const((n, 1, tt)), const((SUBLANES, nb_lanes)), const((SUBLANES, LANES))],
        out_shape=[jax.ShapeDtypeStruct((n, 1, tt), I32),
                   jax.ShapeDtypeStruct((SUBLANES, nb_lanes), F32),
                   jax.ShapeDtypeStruct((SUBLANES, LANES), F32)],
        scratch_shapes=[pltpu.VMEM((LANES, LANES), F32), pltpu.VMEM((LANES, LANES), F32),
                        pltpu.VMEM((n, 1, tt), F32), pltpu.VMEM((n, 1, tt), F32),
                        pltpu.VMEM((DISPATCH_SUB, DISPATCH_SUB), BF16)],
        compiler_params=_cparams(("arbitrary",)),
        name="dispatch",
    )(route, lo_tile, hi_tile)
    blk = blk_tab.astype(I32)
    cls = cls_tab.astype(I32)
    return pos.reshape(nt), blk[0, :n_blocks], blk[1, :n_blocks], blk[2, :1], cls[0], cls[1]


ZERO_FILL_ROWS = tuple(1 << k for k in range(EXPERT_ROWS.bit_length() - 1))


def _tile_rows_wait(buf, sem):
    pltpu.make_async_copy(buf, buf, sem).wait()


def _scatter_body(cnt_ref, pst_ref, pos_ref, xp_ref, xs_hbm, ring, zeros, sem, zsem):
    i = pl.program_id(0)
    n = pl.num_programs(0)
    tm, wrow = xp_ref.shape
    half = tm // 2
    n_tiles = wrow // LANES

    @pl.when(i == 0)
    def _():
        ring[...] = jnp.zeros_like(ring)

    for h in range(2):
        @pl.when(i > 0)
        def _():
            _tile_rows_wait(ring.at[h], sem.at[h])

        for c in range(n_tiles):
            ring.at[h][pl.ds(c, half, stride=SUBLANES), :] = xp_ref[h * half:(h + 1) * half,
                                                                    c * LANES:(c + 1) * LANES]
        for r in range(half):
            pltpu.make_async_copy(ring.at[h, pl.ds(r * SUBLANES, SUBLANES)],
                                  xs_hbm.at[pos_ref[0, 0, h * half + r]], sem.at[h]).start(priority=r % 2)

    @pl.when(i == n - 1)
    def _():
        _tile_rows_wait(ring.at[0], sem.at[0])
        _tile_rows_wait(ring.at[1], sem.at[1])
        zeros[...] = jnp.zeros_like(zeros)

        def fill(wait):
            def per_class(k, carry):
                cnt = cnt_ref[k]
                n_pad = (-cnt) & (EXPERT_ROWS - 1)
                off = pst_ref[k] + cnt
                for size in ZERO_FILL_ROWS:
                    piece = pltpu.make_async_copy(zeros.at[pl.ds(0, size)], xs_hbm.at[pl.ds(off, size)], zsem)

                    @pl.when((n_pad & size) != 0)
                    def _():
                        piece.wait() if wait else piece.start()

                    off = off + (n_pad & size)
                return carry

            lax.fori_loop(0, N_CLASSES, per_class, 0)

            last = N_CLASSES - 1
            used_rows = pst_ref[last] + cnt_ref[last] + ((-cnt_ref[last]) & (EXPERT_ROWS - 1))
            tail = ZERO_FILL_ROWS[-1]

            def per_piece(j, carry):
                piece = pltpu.make_async_copy(zeros, xs_hbm.at[pl.ds(used_rows + j * tail, tail)], zsem)
                piece.wait() if wait else piece.start()
                return carry

            lax.fori_loop(0, (xs_hbm.shape[0] - used_rows) // tail, per_piece, 0)

        fill(False)
        fill(True)


def _scatter_rows(xp, pos, counts, pstarts, n_blocks, tm=1024):
    nt, wrow = xp.shape
    n = nt // tm
    grid_spec = pltpu.PrefetchScalarGridSpec(
        num_scalar_prefetch=2,
        grid=(n,),
        in_specs=[pl.BlockSpec((1, 1, tm), lambda i, c, p: (i, 0, 0), memory_space=pltpu.SMEM),
                  pl.BlockSpec((tm, wrow), lambda i, c, p: (i, 0))],
        out_specs=pl.BlockSpec(memory_space=pl.ANY),
        scratch_shapes=[pltpu.VMEM((2, tm // 2 * SUBLANES, LANES), I32),
                        pltpu.VMEM((ZERO_FILL_ROWS[-1], SUBLANES, LANES), I32),
                        pltpu.SemaphoreType.DMA((2,)), pltpu.SemaphoreType.DMA(())],
    )
    return pl.pallas_call(
        _scatter_body,
        grid_spec=grid_spec,
        out_shape=jax.ShapeDtypeStruct((n_blocks * EXPERT_ROWS, SUBLANES, LANES), I32),
        compiler_params=_cparams(("arbitrary",)),
        name="scatter_rows",
    )(counts, pstarts, pos.reshape(n, 1, tm), xp)


def _unpack_rows(xp):
    left = lax.bitcast_convert_type(lax.shift_left(xp, 16), F32).astype(BF16)
    right = lax.bitcast_convert_type(xp & jnp.int32(-65536), F32).astype(BF16)
    return jnp.concatenate([left, right], axis=1)


def _experts_body(elo_ref, ehi_ref, nu_ref, xs_ref, w13a_ref, w2a_ref, w13b_ref, w2b_ref, o_ref):
    i = pl.program_id(0)
    rows = xs_ref.shape[0] // SUBLANES
    ff, d = w2a_ref.shape[1], w2a_ref.shape[2]
    n_packed = d // 2 // LANES
    lane_tile = lambda ref, c: ref[pl.ds(c, rows, stride=SUBLANES), :]

    @pl.when(i < nu_ref[0])
    def _():
        x = _unpack_rows(jnp.concatenate([lane_tile(xs_ref, c) for c in range(n_packed)], axis=1))
        info = lax.bitcast_convert_type(lane_tile(xs_ref, n_packed), F32)

        def ffn(w13_ref, w2_ref, weight):
            ab = _dot(x, w13_ref[0])
            a = ab[:, :ff]
            hid = (a * jax.nn.sigmoid(a)) * ab[:, ff:] * weight
            return _dot(hid.astype(BF16), w2_ref[0])

        y = ffn(w13a_ref, w2a_ref, info[:, 1:2]) + ffn(w13b_ref, w2b_ref, info[:, 2:3])
        for c in range(d // LANES):
            o_ref[pl.ds(c, rows, stride=SUBLANES), :] = y[:, c * LANES:(c + 1) * LANES]

    @pl.when(i >= nu_ref[0])
    def _():
        o_ref[...] = jnp.zeros_like(o_ref)


def _experts(xs, e_lo, e_hi, n_used, w13, w2):
    n_rows = xs.shape[0]
    n_blocks = n_rows // EXPERT_ROWS
    n_exp, ff, d = w2.shape
    assert d == SUBLANES * LANES
    blk = EXPERT_ROWS * SUBLANES
    pick = lambda a, use_hi: pl.BlockSpec(
        (1,) + a.shape[1:],
        (lambda i, lo, hi, nu: (hi[i], 0, 0)) if use_hi else (lambda i, lo, hi, nu: (lo[i], 0, 0)))
    grid_spec = pltpu.PrefetchScalarGridSpec(
        num_scalar_prefetch=3,
        grid=(n_blocks,),
        in_specs=[pl.BlockSpec((blk, LANES), lambda i, lo, hi, nu: (jnp.minimum(i, nu[0] - 1), 0)),
                  pick(w13, False), pick(w2, False), pick(w13, True), pick(w2, True)],
        out_specs=pl.BlockSpec((blk, LANES), lambda i, lo, hi, nu: (i, 0)),
    )
    y = pl.pallas_call(
        _experts_body,
        grid_spec=grid_spec,
        out_shape=jax.ShapeDtypeStruct((n_rows * SUBLANES, LANES), F32),
        compiler_params=_cparams(("arbitrary",)),
        name="experts",
    )(e_lo, e_hi, n_used, xs.reshape(n_rows * SUBLANES, LANES), w13, w2, w13, w2)
    return y.reshape(n_rows, SUBLANES, LANES)


def _final_body(posc_ref, posn_ref, y_hbm, h1_ref, mod_ref, gf_ref, o_ref, ybuf, sem):
    i = pl.program_id(0)
    n = pl.num_programs(0)
    tm = h1_ref.shape[0]
    half = tm // 2

    def fetch(idx_ref, h):
        for r in range(half):
            pltpu.make_async_copy(y_hbm.at[idx_ref[0, 0, h * half + r]],
                                  ybuf.at[h, pl.ds(r * SUBLANES, SUBLANES)], sem.at[h]).start(priority=r % 2)

    @pl.when(i == 0)
    def _():
        fetch(posc_ref, 0)
        fetch(posc_ref, 1)

    for h in range(2):
        rows = slice(h * half, (h + 1) * half)
        _tile_rows_wait(ybuf.at[h], sem.at[h])
        y = jnp.concatenate([ybuf.at[h][pl.ds(c, half, stride=SUBLANES), :] for c in range(SUBLANES)], axis=1)
        hres = h1_ref[rows, :] + mod_ref[0, 5:6, :] * y
        ms = jnp.mean(hres * hres, axis=-1, keepdims=True)
        o_ref[rows, :] = hres * lax.rsqrt(ms + RMS_EPS) * gf_ref[...]
        fetch(posn_ref, h)

    @pl.when(i == n - 1)
    def _():
        _tile_rows_wait(ybuf.at[0], sem.at[0])
        _tile_rows_wait(ybuf.at[1], sem.at[1])


def _final(pos, y_sorted, h1, mod, gf, seq, tm):
    nt, d = h1.shape
    n = nt // tm
    pos3 = pos.reshape(n, 1, tm)
    idx_spec = lambda f: pl.BlockSpec((1, 1, tm), f, memory_space=pltpu.SMEM)
    return pl.pallas_call(
        _final_body,
        grid=(n,),
        in_specs=[idx_spec(lambda i: (i, 0, 0)),
                  idx_spec(lambda i: (jnp.minimum(i + 1, n - 1), 0, 0)),
                  pl.BlockSpec(memory_space=pl.ANY),
                  pl.BlockSpec((tm, d), lambda i: (i, 0)),
                  pl.BlockSpec((1, 6, d), lambda i: (i * tm // seq, 0, 0)),
                  pl.BlockSpec((1, d), lambda i: (0, 0))],
        out_specs=pl.BlockSpec((tm, d), lambda i: (i, 0)),
        out_shape=jax.ShapeDtypeStruct((nt, d), F32),
        scratch_shapes=[pltpu.VMEM((2, tm // 2 * SUBLANES, LANES), F32), pltpu.SemaphoreType.DMA((2,))],
        compiler_params=_cparams(("arbitrary",)),
        name="final",
    )(pos3, pos3, y_sorted, h1, mod, gf.reshape(1, d))


def kernel(x, c, w_ada, b_ada, norm1_g, w_in, w_gate, b_gate, ssm_a_re, ssm_a_im, ssm_b_re, ssm_b_im, ssm_c_re, ssm_c_im, ssm_d, ssm_log_step, w_glu, b_glu, sgu_ln_g, sgu_ln_b, sgu_w, sgu_b, w_branch_a, w_branch_b, w_out, norm2_g, w_router_group, b_router_group, w_router_expert, b_router_expert, w1, w3, w2, norm_f_g):
    bsz, seq, d = x.shape
    depth = w_ada.shape[0]
    assert depth == 1 and bsz % SCAN_BATCHES == 0 and seq % 512 == 0
    l = 0
    mod = _adaln(c, w_ada[l], b_ada[l]).reshape(bsz, 6, d)
    za, u, v, gates, w13_bf, w2_bf = _inproj(x, mod, norm1_g[l], w_in[l], w_gate[l], b_gate[l],
                                             sgu_ln_g[l], sgu_ln_b[l], w1[l], w3[l], w2[l], tm=1024)
    a_row, s5_in, s5_out = _s5_params(ssm_a_re[l], ssm_a_im[l], ssm_b_re[l], ssm_b_im[l],
                                      ssm_c_re[l], ssm_c_im[l], ssm_log_step[l])
    ya = _s5(za, a_row, s5_in, s5_out, ssm_d[l], w_glu[l], b_glu[l])
    h1, xp, route = _mix(x, u, v, ya, gates, mod, sgu_w[l], sgu_b[l], w_branch_a[l], w_branch_b[l],
                         w_out[l], norm2_g[l], w_router_group[l], b_router_group[l],
                         w_router_expert[l], b_router_expert[l], tm=1024)
    nt = bsz * seq
    n_blocks = nt // EXPERT_ROWS + N_CLASSES
    pos, e_lo, e_hi, n_used, counts, pstarts = _dispatch(route, n_blocks)
    xs = _scatter_rows(xp.reshape(nt, xp.shape[-1]), pos, counts, pstarts, n_blocks)
    y_sorted = _experts(xs, e_lo, e_hi, n_used, w13_bf, w2_bf)
    out = _final(pos, y_sorted, h1.reshape(nt, d), mod, norm_f_g, seq, tm=1024)
    return out.reshape(bsz, seq, d)
```

```python
import math

import jax
import jax.numpy as jnp
from jax import lax
from jax.experimental import pallas as pl
from jax.experimental.pallas import tpu as pltpu

F32 = jnp.float32
BF16 = jnp.bfloat16
I32 = jnp.int32

LANES = 128
SUBLANES = 8
VMEM_LIMIT = 56 * 1024 * 1024

RMS_EPS = 1e-6
LN_EPS = 1e-5

SSM_GROUP_CH = 16
SSM_STATE = 64
GROUPS_PER_BLOCK = LANES // SSM_GROUP_CH
SGU_HEADS = 8
SGU_CHUNK = 128
N_GROUPS = 4
EXPERTS_PER_GROUP = 8
PAIRS_PER_GROUP = EXPERTS_PER_GROUP * (EXPERTS_PER_GROUP - 1) // 2
N_CLASSES = N_GROUPS * PAIRS_PER_GROUP
EXPERT_ROWS = 256

SCAN_BATCHES = SUBLANES
SCAN_TS = 128
SLAB_PITCH = SCAN_TS + 8
SCAN_CHUNK = 64
INPROJ_SUBTILE = 256
MIX_SUBTILE = 512
DISPATCH_SUB = 256


def _gelu(x):
    return 0.5 * x * (1.0 + jnp.tanh(math.sqrt(2.0 / math.pi) * (x + 0.044715 * (x * x * x))))


def _dot(a, b):
    return jnp.dot(a, b, preferred_element_type=F32)


def _split_bf16(a):
    hi = a.astype(BF16)
    lo = (a - hi.astype(F32)).astype(BF16)
    return hi, lo


def _cparams(sem):
    return pltpu.CompilerParams(dimension_semantics=sem, vmem_limit_bytes=VMEM_LIMIT)


def _adaln_body(c_ref, w_ref, b_ref, o_ref):
    c = c_ref[...]
    act = c * jax.nn.sigmoid(c)
    a_hi, a_lo = _split_bf16(act)
    w_hi, w_lo = _split_bf16(w_ref[...])
    o_ref[...] = _dot(a_hi, w_hi) + _dot(a_hi, w_lo) + _dot(a_lo, w_hi) + b_ref[...]


def _adaln(c, w, b):
    bsz, d = c.shape
    n = w.shape[1]
    tn = 1024
    return pl.pallas_call(
        _adaln_body,
        grid=(n // tn,),
        in_specs=[pl.BlockSpec((bsz, d), lambda j: (0, 0)),
                  pl.BlockSpec((d, tn), lambda j: (0, j)),
                  pl.BlockSpec((1, tn), lambda j: (0, j))],
        out_specs=pl.BlockSpec((bsz, tn), lambda j: (0, j)),
        out_shape=jax.ShapeDtypeStruct((bsz, n), F32),
        compiler_params=_cparams(("arbitrary",)),
        name="adaln",
    )(c, w, b.reshape(1, n))


def _inproj_body(x_ref, mod_ref, g1_ref, win_ref, bgate_ref, lng_ref, lnb_ref,
                 w1_ref, w3_ref, w2_ref, za_ref, u_ref, v_ref, gates_ref, w13_ref, w2b_ref):
    w = za_ref.shape[-1]
    tm = x_ref.shape[1]
    ff = w1_ref.shape[2]
    w13_ref[0, :, :ff] = w1_ref[0].astype(BF16)
    w13_ref[0, :, ff:] = w3_ref[0].astype(BF16)
    w2b_ref[0] = w2_ref[0].astype(BF16)
    for r0 in range(0, tm, INPROJ_SUBTILE):
        rs = slice(r0, r0 + INPROJ_SUBTILE)
        x = x_ref[0, rs, :]
        ms = jnp.mean(x * x, axis=-1, keepdims=True)
        xn = x * lax.rsqrt(ms + RMS_EPS) * g1_ref[...]
        xn = xn * (1.0 + mod_ref[0, 1:2, :]) + mod_ref[0, 0:1, :]
        xb = xn.astype(BF16)
        both = _dot(xb, win_ref[...])
        proj = both[:, :3 * w]
        za_ref[0, rs, :] = proj[:, :w]
        u_ref[0, rs, :] = _gelu(proj[:, w:2 * w]).astype(BF16)
        gv = _gelu(proj[:, 2 * w:])
        mu = jnp.mean(gv, axis=-1, keepdims=True)
        cen = gv - mu
        var = jnp.mean(cen * cen, axis=-1, keepdims=True)
        v_ref[0, rs, :] = (cen * lax.rsqrt(var + LN_EPS) * lng_ref[...] + lnb_ref[...]).astype(BF16)
        gates_ref[0, rs, :] = jax.nn.sigmoid(both[:, 3 * w:] + bgate_ref[...]).astype(BF16)


def _inproj(x, mod, g1, w_in, w_gate, b_gate, ln_g, ln_b, w1, w3, w2, tm):
    bsz, s, d = x.shape
    w = w_in.shape[1] // 3
    ng = w_gate.shape[1]
    n_exp, _, ff = w1.shape
    nsteps = s // tm
    parts = bsz * nsteps // n_exp
    assert bsz * nsteps == n_exp * parts and d % (parts * SUBLANES) == 0 and ff % (parts * SUBLANES) == 0
    tok = lambda n: pl.BlockSpec((1, tm, n), lambda b, i: (b, i, 0))
    full = lambda a: pl.BlockSpec(a.shape, lambda b, i: (0,) * a.ndim)
    wslice = lambda rows, cols: pl.BlockSpec(
        (1, rows // parts, cols), lambda b, i: ((b * nsteps + i) // parts, (b * nsteps + i) % parts, 0))
    w_both = jnp.concatenate([w_in, w_gate], axis=1).astype(BF16)
    args = (g1.reshape(1, d), w_both, b_gate.reshape(1, ng), ln_g.reshape(1, w), ln_b.reshape(1, w))
    return pl.pallas_call(
        _inproj_body,
        grid=(bsz, nsteps),
        in_specs=[tok(d), pl.BlockSpec((1, 6, d), lambda b, i: (b, 0, 0))] + [full(a) for a in args]
                 + [wslice(d, ff), wslice(d, ff), wslice(ff, d)],
        out_specs=[tok(w), tok(w), tok(w), tok(ng), wslice(d, 2 * ff), wslice(ff, d)],
        out_shape=[jax.ShapeDtypeStruct((bsz, s, w), F32),
                   jax.ShapeDtypeStruct((bsz, s, w), BF16),
                   jax.ShapeDtypeStruct((bsz, s, w), BF16),
                   jax.ShapeDtypeStruct((bsz, s, ng), BF16),
                   jax.ShapeDtypeStruct((n_exp, d, 2 * ff), BF16),
                   jax.ShapeDtypeStruct((n_exp, ff, d), BF16)],
        compiler_params=_cparams(("parallel", "parallel")),
        name="in_proj",
    )(x, mod, *args, w1, w3, w2)


def _s5_params(a_re, a_im, b_re, b_im, c_re, c_im, log_step):
    g, n = a_re.shape
    nblk = g // GROUPS_PER_BLOCK
    lam_re = jnp.minimum(a_re.astype(F32), -1e-4)
    lam_im = a_im.astype(F32)
    dt = jnp.exp(log_step.astype(F32))[:, None]
    mag = jnp.exp(lam_re * dt)
    ab_re = mag * jnp.cos(lam_im * dt)
    ab_im = mag * jnp.sin(lam_im * dt)
    den = lam_re * lam_re + lam_im * lam_im
    nr = ab_re - 1.0
    f_re = (nr * lam_re + ab_im * lam_im) / den
    f_im = (ab_im * lam_re - nr * lam_im) / den
    bt_re = b_re.astype(F32).transpose(0, 2, 1)
    bt_im = b_im.astype(F32).transpose(0, 2, 1)
    w_re = f_re[:, None, :] * bt_re - f_im[:, None, :] * bt_im
    w_im = f_re[:, None, :] * bt_im + f_im[:, None, :] * bt_re
    eye = jnp.eye(GROUPS_PER_BLOCK, dtype=F32)

    def bdiag_in(wt):
        t = wt.reshape(nblk, GROUPS_PER_BLOCK, SSM_GROUP_CH, 1, n) * eye[None, :, None, :, None]
        return t.reshape(nblk, GROUPS_PER_BLOCK * SSM_GROUP_CH, GROUPS_PER_BLOCK * n)

    def bdiag_out(ct):
        t = ct.transpose(0, 2, 1).reshape(nblk, GROUPS_PER_BLOCK, n, 1, SSM_GROUP_CH)
        t = t * eye[None, :, None, :, None]
        return t.reshape(nblk, GROUPS_PER_BLOCK * n, GROUPS_PER_BLOCK * SSM_GROUP_CH)

    w_in = jnp.concatenate([bdiag_in(w_re), bdiag_in(w_im)], axis=2).astype(BF16)
    w_out = jnp.concatenate([bdiag_out(c_re.astype(F32)), -bdiag_out(c_im.astype(F32))],
                            axis=1).astype(BF16)
    a_row = jnp.concatenate([ab_re.reshape(nblk, -1), ab_im.reshape(nblk, -1)], axis=1).reshape(1, -1)
    return a_row, w_in, w_out


def _s5_body(z_ref, a_ref, win_ref, wout_ref, d_ref, wglu_ref, bglu_ref, o_ref,
             zslab, zsb, xs, hstate, yslab):
    nb, ts, w = z_ref.shape
    rows = nb * ts
    nblk = w // LANES
    half = GROUPS_PER_BLOCK * SSM_STATE
    sw = 2 * half

    @pl.when(pl.program_id(1) == 0)
    def _():
        hstate[...] = jnp.zeros_like(hstate)

    for b in range(nb):
        for c in range(nblk):
            zslab[c, b * SLAB_PITCH:b * SLAB_PITCH + ts, :] = z_ref[b, :, c * LANES:(c + 1) * LANES]

    for s in range(ts):
        for c in range(nblk):
            zsb[s * nb:(s + 1) * nb, c * LANES:(c + 1) * LANES] = zslab.at[c][pl.ds(s, nb, stride=SLAB_PITCH), :]

    n_chunks = ts // SCAN_CHUNK
    crows = SCAN_CHUNK * nb
    units = [(j, q) for j in range(nblk) for q in range(n_chunks)]

    def in_map(j, q):
        rs = slice(q * crows, (q + 1) * crows)
        xs[rs, j * sw:(j + 1) * sw] = _dot(zsb[rs, j * LANES:(j + 1) * LANES].astype(BF16), win_ref[j])

    carry = {}

    def scan(j, q):
        re_sl = slice(j * sw, j * sw + half)
        im_sl = slice(j * sw + half, (j + 1) * sw)
        a_r = jnp.broadcast_to(a_ref[:, re_sl], (nb, half))
        a_i = jnp.broadcast_to(a_ref[:, im_sl], (nb, half))
        h_r, h_i = carry[j] if q else (hstate[:, re_sl], hstate[:, im_sl])
        for s in range(q * SCAN_CHUNK, (q + 1) * SCAN_CHUNK):
            x_r = xs[s * nb:(s + 1) * nb, re_sl]
            x_i = xs[s * nb:(s + 1) * nb, im_sl]
            h_r, h_i = a_r * h_r - a_i * h_i + x_r, a_r * h_i + a_i * h_r + x_i
            xs[s * nb:(s + 1) * nb, re_sl] = h_r
            xs[s * nb:(s + 1) * nb, im_sl] = h_i
        carry[j] = (h_r, h_i)
        if q == n_chunks - 1:
            hstate[:, re_sl] = h_r
            hstate[:, im_sl] = h_i

    gel = {}

    def out_map(j, q):
        rs = slice(q * crows, (q + 1) * crows)
        y = _dot(xs[rs, j * sw:(j + 1) * sw].astype(BF16), wout_ref[j])
        y = y + d_ref[:, j * LANES:(j + 1) * LANES] * zsb[rs, j * LANES:(j + 1) * LANES]
        gel[j, q] = _gelu(y)

    for k in range(len(units) + 2):
        if k < len(units):
            in_map(*units[k])
        if 0 <= k - 1 < len(units):
            scan(*units[k - 1])
        if 0 <= k - 2 < len(units):
            out_map(*units[k - 2])

    for q in range(n_chunks):
        g = jnp.concatenate([gel[j, q] for j in range(nblk)], axis=1)
        g = g * jax.nn.sigmoid(_dot(g.astype(BF16), wglu_ref[...]) + bglu_ref[...])
        for c in range(nblk):
            yslab[c, q * crows:(q + 1) * crows, :] = g[:, c * LANES:(c + 1) * LANES]

    for b in range(nb):
        for c in range(nblk):
            o_ref[b, :, c * LANES:(c + 1) * LANES] = yslab.at[c][pl.ds(b, ts, stride=nb), :].astype(BF16)


def _s5(za, a_row, w_in, w_out, d_skip, w_glu, b_glu):
    bsz, s, w = za.shape
    nb, ts = SCAN_BATCHES, SCAN_TS
    rows = nb * ts
    nblk = w // LANES
    nstate = a_row.shape[1]
    full = lambda a: pl.BlockSpec(a.shape, lambda b, i: (0,) * a.ndim)
    args = (a_row, w_in, w_out, d_skip.reshape(1, w).astype(F32), w_glu.astype(BF16), b_glu.reshape(1, w))
    return pl.pallas_call(
        _s5_body,
        grid=(bsz // nb, s // ts),
        in_specs=[pl.BlockSpec((nb, ts, w), lambda b, i: (b, i, 0))] + [full(a) for a in args],
        out_specs=pl.BlockSpec((nb, ts, w), lambda b, i: (b, i, 0)),
        out_shape=jax.ShapeDtypeStruct((bsz, s, w), BF16),
        scratch_shapes=[pltpu.VMEM((nblk, nb * SLAB_PITCH, LANES), F32),
                        pltpu.VMEM((rows, w), F32),
                        pltpu.VMEM((rows, nstate), F32),
                        pltpu.VMEM((nb, nstate), F32),
                        pltpu.VMEM((nblk, rows, LANES), F32)],
        compiler_params=_cparams(("parallel", "arbitrary")),
        name="s5",
    )(za, *args)


def _mix_body(x_ref, u_ref, v_ref, ya_ref, gates_ref, mod_ref, wp_ref, sb_ref, wba_ref, wbb_ref,
              wout_ref, g2_ref, wr_ref, br_ref, h1_ref, xp_ref, route_ref):
    tm_full = x_ref.shape[1]
    npair = u_ref.shape[2] // LANES
    hd = LANES // 2

    t_idx = lax.broadcasted_iota(I32, (SGU_CHUNK, 2 * SGU_CHUNK), 0)
    s_idx = lax.broadcasted_iota(I32, (SGU_CHUNK, 2 * SGU_CHUNK), 1) & (SGU_CHUNK - 1)
    causal = s_idx <= t_idx
    lane = lax.broadcasted_iota(I32, (SGU_CHUNK, LANES), 1)
    first_head = lane < hd
    w_causal = [jnp.where(causal, wp_ref[j], jnp.zeros_like(wp_ref[j])) for j in range(npair)]
    for r0 in range(0, tm_full, MIX_SUBTILE):
        _mix_rows(slice(r0, r0 + MIX_SUBTILE), w_causal, first_head, x_ref, u_ref, v_ref, ya_ref, gates_ref,
                  mod_ref, sb_ref, wba_ref, wbb_ref, wout_ref, g2_ref, wr_ref, br_ref, h1_ref, xp_ref,
                  route_ref)


def _mix_rows(rs, w_causal, first_head, x_ref, u_ref, v_ref, ya_ref, gates_ref, mod_ref, sb_ref, wba_ref,
              wbb_ref, wout_ref, g2_ref, wr_ref, br_ref, h1_ref, xp_ref, route_ref):
    tm = rs.stop - rs.start
    d = x_ref.shape[2]
    npair = len(w_causal)
    mixed_rows = []
    for c in range(rs.start, rs.stop, SGU_CHUNK):
        vc = v_ref[0, c:c + SGU_CHUNK, :]
        blocks = []
        for j in range(npair):
            vb = vc[:, j * LANES:(j + 1) * LANES]
            zero = jnp.zeros_like(vb)
            rhs = jnp.concatenate([jnp.where(first_head, vb, zero), jnp.where(first_head, zero, vb)], axis=0)
            blocks.append(_dot(w_causal[j], rhs))
        mixed_rows.append(jnp.concatenate(blocks, axis=1) + sb_ref[...])
    mixed = jnp.concatenate(mixed_rows, axis=0)
    yb = (u_ref[0, rs, :].astype(F32) * mixed).astype(BF16)

    pa = _dot(ya_ref[0, rs, :], wba_ref[...])
    pb = _dot(yb, wbb_ref[...])
    merged = gates_ref[0, rs, :d].astype(F32) * pa + gates_ref[0, rs, d:].astype(F32) * pb
    o = _dot(merged.astype(BF16), wout_ref[...])
    h1 = x_ref[0, rs, :] + mod_ref[0, 2:3, :] * o
    h1_ref[0, rs, :] = h1

    ms = jnp.mean(h1 * h1, axis=-1, keepdims=True)
    xn = h1 * lax.rsqrt(ms + RMS_EPS) * g2_ref[...]
    xn = xn * (1.0 + mod_ref[0, 4:5, :]) + mod_ref[0, 3:4, :]
    x_hi = xn.astype(BF16)

    logits = _dot(x_hi, wr_ref[...]) + br_ref[...]
    lt = logits.T
    best = lt[0:1, :]
    grp = jnp.zeros((1, tm), I32)
    for gi in range(1, N_GROUPS):
        better = lt[gi:gi + 1, :] > best
        grp = jnp.where(better, gi, grp)
        best = jnp.where(better, lt[gi:gi + 1, :], best)
    den = jnp.zeros((1, tm), F32)
    for gi in range(N_GROUPS):
        den = den + jnp.exp(lt[gi:gi + 1, :] - best)
    pg = 1.0 / den
    le = lt[SUBLANES:2 * SUBLANES, :]
    for gi in range(1, N_GROUPS):
        le = jnp.where(grp == gi, lt[SUBLANES * (gi + 1):SUBLANES * (gi + 2), :], le)
    eidx = lax.broadcasted_iota(I32, (EXPERTS_PER_GROUP, tm), 0).astype(F32)
    none = float(EXPERTS_PER_GROUP)
    v1 = jnp.max(le, axis=0, keepdims=True)
    i1 = jnp.min(jnp.where(le == v1, eidx, none), axis=0, keepdims=True)
    rest = jnp.where(eidx == i1, -jnp.inf, le)
    v2 = jnp.max(rest, axis=0, keepdims=True)
    i2 = jnp.min(jnp.where(rest == v2, eidx, none), axis=0, keepdims=True)
    ex = jnp.exp(v2 - v1)
    p1 = 1.0 / (1.0 + ex)
    wt1 = pg * p1
    wt2 = pg * (ex * p1)
    first_lo = i1 < i2
    lo = jnp.where(first_lo, i1, i2)
    hi = jnp.where(first_lo, i2, i1)
    pair = lo * (2 * EXPERTS_PER_GROUP - 1.0 - lo) * 0.5 + (hi - lo - 1.0)
    cls = grp.astype(F32) * PAIRS_PER_GROUP + pair
    info = jnp.concatenate([cls, jnp.where(first_lo, wt1, wt2), jnp.where(first_lo, wt2, wt1)], axis=0)
    route_ref[:, rs] = jnp.concatenate([info, jnp.zeros((SUBLANES - 3, tm), F32)], axis=0)

    bits = lax.bitcast_convert_type(x_hi.astype(F32), I32)
    packed = lax.shift_right_logical(bits[:, :d // 2], 16) | (bits[:, d // 2:] & jnp.int32(-65536))
    info_cols = jnp.concatenate([info, jnp.zeros((LANES - 3, tm), F32)], axis=0).T
    xp_ref[0, rs, :] = jnp.concatenate([packed, lax.bitcast_convert_type(info_cols, I32)], axis=1)


def _mix(x, u, v, ya, gates, mod, sgu_w, sgu_b, w_ba, w_bb, w_out, g2, w_rg, b_rg, w_re, b_re, tm):
    bsz, s, d = x.shape
    w = u.shape[2]
    nt = bsz * s
    wp = sgu_w.reshape(SGU_HEADS // 2, 2, SGU_CHUNK, SGU_CHUNK).transpose(0, 2, 1, 3)
    wp = wp.reshape(SGU_HEADS // 2, SGU_CHUNK, 2 * SGU_CHUNK).astype(BF16)
    sb = jnp.repeat(sgu_b.T, w // SGU_HEADS, axis=1).astype(F32)
    wr = jnp.zeros((d, LANES), F32)
    wr = wr.at[:, :N_GROUPS].set(w_rg)
    wr = wr.at[:, SUBLANES:SUBLANES + N_GROUPS * EXPERTS_PER_GROUP].set(
        w_re.transpose(1, 0, 2).reshape(d, N_GROUPS * EXPERTS_PER_GROUP))
    br = jnp.zeros((1, LANES), F32)
    br = br.at[0, :N_GROUPS].set(b_rg)
    br = br.at[0, SUBLANES:SUBLANES + N_GROUPS * EXPERTS_PER_GROUP].set(b_re.reshape(-1))
    tok = lambda n: pl.BlockSpec((1, tm, n), lambda b, i: (b, i, 0))
    full = lambda a: pl.BlockSpec(a.shape, lambda b, i: (0,) * a.ndim)
    args = (wp, sb, w_ba.astype(BF16), w_bb.astype(BF16), w_out.astype(BF16), g2.reshape(1, d),
            wr.astype(BF16), br)
    nsteps = s // tm
    return pl.pallas_call(
        _mix_body,
        grid=(bsz, nsteps),
        in_specs=[tok(d), tok(w), tok(w), tok(w), tok(2 * d),
                  pl.BlockSpec((1, 6, d), lambda b, i: (b, 0, 0))] + [full(a) for a in args],
        out_specs=[tok(d), tok(d // 2 + LANES),
                   pl.BlockSpec((SUBLANES, tm), lambda b, i: (0, b * nsteps + i))],
        out_shape=[jax.ShapeDtypeStruct((bsz, s, d), F32),
                   jax.ShapeDtypeStruct((bsz, s, d // 2 + LANES), I32),
                   jax.ShapeDtypeStruct((SUBLANES, nt), F32)],
        compiler_params=_cparams(("parallel", "parallel")),
        name="mix",
    )(x, u, v, ya, gates, mod, *args)


def _tile_lanes(a, reps):
    return jnp.concatenate([a] * reps, axis=1)


def _dispatch_body(route_ref, lo_ref, hi_ref, pos_ref, blk_ref, cls_ref, run, pstart, ranks, classes,
                   earlier):
    i = pl.program_id(0)
    n = pl.num_programs(0)
    tt = route_ref.shape[1]
    reps = tt // LANES
    cid = lax.broadcasted_iota(I32, (LANES, tt), 0).astype(F32)

    sub = earlier.shape[0]

    @pl.when(i == 0)
    def _():
        run[...] = jnp.zeros_like(run)
        earlier_tok = lax.broadcasted_iota(I32, (sub, sub), 0) < lax.broadcasted_iota(I32, (sub, sub), 1)
        earlier[...] = jnp.where(earlier_tok, 1.0, 0.0).astype(BF16)

    cls = route_ref[0:1, :]
    classes[i] = cls
    onehot = jnp.where(cls == cid, 1.0, 0.0)
    seen = run[...]
    before = []
    for k in range(tt // sub):
        oh = onehot[:, k * sub:(k + 1) * sub]
        before.append(_dot(oh.astype(BF16), earlier[...]) + _tile_lanes(seen, sub // LANES))
        seen = seen + jnp.sum(oh, axis=1, keepdims=True)
    ranks[i] = jnp.sum(onehot * jnp.concatenate(before, axis=1), axis=0, keepdims=True)
    run[...] = seen

    @pl.when(i == n - 1)
    def _():
        counts = run[...]
        nblk = jnp.floor((counts + (EXPERT_ROWS - 1.0)) * (1.0 / EXPERT_ROWS))
        hi_part = jnp.floor(nblk * (1.0 / 16.0))
        lo_part = nblk - 16.0 * hi_part
        upto = lax.broadcasted_iota(I32, (LANES, LANES), 1) <= lax.broadcasted_iota(I32, (LANES, LANES), 0)
        upto = jnp.where(upto, 1.0, 0.0).astype(BF16)
        ends = 16.0 * _dot(upto, hi_part.astype(BF16)) + _dot(upto, lo_part.astype(BF16))
        pstart[...] = (ends - nblk) * EXPERT_ROWS
        diag = lax.broadcasted_iota(I32, (LANES, LANES), 0) == lax.broadcasted_iota(I32, (LANES, LANES), 1)
        on_lanes = lambda a: jnp.sum(jnp.where(diag, a, 0.0), axis=0, keepdims=True)
        cls_ref[...] = jnp.concatenate([on_lanes(counts), on_lanes(pstart[...]),
                                        jnp.zeros((SUBLANES - 2, LANES), F32)], axis=0)
        nb_lanes = blk_ref.shape[1]
        breps = nb_lanes // LANES
        n_used = _tile_lanes(ends[LANES - 1:LANES, :], breps)
        b_idx = lax.broadcasted_iota(I32, (LANES, nb_lanes), 1).astype(F32)
        b_idx = jnp.minimum(b_idx, n_used - 1.0)
        b_cls = jnp.sum(jnp.where(_tile_lanes(ends, breps) <= b_idx, 1.0, 0.0), axis=0, keepdims=True)
        b_member = b_cls == lax.broadcasted_iota(I32, (LANES, nb_lanes), 0).astype(F32)
        pick = lambda tab: jnp.sum(jnp.where(b_member, _tile_lanes(tab, breps), 0.0), axis=0, keepdims=True)
        blk_ref[...] = jnp.concatenate([pick(lo_ref[...]), pick(hi_ref[...]), n_used,
                                        jnp.zeros((SUBLANES - 3, nb_lanes), F32)], axis=0)
        starts = _tile_lanes(pstart[...], reps)

        def tile_positions(k, carry):
            base = jnp.sum(jnp.where(classes[k] == cid, starts, 0.0), axis=0, keepdims=True)
            pos_ref[k] = (base + ranks[k]).astype(I32)
            return carry

        lax.fori_loop(0, n, tile_positions, 0)


def _dispatch(route, n_blocks, tt=4096):
    nt = route.shape[1]
    n = nt // tt
    nb_lanes = pl.cdiv(n_blocks, LANES) * LANES
    lo_tab, hi_tab = [], []
    for g in range(N_GROUPS):
        for a in range(EXPERTS_PER_GROUP):
            for b in range(a + 1, EXPERTS_PER_GROUP):
                lo_tab.append(g * EXPERTS_PER_GROUP + a)
                hi_tab.append(g * EXPERTS_PER_GROUP + b)
    pad = [0] * (LANES - N_CLASSES)
    lo_tile = jnp.broadcast_to(jnp.asarray(lo_tab + pad, F32)[:, None], (LANES, LANES))
    hi_tile = jnp.broadcast_to(jnp.asarray(hi_tab + pad, F32)[:, None], (LANES, LANES))
    const = lambda shape: pl.BlockSpec(shape, lambda i: (0,) * len(shape))
    pos, blk_tab, cls_tab = pl.pallas_call(
        _dispatch_body,
        grid=(n,),
        in_specs=[pl.BlockSpec((SUBLANES, tt), lambda i: (0, i)),
                  const((LANES, LANES)), const((LANES, LANES))],
        out_specs=[const((n, 1, tt)), const((SUBLANES, nb_lanes)), const((SUBLANES, LANES))],
        out_shape=[jax.ShapeDtypeStruct((n, 1, tt), I32),
                   jax.ShapeDtypeStruct((SUBLANES, nb_lanes), F32),
                   jax.ShapeDtypeStruct((SUBLANES, LANES), F32)],
        scratch_shapes=[pltpu.VMEM((LANES, LANES), F32), pltpu.VMEM((LANES, LANES), F32),
                        pltpu.VMEM((n, 1, tt), F32), pltpu.VMEM((n, 1, tt), F32),
                        pltpu.VMEM((DISPATCH_SUB, DISPATCH_SUB), BF16)],
        compiler_params=_cparams(("arbitrary",)),
        name="dispatch",
    )(route, lo_tile, hi_tile)
    blk = blk_tab.astype(I32)
    cls = cls_tab.astype(I32)
    return pos.reshape(nt), blk[0, :n_blocks], blk[1, :n_blocks], blk[2, :1], cls[0], cls[1]


ZERO_FILL_ROWS = tuple(1 << k for k in range(EXPERT_ROWS.bit_length() - 1))


def _tile_rows_wait(buf, sem):
    pltpu.make_async_copy(buf, buf, sem).wait()


def _scatter_body(cnt_ref, pst_ref, pos_ref, xp_ref, xs_hbm, ring, zeros, sem, zsem):
    i = pl.program_id(0)
    n = pl.num_programs(0)
    tm, wrow = xp_ref.shape
    half = tm // 2
    n_tiles = wrow // LANES

    @pl.when(i == 0)
    def _():
        ring[...] = jnp.zeros_like(ring)

    for h in range(2):
        @pl.when(i > 0)
        def _():
            _tile_rows_wait(ring.at[h], sem.at[h])

        for c in range(n_tiles):
            ring.at[h][pl.ds(c, half, stride=SUBLANES), :] = xp_ref[h * half:(h + 1) * half,
                                                                    c * LANES:(c + 1) * LANES]
        for r in range(half):
            pltpu.make_async_copy(ring.at[h, pl.ds(r * SUBLANES, SUBLANES)],
                                  xs_hbm.at[pos_ref[0, 0, h * half + r]], sem.at[h]).start(priority=r % 2)

    @pl.when(i == n - 1)
    def _():
        _tile_rows_wait(ring.at[0], sem.at[0])
        _tile_rows_wait(ring.at[1], sem.at[1])
        zeros[...] = jnp.zeros_like(zeros)

        def fill(wait):
            def per_class(k, carry):
                cnt = cnt_ref[k]
                n_pad = (-cnt) & (EXPERT_ROWS - 1)
                off = pst_ref[k] + cnt
                for size in ZERO_FILL_ROWS:
                    piece = pltpu.make_async_copy(zeros.at[pl.ds(0, size)], xs_hbm.at[pl.ds(off, size)], zsem)

                    @pl.when((n_pad & size) != 0)
                    def _():
                        piece.wait() if wait else piece.start()

                    off = off + (n_pad & size)
                return carry

            lax.fori_loop(0, N_CLASSES, per_class, 0)

            last = N_CLASSES - 1
            used_rows = pst_ref[last] + cnt_ref[last] + ((-cnt_ref[last]) & (EXPERT_ROWS - 1))
            tail = ZERO_FILL_ROWS[-1]

            def per_piece(j, carry):
                piece = pltpu.make_async_copy(zeros, xs_hbm.at[pl.ds(used_rows + j * tail, tail)], zsem)
                piece.wait() if wait else piece.start()
                return carry

            lax.fori_loop(0, (xs_hbm.shape[0] - used_rows) // tail, per_piece, 0)

        fill(False)
        fill(True)


def _scatter_rows(xp, pos, counts, pstarts, n_blocks, tm=1024):
    nt, wrow = xp.shape
    n = nt // tm
    grid_spec = pltpu.PrefetchScalarGridSpec(
        num_scalar_prefetch=2,
        grid=(n,),
        in_specs=[pl.BlockSpec((1, 1, tm), lambda i, c, p: (i, 0, 0), memory_space=pltpu.SMEM),
                  pl.BlockSpec((tm, wrow), lambda i, c, p: (i, 0))],
        out_specs=pl.BlockSpec(memory_space=pl.ANY),
        scratch_shapes=[pltpu.VMEM((2, tm // 2 * SUBLANES, LANES), I32),
                        pltpu.VMEM((ZERO_FILL_ROWS[-1], SUBLANES, LANES), I32),
                        pltpu.SemaphoreType.DMA((2,)), pltpu.SemaphoreType.DMA(())],
    )
    return pl.pallas_call(
        _scatter_body,
        grid_spec=grid_spec,
        out_shape=jax.ShapeDtypeStruct((n_blocks * EXPERT_ROWS, SUBLANES, LANES), I32),
        compiler_params=_cparams(("arbitrary",)),
        name="scatter_rows",
    )(counts, pstarts, pos.reshape(n, 1, tm), xp)


def _unpack_rows(xp):
    left = lax.bitcast_convert_type(lax.shift_left(xp, 16), F32).astype(BF16)
    right = lax.bitcast_convert_type(xp & jnp.int32(-65536), F32).astype(BF16)
    return jnp.concatenate([left, right], axis=1)


def _experts_body(elo_ref, ehi_ref, nu_ref, xs_ref, w13a_ref, w2a_ref, w13b_ref, w2b_ref, o_ref):
    i = pl.program_id(0)
    rows = xs_ref.shape[0] // SUBLANES
    ff, d = w2a_ref.shape[1], w2a_ref.shape[2]
    n_packed = d // 2 // LANES
    lane_tile = lambda ref, c: ref[pl.ds(c, rows, stride=SUBLANES), :]

    @pl.when(i < nu_ref[0])
    def _():
        x = _unpack_rows(jnp.concatenate([lane_tile(xs_ref, c) for c in range(n_packed)], axis=1))
        info = lax.bitcast_convert_type(lane_tile(xs_ref, n_packed), F32)

        def ffn(w13_ref, w2_ref, weight):
            ab = _dot(x, w13_ref[0])
            a = ab[:, :ff]
            hid = (a * jax.nn.sigmoid(a)) * ab[:, ff:] * weight
            return _dot(hid.astype(BF16), w2_ref[0])

        y = ffn(w13a_ref, w2a_ref, info[:, 1:2]) + ffn(w13b_ref, w2b_ref, info[:, 2:3])
        for c in range(d // LANES):
            o_ref[pl.ds(c, rows, stride=SUBLANES), :] = y[:, c * LANES:(c + 1) * LANES]

    @pl.when(i >= nu_ref[0])
    def _():
        o_ref[...] = jnp.zeros_like(o_ref)


def _experts(xs, e_lo, e_hi, n_used, w13, w2):
    n_rows = xs.shape[0]
    n_blocks = n_rows // EXPERT_ROWS
    n_exp, ff, d = w2.shape
    assert d == SUBLANES * LANES
    blk = EXPERT_ROWS * SUBLANES
    pick = lambda a, use_hi: pl.BlockSpec(
        (1,) + a.shape[1:],
        (lambda i, lo, hi, nu: (hi[i], 0, 0)) if use_hi else (lambda i, lo, hi, nu: (lo[i], 0, 0)))
    grid_spec = pltpu.PrefetchScalarGridSpec(
        num_scalar_prefetch=3,
        grid=(n_blocks,),
        in_specs=[pl.BlockSpec((blk, LANES), lambda i, lo, hi, nu: (jnp.minimum(i, nu[0] - 1), 0)),
                  pick(w13, False), pick(w2, False), pick(w13, True), pick(w2, True)],
        out_specs=pl.BlockSpec((blk, LANES), lambda i, lo, hi, nu: (i, 0)),
    )
    y = pl.pallas_call(
        _experts_body,
        grid_spec=grid_spec,
        out_shape=jax.ShapeDtypeStruct((n_rows * SUBLANES, LANES), F32),
        compiler_params=_cparams(("arbitrary",)),
        name="experts",
    )(e_lo, e_hi, n_used, xs.reshape(n_rows * SUBLANES, LANES), w13, w2, w13, w2)
    return y.reshape(n_rows, SUBLANES, LANES)


def _final_body(posc_ref, posn_ref, y_hbm, h1_ref, mod_ref, gf_ref, o_ref, ybuf, sem):
    i = pl.program_id(0)
    n = pl.num_programs(0)
    tm = h1_ref.shape[0]
    half = tm // 2

    def fetch(idx_ref, h):
        for r in range(half):
            pltpu.make_async_copy(y_hbm.at[idx_ref[0, 0, h * half + r]],
                                  ybuf.at[h, pl.ds(r * SUBLANES, SUBLANES)], sem.at[h]).start(priority=r % 2)

    @pl.when(i == 0)
    def _():
        fetch(posc_ref, 0)
        fetch(posc_ref, 1)

    for h in range(2):
        rows = slice(h * half, (h + 1) * half)
        _tile_rows_wait(ybuf.at[h], sem.at[h])
        y = jnp.concatenate([ybuf.at[h][pl.ds(c, half, stride=SUBLANES), :] for c in range(SUBLANES)], axis=1)
        hres = h1_ref[rows, :] + mod_ref[0, 5:6, :] * y
        ms = jnp.mean(hres * hres, axis=-1, keepdims=True)
        o_ref[rows, :] = hres * lax.rsqrt(ms + RMS_EPS) * gf_ref[...]
        fetch(posn_ref, h)

    @pl.when(i == n - 1)
    def _():
        _tile_rows_wait(ybuf.at[0], sem.at[0])
        _tile_rows_wait(ybuf.at[1], sem.at[1])


def _final(pos, y_sorted, h1, mod, gf, seq, tm):
    nt, d = h1.shape
    n = nt // tm
    pos3 = pos.reshape(n, 1, tm)
    idx_spec = lambda f: pl.BlockSpec((1, 1, tm), f, memory_space=pltpu.SMEM)
    return pl.pallas_call(
        _final_body,
        grid=(n,),
        in_specs=[idx_spec(lambda i: (i, 0, 0)),
                  idx_spec(lambda i: (jnp.minimum(i + 1, n - 1), 0, 0)),
                  pl.BlockSpec(memory_space=pl.ANY),
                  pl.BlockSpec((tm, d), lambda i: (i, 0)),
                  pl.BlockSpec((1, 6, d), lambda i: (i * tm // seq, 0, 0)),
                  pl.BlockSpec((1, d), lambda i: (0, 0))],
        out_specs=pl.BlockSpec((tm, d), lambda i: (i, 0)),
        out_shape=jax.ShapeDtypeStruct((nt, d), F32),
        scratch_shapes=[pltpu.VMEM((2, tm // 2 * SUBLANES, LANES), F32), pltpu.SemaphoreType.DMA((2,))],
        compiler_params=_cparams(("arbitrary",)),
        name="final",
    )(pos3, pos3, y_sorted, h1, mod, gf.reshape(1, d))


def kernel(x, c, w_ada, b_ada, norm1_g, w_in, w_gate, b_gate, ssm_a_re, ssm_a_im, ssm_b_re, ssm_b_im, ssm_c_re, ssm_c_im, ssm_d, ssm_log_step, w_glu, b_glu, sgu_ln_g, sgu_ln_b, sgu_w, sgu_b, w_branch_a, w_branch_b, w_out, norm2_g, w_router_group, b_router_group, w_router_expert, b_router_expert, w1, w3, w2, norm_f_g):
    bsz, seq, d = x.shape
    depth = w_ada.shape[0]
    assert depth == 1 and bsz % SCAN_BATCHES == 0 and seq % 512 == 0
    l = 0
    mod = _adaln(c, w_ada[l], b_ada[l]).reshape(bsz, 6, d)
    za, u, v, gates, w13_bf, w2_bf = _inproj(x, mod, norm1_g[l], w_in[l], w_gate[l], b_gate[l],
                                             sgu_ln_g[l], sgu_ln_b[l], w1[l], w3[l], w2[l], tm=1024)
    a_row, s5_in, s5_out = _s5_params(ssm_a_re[l], ssm_a_im[l], ssm_b_re[l], ssm_b_im[l],
                                      ssm_c_re[l], ssm_c_im[l], ssm_log_step[l])
    ya = _s5(za, a_row, s5_in, s5_out, ssm_d[l], w_glu[l], b_glu[l])
    h1, xp, route = _mix(x, u, v, ya, gates, mod, sgu_w[l], sgu_b[l], w_branch_a[l], w_branch_b[l],
                         w_out[l], norm2_g[l], w_router_group[l], b_router_group[l],
                         w_router_expert[l], b_router_expert[l], tm=1024)
    nt = bsz * seq
    n_blocks = nt // EXPERT_ROWS + N_CLASSES
    pos, e_lo, e_hi, n_used, counts, pstarts = _dispatch(route, n_blocks)
    xs = _scatter_rows(xp.reshape(nt, xp.shape[-1]), pos, counts, pstarts, n_blocks)
    y_sorted = _experts(xs, e_lo, e_hi, n_used, w13_bf, w2_bf)
    out = _final(pos, y_sorted, h1.reshape(nt, d), mod, norm_f_g, seq, tm=1024)
    return out.reshape(bsz, seq, d)
```

```python
import math

import jax
import jax.numpy as jnp
from jax import lax
from jax.experimental import pallas as pl
from jax.experimental.pallas import tpu as pltpu

F32 = jnp.float32
BF16 = jnp.bfloat16
I32 = jnp.int32

LANES = 128
SUBLANES = 8
VMEM_LIMIT = 56 * 1024 * 1024

RMS_EPS = 1e-6
LN_EPS = 1e-5

SSM_GROUP_CH = 16
SSM_STATE = 64
GROUPS_PER_BLOCK = LANES // SSM_GROUP_CH
SGU_HEADS = 8
SGU_CHUNK = 128
N_GROUPS = 4
EXPERTS_PER_GROUP = 8
PAIRS_PER_GROUP = EXPERTS_PER_GROUP * (EXPERTS_PER_GROUP - 1) // 2
N_CLASSES = N_GROUPS * PAIRS_PER_GROUP
EXPERT_ROWS = 256

SCAN_BATCHES = SUBLANES
SCAN_TS = 128
SLAB_PITCH = SCAN_TS + 8
SCAN_CHUNK = 64
INPROJ_SUBTILE = 256
MIX_SUBTILE = 512
DISPATCH_SUB = 256


def _gelu(x):
    return 0.5 * x * (1.0 + jnp.tanh(math.sqrt(2.0 / math.pi) * (x + 0.044715 * (x * x * x))))


def _dot(a, b):
    return jnp.dot(a, b, preferred_element_type=F32)


def _split_bf16(a):
    hi = a.astype(BF16)
    lo = (a - hi.astype(F32)).astype(BF16)
    return hi, lo


def _cparams(sem):
    return pltpu.CompilerParams(dimension_semantics=sem, vmem_limit_bytes=VMEM_LIMIT)


def _adaln_body(c_ref, w_ref, b_ref, o_ref):
    c = c_ref[...]
    act = c * jax.nn.sigmoid(c)
    a_hi, a_lo = _split_bf16(act)
    w_hi, w_lo = _split_bf16(w_ref[...])
    o_ref[...] = _dot(a_hi, w_hi) + _dot(a_hi, w_lo) + _dot(a_lo, w_hi) + b_ref[...]


def _adaln(c, w, b):
    bsz, d = c.shape
    n = w.shape[1]
    tn = 1024
    return pl.pallas_call(
        _adaln_body,
        grid=(n // tn,),
        in_specs=[pl.BlockSpec((bsz, d), lambda j: (0, 0)),
                  pl.BlockSpec((d, tn), lambda j: (0, j)),
                  pl.BlockSpec((1, tn), lambda j: (0, j))],
        out_specs=pl.BlockSpec((bsz, tn), lambda j: (0, j)),
        out_shape=jax.ShapeDtypeStruct((bsz, n), F32),
        compiler_params=_cparams(("arbitrary",)),
        name="adaln",
    )(c, w, b.reshape(1, n))


def _inproj_body(x_ref, mod_ref, g1_ref, win_ref, bgate_ref, lng_ref, lnb_ref,
                 w1_ref, w3_ref, w2_ref, za_ref, u_ref, v_ref, gates_ref, w13_ref, w2b_ref):
    w = za_ref.shape[-1]
    tm = x_ref.shape[1]
    ff = w1_ref.shape[2]
    w13_ref[0, :, :ff] = w1_ref[0].astype(BF16)
    w13_ref[0, :, ff:] = w3_ref[0].astype(BF16)
    w2b_ref[0] = w2_ref[0].astype(BF16)
    for r0 in range(0, tm, INPROJ_SUBTILE):
        rs = slice(r0, r0 + INPROJ_SUBTILE)
        x = x_ref[0, rs, :]
        ms = jnp.mean(x * x, axis=-1, keepdims=True)
        xn = x * lax.rsqrt(ms + RMS_EPS) * g1_ref[...]
        xn = xn * (1.0 + mod_ref[0, 1:2, :]) + mod_ref[0, 0:1, :]
        xb = xn.astype(BF16)
        both = _dot(xb, win_ref[...])
        proj = both[:, :3 * w]
        za_ref[0, rs, :] = proj[:, :w]
        u_ref[0, rs, :] = _gelu(proj[:, w:2 * w]).astype(BF16)
        gv = _gelu(proj[:, 2 * w:])
        mu = jnp.mean(gv, axis=-1, keepdims=True)
        cen = gv - mu
        var = jnp.mean(cen * cen, axis=-1, keepdims=True)
        v_ref[0, rs, :] = (cen * lax.rsqrt(var + LN_EPS) * lng_ref[...] + lnb_ref[...]).astype(BF16)
        gates_ref[0, rs, :] = jax.nn.sigmoid(both[:, 3 * w:] + bgate_ref[...]).astype(BF16)


def _inproj(x, mod, g1, w_in, w_gate, b_gate, ln_g, ln_b, w1, w3, w2, tm):
    bsz, s, d = x.shape
    w = w_in.shape[1] // 3
    ng = w_gate.shape[1]
    n_exp, _, ff = w1.shape
    nsteps = s // tm
    parts = bsz * nsteps // n_exp
    assert bsz * nsteps == n_exp * parts and d % (parts * SUBLANES) == 0 and ff % (parts * SUBLANES) == 0
    tok = lambda n: pl.BlockSpec((1, tm, n), lambda b, i: (b, i, 0))
    full = lambda a: pl.BlockSpec(a.shape, lambda b, i: (0,) * a.ndim)
    wslice = lambda rows, cols: pl.BlockSpec(
        (1, rows // parts, cols), lambda b, i: ((b * nsteps + i) // parts, (b * nsteps + i) % parts, 0))
    w_both = jnp.concatenate([w_in, w_gate], axis=1).astype(BF16)
    args = (g1.reshape(1, d), w_both, b_gate.reshape(1, ng), ln_g.reshape(1, w), ln_b.reshape(1, w))
    return pl.pallas_call(
        _inproj_body,
        grid=(bsz, nsteps),
        in_specs=[tok(d), pl.BlockSpec((1, 6, d), lambda b, i: (b, 0, 0))] + [full(a) for a in args]
                 + [wslice(d, ff), wslice(d, ff), wslice(ff, d)],
        out_specs=[tok(w), tok(w), tok(w), tok(ng), wslice(d, 2 * ff), wslice(ff, d)],
        out_shape=[jax.ShapeDtypeStruct((bsz, s, w), F32),
                   jax.ShapeDtypeStruct((bsz, s, w), BF16),
                   jax.ShapeDtypeStruct((bsz, s, w), BF16),
                   jax.ShapeDtypeStruct((bsz, s, ng), BF16),
                   jax.ShapeDtypeStruct((n_exp, d, 2 * ff), BF16),
                   jax.ShapeDtypeStruct((n_exp, ff, d), BF16)],
        compiler_params=_cparams(("parallel", "parallel")),
        name="in_proj",
    )(x, mod, *args, w1, w3, w2)


def _s5_params(a_re, a_im, b_re, b_im, c_re, c_im, log_step):
    g, n = a_re.shape
    nblk = g // GROUPS_PER_BLOCK
    lam_re = jnp.minimum(a_re.astype(F32), -1e-4)
    lam_im = a_im.astype(F32)
    dt = jnp.exp(log_step.astype(F32))[:, None]
    mag = jnp.exp(lam_re * dt)
    ab_re = mag * jnp.cos(lam_im * dt)
    ab_im = mag * jnp.sin(lam_im * dt)
    den = lam_re * lam_re + lam_im * lam_im
    nr = ab_re - 1.0
    f_re = (nr * lam_re + ab_im * lam_im) / den
    f_im = (ab_im * lam_re - nr * lam_im) / den
    bt_re = b_re.astype(F32).transpose(0, 2, 1)
    bt_im = b_im.astype(F32).transpose(0, 2, 1)
    w_re = f_re[:, None, :] * bt_re - f_im[:, None, :] * bt_im
    w_im = f_re[:, None, :] * bt_im + f_im[:, None, :] * bt_re
    eye = jnp.eye(GROUPS_PER_BLOCK, dtype=F32)

    def bdiag_in(wt):
        t = wt.reshape(nblk, GROUPS_PER_BLOCK, SSM_GROUP_CH, 1, n) * eye[None, :, None, :, None]
        return t.reshape(nblk, GROUPS_PER_BLOCK * SSM_GROUP_CH, GROUPS_PER_BLOCK * n)

    def bdiag_out(ct):
        t = ct.transpose(0, 2, 1).reshape(nblk, GROUPS_PER_BLOCK, n, 1, SSM_GROUP_CH)
        t = t * eye[None, :, None, :, None]
        return t.reshape(nblk, GROUPS_PER_BLOCK * n, GROUPS_PER_BLOCK * SSM_GROUP_CH)

    w_in = jnp.concatenate([bdiag_in(w_re), bdiag_in(w_im)], axis=2).astype(BF16)
    w_out = jnp.concatenate([bdiag_out(c_re.astype(F32)), -bdiag_out(c_im.astype(F32))],
                            axis=1).astype(BF16)
    a_row = jnp.concatenate([ab_re.reshape(nblk, -1), ab_im.reshape(nblk, -1)], axis=1).reshape(1, -1)
    return a_row, w_in, w_out


def _s5_body(z_ref, a_ref, win_ref, wout_ref, d_ref, wglu_ref, bglu_ref, o_ref,
             zslab, zsb, xs, hstate, yslab):
    nb, ts, w = z_ref.shape
    rows = nb * ts
    nblk = w // LANES
    half = GROUPS_PER_BLOCK * SSM_STATE
    sw = 2 * half

    @pl.when(pl.program_id(1) == 0)
    def _():
        hstate[...] = jnp.zeros_like(hstate)

    for b in range(nb):
        for c in range(nblk):
            zslab[c, b * SLAB_PITCH:b * SLAB_PITCH + ts, :] = z_ref[b, :, c * LANES:(c + 1) * LANES]

    for s in range(ts):
        for c in range(nblk):
            zsb[s * nb:(s + 1) * nb, c * LANES:(c + 1) * LANES] = zslab.at[c][pl.ds(s, nb, stride=SLAB_PITCH), :]

    n_chunks = ts // SCAN_CHUNK
    crows = SCAN_CHUNK * nb
    units = [(j, q) for j in range(nblk) for q in range(n_chunks)]

    def in_map(j, q):
        rs = slice(q * crows, (q + 1) * crows)
        xs[rs, j * sw:(j + 1) * sw] = _dot(zsb[rs, j * LANES:(j + 1) * LANES].astype(BF16), win_ref[j])

    carry = {}

    def scan(j, q):
        re_sl = slice(j * sw, j * sw + half)
        im_sl = slice(j * sw + half, (j + 1) * sw)
        a_r = jnp.broadcast_to(a_ref[:, re_sl], (nb, half))
        a_i = jnp.broadcast_to(a_ref[:, im_sl], (nb, half))
        h_r, h_i = carry[j] if q else (hstate[:, re_sl], hstate[:, im_sl])
        for s in range(q * SCAN_CHUNK, (q + 1) * SCAN_CHUNK):
            x_r = xs[s * nb:(s + 1) * nb, re_sl]
            x_i = xs[s * nb:(s + 1) * nb, im_sl]
            h_r, h_i = a_r * h_r - a_i * h_i + x_r, a_r * h_i + a_i * h_r + x_i
            xs[s * nb:(s + 1) * nb, re_sl] = h_r
            xs[s * nb:(s + 1) * nb, im_sl] = h_i
        carry[j] = (h_r, h_i)
        if q == n_chunks - 1:
            hstate[:, re_sl] = h_r
            hstate[:, im_sl] = h_i

    gel = {}

    def out_map(j, q):
        rs = slice(q * crows, (q + 1) * crows)
        y = _dot(xs[rs, j * sw:(j + 1) * sw].astype(BF16), wout_ref[j])
        y = y + d_ref[:, j * LANES:(j + 1) * LANES] * zsb[rs, j * LANES:(j + 1) * LANES]
        gel[j, q] = _gelu(y)

    for k in range(len(units) + 2):
        if k < len(units):
            in_map(*units[k])
        if 0 <= k - 1 < len(units):
            scan(*units[k - 1])
        if 0 <= k - 2 < len(units):
            out_map(*units[k - 2])

    for q in range(n_chunks):
        g = jnp.concatenate([gel[j, q] for j in range(nblk)], axis=1)
        g = g * jax.nn.sigmoid(_dot(g.astype(BF16), wglu_ref[...]) + bglu_ref[...])
        for c in range(nblk):
            yslab[c, q * crows:(q + 1) * crows, :] = g[:, c * LANES:(c + 1) * LANES]

    for b in range(nb):
        for c in range(nblk):
            o_ref[b, :, c * LANES:(c + 1) * LANES] = yslab.at[c][pl.ds(b, ts, stride=nb), :].astype(BF16)


def _s5(za, a_row, w_in, w_out, d_skip, w_glu, b_glu):
    bsz, s, w = za.shape
    nb, ts = SCAN_BATCHES, SCAN_TS
    rows = nb * ts
    nblk = w // LANES
    nstate = a_row.shape[1]
    full = lambda a: pl.BlockSpec(a.shape, lambda b, i: (0,) * a.ndim)
    args = (a_row, w_in, w_out, d_skip.reshape(1, w).astype(F32), w_glu.astype(BF16), b_glu.reshape(1, w))
    return pl.pallas_call(
        _s5_body,
        grid=(bsz // nb, s // ts),
        in_specs=[pl.BlockSpec((nb, ts, w), lambda b, i: (b, i, 0))] + [full(a) for a in args],
        out_specs=pl.BlockSpec((nb, ts, w), lambda b, i: (b, i, 0)),
        out_shape=jax.ShapeDtypeStruct((bsz, s, w), BF16),
        scratch_shapes=[pltpu.VMEM((nblk, nb * SLAB_PITCH, LANES), F32),
                        pltpu.VMEM((rows, w), F32),
                        pltpu.VMEM((rows, nstate), F32),
                        pltpu.VMEM((nb, nstate), F32),
                        pltpu.VMEM((nblk, rows, LANES), F32)],
        compiler_params=_cparams(("parallel", "arbitrary")),
        name="s5",
    )(za, *args)


def _mix_body(x_ref, u_ref, v_ref, ya_ref, gates_ref, mod_ref, wp_ref, sb_ref, wba_ref, wbb_ref,
              wout_ref, g2_ref, wr_ref, br_ref, h1_ref, xp_ref, route_ref):
    tm_full = x_ref.shape[1]
    npair = u_ref.shape[2] // LANES
    hd = LANES // 2

    t_idx = lax.broadcasted_iota(I32, (SGU_CHUNK, 2 * SGU_CHUNK), 0)
    s_idx = lax.broadcasted_iota(I32, (SGU_CHUNK, 2 * SGU_CHUNK), 1) & (SGU_CHUNK - 1)
    causal = s_idx <= t_idx
    lane = lax.broadcasted_iota(I32, (SGU_CHUNK, LANES), 1)
    first_head = lane < hd
    w_causal = [jnp.where(causal, wp_ref[j], jnp.zeros_like(wp_ref[j])) for j in range(npair)]
    for r0 in range(0, tm_full, MIX_SUBTILE):
        _mix_rows(slice(r0, r0 + MIX_SUBTILE), w_causal, first_head, x_ref, u_ref, v_ref, ya_ref, gates_ref,
                  mod_ref, sb_ref, wba_ref, wbb_ref, wout_ref, g2_ref, wr_ref, br_ref, h1_ref, xp_ref,
                  route_ref)


def _mix_rows(rs, w_causal, first_head, x_ref, u_ref, v_ref, ya_ref, gates_ref, mod_ref, sb_ref, wba_ref,
              wbb_ref, wout_ref, g2_ref, wr_ref, br_ref, h1_ref, xp_ref, route_ref):
    tm = rs.stop - rs.start
    d = x_ref.shape[2]
    npair = len(w_causal)
    mixed_rows = []
    for c in range(rs.start, rs.stop, SGU_CHUNK):
        vc = v_ref[0, c:c + SGU_CHUNK, :]
        blocks = []
        for j in range(npair):
            vb = vc[:, j * LANES:(j + 1) * LANES]
            zero = jnp.zeros_like(vb)
            rhs = jnp.concatenate([jnp.where(first_head, vb, zero), jnp.where(first_head, zero, vb)], axis=0)
            blocks.append(_dot(w_causal[j], rhs))
        mixed_rows.append(jnp.concatenate(blocks, axis=1) + sb_ref[...])
    mixed = jnp.concatenate(mixed_rows, axis=0)
    yb = (u_ref[0, rs, :].astype(F32) * mixed).astype(BF16)

    pa = _dot(ya_ref[0, rs, :], wba_ref[...])
    pb = _dot(yb, wbb_ref[...])
    merged = gates_ref[0, rs, :d].astype(F32) * pa + gates_ref[0, rs, d:].astype(F32) * pb
    o = _dot(merged.astype(BF16), wout_ref[...])
    h1 = x_ref[0, rs, :] + mod_ref[0, 2:3, :] * o
    h1_ref[0, rs, :] = h1

    ms = jnp.mean(h1 * h1, axis=-1, keepdims=True)
    xn = h1 * lax.rsqrt(ms + RMS_EPS) * g2_ref[...]
    xn = xn * (1.0 + mod_ref[0, 4:5, :]) + mod_ref[0, 3:4, :]
    x_hi = xn.astype(BF16)

    logits = _dot(x_hi, wr_ref[...]) + br_ref[...]
    lt = logits.T
    best = lt[0:1, :]
    grp = jnp.zeros((1, tm), I32)
    for gi in range(1, N_GROUPS):
        better = lt[gi:gi + 1, :] > best
        grp = jnp.where(better, gi, grp)
        best = jnp.where(better, lt[gi:gi + 1, :], best)
    den = jnp.zeros((1, tm), F32)
    for gi in range(N_GROUPS):
        den = den + jnp.exp(lt[gi:gi + 1, :] - best)
    pg = 1.0 / den
    le = lt[SUBLANES:2 * SUBLANES, :]
    for gi in range(1, N_GROUPS):
        le = jnp.where(grp == gi, lt[SUBLANES * (gi + 1):SUBLANES * (gi + 2), :], le)
    eidx = lax.broadcasted_iota(I32, (EXPERTS_PER_GROUP, tm), 0).astype(F32)
    none = float(EXPERTS_PER_GROUP)
    v1 = jnp.max(le, axis=0, keepdims=True)
    i1 = jnp.min(jnp.where(le == v1, eidx, none), axis=0, keepdims=True)
    rest = jnp.where(eidx == i1, -jnp.inf, le)
    v2 = jnp.max(rest, axis=0, keepdims=True)
    i2 = jnp.min(jnp.where(rest == v2, eidx, none), axis=0, keepdims=True)
    ex = jnp.exp(v2 - v1)
    p1 = 1.0 / (1.0 + ex)
    wt1 = pg * p1
    wt2 = pg * (ex * p1)
    first_lo = i1 < i2
    lo = jnp.where(first_lo, i1, i2)
    hi = jnp.where(first_lo, i2, i1)
    lo_odd = lo - 2.0 * jnp.floor(lo * 0.5)
    within = jnp.where(lo_odd > 0.5, (EXPERTS_PER_GROUP - 1.0) - hi, hi - lo - 1.0)
    pair = lo * (2 * EXPERTS_PER_GROUP - 1.0 - lo) * 0.5 + within
    cls = grp.astype(F32) * PAIRS_PER_GROUP + pair
    info = jnp.concatenate([cls, jnp.where(first_lo, wt1, wt2), jnp.where(first_lo, wt2, wt1)], axis=0)
    route_ref[:, rs] = jnp.concatenate([info, jnp.zeros((SUBLANES - 3, tm), F32)], axis=0)

    bits = lax.bitcast_convert_type(x_hi.astype(F32), I32)
    packed = lax.shift_right_logical(bits[:, :d // 2], 16) | (bits[:, d // 2:] & jnp.int32(-65536))
    info_cols = jnp.concatenate([info, jnp.zeros((LANES - 3, tm), F32)], axis=0).T
    xp_ref[0, rs, :] = jnp.concatenate([packed, lax.bitcast_convert_type(info_cols, I32)], axis=1)


def _mix(x, u, v, ya, gates, mod, sgu_w, sgu_b, w_ba, w_bb, w_out, g2, w_rg, b_rg, w_re, b_re, tm):
    bsz, s, d = x.shape
    w = u.shape[2]
    nt = bsz * s
    wp = sgu_w.reshape(SGU_HEADS // 2, 2, SGU_CHUNK, SGU_CHUNK).transpose(0, 2, 1, 3)
    wp = wp.reshape(SGU_HEADS // 2, SGU_CHUNK, 2 * SGU_CHUNK).astype(BF16)
    sb = jnp.repeat(sgu_b.T, w // SGU_HEADS, axis=1).astype(F32)
    wr = jnp.zeros((d, LANES), F32)
    wr = wr.at[:, :N_GROUPS].set(w_rg)
    wr = wr.at[:, SUBLANES:SUBLANES + N_GROUPS * EXPERTS_PER_GROUP].set(
        w_re.transpose(1, 0, 2).reshape(d, N_GROUPS * EXPERTS_PER_GROUP))
    br = jnp.zeros((1, LANES), F32)
    br = br.at[0, :N_GROUPS].set(b_rg)
    br = br.at[0, SUBLANES:SUBLANES + N_GROUPS * EXPERTS_PER_GROUP].set(b_re.reshape(-1))
    tok = lambda n: pl.BlockSpec((1, tm, n), lambda b, i: (b, i, 0))
    full = lambda a: pl.BlockSpec(a.shape, lambda b, i: (0,) * a.ndim)
    args = (wp, sb, w_ba.astype(BF16), w_bb.astype(BF16), w_out.astype(BF16), g2.reshape(1, d),
            wr.astype(BF16), br)
    nsteps = s // tm
    return pl.pallas_call(
        _mix_body,
        grid=(bsz, nsteps),
        in_specs=[tok(d), tok(w), tok(w), tok(w), tok(2 * d),
                  pl.BlockSpec((1, 6, d), lambda b, i: (b, 0, 0))] + [full(a) for a in args],
        out_specs=[tok(d), tok(d // 2 + LANES),
                   pl.BlockSpec((SUBLANES, tm), lambda b, i: (0, b * nsteps + i))],
        out_shape=[jax.ShapeDtypeStruct((bsz, s, d), F32),
                   jax.ShapeDtypeStruct((bsz, s, d // 2 + LANES), I32),
                   jax.ShapeDtypeStruct((SUBLANES, nt), F32)],
        compiler_params=_cparams(("parallel", "parallel")),
        name="mix",
    )(x, u, v, ya, gates, mod, *args)


def _tile_lanes(a, reps):
    return jnp.concatenate([a] * reps, axis=1)


def _dispatch_body(route_ref, lo_ref, hi_ref, pos_ref, blk_ref, cls_ref, run, pstart, ranks, classes,
                   earlier):
    i = pl.program_id(0)
    n = pl.num_programs(0)
    tt = route_ref.shape[1]
    reps = tt // LANES
    cid = lax.broadcasted_iota(I32, (LANES, tt), 0).astype(F32)

    sub = earlier.shape[0]

    @pl.when(i == 0)
    def _():
        run[...] = jnp.zeros_like(run)
        earlier_tok = lax.broadcasted_iota(I32, (sub, sub), 0) < lax.broadcasted_iota(I32, (sub, sub), 1)
        earlier[...] = jnp.where(earlier_tok, 1.0, 0.0).astype(BF16)

    cls = route_ref[0:1, :]
    classes[i] = cls
    onehot = jnp.where(cls == cid, 1.0, 0.0)
    seen = run[...]
    before = []
    for k in range(tt // sub):
        oh = onehot[:, k * sub:(k + 1) * sub]
        before.append(_dot(oh.astype(BF16), earlier[...]) + _tile_lanes(seen, sub // LANES))
        seen = seen + jnp.sum(oh, axis=1, keepdims=True)
    ranks[i] = jnp.sum(onehot * jnp.concatenate(before, axis=1), axis=0, keepdims=True)
    run[...] = seen

    @pl.when(i == n - 1)
    def _():
        counts = run[...]
        nblk = jnp.floor((counts + (EXPERT_ROWS - 1.0)) * (1.0 / EXPERT_ROWS))
        hi_part = jnp.floor(nblk * (1.0 / 16.0))
        lo_part = nblk - 16.0 * hi_part
        upto = lax.broadcasted_iota(I32, (LANES, LANES), 1) <= lax.broadcasted_iota(I32, (LANES, LANES), 0)
        upto = jnp.where(upto, 1.0, 0.0).astype(BF16)
        ends = 16.0 * _dot(upto, hi_part.astype(BF16)) + _dot(upto, lo_part.astype(BF16))
        pstart[...] = (ends - nblk) * EXPERT_ROWS
        diag = lax.broadcasted_iota(I32, (LANES, LANES), 0) == lax.broadcasted_iota(I32, (LANES, LANES), 1)
        on_lanes = lambda a: jnp.sum(jnp.where(diag, a, 0.0), axis=0, keepdims=True)
        cls_ref[...] = jnp.concatenate([on_lanes(counts), on_lanes(pstart[...]),
                                        jnp.zeros((SUBLANES - 2, LANES), F32)], axis=0)
        nb_lanes = blk_ref.shape[1]
        breps = nb_lanes // LANES
        n_used = _tile_lanes(ends[LANES - 1:LANES, :], breps)
        b_idx = lax.broadcasted_iota(I32, (LANES, nb_lanes), 1).astype(F32)
        b_idx = jnp.minimum(b_idx, n_used - 1.0)
        b_cls = jnp.sum(jnp.where(_tile_lanes(ends, breps) <= b_idx, 1.0, 0.0), axis=0, keepdims=True)
        b_member = b_cls == lax.broadcasted_iota(I32, (LANES, nb_lanes), 0).astype(F32)
        pick = lambda tab: jnp.sum(jnp.where(b_member, _tile_lanes(tab, breps), 0.0), axis=0, keepdims=True)
        blk_ref[...] = jnp.concatenate([pick(lo_ref[...]), pick(hi_ref[...]), n_used,
                                        jnp.zeros((SUBLANES - 3, nb_lanes), F32)], axis=0)
        starts = _tile_lanes(pstart[...], reps)

        def tile_positions(k, carry):
            base = jnp.sum(jnp.where(classes[k] == cid, starts, 0.0), axis=0, keepdims=True)
            pos_ref[k] = (base + ranks[k]).astype(I32)
            return carry

        lax.fori_loop(0, n, tile_positions, 0)


def _dispatch(route, n_blocks, tt=4096):
    nt = route.shape[1]
    n = nt // tt
    nb_lanes = pl.cdiv(n_blocks, LANES) * LANES
    lo_tab, hi_tab = [], []
    for g in range(N_GROUPS):
        for a in range(EXPERTS_PER_GROUP):
            his = range(a + 1, EXPERTS_PER_GROUP)
            for b in (reversed(his) if a % 2 else his):
                lo_tab.append(g * EXPERTS_PER_GROUP + a)
                hi_tab.append(g * EXPERTS_PER_GROUP + b)
    pad = [0] * (LANES - N_CLASSES)
    lo_tile = jnp.broadcast_to(jnp.asarray(lo_tab + pad, F32)[:, None], (LANES, LANES))
    hi_tile = jnp.broadcast_to(jnp.asarray(hi_tab + pad, F32)[:, None], (LANES, LANES))
    const = lambda shape: pl.BlockSpec(shape, lambda i: (0,) * len(shape))
    pos, blk_tab, cls_tab = pl.pallas_call(
        _dispatch_body,
        grid=(n,),
        in_specs=[pl.BlockSpec((SUBLANES, tt), lambda i: (0, i)),
                  const((LANES, LANES)), const((LANES, LANES))],
        out_specs=[const((n, 1, tt)), const((SUBLANES, nb_lanes)), const((SUBLANES, LANES))],
        out_shape=[jax.ShapeDtypeStruct((n, 1, tt), I32),
                   jax.ShapeDtypeStruct((SUBLANES, nb_lanes), F32),
                   jax.ShapeDtypeStruct((SUBLANES, LANES), F32)],
        scratch_shapes=[pltpu.VMEM((LANES, LANES), F32), pltpu.VMEM((LANES, LANES), F32),
                        pltpu.VMEM((n, 1, tt), F32), pltpu.VMEM((n, 1, tt), F32),
                        pltpu.VMEM((DISPATCH_SUB, DISPATCH_SUB), BF16)],
        compiler_params=_cparams(("arbitrary",)),
        name="dispatch",
    )(route, lo_tile, hi_tile)
    blk = blk_tab.astype(I32)
    cls = cls_tab.astype(I32)
    return pos.reshape(nt), blk[0, :n_blocks], blk[1, :n_blocks], blk[2, :1], cls[0], cls[1]


ZERO_FILL_ROWS = tuple(1 << k for k in range(EXPERT_ROWS.bit_length() - 1))


def _tile_rows_wait(buf, sem):
    pltpu.make_async_copy(buf, buf, sem).wait()


def _scatter_body(cnt_ref, pst_ref, pos_ref, xp_ref, xs_hbm, ring, zeros, sem, zsem):
    i = pl.program_id(0)
    n = pl.num_programs(0)
    tm, wrow = xp_ref.shape
    half = tm // 2
    n_tiles = wrow // LANES

    @pl.when(i == 0)
    def _():
        ring[...] = jnp.zeros_like(ring)

    for h in range(2):
        @pl.when(i > 0)
        def _():
            _tile_rows_wait(ring.at[h], sem.at[h])

        for c in range(n_tiles):
            ring.at[h][pl.ds(c, half, stride=SUBLANES), :] = xp_ref[h * half:(h + 1) * half,
                                                                    c * LANES:(c + 1) * LANES]
        for r in range(half):
            pltpu.make_async_copy(ring.at[h, pl.ds(r * SUBLANES, SUBLANES)],
                                  xs_hbm.at[pos_ref[0, 0, h * half + r]], sem.at[h]).start(priority=r % 2)

    @pl.when(i == n - 1)
    def _():
        _tile_rows_wait(ring.at[0], sem.at[0])
        _tile_rows_wait(ring.at[1], sem.at[1])
        zeros[...] = jnp.zeros_like(zeros)

        def fill(wait):
            def per_class(k, carry):
                cnt = cnt_ref[k]
                n_pad = (-cnt) & (EXPERT_ROWS - 1)
                off = pst_ref[k] + cnt
                for size in ZERO_FILL_ROWS:
                    piece = pltpu.make_async_copy(zeros.at[pl.ds(0, size)], xs_hbm.at[pl.ds(off, size)], zsem)

                    @pl.when((n_pad & size) != 0)
                    def _():
                        piece.wait() if wait else piece.start()

                    off = off + (n_pad & size)
                return carry

            lax.fori_loop(0, N_CLASSES, per_class, 0)

            last = N_CLASSES - 1
            used_rows = pst_ref[last] + cnt_ref[last] + ((-cnt_ref[last]) & (EXPERT_ROWS - 1))
            tail = ZERO_FILL_ROWS[-1]

            def per_piece(j, carry):
                piece = pltpu.make_async_copy(zeros, xs_hbm.at[pl.ds(used_rows + j * tail, tail)], zsem)
                piece.wait() if wait else piece.start()
                return carry

            lax.fori_loop(0, (xs_hbm.shape[0] - used_rows) // tail, per_piece, 0)

        fill(False)
        fill(True)


def _scatter_rows(xp, pos, counts, pstarts, n_blocks, tm=1024):
    nt, wrow = xp.shape
    n = nt // tm
    grid_spec = pltpu.PrefetchScalarGridSpec(
        num_scalar_prefetch=2,
        grid=(n,),
        in_specs=[pl.BlockSpec((1, 1, tm), lambda i, c, p: (i, 0, 0), memory_space=pltpu.SMEM),
                  pl.BlockSpec((tm, wrow), lambda i, c, p: (i, 0))],
        out_specs=pl.BlockSpec(memory_space=pl.ANY),
        scratch_shapes=[pltpu.VMEM((2, tm // 2 * SUBLANES, LANES), I32),
                        pltpu.VMEM((ZERO_FILL_ROWS[-1], SUBLANES, LANES), I32),
                        pltpu.SemaphoreType.DMA((2,)), pltpu.SemaphoreType.DMA(())],
    )
    return pl.pallas_call(
        _scatter_body,
        grid_spec=grid_spec,
        out_shape=jax.ShapeDtypeStruct((n_blocks * EXPERT_ROWS, SUBLANES, LANES), I32),
        compiler_params=_cparams(("arbitrary",)),
        name="scatter_rows",
    )(counts, pstarts, pos.reshape(n, 1, tm), xp)


def _unpack_rows(xp):
    left = lax.bitcast_convert_type(lax.shift_left(xp, 16), F32).astype(BF16)
    right = lax.bitcast_convert_type(xp & jnp.int32(-65536), F32).astype(BF16)
    return jnp.concatenate([left, right], axis=1)


def _experts_body(elo_ref, ehi_ref, nu_ref, xs_ref, w13a_ref, w2a_ref, w13b_ref, w2b_ref, o_ref):
    i = pl.program_id(0)
    rows = xs_ref.shape[0] // SUBLANES
    ff, d = w2a_ref.shape[1], w2a_ref.shape[2]
    n_packed = d // 2 // LANES
    lane_tile = lambda ref, c: ref[pl.ds(c, rows, stride=SUBLANES), :]

    @pl.when(i < nu_ref[0])
    def _():
        x = _unpack_rows(jnp.concatenate([lane_tile(xs_ref, c) for c in range(n_packed)], axis=1))
        info = lax.bitcast_convert_type(lane_tile(xs_ref, n_packed), F32)

        def ffn(w13_ref, w2_ref, weight):
            ab = _dot(x, w13_ref[0])
            a = ab[:, :ff]
            hid = (a * jax.nn.sigmoid(a)) * ab[:, ff:] * weight
            return _dot(hid.astype(BF16), w2_ref[0])

        y = ffn(w13a_ref, w2a_ref, info[:, 1:2]) + ffn(w13b_ref, w2b_ref, info[:, 2:3])
        for c in range(d // LANES):
            o_ref[pl.ds(c, rows, stride=SUBLANES), :] = y[:, c * LANES:(c + 1) * LANES]

    @pl.when(i >= nu_ref[0])
    def _():
        o_ref[...] = jnp.zeros_like(o_ref)


def _experts(xs, e_lo, e_hi, n_used, w13, w2):
    n_rows = xs.shape[0]
    n_blocks = n_rows // EXPERT_ROWS
    n_exp, ff, d = w2.shape
    assert d == SUBLANES * LANES
    blk = EXPERT_ROWS * SUBLANES
    pick = lambda a, use_hi: pl.BlockSpec(
        (1,) + a.shape[1:],
        (lambda i, lo, hi, nu: (hi[i], 0, 0)) if use_hi else (lambda i, lo, hi, nu: (lo[i], 0, 0)))
    grid_spec = pltpu.PrefetchScalarGridSpec(
        num_scalar_prefetch=3,
        grid=(n_blocks,),
        in_specs=[pl.BlockSpec((blk, LANES), lambda i, lo, hi, nu: (jnp.minimum(i, nu[0] - 1), 0)),
                  pick(w13, False), pick(w2, False), pick(w13, True), pick(w2, True)],
        out_specs=pl.BlockSpec((blk, LANES), lambda i, lo, hi, nu: (i, 0)),
    )
    y = pl.pallas_call(
        _experts_body,
        grid_spec=grid_spec,
        out_shape=jax.ShapeDtypeStruct((n_rows * SUBLANES, LANES), F32),
        compiler_params=_cparams(("arbitrary",)),
        name="experts",
    )(e_lo, e_hi, n_used, xs.reshape(n_rows * SUBLANES, LANES), w13, w2, w13, w2)
    return y.reshape(n_rows, SUBLANES, LANES)


def _final_body(posc_ref, posn_ref, y_hbm, h1_ref, mod_ref, gf_ref, o_ref, ybuf, sem):
    i = pl.program_id(0)
    n = pl.num_programs(0)
    tm = h1_ref.shape[0]
    half = tm // 2

    def fetch(idx_ref, h):
        for r in range(half):
            pltpu.make_async_copy(y_hbm.at[idx_ref[0, 0, h * half + r]],
                                  ybuf.at[h, pl.ds(r * SUBLANES, SUBLANES)], sem.at[h]).start(priority=r % 2)

    @pl.when(i == 0)
    def _():
        fetch(posc_ref, 0)
        fetch(posc_ref, 1)

    for h in range(2):
        rows = slice(h * half, (h + 1) * half)
        _tile_rows_wait(ybuf.at[h], sem.at[h])
        y = jnp.concatenate([ybuf.at[h][pl.ds(c, half, stride=SUBLANES), :] for c in range(SUBLANES)], axis=1)
        hres = h1_ref[rows, :] + mod_ref[0, 5:6, :] * y
        ms = jnp.mean(hres * hres, axis=-1, keepdims=True)
        o_ref[rows, :] = hres * lax.rsqrt(ms + RMS_EPS) * gf_ref[...]
        fetch(posn_ref, h)

    @pl.when(i == n - 1)
    def _():
        _tile_rows_wait(ybuf.at[0], sem.at[0])
        _tile_rows_wait(ybuf.at[1], sem.at[1])


def _final(pos, y_sorted, h1, mod, gf, seq, tm):
    nt, d = h1.shape
    n = nt // tm
    pos3 = pos.reshape(n, 1, tm)
    idx_spec = lambda f: pl.BlockSpec((1, 1, tm), f, memory_space=pltpu.SMEM)
    return pl.pallas_call(
        _final_body,
        grid=(n,),
        in_specs=[idx_spec(lambda i: (i, 0, 0)),
                  idx_spec(lambda i: (jnp.minimum(i + 1, n - 1), 0, 0)),
                  pl.BlockSpec(memory_space=pl.ANY),
                  pl.BlockSpec((tm, d), lambda i: (i, 0)),
                  pl.BlockSpec((1, 6, d), lambda i: (i * tm // seq, 0, 0)),
                  pl.BlockSpec((1, d), lambda i: (0, 0))],
        out_specs=pl.BlockSpec((tm, d), lambda i: (i, 0)),
        out_shape=jax.ShapeDtypeStruct((nt, d), F32),
        scratch_shapes=[pltpu.VMEM((2, tm // 2 * SUBLANES, LANES), F32), pltpu.SemaphoreType.DMA((2,))],
        compiler_params=_cparams(("arbitrary",)),
        name="final",
    )(pos3, pos3, y_sorted, h1, mod, gf.reshape(1, d))


def kernel(x, c, w_ada, b_ada, norm1_g, w_in, w_gate, b_gate, ssm_a_re, ssm_a_im, ssm_b_re, ssm_b_im, ssm_c_re, ssm_c_im, ssm_d, ssm_log_step, w_glu, b_glu, sgu_ln_g, sgu_ln_b, sgu_w, sgu_b, w_branch_a, w_branch_b, w_out, norm2_g, w_router_group, b_router_group, w_router_expert, b_router_expert, w1, w3, w2, norm_f_g):
    bsz, seq, d = x.shape
    depth = w_ada.shape[0]
    assert depth == 1 and bsz % SCAN_BATCHES == 0 and seq % 512 == 0
    l = 0
    mod = _adaln(c, w_ada[l], b_ada[l]).reshape(bsz, 6, d)
    za, u, v, gates, w13_bf, w2_bf = _inproj(x, mod, norm1_g[l], w_in[l], w_gate[l], b_gate[l],
                                             sgu_ln_g[l], sgu_ln_b[l], w1[l], w3[l], w2[l], tm=1024)
    a_row, s5_in, s5_out = _s5_params(ssm_a_re[l], ssm_a_im[l], ssm_b_re[l], ssm_b_im[l],
                                      ssm_c_re[l], ssm_c_im[l], ssm_log_step[l])
    ya = _s5(za, a_row, s5_in, s5_out, ssm_d[l], w_glu[l], b_glu[l])
    h1, xp, route = _mix(x, u, v, ya, gates, mod, sgu_w[l], sgu_b[l], w_branch_a[l], w_branch_b[l],
                         w_out[l], norm2_g[l], w_router_group[l], b_router_group[l],
                         w_router_expert[l], b_router_expert[l], tm=1024)
    nt = bsz * seq
    n_blocks = nt // EXPERT_ROWS + N_CLASSES
    pos, e_lo, e_hi, n_used, counts, pstarts = _dispatch(route, n_blocks)
    xs = _scatter_rows(xp.reshape(nt, xp.shape[-1]), pos, counts, pstarts, n_blocks)
    y_sorted = _experts(xs, e_lo, e_hi, n_used, w13_bf, w2_bf)
    out = _final(pos, y_sorted, h1.reshape(nt, d), mod, norm_f_g, seq, tm=1024)
    return out.reshape(bsz, seq, d)
```
